```python
import math
import jax, jax.numpy as jnp
from jax import lax
import numpy as np

D_MODEL = 2048
BATCH = 4
SEQ = 4096
DEPTH = 2

N_META = 16
EPS = 1e-6
D_CONV = D_MODEL // 2
CONV_WIDTH = 3
N_HEADS = D_MODEL // 128
QK_NOPE = 128
QK_ROPE = 64
QK_HEAD = QK_NOPE + QK_ROPE
V_HEAD = 128
Q_LORA = 512
KV_LORA = 512
ROPE_THETA = 10000.0
Q_BLOCK = 128
D_POOL = D_MODEL // 2
POOL_WINDOWS = (2, 4, 8, 16)
POOL_GROUP = D_POOL // len(POOL_WINDOWS)
N_BRANCH = 3
D_FF = 4 * D_MODEL
D_IN = 3 * D_CONV + Q_LORA + KV_LORA + QK_ROPE + D_POOL + N_BRANCH * D_MODEL
IN_SPLITS = (3 * D_CONV,
             3 * D_CONV + Q_LORA,
             3 * D_CONV + Q_LORA + KV_LORA,
             3 * D_CONV + Q_LORA + KV_LORA + QK_ROPE,
             3 * D_CONV + Q_LORA + KV_LORA + QK_ROPE + D_POOL)

kernel_name = 'hybrid_gated_conv_mla_pool_block'


def rms_norm(x, g):
    xf = x.astype(jnp.float32)
    y = xf * lax.rsqrt(jnp.mean(xf * xf, axis=-1, keepdims=True) + EPS)
    return (y * g.astype(jnp.float32)).astype(x.dtype)


def rope_tables(T, dtype):
    pos = jnp.arange(T, dtype=jnp.float32)
    inv = ROPE_THETA ** (-jnp.arange(0, QK_ROPE, 2, dtype=jnp.float32) / QK_ROPE)
    ang = pos[:, None] * inv[None, :]
    return jnp.cos(ang).astype(dtype), jnp.sin(ang).astype(dtype)


def apply_rope_tail(x, cos, sin):
    x_nope, x_rope = x[..., :QK_NOPE], x[..., QK_NOPE:]
    x1, x2 = jnp.split(x_rope, 2, axis=-1)
    c, s = cos[None, :, None, :], sin[None, :, None, :]
    return jnp.concatenate([x_nope, x1 * c - x2 * s, x2 * c + x1 * s], axis=-1)


def causal_short_conv(u, w):
    T = u.shape[1]
    up = jnp.pad(u, ((0, 0), (CONV_WIDTH - 1, 0), (0, 0)))
    return sum(w[j] * up[:, j:j + T] for j in range(CONV_WIDTH))


def causal_block_attention(q, k, v):
    B, T, H, dk = q.shape
    Tp = -(-T // Q_BLOCK) * Q_BLOCK
    pad = ((0, 0), (0, Tp - T), (0, 0), (0, 0))
    q, k, v = jnp.pad(q, pad), jnp.pad(k, pad), jnp.pad(v, pad)
    nb = Tp // Q_BLOCK
    q_blocks = q.reshape(B, nb, Q_BLOCK, H, dk).swapaxes(0, 1)
    k_pos = jnp.arange(Tp)
    scale = dk ** -0.5

    def one_block(args):
        qb, blk = args
        s = jnp.einsum('bqhd,bkhd->bhqk', qb, k).astype(jnp.float32) * scale
        q_pos = blk * Q_BLOCK + jnp.arange(Q_BLOCK)
        s = jnp.where(q_pos[:, None] >= k_pos[None, :], s, -jnp.inf)
        p = jax.nn.softmax(s, axis=-1).astype(v.dtype)
        return jnp.einsum('bhqk,bkhd->bqhd', p, v)

    out = lax.map(one_block, (q_blocks, jnp.arange(nb)))
    return out.swapaxes(0, 1).reshape(B, Tp, H, -1)[:, :T]


def multiscale_pool(u, pool_w, pool_scale):
    B, T, _ = u.shape
    uf = u.astype(jnp.float32)
    groups = jnp.split(uf, len(POOL_WINDOWS), axis=-1)
    seen = jnp.arange(1, T + 1, dtype=jnp.float32)
    outs = []
    for g, w in zip(groups, POOL_WINDOWS):
        cs = jnp.cumsum(g, axis=1)
        lagged = jnp.pad(cs, ((0, 0), (w, 0), (0, 0)))[:, :T]
        count = jnp.minimum(seen, float(w))[None, :, None]
        outs.append((cs - lagged) / count - g)
    pooled = jnp.stack(outs, axis=2).astype(u.dtype)
    mixed = jnp.einsum('btgc,gcd->btgd', pooled, pool_w).reshape(B, T, D_POOL)
    return mixed * pool_scale


def hybrid_layer(x, cos, sin, attn_norm, w_in, conv_w, q_lat_norm, kv_lat_norm, w_uq, w_ukv,
                 q_norm, k_norm, pool_w, pool_scale, w_branch_a, w_branch_b, w_branch_c, w_o,
                 mlp_norm, w_up, w_down):
    B, T, _ = x.shape
    h = rms_norm(x, attn_norm)
    proj = h @ w_in
    a_in, q_lat, kv_lat, k_rope, pool_in, gate_logits = jnp.split(proj, IN_SPLITS, axis=-1)

    u_a, b_a, c_a = jnp.split(a_in, 3, axis=-1)
    y_a = b_a * causal_short_conv(c_a * u_a, conv_w)

    q = (rms_norm(q_lat, q_lat_norm) @ w_uq).reshape(B, T, N_HEADS, QK_HEAD)
    kv = (rms_norm(kv_lat, kv_lat_norm) @ w_ukv).reshape(B, T, N_HEADS, QK_NOPE + V_HEAD)
    k_nope, v = kv[..., :QK_NOPE], kv[..., QK_NOPE:]
    k = jnp.concatenate(
        [k_nope, jnp.broadcast_to(k_rope[:, :, None, :], (B, T, N_HEADS, QK_ROPE))], axis=-1)
    q = apply_rope_tail(rms_norm(q, q_norm), cos, sin)
    k = apply_rope_tail(rms_norm(k, k_norm), cos, sin)
    y_b = causal_block_attention(q, k, v).reshape(B, T, N_HEADS * V_HEAD)

    y_c = multiscale_pool(pool_in, pool_w, pool_scale)

    gates = jax.nn.sigmoid(gate_logits).reshape(B, T, N_BRANCH, D_MODEL)
    merged = (gates[:, :, 0] * (y_a @ w_branch_a)
              + gates[:, :, 1] * (y_b @ w_branch_b)
              + gates[:, :, 2] * (y_c @ w_branch_c))
    x = x + merged @ w_o

    h2 = rms_norm(x, mlp_norm)
    return x + jnp.square(jax.nn.relu(h2 @ w_up)) @ w_down


def setup_inputs(seed: int = 0) -> dict:
    key = jax.random.key(seed)
    ks = jax.random.split(key, 24)
    f32 = jnp.float32

    def w(k, shape, fan_in):
        return jax.random.normal(k, shape, f32) * fan_in ** -0.5

    def gain(k, shape):
        return 1.0 + 0.05 * jax.random.normal(k, shape, f32)

    L = DEPTH
    return {
        'x': jax.random.normal(ks[0], (BATCH, SEQ, D_MODEL), f32),
        'meta_tokens': jax.random.normal(ks[1], (N_META, D_MODEL), f32),
        'attn_norm': gain(ks[2], (L, D_MODEL)),
        'w_in': w(ks[3], (L, D_MODEL, D_IN), D_MODEL),
        'conv_w': w(ks[4], (L, CONV_WIDTH, D_CONV), CONV_WIDTH),
        'q_lat_norm': gain(ks[5], (L, Q_LORA)),
        'kv_lat_norm': gain(ks[6], (L, KV_LORA)),
        'w_uq': w(ks[7], (L, Q_LORA, N_HEADS * QK_HEAD), Q_LORA),
        'w_ukv': w(ks[8], (L, KV_LORA, N_HEADS * (QK_NOPE + V_HEAD)), KV_LORA),
        'q_norm': gain(ks[9], (L, QK_HEAD)),
        'k_norm': gain(ks[10], (L, QK_HEAD)),
        'pool_w': w(ks[11], (L, len(POOL_WINDOWS), POOL_GROUP, POOL_GROUP), POOL_GROUP),
        'pool_scale': gain(ks[12], (L, D_POOL)),
        'w_branch_a': w(ks[13], (L, D_CONV, D_MODEL), D_CONV),
        'w_branch_b': w(ks[14], (L, N_HEADS * V_HEAD, D_MODEL), N_HEADS * V_HEAD),
        'w_branch_c': w(ks[15], (L, D_POOL, D_MODEL), D_POOL),
        'w_o': w(ks[16], (L, D_MODEL, D_MODEL), D_MODEL),
        'mlp_norm': gain(ks[17], (L, D_MODEL)),
        'w_up': w(ks[18], (L, D_MODEL, D_FF), D_MODEL),
        'w_down': w(ks[19], (L, D_FF, D_MODEL), D_FF),
    }


def reference(x, meta_tokens, attn_norm, w_in, conv_w, q_lat_norm, kv_lat_norm, w_uq, w_ukv,
              q_norm, k_norm, pool_w, pool_scale, w_branch_a, w_branch_b, w_branch_c, w_o,
              mlp_norm, w_up, w_down):
    B = x.shape[0]
    meta = jnp.broadcast_to(meta_tokens[None].astype(x.dtype), (B, N_META, D_MODEL))
    h = jnp.concatenate([meta, x], axis=1)
    cos, sin = rope_tables(h.shape[1], h.dtype)
    for l in range(DEPTH):
        h = hybrid_layer(h, cos, sin, attn_norm[l], w_in[l], conv_w[l], q_lat_norm[l],
                         kv_lat_norm[l], w_uq[l], w_ukv[l], q_norm[l], k_norm[l], pool_w[l],
                         pool_scale[l], w_branch_a[l], w_branch_b[l], w_branch_c[l], w_o[l],
                         mlp_norm[l], w_up[l], w_down[l])
    return h[:, N_META:]
```

```python
import functools

import jax
import jax.numpy as jnp
from jax import lax
from jax.experimental import pallas as pl
from jax.experimental.pallas import tpu as pltpu

D_MODEL = 2048
N_META = 16
EPS = 1e-6
D_CONV = 1024
N_HEADS = 16
QK_NOPE = 128
QK_ROPE = 64
QK_HEAD = QK_NOPE + QK_ROPE
HALF_ROPE = QK_ROPE // 2
V_HEAD = 128
Q_LORA = 512
KV_LORA = 512
ROPE_THETA = 10000.0
D_POOL = 1024
POOL_WINDOWS = (2, 4, 8, 16)
POOL_GROUP = D_POOL // len(POOL_WINDOWS)
D_FF = 4 * D_MODEL
HIST = 16
HEAD_PAD = 256
D_LAT = Q_LORA + KV_LORA + 128
VMEM_LIMIT = 56 * 2**20
NEG_BIG = -1e30

BF = jnp.bfloat16
F32 = jnp.float32


def _dot(a, b):
    return jnp.dot(a, b, preferred_element_type=F32)


def _dot_nt(a, b):
    return lax.dot_general(a, b, (((1,), (1,)), ((), ())), preferred_element_type=F32)


def _params(*sem):
    return pltpu.CompilerParams(dimension_semantics=sem, vmem_limit_bytes=VMEM_LIMIT)


def _rmsnorm_kernel(x_ref, g_ref, o_ref):
    x = x_ref[...]
    ms = jnp.mean(x * x, axis=-1, keepdims=True)
    o_ref[...] = (x * lax.rsqrt(ms + EPS) * g_ref[...]).astype(o_ref.dtype)


def _rmsnorm(x, g, tm):
    rows, d = x.shape
    return pl.pallas_call(
        _rmsnorm_kernel,
        grid=(rows // tm,),
        in_specs=[pl.BlockSpec((tm, d), lambda i: (i, 0)),
                  pl.BlockSpec((1, d), lambda i: (0, 0))],
        out_specs=pl.BlockSpec((tm, d), lambda i: (i, 0)),
        out_shape=jax.ShapeDtypeStruct((rows, d), BF),
        compiler_params=_params("parallel"),
    )(x, g)


def _conv_branch_kernel(h_ref, wu_ref, wb_ref, wc_ref, cw_ref, hist_ref, ya_ref, *rest,
                        tm, tiles_per_seq):
    carry_ref = rest[-1]
    i, j = pl.program_id(0), pl.program_id(1)
    h = h_ref[...]
    cu = _dot(h, wc_ref[...]) * _dot(h, wu_ref[...])

    @pl.when(i % tiles_per_seq == 0)
    def _():
        carry_ref[j] = hist_ref[...]

    ext = jnp.concatenate([carry_ref[j], cu], axis=0)
    tail = cu[tm - HIST:]
    carry_ref[j] = tail
    if len(rest) == 2:
        rest[0][...] = tail
    cw = cw_ref[...]
    y = (cw[2:3] * cu + cw[1:2] * ext[HIST - 1:HIST - 1 + tm] + cw[0:1] * ext[HIST - 2:HIST - 2 + tm])
    ya_ref[...] = (_dot(h, wb_ref[...]) * y).astype(ya_ref.dtype)


def _conv_branch(h, w_a, conv_w, hist, tm, tc, tiles_per_seq, emit_hist):
    rows = h.shape[0]
    ncb = D_CONV // tc
    assert not emit_hist or rows == tm
    kern = functools.partial(_conv_branch_kernel, tm=tm, tiles_per_seq=tiles_per_seq)
    n_out = 2 if emit_hist else 1
    return pl.pallas_call(
        kern,
        grid=(rows // tm, ncb),
        in_specs=[pl.BlockSpec((tm, D_MODEL), lambda i, j: (i, 0)),
                  pl.BlockSpec((D_MODEL, tc), lambda i, j: (0, j)),
                  pl.BlockSpec((D_MODEL, tc), lambda i, j: (0, j + ncb)),
                  pl.BlockSpec((D_MODEL, tc), lambda i, j: (0, j + 2 * ncb)),
                  pl.BlockSpec((3, tc), lambda i, j: (0, j)),
                  pl.BlockSpec((HIST, tc), lambda i, j: (0, j))],
        out_specs=[pl.BlockSpec((tm, tc), lambda i, j: (i, j)),
                   pl.BlockSpec((HIST, tc), lambda i, j: (0, j))][:n_out],
        out_shape=[jax.ShapeDtypeStruct((rows, D_CONV), BF),
                   jax.ShapeDtypeStruct((HIST, D_CONV), F32)][:n_out],
        scratch_shapes=[pltpu.VMEM((ncb, HIST, tc), F32)],
        compiler_params=_params("arbitrary", "arbitrary"),
    )(h, w_a, w_a, w_a, conv_w, hist)


def _latent_kernel(h_ref, w_ref, gq_ref, gkv_ref, qn_ref, kvn_ref, kr_ref):
    acc = _dot(h_ref[...], w_ref[...])

    def norm(a, g):
        ms = jnp.mean(a * a, axis=-1, keepdims=True)
        return a * lax.rsqrt(ms + EPS) * g

    qn_ref[...] = norm(acc[:, :Q_LORA], gq_ref[...]).astype(qn_ref.dtype)
    kvn_ref[...] = norm(acc[:, Q_LORA:Q_LORA + KV_LORA], gkv_ref[...]).astype(kvn_ref.dtype)
    kr_ref[...] = acc[:, Q_LORA + KV_LORA:]


def _latent(h, w_lat, gq, gkv, tm):
    rows = h.shape[0]
    return pl.pallas_call(
        _latent_kernel,
        grid=(rows // tm,),
        in_specs=[pl.BlockSpec((tm, D_MODEL), lambda i: (i, 0)),
                  pl.BlockSpec((D_MODEL, D_LAT), lambda i: (0, 0)),
                  pl.BlockSpec((1, Q_LORA), lambda i: (0, 0)),
                  pl.BlockSpec((1, KV_LORA), lambda i: (0, 0))],
        out_specs=[pl.BlockSpec((tm, Q_LORA), lambda i: (i, 0)),
                   pl.BlockSpec((tm, KV_LORA), lambda i: (i, 0)),
                   pl.BlockSpec((tm, 128), lambda i: (i, 0))],
        out_shape=[jax.ShapeDtypeStruct((rows, Q_LORA), BF),
                   jax.ShapeDtypeStruct((rows, KV_LORA), BF),
                   jax.ShapeDtypeStruct((rows, 128), F32)],
        compiler_params=_params("parallel"),
    )(h, w_lat, gq, gkv)


def _rope(x, cos_t, sin_t):
    return x * cos_t + pltpu.roll(x, 64, axis=1) * sin_t


def _q_kernel(qn_ref, w_ref, g_ref, cos_ref, sin_ref, q_ref, *, heads):
    acc = _dot(qn_ref[...], w_ref[...])
    g = g_ref[...]
    cos_t, sin_t = cos_ref[...], sin_ref[...]
    for hd in range(heads):
        a = acc[:, hd * HEAD_PAD:(hd + 1) * HEAD_PAD]
        ss = jnp.sum(a * a, axis=-1, keepdims=True)
        an = a * lax.rsqrt(ss * (1.0 / QK_HEAD) + EPS) * g
        q_ref[:, hd * HEAD_PAD:hd * HEAD_PAD + QK_NOPE] = an[:, :QK_NOPE].astype(q_ref.dtype)
        q_ref[:, hd * HEAD_PAD + QK_NOPE:(hd + 1) * HEAD_PAD] = _rope(
            an[:, QK_NOPE:], cos_t, sin_t).astype(q_ref.dtype)


def _q_proj(qn, w_uq, gq, cos_t, sin_t, tm, heads, tiles_per_seq):
    rows = qn.shape[0]
    kern = functools.partial(_q_kernel, heads=heads)
    return pl.pallas_call(
        kern,
        grid=(rows // tm, N_HEADS // heads),
        in_specs=[pl.BlockSpec((tm, Q_LORA), lambda i, j: (i, 0)),
                  pl.BlockSpec((Q_LORA, heads * HEAD_PAD), lambda i, j: (0, j)),
                  pl.BlockSpec((1, HEAD_PAD), lambda i, j: (0, 0)),
                  pl.BlockSpec((tm, 128), lambda i, j: (i % tiles_per_seq, 0)),
                  pl.BlockSpec((tm, 128), lambda i, j: (i % tiles_per_seq, 0))],
        out_specs=pl.BlockSpec((tm, heads * HEAD_PAD), lambda i, j: (i, j)),
        out_shape=jax.ShapeDtypeStruct((rows, N_HEADS * HEAD_PAD), BF),
        compiler_params=_params("parallel", "parallel"),
    )(qn, w_uq, gq, cos_t, sin_t)


def _kv_kernel(kvn_ref, wk_ref, wv_ref, kr_ref, gn_ref, gr_ref, cos_ref, sin_ref, k_ref, v_ref,
               *, heads):
    kvn = kvn_ref[...]
    acck = _dot(kvn, wk_ref[...])
    v_ref[...] = _dot(kvn, wv_ref[...]).astype(v_ref.dtype)
    kr = kr_ref[...]
    kr_ss = jnp.sum(kr * kr, axis=-1, keepdims=True)
    kr_rot = _rope(kr * gr_ref[...], cos_ref[...], sin_ref[...])
    gn = gn_ref[...]
    for hd in range(heads):
        kn = acck[:, hd * QK_NOPE:(hd + 1) * QK_NOPE]
        ss = jnp.sum(kn * kn, axis=-1, keepdims=True) + kr_ss
        r = lax.rsqrt(ss * (1.0 / QK_HEAD) + EPS)
        k_ref[:, hd * HEAD_PAD:hd * HEAD_PAD + QK_NOPE] = (kn * r * gn).astype(k_ref.dtype)
        k_ref[:, hd * HEAD_PAD + QK_NOPE:(hd + 1) * HEAD_PAD] = (kr_rot * r).astype(k_ref.dtype)


def _kv_proj(kvn, w_uk, w_uv, kr, gn, gr, cos_t, sin_t, tm, heads, tiles_per_seq):
    rows = kvn.shape[0]
    kern = functools.partial(_kv_kernel, heads=heads)
    return pl.pallas_call(
        kern,
        grid=(rows // tm, N_HEADS // heads),
        in_specs=[pl.BlockSpec((tm, KV_LORA), lambda i, j: (i, 0)),
                  pl.BlockSpec((KV_LORA, heads * QK_NOPE), lambda i, j: (0, j)),
                  pl.BlockSpec((KV_LORA, heads * V_HEAD), lambda i, j: (0, j)),
                  pl.BlockSpec((tm, 128), lambda i, j: (i, 0)),
                  pl.BlockSpec((1, 128), lambda i, j: (0, 0)),
                  pl.BlockSpec((1, 128), lambda i, j: (0, 0)),
                  pl.BlockSpec((tm, 128), lambda i, j: (i % tiles_per_seq, 0)),
                  pl.BlockSpec((tm, 128), lambda i, j: (i % tiles_per_seq, 0))],
        out_specs=[pl.BlockSpec((tm, heads * HEAD_PAD), lambda i, j: (i, j)),
                   pl.BlockSpec((tm, heads * V_HEAD), lambda i, j: (i, j))],
        out_shape=[jax.ShapeDtypeStruct((rows, N_HEADS * HEAD_PAD), BF),
                   jax.ShapeDtypeStruct((rows, N_HEADS * V_HEAD), BF)],
        compiler_params=_params("parallel", "parallel"),
    )(kvn, w_uk, w_uv, kr, gn, gr, cos_t, sin_t)


def _softmax_step(s, m, l, acc, v):
    m_new = jnp.maximum(m, jnp.max(s, axis=-1, keepdims=True))
    alpha = jnp.exp(m - m_new)
    p = jnp.exp(s - m_new)
    l = alpha * l + jnp.sum(p, axis=-1, keepdims=True)
    acc = alpha * acc + _dot(p.astype(BF), v)
    return m_new, l, acc


def _attn_kernel(q_ref, k_ref, v_ref, kp_ref, vp_ref, o_ref, *, tq, n_prefix):
    i = pl.program_id(2)
    q = q_ref[...]
    sp = _dot_nt(q, kp_ref[...])
    colp = lax.broadcasted_iota(jnp.int32, sp.shape, 1)
    sp = jnp.where(colp < n_prefix, sp, NEG_BIG)
    m = jnp.max(sp, axis=-1, keepdims=True)
    p = jnp.exp(sp - m)
    l = jnp.sum(p, axis=-1, keepdims=True)
    acc = _dot(p.astype(BF), vp_ref[...])

    def body(c, carry):
        start = pl.multiple_of(c * tq, tq)
        s = _dot_nt(q, k_ref[pl.ds(start, tq), :])
        return _softmax_step(s, *carry, v_ref[pl.ds(start, tq), :])

    m, l, acc = lax.fori_loop(0, i, body, (m, l, acc))
    start = pl.multiple_of(i * tq, tq)
    s = _dot_nt(q, k_ref[pl.ds(start, tq), :])
    row = lax.broadcasted_iota(jnp.int32, s.shape, 0)
    col = lax.broadcasted_iota(jnp.int32, s.shape, 1)
    s = jnp.where(row >= col, s, NEG_BIG)
    m, l, acc = _softmax_step(s, m, l, acc, v_ref[pl.ds(start, tq), :])
    o_ref[...] = (acc / l).astype(o_ref.dtype)


def _attention(q, k, v, kp, vp, batch, seq, tq):
    nq = seq // tq
    kern = functools.partial(_attn_kernel, tq=tq, n_prefix=N_META)
    return pl.pallas_call(
        kern,
        grid=(batch, N_HEADS, nq),
        in_specs=[pl.BlockSpec((tq, HEAD_PAD), lambda b, h, i: (b * nq + i, h)),
                  pl.BlockSpec((seq, HEAD_PAD), lambda b, h, i: (b, h)),
                  pl.BlockSpec((seq, V_HEAD), lambda b, h, i: (b, h)),
                  pl.BlockSpec((128, HEAD_PAD), lambda b, h, i: (0, h)),
                  pl.BlockSpec((128, V_HEAD), lambda b, h, i: (0, h))],
        out_specs=pl.BlockSpec((tq, V_HEAD), lambda b, h, i: (b * nq + i, h)),
        out_shape=jax.ShapeDtypeStruct((batch * seq, N_HEADS * V_HEAD), BF),
        compiler_params=_params("parallel", "parallel", "arbitrary"),
    )(q, k, v, kp, vp)


def _attn_meta_kernel(q_ref, k_ref, v_ref, o_ref):
    s = _dot_nt(q_ref[...], k_ref[...])
    row = lax.broadcasted_iota(jnp.int32, s.shape, 0)
    col = lax.broadcasted_iota(jnp.int32, s.shape, 1)
    s = jnp.where(row >= col, s, NEG_BIG)
    p = jnp.exp(s - jnp.max(s, axis=-1, keepdims=True))
    l = jnp.sum(p, axis=-1, keepdims=True)
    o_ref[...] = (_dot(p.astype(BF), v_ref[...]) / l).astype(o_ref.dtype)


def _attention_meta(q, k, v):
    return pl.pallas_call(
        _attn_meta_kernel,
        grid=(N_HEADS,),
        in_specs=[pl.BlockSpec((N_META, HEAD_PAD), lambda h: (0, h)),
                  pl.BlockSpec((N_META, HEAD_PAD), lambda h: (0, h)),
                  pl.BlockSpec((N_META, V_HEAD), lambda h: (0, h))],
        out_specs=pl.BlockSpec((N_META, V_HEAD), lambda h: (0, h)),
        out_shape=jax.ShapeDtypeStruct((N_META, N_HEADS * V_HEAD), BF),
        compiler_params=_params("parallel"),
    )(q, k, v)


def _pool_branch_kernel(h_ref, w_ref, pw_ref, ps_ref, hist_ref, yc_ref, *rest,
                        tm, tiles_per_seq, pos_offset):
    carry_ref = rest[-1]
    i = pl.program_id(0)
    pin = _dot(h_ref[...], w_ref[...])

    @pl.when(i % tiles_per_seq == 0)
    def _():
        carry_ref[...] = hist_ref[...]

    ext = jnp.concatenate([carry_ref[...], pin], axis=0)
    tail = pin[tm - HIST:]
    carry_ref[...] = tail
    if len(rest) == 2:
        rest[0][...] = tail
    n_ext = tm + HIST
    seen = (lax.broadcasted_iota(jnp.int32, (tm, 1), 0)
            + ((i % tiles_per_seq) * tm + pos_offset + 1)).astype(F32)
    for g, w in enumerate(POOL_WINDOWS):
        xg = ext[:, g * POOL_GROUP:(g + 1) * POOL_GROUP]
        s, span = xg, 1
        while span < w:
            s = s[span:] + s[:s.shape[0] - span]
            span *= 2
        s = s[n_ext - (w - 1) - tm:]
        pooled = s / jnp.minimum(seen, float(w)) - xg[HIST:]
        mixed = _dot(pooled.astype(BF), pw_ref[g])
        yc_ref[:, g * POOL_GROUP:(g + 1) * POOL_GROUP] = (
            mixed * ps_ref[:, g * POOL_GROUP:(g + 1) * POOL_GROUP]).astype(yc_ref.dtype)


def _pool_branch(h, w_pool, pool_w, pool_scale, hist, tm, tiles_per_seq, pos_offset, emit_hist):
    rows = h.shape[0]
    assert not emit_hist or rows == tm
    kern = functools.partial(_pool_branch_kernel, tm=tm, tiles_per_seq=tiles_per_seq,
                             pos_offset=pos_offset)
    ng = len(POOL_WINDOWS)
    n_out = 2 if emit_hist else 1
    return pl.pallas_call(
        kern,
        grid=(rows // tm,),
        in_specs=[pl.BlockSpec((tm, D_MODEL), lambda i: (i, 0)),
                  pl.BlockSpec((D_MODEL, D_POOL), lambda i: (0, 0)),
                  pl.BlockSpec((ng, POOL_GROUP, POOL_GROUP), lambda i: (0, 0, 0)),
                  pl.BlockSpec((1, D_POOL), lambda i: (0, 0)),
                  pl.BlockSpec((HIST, D_POOL), lambda i: (0, 0))],
        out_specs=[pl.BlockSpec((tm, D_POOL), lambda i: (i, 0)),
                   pl.BlockSpec((HIST, D_POOL), lambda i: (0, 0))][:n_out],
        out_shape=[jax.ShapeDtypeStruct((rows, D_POOL), BF),
                   jax.ShapeDtypeStruct((HIST, D_POOL), F32)][:n_out],
        scratch_shapes=[pltpu.VMEM((HIST, D_POOL), F32)],
        compiler_params=_params("arbitrary"),
    )(h, w_pool, pool_w, pool_scale, hist)


def _merge_kernel(h_ref, ya_ref, yb_ref, yc_ref, wg0_ref, wg1_ref, wg2_ref, wa_ref, wb_ref, wc_ref,
                  o_ref):
    h = h_ref[...]
    m = jax.nn.sigmoid(_dot(h, wg0_ref[...])) * _dot(ya_ref[...], wa_ref[...])
    m += jax.nn.sigmoid(_dot(h, wg1_ref[...])) * _dot(yb_ref[...], wb_ref[...])
    m += jax.nn.sigmoid(_dot(h, wg2_ref[...])) * _dot(yc_ref[...], wc_ref[...])
    o_ref[...] = m.astype(o_ref.dtype)


def _merge(h, ya, yb, yc, w_gate, wa, wb, wc, tm, tn):
    rows = h.shape[0]
    ncb = D_MODEL // tn
    row = lambda width: pl.BlockSpec((tm, width), lambda i, j: (i, 0))
    col = lambda depth, off: pl.BlockSpec((depth, tn), lambda i, j: (0, j + off))
    return pl.pallas_call(
        _merge_kernel,
        grid=(rows // tm, ncb),
        in_specs=[row(D_MODEL), row(D_CONV), row(N_HEADS * V_HEAD), row(D_POOL),
                  col(D_MODEL, 0), col(D_MODEL, ncb), col(D_MODEL, 2 * ncb),
                  col(D_CONV, 0), col(N_HEADS * V_HEAD, 0), col(D_POOL, 0)],
        out_specs=pl.BlockSpec((tm, tn), lambda i, j: (i, j)),
        out_shape=jax.ShapeDtypeStruct((rows, D_MODEL), BF),
        compiler_params=_params("parallel", "arbitrary"),
    )(h, ya, yb, yc, w_gate, w_gate, w_gate, wa, wb, wc)


def _oproj_kernel(m_ref, w_ref, x_ref, g_ref, xo_ref, h2_ref):
    x = x_ref[...] + _dot(m_ref[...], w_ref[...])
    xo_ref[...] = x
    ms = jnp.mean(x * x, axis=-1, keepdims=True)
    h2_ref[...] = (x * lax.rsqrt(ms + EPS) * g_ref[...]).astype(h2_ref.dtype)


def _oproj(merged, w_o, x, g, tm):
    rows = x.shape[0]
    return pl.pallas_call(
        _oproj_kernel,
        grid=(rows // tm,),
        in_specs=[pl.BlockSpec((tm, D_MODEL), lambda i: (i, 0)),
                  pl.BlockSpec((D_MODEL, D_MODEL), lambda i: (0, 0)),
                  pl.BlockSpec((tm, D_MODEL), lambda i: (i, 0)),
                  pl.BlockSpec((1, D_MODEL), lambda i: (0, 0))],
        out_specs=[pl.BlockSpec((tm, D_MODEL), lambda i: (i, 0)),
                   pl.BlockSpec((tm, D_MODEL), lambda i: (i, 0))],
        out_shape=[jax.ShapeDtypeStruct((rows, D_MODEL), F32),
                   jax.ShapeDtypeStruct((rows, D_MODEL), BF)],
        compiler_params=_params("parallel"),
    )(merged, w_o, x, g)


def _mlp_kernel(h2_ref, wu_ref, wd_ref, x_ref, o_ref):
    k = pl.program_id(1)
    a = jnp.maximum(_dot(h2_ref[...], wu_ref[...]), 0.0)
    y = _dot((a * a).astype(BF), wd_ref[...])

    @pl.when(k == 0)
    def _():
        o_ref[...] = x_ref[...] + y

    @pl.when(k != 0)
    def _():
        o_ref[...] += y


def _mlp(h2, w_up, w_down, x, tm, tf):
    rows = x.shape[0]
    return pl.pallas_call(
        _mlp_kernel,
        grid=(rows // tm, D_FF // tf),
        in_specs=[pl.BlockSpec((tm, D_MODEL), lambda i, k: (i, 0)),
                  pl.BlockSpec((D_MODEL, tf), lambda i, k: (0, k)),
                  pl.BlockSpec((tf, D_MODEL), lambda i, k: (k, 0)),
                  pl.BlockSpec((tm, D_MODEL), lambda i, k: (i, 0))],
        out_specs=pl.BlockSpec((tm, D_MODEL), lambda i, k: (i, 0)),
        out_shape=jax.ShapeDtypeStruct((rows, D_MODEL), F32),
        compiler_params=_params("parallel", "arbitrary"),
    )(h2, w_up, w_down, x)


def _spread_rope(a):
    z = jnp.zeros(a.shape[:-1] + (HALF_ROPE,), a.dtype)
    return jnp.concatenate([a[..., :HALF_ROPE], z, a[..., HALF_ROPE:], z], axis=-1)


def _pad_head(a):
    return jnp.concatenate([a[..., :QK_NOPE], _spread_rope(a[..., QK_NOPE:])], axis=-1)


def _layer_params(l, attn_norm, w_in, conv_w, q_lat_norm, kv_lat_norm, w_uq, w_ukv, q_norm, k_norm,
                  pool_w, pool_scale, w_branch_a, w_branch_b, w_branch_c, w_o, mlp_norm, w_up,
                  w_down):
    wi = w_in[l].astype(BF)
    o_q = 3 * D_CONV
    o_kv = o_q + Q_LORA
    o_kr = o_kv + KV_LORA
    o_pool = o_kr + QK_ROPE
    o_gate = o_pool + D_POOL
    w_ukv_h = w_ukv[l].astype(BF).reshape(KV_LORA, N_HEADS, QK_NOPE + V_HEAD)
    return dict(
        attn_norm=attn_norm[l][None],
        w_a=wi[:, :o_q],
        w_lat=jnp.concatenate([wi[:, o_q:o_kr], _spread_rope(wi[:, o_kr:o_pool])], axis=1),
        w_pool=wi[:, o_pool:o_gate],
        w_gate=wi[:, o_gate:],
        conv_w=conv_w[l],
        q_lat_norm=q_lat_norm[l][None],
        kv_lat_norm=kv_lat_norm[l][None],
        w_uq=_pad_head(w_uq[l].astype(BF).reshape(Q_LORA, N_HEADS, QK_HEAD)).reshape(
            Q_LORA, N_HEADS * HEAD_PAD),
        w_uk=w_ukv_h[:, :, :QK_NOPE].reshape(KV_LORA, N_HEADS * QK_NOPE),
        w_uv=w_ukv_h[:, :, QK_NOPE:].reshape(KV_LORA, N_HEADS * V_HEAD),
        gq=_pad_head(q_norm[l] * QK_HEAD ** -0.5)[None],
        gk_nope=k_norm[l][None, :QK_NOPE],
        gk_rope=_spread_rope(k_norm[l][QK_NOPE:])[None],
        pool_w=pool_w[l].astype(BF),
        pool_scale=pool_scale[l][None],
        wa=w_branch_a[l].astype(BF),
        wb=w_branch_b[l].astype(BF),
        wc=w_branch_c[l].astype(BF),
        w_o=w_o[l].astype(BF),
        mlp_norm=mlp_norm[l][None],
        w_up=w_up[l].astype(BF),
        w_down=w_down[l].astype(BF),
    )


def _rope_tables(total):
    pos = jnp.arange(total, dtype=F32)
    inv = ROPE_THETA ** (-jnp.arange(0, QK_ROPE, 2, dtype=F32) / QK_ROPE)
    ang = pos[:, None] * inv[None, :]
    cos, sin = jnp.cos(ang), jnp.sin(ang)
    z = jnp.zeros_like(cos)
    return (jnp.concatenate([cos, z, cos, z], axis=1),
            jnp.concatenate([-sin, z, sin, z], axis=1))


def _layer(x, p, cos_t, sin_t, prefix, *, batch, seq, finish):
    is_meta = prefix is None
    tm = min(512, seq)
    tps = seq // tm
    if is_meta:
        conv_hist = jnp.zeros((HIST, D_CONV), F32)
        pool_hist = jnp.zeros((HIST, D_POOL), F32)
    else:
        conv_hist, pool_hist, kp, vp = prefix

    h = _rmsnorm(x, p['attn_norm'], tm)
    ya, *conv_tail = _conv_branch(h, p['w_a'], p['conv_w'], conv_hist, tm, 512, tps, is_meta)
    yc, *pool_tail = _pool_branch(h, p['w_pool'], p['pool_w'], p['pool_scale'], pool_hist, tm, tps,
                                  0 if is_meta else N_META, is_meta)
    qn, kvn, kr = _latent(h, p['w_lat'], p['q_lat_norm'], p['kv_lat_norm'], tm)
    k, v = _kv_proj(kvn, p['w_uk'], p['w_uv'], kr, p['gk_nope'], p['gk_rope'], cos_t, sin_t,
                    tm, 4, tps)
    out_prefix = (conv_tail[0], pool_tail[0], k, v) if is_meta else None
    if not finish:
        return None, out_prefix
    q = _q_proj(qn, p['w_uq'], p['gq'], cos_t, sin_t, tm, 4, tps)
    if is_meta:
        yb = _attention_meta(q, k, v)
    else:
        pad = ((0, 128 - N_META), (0, 0))
        yb = _attention(q, k, v, jnp.pad(kp, pad), jnp.pad(vp, pad), batch, seq, 512)
    merged = _merge(h, ya, yb, yc, p['w_gate'], p['wa'], p['wb'], p['wc'], tm, 256)
    x_mid, h2 = _oproj(merged, p['w_o'], x, p['mlp_norm'], min(256, seq))
    return _mlp(h2, p['w_up'], p['w_down'], x_mid, tm, 512), out_prefix


def kernel(x, meta_tokens, attn_norm, w_in, conv_w, q_lat_norm, kv_lat_norm, w_uq, w_ukv, q_norm, k_norm, pool_w, pool_scale, w_branch_a, w_branch_b, w_branch_c, w_o, mlp_norm, w_up, w_down):
    batch, seq, d = x.shape
    depth = w_in.shape[0]
    cos_t, sin_t = _rope_tables(N_META + seq)
    xm = meta_tokens.astype(F32)
    xr = x.reshape(batch * seq, d)
    for l in range(depth):
        p = _layer_params(l, attn_norm, w_in, conv_w, q_lat_norm, kv_lat_norm, w_uq, w_ukv, q_norm,
                          k_norm, pool_w, pool_scale, w_branch_a, w_branch_b, w_branch_c, w_o,
                          mlp_norm, w_up, w_down)
        last = l == depth - 1
        xm, prefix = _layer(xm, p, cos_t[:N_META], sin_t[:N_META], None,
                            batch=1, seq=N_META, finish=not last)
        xr, _ = _layer(xr, p, cos_t[N_META:], sin_t[N_META:], prefix,
                       batch=batch, seq=seq, finish=True)
    return xr.reshape(batch, seq, d)
```

```python
import functools

import jax
import jax.numpy as jnp
from jax import lax
from jax.experimental import pallas as pl
from jax.experimental.pallas import tpu as pltpu

D_MODEL = 2048
N_META = 16
EPS = 1e-6
D_CONV = 1024
N_HEADS = 16
QK_NOPE = 128
QK_ROPE = 64
QK_HEAD = QK_NOPE + QK_ROPE
HALF_ROPE = QK_ROPE // 2
V_HEAD = 128
Q_LORA = 512
KV_LORA = 512
ROPE_THETA = 10000.0
D_POOL = 1024
POOL_WINDOWS = (2, 4, 8, 16)
POOL_GROUP = D_POOL // len(POOL_WINDOWS)
D_FF = 4 * D_MODEL
HIST = 16
HEAD_PAD = 256
D_LAT = Q_LORA + KV_LORA + 128
VMEM_LIMIT = 56 * 2**20
NEG_BIG = -1e30

BF = jnp.bfloat16
F32 = jnp.float32


def _dot(a, b):
    return jnp.dot(a, b, preferred_element_type=F32)


def _dot_nt(a, b):
    return lax.dot_general(a, b, (((1,), (1,)), ((), ())), preferred_element_type=F32)


def _params(*sem):
    return pltpu.CompilerParams(dimension_semantics=sem, vmem_limit_bytes=VMEM_LIMIT)


def _rmsnorm_kernel(x_ref, g_ref, o_ref):
    x = x_ref[...]
    ms = jnp.mean(x * x, axis=-1, keepdims=True)
    o_ref[...] = (x * lax.rsqrt(ms + EPS) * g_ref[...]).astype(o_ref.dtype)


def _rmsnorm(x, g, tm):
    rows, d = x.shape
    return pl.pallas_call(
        _rmsnorm_kernel,
        grid=(rows // tm,),
        in_specs=[pl.BlockSpec((tm, d), lambda i: (i, 0)),
                  pl.BlockSpec((1, d), lambda i: (0, 0))],
        out_specs=pl.BlockSpec((tm, d), lambda i: (i, 0)),
        out_shape=jax.ShapeDtypeStruct((rows, d), BF),
        name="rmsnorm",
        compiler_params=_params("parallel"),
    )(x, g)


def _conv_branch_kernel(h_ref, wu_ref, wb_ref, wc_ref, cw_ref, hist_ref, ya_ref, *rest,
                        tm, tiles_per_seq):
    carry_ref = rest[-1]
    i, j = pl.program_id(0), pl.program_id(1)
    h = h_ref[...]
    cu = _dot(h, wc_ref[...]) * _dot(h, wu_ref[...])

    @pl.when(i % tiles_per_seq == 0)
    def _():
        carry_ref[j] = hist_ref[...]

    ext = jnp.concatenate([carry_ref[j], cu], axis=0)
    tail = cu[tm - HIST:]
    carry_ref[j] = tail
    if len(rest) == 2:
        rest[0][...] = tail
    cw = cw_ref[...]
    y = (cw[2:3] * cu + cw[1:2] * ext[HIST - 1:HIST - 1 + tm] + cw[0:1] * ext[HIST - 2:HIST - 2 + tm])
    ya_ref[...] = (_dot(h, wb_ref[...]) * y).astype(ya_ref.dtype)


def _conv_branch(h, w_a, conv_w, hist, tm, tc, tiles_per_seq, emit_hist):
    rows = h.shape[0]
    ncb = D_CONV // tc
    assert not emit_hist or rows == tm
    kern = functools.partial(_conv_branch_kernel, tm=tm, tiles_per_seq=tiles_per_seq)
    n_out = 2 if emit_hist else 1
    return pl.pallas_call(
        kern,
        grid=(rows // tm, ncb),
        in_specs=[pl.BlockSpec((tm, D_MODEL), lambda i, j: (i, 0)),
                  pl.BlockSpec((D_MODEL, tc), lambda i, j: (0, j)),
                  pl.BlockSpec((D_MODEL, tc), lambda i, j: (0, j + ncb)),
                  pl.BlockSpec((D_MODEL, tc), lambda i, j: (0, j + 2 * ncb)),
                  pl.BlockSpec((3, tc), lambda i, j: (0, j)),
                  pl.BlockSpec((HIST, tc), lambda i, j: (0, j))],
        out_specs=[pl.BlockSpec((tm, tc), lambda i, j: (i, j)),
                   pl.BlockSpec((HIST, tc), lambda i, j: (0, j))][:n_out],
        out_shape=[jax.ShapeDtypeStruct((rows, D_CONV), BF),
                   jax.ShapeDtypeStruct((HIST, D_CONV), F32)][:n_out],
        scratch_shapes=[pltpu.VMEM((ncb, HIST, tc), F32)],
        name="conv_branch",
        compiler_params=_params("arbitrary", "arbitrary"),
    )(h, w_a, w_a, w_a, conv_w, hist)


def _latent_kernel(h_ref, w_ref, gq_ref, gkv_ref, qn_ref, kvn_ref, kr_ref):
    acc = _dot(h_ref[...], w_ref[...])

    def norm(a, g):
        ms = jnp.mean(a * a, axis=-1, keepdims=True)
        return a * lax.rsqrt(ms + EPS) * g

    qn_ref[...] = norm(acc[:, :Q_LORA], gq_ref[...]).astype(qn_ref.dtype)
    kvn_ref[...] = norm(acc[:, Q_LORA:Q_LORA + KV_LORA], gkv_ref[...]).astype(kvn_ref.dtype)
    kr_ref[...] = acc[:, Q_LORA + KV_LORA:]


def _latent(h, w_lat, gq, gkv, tm):
    rows = h.shape[0]
    return pl.pallas_call(
        _latent_kernel,
        grid=(rows // tm,),
        in_specs=[pl.BlockSpec((tm, D_MODEL), lambda i: (i, 0)),
                  pl.BlockSpec((D_MODEL, D_LAT), lambda i: (0, 0)),
                  pl.BlockSpec((1, Q_LORA), lambda i: (0, 0)),
                  pl.BlockSpec((1, KV_LORA), lambda i: (0, 0))],
        out_specs=[pl.BlockSpec((tm, Q_LORA), lambda i: (i, 0)),
                   pl.BlockSpec((tm, KV_LORA), lambda i: (i, 0)),
                   pl.BlockSpec((tm, 128), lambda i: (i, 0))],
        out_shape=[jax.ShapeDtypeStruct((rows, Q_LORA), BF),
                   jax.ShapeDtypeStruct((rows, KV_LORA), BF),
                   jax.ShapeDtypeStruct((rows, 128), F32)],
        name="latent",
        compiler_params=_params("parallel"),
    )(h, w_lat, gq, gkv)


def _rope(x, cos_t, sin_t):
    return x * cos_t + pltpu.roll(x, 64, axis=1) * sin_t


def _q_kernel(qn_ref, w_ref, g_ref, cos_ref, sin_ref, q_ref, *, heads):
    acc = _dot(qn_ref[...], w_ref[...])
    g = g_ref[...]
    cos_t, sin_t = cos_ref[...], sin_ref[...]
    for hd in range(heads):
        a = acc[:, hd * HEAD_PAD:(hd + 1) * HEAD_PAD]
        ss = jnp.sum(a * a, axis=-1, keepdims=True)
        an = a * lax.rsqrt(ss * (1.0 / QK_HEAD) + EPS) * g
        q_ref[:, hd * HEAD_PAD:hd * HEAD_PAD + QK_NOPE] = an[:, :QK_NOPE].astype(q_ref.dtype)
        q_ref[:, hd * HEAD_PAD + QK_NOPE:(hd + 1) * HEAD_PAD] = _rope(
            an[:, QK_NOPE:], cos_t, sin_t).astype(q_ref.dtype)


def _q_proj(qn, w_uq, gq, cos_t, sin_t, tm, heads, tiles_per_seq):
    rows = qn.shape[0]
    kern = functools.partial(_q_kernel, heads=heads)
    return pl.pallas_call(
        kern,
        grid=(rows // tm, N_HEADS // heads),
        in_specs=[pl.BlockSpec((tm, Q_LORA), lambda i, j: (i, 0)),
                  pl.BlockSpec((Q_LORA, heads * HEAD_PAD), lambda i, j: (0, j)),
                  pl.BlockSpec((1, HEAD_PAD), lambda i, j: (0, 0)),
                  pl.BlockSpec((tm, 128), lambda i, j: (i % tiles_per_seq, 0)),
                  pl.BlockSpec((tm, 128), lambda i, j: (i % tiles_per_seq, 0))],
        out_specs=pl.BlockSpec((tm, heads * HEAD_PAD), lambda i, j: (i, j)),
        out_shape=jax.ShapeDtypeStruct((rows, N_HEADS * HEAD_PAD), BF),
        name="q_proj",
        compiler_params=_params("parallel", "parallel"),
    )(qn, w_uq, gq, cos_t, sin_t)


def _kv_kernel(kvn_ref, wk_ref, wv_ref, kr_ref, gn_ref, gr_ref, cos_ref, sin_ref, k_ref, v_ref,
               *, heads):
    kvn = kvn_ref[...]
    acck = _dot(kvn, wk_ref[...])
    v_ref[...] = _dot(kvn, wv_ref[...]).astype(v_ref.dtype)
    kr = kr_ref[...]
    kr_ss = jnp.sum(kr * kr, axis=-1, keepdims=True)
    kr_rot = _rope(kr * gr_ref[...], cos_ref[...], sin_ref[...])
    gn = gn_ref[...]
    for hd in range(heads):
        kn = acck[:, hd * QK_NOPE:(hd + 1) * QK_NOPE]
        ss = jnp.sum(kn * kn, axis=-1, keepdims=True) + kr_ss
        r = lax.rsqrt(ss * (1.0 / QK_HEAD) + EPS)
        k_ref[:, hd * HEAD_PAD:hd * HEAD_PAD + QK_NOPE] = (kn * r * gn).astype(k_ref.dtype)
        k_ref[:, hd * HEAD_PAD + QK_NOPE:(hd + 1) * HEAD_PAD] = (kr_rot * r).astype(k_ref.dtype)


def _kv_proj(kvn, w_uk, w_uv, kr, gn, gr, cos_t, sin_t, tm, heads, tiles_per_seq):
    rows = kvn.shape[0]
    kern = functools.partial(_kv_kernel, heads=heads)
    return pl.pallas_call(
        kern,
        grid=(rows // tm, N_HEADS // heads),
        in_specs=[pl.BlockSpec((tm, KV_LORA), lambda i, j: (i, 0)),
                  pl.BlockSpec((KV_LORA, heads * QK_NOPE), lambda i, j: (0, j)),
                  pl.BlockSpec((KV_LORA, heads * V_HEAD), lambda i, j: (0, j)),
                  pl.BlockSpec((tm, 128), lambda i, j: (i, 0)),
                  pl.BlockSpec((1, 128), lambda i, j: (0, 0)),
                  pl.BlockSpec((1, 128), lambda i, j: (0, 0)),
                  pl.BlockSpec((tm, 128), lambda i, j: (i % tiles_per_seq, 0)),
                  pl.BlockSpec((tm, 128), lambda i, j: (i % tiles_per_seq, 0))],
        out_specs=[pl.BlockSpec((tm, heads * HEAD_PAD), lambda i, j: (i, j)),
                   pl.BlockSpec((tm, heads * V_HEAD), lambda i, j: (i, j))],
        out_shape=[jax.ShapeDtypeStruct((rows, N_HEADS * HEAD_PAD), BF),
                   jax.ShapeDtypeStruct((rows, N_HEADS * V_HEAD), BF)],
        name="kv_proj",
        compiler_params=_params("parallel", "parallel"),
    )(kvn, w_uk, w_uv, kr, gn, gr, cos_t, sin_t)


def _softmax_step(s, m, l, acc, v):
    m_new = jnp.maximum(m, jnp.max(s, axis=-1, keepdims=True))
    alpha = jnp.exp(m - m_new)
    p = jnp.exp(s - m_new)
    l = alpha * l + jnp.sum(p, axis=-1, keepdims=True)
    acc = alpha * acc + _dot(p.astype(BF), v)
    return m_new, l, acc


def _attn_kernel(q_ref, k_ref, v_ref, kp_ref, vp_ref, o_ref, *, tq, n_prefix):
    i = pl.program_id(2)
    q = q_ref[...]
    sp = _dot_nt(q, kp_ref[...])
    colp = lax.broadcasted_iota(jnp.int32, sp.shape, 1)
    sp = jnp.where(colp < n_prefix, sp, NEG_BIG)
    m = jnp.max(sp, axis=-1, keepdims=True)
    p = jnp.exp(sp - m)
    l = jnp.sum(p, axis=-1, keepdims=True)
    acc = _dot(p.astype(BF), vp_ref[...])

    def body(c, carry):
        start = pl.multiple_of(c * tq, tq)
        s = _dot_nt(q, k_ref[pl.ds(start, tq), :])
        return _softmax_step(s, *carry, v_ref[pl.ds(start, tq), :])

    m, l, acc = lax.fori_loop(0, i, body, (m, l, acc))
    start = pl.multiple_of(i * tq, tq)
    s = _dot_nt(q, k_ref[pl.ds(start, tq), :])
    row = lax.broadcasted_iota(jnp.int32, s.shape, 0)
    col = lax.broadcasted_iota(jnp.int32, s.shape, 1)
    s = jnp.where(row >= col, s, NEG_BIG)
    m, l, acc = _softmax_step(s, m, l, acc, v_ref[pl.ds(start, tq), :])
    o_ref[...] = (acc / l).astype(o_ref.dtype)


def _attention(q, k, v, kp, vp, batch, seq, tq):
    nq = seq // tq
    kern = functools.partial(_attn_kernel, tq=tq, n_prefix=N_META)
    return pl.pallas_call(
        kern,
        grid=(batch, N_HEADS, nq),
        in_specs=[pl.BlockSpec((tq, HEAD_PAD), lambda b, h, i: (b * nq + i, h)),
                  pl.BlockSpec((seq, HEAD_PAD), lambda b, h, i: (b, h)),
                  pl.BlockSpec((seq, V_HEAD), lambda b, h, i: (b, h)),
                  pl.BlockSpec((128, HEAD_PAD), lambda b, h, i: (0, h)),
                  pl.BlockSpec((128, V_HEAD), lambda b, h, i: (0, h))],
        out_specs=pl.BlockSpec((tq, V_HEAD), lambda b, h, i: (b * nq + i, h)),
        out_shape=jax.ShapeDtypeStruct((batch * seq, N_HEADS * V_HEAD), BF),
        name="attention",
        compiler_params=_params("parallel", "parallel", "arbitrary"),
    )(q, k, v, kp, vp)


def _attn_meta_kernel(q_ref, k_ref, v_ref, o_ref):
    s = _dot_nt(q_ref[...], k_ref[...])
    row = lax.broadcasted_iota(jnp.int32, s.shape, 0)
    col = lax.broadcasted_iota(jnp.int32, s.shape, 1)
    s = jnp.where(row >= col, s, NEG_BIG)
    p = jnp.exp(s - jnp.max(s, axis=-1, keepdims=True))
    l = jnp.sum(p, axis=-1, keepdims=True)
    o_ref[...] = (_dot(p.astype(BF), v_ref[...]) / l).astype(o_ref.dtype)


def _attention_meta(q, k, v):
    return pl.pallas_call(
        _attn_meta_kernel,
        grid=(N_HEADS,),
        in_specs=[pl.BlockSpec((N_META, HEAD_PAD), lambda h: (0, h)),
                  pl.BlockSpec((N_META, HEAD_PAD), lambda h: (0, h)),
                  pl.BlockSpec((N_META, V_HEAD), lambda h: (0, h))],
        out_specs=pl.BlockSpec((N_META, V_HEAD), lambda h: (0, h)),
        out_shape=jax.ShapeDtypeStruct((N_META, N_HEADS * V_HEAD), BF),
        name="attention_meta",
        compiler_params=_params("parallel"),
    )(q, k, v)


def _pool_branch_kernel(h_ref, w_ref, pw_ref, ps_ref, hist_ref, yc_ref, *rest,
                        tm, tiles_per_seq, pos_offset):
    carry_ref = rest[-1]
    i = pl.program_id(0)
    pin = _dot(h_ref[...], w_ref[...])

    @pl.when(i % tiles_per_seq == 0)
    def _():
        carry_ref[...] = hist_ref[...]

    ext = jnp.concatenate([carry_ref[...], pin], axis=0)
    tail = pin[tm - HIST:]
    carry_ref[...] = tail
    if len(rest) == 2:
        rest[0][...] = tail
    n_ext = tm + HIST
    seen = (lax.broadcasted_iota(jnp.int32, (tm, 1), 0)
            + ((i % tiles_per_seq) * tm + pos_offset + 1)).astype(F32)
    for g, w in enumerate(POOL_WINDOWS):
        xg = ext[:, g * POOL_GROUP:(g + 1) * POOL_GROUP]
        s, span = xg, 1
        while span < w:
            s = s[span:] + s[:s.shape[0] - span]
            span *= 2
        s = s[n_ext - (w - 1) - tm:]
        pooled = s / jnp.minimum(seen, float(w)) - xg[HIST:]
        mixed = _dot(pooled.astype(BF), pw_ref[g])
        yc_ref[:, g * POOL_GROUP:(g + 1) * POOL_GROUP] = (
            mixed * ps_ref[:, g * POOL_GROUP:(g + 1) * POOL_GROUP]).astype(yc_ref.dtype)


def _pool_branch(h, w_pool, pool_w, pool_scale, hist, tm, tiles_per_seq, pos_offset, emit_hist):
    rows = h.shape[0]
    assert not emit_hist or rows == tm
    kern = functools.partial(_pool_branch_kernel, tm=tm, tiles_per_seq=tiles_per_seq,
                             pos_offset=pos_offset)
    ng = len(POOL_WINDOWS)
    n_out = 2 if emit_hist else 1
    return pl.pallas_call(
        kern,
        grid=(rows // tm,),
        in_specs=[pl.BlockSpec((tm, D_MODEL), lambda i: (i, 0)),
                  pl.BlockSpec((D_MODEL, D_POOL), lambda i: (0, 0)),
                  pl.BlockSpec((ng, POOL_GROUP, POOL_GROUP), lambda i: (0, 0, 0)),
                  pl.BlockSpec((1, D_POOL), lambda i: (0, 0)),
                  pl.BlockSpec((HIST, D_POOL), lambda i: (0, 0))],
        out_specs=[pl.BlockSpec((tm, D_POOL), lambda i: (i, 0)),
                   pl.BlockSpec((HIST, D_POOL), lambda i: (0, 0))][:n_out],
        out_shape=[jax.ShapeDtypeStruct((rows, D_POOL), BF),
                   jax.ShapeDtypeStruct((HIST, D_POOL), F32)][:n_out],
        scratch_shapes=[pltpu.VMEM((HIST, D_POOL), F32)],
        name="pool_branch",
        compiler_params=_params("arbitrary"),
    )(h, w_pool, pool_w, pool_scale, hist)


def _merge_kernel(h_ref, ya_ref, yb_ref, yc_ref, wg0_ref, wg1_ref, wg2_ref, wa_ref, wb_ref, wc_ref,
                  o_ref):
    h = h_ref[...]
    m = jax.nn.sigmoid(_dot(h, wg0_ref[...])) * _dot(ya_ref[...], wa_ref[...])
    m += jax.nn.sigmoid(_dot(h, wg1_ref[...])) * _dot(yb_ref[...], wb_ref[...])
    m += jax.nn.sigmoid(_dot(h, wg2_ref[...])) * _dot(yc_ref[...], wc_ref[...])
    o_ref[...] = m.astype(o_ref.dtype)


def _merge(h, ya, yb, yc, w_gate, wa, wb, wc, tm, tn):
    rows = h.shape[0]
    ncb = D_MODEL // tn
    row = lambda width: pl.BlockSpec((tm, width), lambda i, j: (i, 0))
    col = lambda depth, off: pl.BlockSpec((depth, tn), lambda i, j: (0, j + off))
    return pl.pallas_call(
        _merge_kernel,
        grid=(rows // tm, ncb),
        in_specs=[row(D_MODEL), row(D_CONV), row(N_HEADS * V_HEAD), row(D_POOL),
                  col(D_MODEL, 0), col(D_MODEL, ncb), col(D_MODEL, 2 * ncb),
                  col(D_CONV, 0), col(N_HEADS * V_HEAD, 0), col(D_POOL, 0)],
        out_specs=pl.BlockSpec((tm, tn), lambda i, j: (i, j)),
        out_shape=jax.ShapeDtypeStruct((rows, D_MODEL), BF),
        name="merge",
        compiler_params=_params("parallel", "arbitrary"),
    )(h, ya, yb, yc, w_gate, w_gate, w_gate, wa, wb, wc)


def _oproj_kernel(m_ref, w_ref, x_ref, g_ref, xo_ref, h2_ref):
    x = x_ref[...] + _dot(m_ref[...], w_ref[...])
    xo_ref[...] = x
    ms = jnp.mean(x * x, axis=-1, keepdims=True)
    h2_ref[...] = (x * lax.rsqrt(ms + EPS) * g_ref[...]).astype(h2_ref.dtype)


def _oproj(merged, w_o, x, g, tm):
    rows = x.shape[0]
    return pl.pallas_call(
        _oproj_kernel,
        grid=(rows // tm,),
        in_specs=[pl.BlockSpec((tm, D_MODEL), lambda i: (i, 0)),
                  pl.BlockSpec((D_MODEL, D_MODEL), lambda i: (0, 0)),
                  pl.BlockSpec((tm, D_MODEL), lambda i: (i, 0)),
                  pl.BlockSpec((1, D_MODEL), lambda i: (0, 0))],
        out_specs=[pl.BlockSpec((tm, D_MODEL), lambda i: (i, 0)),
                   pl.BlockSpec((tm, D_MODEL), lambda i: (i, 0))],
        out_shape=[jax.ShapeDtypeStruct((rows, D_MODEL), F32),
                   jax.ShapeDtypeStruct((rows, D_MODEL), BF)],
        name="oproj",
        compiler_params=_params("parallel"),
    )(merged, w_o, x, g)


def _mlp_kernel(h2_ref, wu_ref, wd_ref, x_ref, o_ref):
    @pl.when(pl.program_id(1) == 0)
    def _():
        o_ref[...] = x_ref[...]

    a = jnp.maximum(_dot(h2_ref[...], wu_ref[...]), 0.0)
    o_ref[...] += _dot((a * a).astype(BF), wd_ref[...])


def _mlp(h2, w_up, w_down, x, tm, tf):
    rows = x.shape[0]
    return pl.pallas_call(
        _mlp_kernel,
        grid=(rows // tm, D_FF // tf),
        in_specs=[pl.BlockSpec((tm, D_MODEL), lambda i, k: (i, 0)),
                  pl.BlockSpec((D_MODEL, tf), lambda i, k: (0, k)),
                  pl.BlockSpec((tf, D_MODEL), lambda i, k: (k, 0)),
                  pl.BlockSpec((tm, D_MODEL), lambda i, k: (i, 0))],
        out_specs=pl.BlockSpec((tm, D_MODEL), lambda i, k: (i, 0)),
        out_shape=jax.ShapeDtypeStruct((rows, D_MODEL), F32),
        name="mlp",
        compiler_params=_params("parallel", "arbitrary"),
    )(h2, w_up, w_down, x)


def _spread_rope(a):
    z = jnp.zeros(a.shape[:-1] + (HALF_ROPE,), a.dtype)
    return jnp.concatenate([a[..., :HALF_ROPE], z, a[..., HALF_ROPE:], z], axis=-1)


def _pad_head(a):
    return jnp.concatenate([a[..., :QK_NOPE], _spread_rope(a[..., QK_NOPE:])], axis=-1)


def _layer_params(l, attn_norm, w_in, conv_w, q_lat_norm, kv_lat_norm, w_uq, w_ukv, q_norm, k_norm,
                  pool_w, pool_scale, w_branch_a, w_branch_b, w_branch_c, w_o, mlp_norm, w_up,
                  w_down):
    wi = w_in[l].astype(BF)
    o_q = 3 * D_CONV
    o_kv = o_q + Q_LORA
    o_kr = o_kv + KV_LORA
    o_pool = o_kr + QK_ROPE
    o_gate = o_pool + D_POOL
    w_ukv_h = w_ukv[l].astype(BF).reshape(KV_LORA, N_HEADS, QK_NOPE + V_HEAD)
    return dict(
        attn_norm=attn_norm[l][None],
        w_a=wi[:, :o_q],
        w_lat=jnp.concatenate([wi[:, o_q:o_kr], _spread_rope(wi[:, o_kr:o_pool])], axis=1),
        w_pool=wi[:, o_pool:o_gate],
        w_gate=wi[:, o_gate:],
        conv_w=conv_w[l],
        q_lat_norm=q_lat_norm[l][None],
        kv_lat_norm=kv_lat_norm[l][None],
        w_uq=_pad_head(w_uq[l].astype(BF).reshape(Q_LORA, N_HEADS, QK_HEAD)).reshape(
            Q_LORA, N_HEADS * HEAD_PAD),
        w_uk=w_ukv_h[:, :, :QK_NOPE].reshape(KV_LORA, N_HEADS * QK_NOPE),
        w_uv=w_ukv_h[:, :, QK_NOPE:].reshape(KV_LORA, N_HEADS * V_HEAD),
        gq=_pad_head(q_norm[l] * QK_HEAD ** -0.5)[None],
        gk_nope=k_norm[l][None, :QK_NOPE],
        gk_rope=_spread_rope(k_norm[l][QK_NOPE:])[None],
        pool_w=pool_w[l].astype(BF),
        pool_scale=pool_scale[l][None],
        wa=w_branch_a[l].astype(BF),
        wb=w_branch_b[l].astype(BF),
        wc=w_branch_c[l].astype(BF),
        w_o=w_o[l].astype(BF),
        mlp_norm=mlp_norm[l][None],
        w_up=w_up[l].astype(BF),
        w_down=w_down[l].astype(BF),
    )


def _rope_tables(total):
    pos = jnp.arange(total, dtype=F32)
    inv = ROPE_THETA ** (-jnp.arange(0, QK_ROPE, 2, dtype=F32) / QK_ROPE)
    ang = pos[:, None] * inv[None, :]
    cos, sin = jnp.cos(ang), jnp.sin(ang)
    z = jnp.zeros_like(cos)
    return (jnp.concatenate([cos, z, cos, z], axis=1),
            jnp.concatenate([-sin, z, sin, z], axis=1))


def _layer(x, p, cos_t, sin_t, prefix, *, batch, seq, finish):
    is_meta = prefix is None
    tm = min(512, seq)
    tps = seq // tm
    tm_big = min(1024, seq)
    if is_meta:
        conv_hist = jnp.zeros((HIST, D_CONV), F32)
        pool_hist = jnp.zeros((HIST, D_POOL), F32)
    else:
        conv_hist, pool_hist, kp, vp = prefix

    h = _rmsnorm(x, p['attn_norm'], tm)
    ya, *conv_tail = _conv_branch(h, p['w_a'], p['conv_w'], conv_hist, tm_big, 512,
                                  seq // tm_big, is_meta)
    yc, *pool_tail = _pool_branch(h, p['w_pool'], p['pool_w'], p['pool_scale'], pool_hist, tm, tps,
                                  0 if is_meta else N_META, is_meta)
    qn, kvn, kr = _latent(h, p['w_lat'], p['q_lat_norm'], p['kv_lat_norm'], tm)
    k, v = _kv_proj(kvn, p['w_uk'], p['w_uv'], kr, p['gk_nope'], p['gk_rope'], cos_t, sin_t,
                    tm, 4, tps)
    out_prefix = (conv_tail[0], pool_tail[0], k, v) if is_meta else None
    if not finish:
        return None, out_prefix
    q = _q_proj(qn, p['w_uq'], p['gq'], cos_t, sin_t, tm, 4, tps)
    if is_meta:
        yb = _attention_meta(q, k, v)
    else:
        pad = ((0, 128 - N_META), (0, 0))
        yb = _attention(q, k, v, jnp.pad(kp, pad), jnp.pad(vp, pad), batch, seq, 512)
    merged = _merge(h, ya, yb, yc, p['w_gate'], p['wa'], p['wb'], p['wc'], tm_big, 256)
    x_mid, h2 = _oproj(merged, p['w_o'], x, p['mlp_norm'], min(256, seq))
    return _mlp(h2, p['w_up'], p['w_down'], x_mid, tm_big, 512), out_prefix


def kernel(x, meta_tokens, attn_norm, w_in, conv_w, q_lat_norm, kv_lat_norm, w_uq, w_ukv, q_norm, k_norm, pool_w, pool_scale, w_branch_a, w_branch_b, w_branch_c, w_o, mlp_norm, w_up, w_down):
    batch, seq, d = x.shape
    depth = w_in.shape[0]
    cos_t, sin_t = _rope_tables(N_META + seq)
    xm = meta_tokens.astype(F32)
    xr = x.reshape(batch * seq, d)
    for l in range(depth):
        p = _layer_params(l, attn_norm, w_in, conv_w, q_lat_norm, kv_lat_norm, w_uq, w_ukv, q_norm,
                          k_norm, pool_w, pool_scale, w_branch_a, w_branch_b, w_branch_c, w_o,
                          mlp_norm, w_up, w_down)
        last = l == depth - 1
        xm, prefix = _layer(xm, p, cos_t[:N_META], sin_t[:N_META], None,
                            batch=1, seq=N_META, finish=not last)
        xr, _ = _layer(xr, p, cos_t[N_META:], sin_t[N_META:], prefix,
                       batch=batch, seq=seq, finish=True)
    return xr.reshape(batch, seq, d)
```

```python
import functools

import jax
import jax.numpy as jnp
from jax import lax
from jax.experimental import pallas as pl
from jax.experimental.pallas import tpu as pltpu

D_MODEL = 2048
N_META = 16
EPS = 1e-6
D_CONV = 1024
N_HEADS = 16
QK_NOPE = 128
QK_ROPE = 64
QK_HEAD = QK_NOPE + QK_ROPE
HALF_ROPE = QK_ROPE // 2
V_HEAD = 128
Q_LORA = 512
KV_LORA = 512
ROPE_THETA = 10000.0
D_POOL = 1024
POOL_WINDOWS = (2, 4, 8, 16)
POOL_GROUP = D_POOL // len(POOL_WINDOWS)
D_FF = 4 * D_MODEL
HIST = 16
HEAD_PAD = 256
D_LAT = Q_LORA + KV_LORA + 128
VMEM_LIMIT = 56 * 2**20
NEG_BIG = -1e30
LOG2_E = 1.4426950408889634

BF = jnp.bfloat16
F32 = jnp.float32


def _dot(a, b):
    return jnp.dot(a, b, preferred_element_type=F32)


def _dot_nt(a, b):
    return lax.dot_general(a, b, (((1,), (1,)), ((), ())), preferred_element_type=F32)


def _params(*sem):
    return pltpu.CompilerParams(dimension_semantics=sem, vmem_limit_bytes=VMEM_LIMIT)


def _rmsnorm_kernel(x_ref, g_ref, o_ref):
    x = x_ref[...]
    ms = jnp.mean(x * x, axis=-1, keepdims=True)
    o_ref[...] = (x * lax.rsqrt(ms + EPS) * g_ref[...]).astype(o_ref.dtype)


def _rmsnorm(x, g, tm):
    rows, d = x.shape
    return pl.pallas_call(
        _rmsnorm_kernel,
        grid=(rows // tm,),
        in_specs=[pl.BlockSpec((tm, d), lambda i: (i, 0)),
                  pl.BlockSpec((1, d), lambda i: (0, 0))],
        out_specs=pl.BlockSpec((tm, d), lambda i: (i, 0)),
        out_shape=jax.ShapeDtypeStruct((rows, d), BF),
        name="rmsnorm",
        compiler_params=_params("parallel"),
    )(x, g)


def _conv_branch_kernel(h_ref, wu_ref, wb_ref, wc_ref, cw_ref, hist_ref, ya_ref, *rest,
                        tm, tiles_per_seq):
    carry_ref = rest[-1]
    i, j = pl.program_id(0), pl.program_id(1)
    h = h_ref[...]
    cu = _dot(h, wc_ref[...]) * _dot(h, wu_ref[...])

    @pl.when(i % tiles_per_seq == 0)
    def _():
        carry_ref[j] = hist_ref[...]

    ext = jnp.concatenate([carry_ref[j], cu], axis=0)
    tail = cu[tm - HIST:]
    carry_ref[j] = tail
    if len(rest) == 2:
        rest[0][...] = tail
    cw = cw_ref[...]
    y = (cw[2:3] * cu + cw[1:2] * ext[HIST - 1:HIST - 1 + tm] + cw[0:1] * ext[HIST - 2:HIST - 2 + tm])
    ya_ref[...] = (_dot(h, wb_ref[...]) * y).astype(ya_ref.dtype)


def _conv_branch(h, w_a, conv_w, hist, tm, tc, tiles_per_seq, emit_hist):
    rows = h.shape[0]
    ncb = D_CONV // tc
    assert not emit_hist or rows == tm
    kern = functools.partial(_conv_branch_kernel, tm=tm, tiles_per_seq=tiles_per_seq)
    n_out = 2 if emit_hist else 1
    return pl.pallas_call(
        kern,
        grid=(rows // tm, ncb),
        in_specs=[pl.BlockSpec((tm, D_MODEL), lambda i, j: (i, 0)),
                  pl.BlockSpec((D_MODEL, tc), lambda i, j: (0, j)),
                  pl.BlockSpec((D_MODEL, tc), lambda i, j: (0, j + ncb)),
                  pl.BlockSpec((D_MODEL, tc), lambda i, j: (0, j + 2 * ncb)),
                  pl.BlockSpec((3, tc), lambda i, j: (0, j)),
                  pl.BlockSpec((HIST, tc), lambda i, j: (0, j))],
        out_specs=[pl.BlockSpec((tm, tc), lambda i, j: (i, j)),
                   pl.BlockSpec((HIST, tc), lambda i, j: (0, j))][:n_out],
        out_shape=[jax.ShapeDtypeStruct((rows, D_CONV), BF),
                   jax.ShapeDtypeStruct((HIST, D_CONV), F32)][:n_out],
        scratch_shapes=[pltpu.VMEM((ncb, HIST, tc), F32)],
        name="conv_branch",
        compiler_params=_params("arbitrary", "arbitrary"),
    )(h, w_a, w_a, w_a, conv_w, hist)


def _latent_kernel(h_ref, w_ref, gq_ref, gkv_ref, qn_ref, kvn_ref, kr_ref):
    acc = _dot(h_ref[...], w_ref[...])

    def norm(a, g):
        ms = jnp.mean(a * a, axis=-1, keepdims=True)
        return a * lax.rsqrt(ms + EPS) * g

    qn_ref[...] = norm(acc[:, :Q_LORA], gq_ref[...]).astype(qn_ref.dtype)
    kvn_ref[...] = norm(acc[:, Q_LORA:Q_LORA + KV_LORA], gkv_ref[...]).astype(kvn_ref.dtype)
    kr_ref[...] = acc[:, Q_LORA + KV_LORA:]


def _latent(h, w_lat, gq, gkv, tm):
    rows = h.shape[0]
    return pl.pallas_call(
        _latent_kernel,
        grid=(rows // tm,),
        in_specs=[pl.BlockSpec((tm, D_MODEL), lambda i: (i, 0)),
                  pl.BlockSpec((D_MODEL, D_LAT), lambda i: (0, 0)),
                  pl.BlockSpec((1, Q_LORA), lambda i: (0, 0)),
                  pl.BlockSpec((1, KV_LORA), lambda i: (0, 0))],
        out_specs=[pl.BlockSpec((tm, Q_LORA), lambda i: (i, 0)),
                   pl.BlockSpec((tm, KV_LORA), lambda i: (i, 0)),
                   pl.BlockSpec((tm, 128), lambda i: (i, 0))],
        out_shape=[jax.ShapeDtypeStruct((rows, Q_LORA), BF),
                   jax.ShapeDtypeStruct((rows, KV_LORA), BF),
                   jax.ShapeDtypeStruct((rows, 128), F32)],
        name="latent",
        compiler_params=_params("parallel"),
    )(h, w_lat, gq, gkv)


def _rope(x, cos_t, sin_t):
    return x * cos_t + pltpu.roll(x, 64, axis=1) * sin_t


def _q_kernel(qn_ref, w_ref, g_ref, cos_ref, sin_ref, q_ref, *, heads):
    acc = _dot(qn_ref[...], w_ref[...])
    g = g_ref[...]
    cos_t, sin_t = cos_ref[...], sin_ref[...]
    for hd in range(heads):
        a = acc[:, hd * HEAD_PAD:(hd + 1) * HEAD_PAD]
        ss = jnp.sum(a * a, axis=-1, keepdims=True)
        an = a * lax.rsqrt(ss * (1.0 / QK_HEAD) + EPS) * g
        q_ref[:, hd * HEAD_PAD:hd * HEAD_PAD + QK_NOPE] = an[:, :QK_NOPE].astype(q_ref.dtype)
        q_ref[:, hd * HEAD_PAD + QK_NOPE:(hd + 1) * HEAD_PAD] = _rope(
            an[:, QK_NOPE:], cos_t, sin_t).astype(q_ref.dtype)


def _q_proj(qn, w_uq, gq, cos_t, sin_t, tm, heads, tiles_per_seq):
    rows = qn.shape[0]
    kern = functools.partial(_q_kernel, heads=heads)
    return pl.pallas_call(
        kern,
        grid=(rows // tm, N_HEADS // heads),
        in_specs=[pl.BlockSpec((tm, Q_LORA), lambda i, j: (i, 0)),
                  pl.BlockSpec((Q_LORA, heads * HEAD_PAD), lambda i, j: (0, j)),
                  pl.BlockSpec((1, HEAD_PAD), lambda i, j: (0, 0)),
                  pl.BlockSpec((tm, 128), lambda i, j: (i % tiles_per_seq, 0)),
                  pl.BlockSpec((tm, 128), lambda i, j: (i % tiles_per_seq, 0))],
        out_specs=pl.BlockSpec((tm, heads * HEAD_PAD), lambda i, j: (i, j)),
        out_shape=jax.ShapeDtypeStruct((rows, N_HEADS * HEAD_PAD), BF),
        name="q_proj",
        compiler_params=_params("parallel", "parallel"),
    )(qn, w_uq, gq, cos_t, sin_t)


def _kv_kernel(kvn_ref, wk_ref, wv_ref, kr_ref, gn_ref, gr_ref, cos_ref, sin_ref, k_ref, v_ref,
               *, heads):
    kvn = kvn_ref[...]
    acck = _dot(kvn, wk_ref[...])
    v_ref[...] = _dot(kvn, wv_ref[...]).astype(v_ref.dtype)
    kr = kr_ref[...]
    kr_ss = jnp.sum(kr * kr, axis=-1, keepdims=True)
    kr_rot = _rope(kr * gr_ref[...], cos_ref[...], sin_ref[...])
    gn = gn_ref[...]
    for hd in range(heads):
        kn = acck[:, hd * QK_NOPE:(hd + 1) * QK_NOPE]
        ss = jnp.sum(kn * kn, axis=-1, keepdims=True) + kr_ss
        r = lax.rsqrt(ss * (1.0 / QK_HEAD) + EPS)
        k_ref[:, hd * HEAD_PAD:hd * HEAD_PAD + QK_NOPE] = (kn * r * gn).astype(k_ref.dtype)
        k_ref[:, hd * HEAD_PAD + QK_NOPE:(hd + 1) * HEAD_PAD] = (kr_rot * r).astype(k_ref.dtype)


def _kv_proj(kvn, w_uk, w_uv, kr, gn, gr, cos_t, sin_t, tm, heads, tiles_per_seq):
    rows = kvn.shape[0]
    kern = functools.partial(_kv_kernel, heads=heads)
    return pl.pallas_call(
        kern,
        grid=(rows // tm, N_HEADS // heads),
        in_specs=[pl.BlockSpec((tm, KV_LORA), lambda i, j: (i, 0)),
                  pl.BlockSpec((KV_LORA, heads * QK_NOPE), lambda i, j: (0, j)),
                  pl.BlockSpec((KV_LORA, heads * V_HEAD), lambda i, j: (0, j)),
                  pl.BlockSpec((tm, 128), lambda i, j: (i, 0)),
                  pl.BlockSpec((1, 128), lambda i, j: (0, 0)),
                  pl.BlockSpec((1, 128), lambda i, j: (0, 0)),
                  pl.BlockSpec((tm, 128), lambda i, j: (i % tiles_per_seq, 0)),
                  pl.BlockSpec((tm, 128), lambda i, j: (i % tiles_per_seq, 0))],
        out_specs=[pl.BlockSpec((tm, heads * HEAD_PAD), lambda i, j: (i, j)),
                   pl.BlockSpec((tm, heads * V_HEAD), lambda i, j: (i, j))],
        out_shape=[jax.ShapeDtypeStruct((rows, N_HEADS * HEAD_PAD), BF),
                   jax.ShapeDtypeStruct((rows, N_HEADS * V_HEAD), BF)],
        name="kv_proj",
        compiler_params=_params("parallel", "parallel"),
    )(kvn, w_uk, w_uv, kr, gn, gr, cos_t, sin_t)


def _attn_tile_kernel(q_ref, k_ref, v_ref, kp_ref, vp_ref, *rest, q_tile, tq, tc, rb, n_prefix):
    o_ref, s_refs, p_refs, a_refs, m_ref, l_ref, acc_ref = rest[-7:]
    depth = len(s_refs)
    d0 = q_tile * tq
    items = [("full", c * tc, 0) for c in range(d0 // tc)]
    items += [("diag", d0 + h * tc, h * tc) for h in range(tq // tc)]
    items += [("prefix", 0, 0)]

    def scores(item, s_ref):
        kind, k0, r0 = item
        q = q_ref[r0:, :]
        if kind == "prefix":
            s_ref[:, :128] = _dot_nt(q, kp_ref[...])
        else:
            s_ref[r0:, :] = _dot_nt(q, k_ref[k0:k0 + tc, :])

    def softmax(item, s_ref, p_ref, a_ref):
        kind, k0, r0 = item
        width = 128 if kind == "prefix" else tc
        for r in range(r0, tq, rb):
            rs = slice(r, r + rb)
            s = s_ref[rs, :width]
            col = lax.broadcasted_iota(jnp.int32, (rb, width), 1)
            if kind == "prefix":
                s = jnp.where(col < n_prefix, s, NEG_BIG)
            elif kind == "diag" and r < r0 + tc:
                row = lax.broadcasted_iota(jnp.int32, (rb, width), 0) + (r - r0)
                s = jnp.where(col <= row, s, NEG_BIG)
            m_old = m_ref[rs, :]
            m_new = jnp.maximum(m_old, jnp.max(s, axis=-1, keepdims=True))
            alpha = jnp.exp2(m_old - m_new)
            p = jnp.exp2(s - jnp.concatenate([m_new] * (width // 128), axis=1))
            l_ref[rs, :] = alpha * l_ref[rs, :] + jnp.sum(p, axis=-1, keepdims=True)
            m_ref[rs, :] = m_new
            a_ref[rs, :] = alpha
            p_ref[rs, :width] = p.astype(p_ref.dtype)

    def accumulate(item, p_ref, a_ref):
        kind, k0, r0 = item
        if kind == "prefix":
            pv = _dot(p_ref[:, :128], vp_ref[...])
        else:
            pv = _dot(p_ref[r0:, :], v_ref[k0:k0 + tc, :])
        acc_ref[r0:, :] = a_ref[r0:, :] * acc_ref[r0:, :] + pv

    m_ref[...] = jnp.full(m_ref.shape, NEG_BIG, F32)
    l_ref[...] = jnp.zeros(l_ref.shape, F32)
    acc_ref[...] = jnp.zeros(acc_ref.shape, F32)
    n = len(items)
    for t in range(n + 2):
        if t < n:
            scores(items[t], s_refs[t % depth])
        if 1 <= t <= n:
            softmax(items[t - 1], s_refs[(t - 1) % depth], p_refs[(t - 1) % depth],
                    a_refs[(t - 1) % depth])
        if t >= 2:
            accumulate(items[t - 2], p_refs[(t - 2) % depth], a_refs[(t - 2) % depth])
    o_ref[...] = (acc_ref[...] / l_ref[...]).astype(o_ref.dtype)


def _attention_tiles(q, k, v, kp, vp, batch, seq, tq):
    nq = seq // tq
    tc = tq // 2
    depth = 3
    out = None
    for i in range(nq):
        keys = (i + 1) * tq
        kern = functools.partial(_attn_tile_kernel, q_tile=i, tq=tq, tc=tc, rb=64,
                                 n_prefix=N_META)
        in_specs = [pl.BlockSpec((None, tq, HEAD_PAD), lambda b, h, i=i: (b, i, h)),
                    pl.BlockSpec((None, keys, HEAD_PAD), lambda b, h: (b, 0, h)),
                    pl.BlockSpec((None, keys, V_HEAD), lambda b, h: (b, 0, h)),
                    pl.BlockSpec((128, HEAD_PAD), lambda b, h: (0, h)),
                    pl.BlockSpec((128, V_HEAD), lambda b, h: (0, h))]
        args = [q, k, v, kp, vp]
        if out is not None:
            in_specs.append(pl.BlockSpec(memory_space=pl.ANY))
            args.append(out)
        out = pl.pallas_call(
            kern,
            grid=(batch, N_HEADS),
            in_specs=in_specs,
            out_specs=pl.BlockSpec((None, tq, V_HEAD), lambda b, h, i=i: (b, i, h)),
            out_shape=jax.ShapeDtypeStruct((batch, seq, N_HEADS * V_HEAD), BF),
            scratch_shapes=[[pltpu.VMEM((tq, tc), F32)] * depth,
                            [pltpu.VMEM((tq, tc), BF)] * depth,
                            [pltpu.VMEM((tq, 128), F32)] * depth,
                            pltpu.VMEM((tq, 128), F32), pltpu.VMEM((tq, 128), F32),
                            pltpu.VMEM((tq, V_HEAD), F32)],
            input_output_aliases={5: 0} if len(args) == 6 else {},
            name=f"attention_q{i}",
            compiler_params=_params("parallel", "parallel"),
        )(*args)
    return out


def _attn_meta_kernel(q_ref, k_ref, v_ref, o_ref):
    s = _dot_nt(q_ref[...], k_ref[...])
    row = lax.broadcasted_iota(jnp.int32, s.shape, 0)
    col = lax.broadcasted_iota(jnp.int32, s.shape, 1)
    s = jnp.where(row >= col, s, NEG_BIG)
    p = jnp.exp2(s - jnp.max(s, axis=-1, keepdims=True))
    l = jnp.sum(p, axis=-1, keepdims=True)
    o_ref[...] = (_dot(p.astype(BF), v_ref[...]) / l).astype(o_ref.dtype)


def _attention_meta(q, k, v):
    return pl.pallas_call(
        _attn_meta_kernel,
        grid=(N_HEADS,),
        in_specs=[pl.BlockSpec((N_META, HEAD_PAD), lambda h: (0, h)),
                  pl.BlockSpec((N_META, HEAD_PAD), lambda h: (0, h)),
                  pl.BlockSpec((N_META, V_HEAD), lambda h: (0, h))],
        out_specs=pl.BlockSpec((N_META, V_HEAD), lambda h: (0, h)),
        out_shape=jax.ShapeDtypeStruct((N_META, N_HEADS * V_HEAD), BF),
        name="attention_meta",
        compiler_params=_params("parallel"),
    )(q, k, v)


def _pool_branch_kernel(h_ref, w_ref, pw_ref, ps_ref, hist_ref, yc_ref, *rest,
                        tm, tiles_per_seq, pos_offset):
    carry_ref = rest[-1]
    i = pl.program_id(0)
    pin = _dot(h_ref[...], w_ref[...])

    @pl.when(i % tiles_per_seq == 0)
    def _():
        carry_ref[...] = hist_ref[...]

    ext = jnp.concatenate([carry_ref[...], pin], axis=0)
    tail = pin[tm - HIST:]
    carry_ref[...] = tail
    if len(rest) == 2:
        rest[0][...] = tail
    n_ext = tm + HIST
    seen = (lax.broadcasted_iota(jnp.int32, (tm, 1), 0)
            + ((i % tiles_per_seq) * tm + pos_offset + 1)).astype(F32)
    for g, w in enumerate(POOL_WINDOWS):
        xg = ext[:, g * POOL_GROUP:(g + 1) * POOL_GROUP]
        s, span = xg, 1
        while span < w:
            s = s[span:] + s[:s.shape[0] - span]
            span *= 2
        s = s[n_ext - (w - 1) - tm:]
        pooled = s / jnp.minimum(seen, float(w)) - xg[HIST:]
        mixed = _dot(pooled.astype(BF), pw_ref[g])
        yc_ref[:, g * POOL_GROUP:(g + 1) * POOL_GROUP] = (
            mixed * ps_ref[:, g * POOL_GROUP:(g + 1) * POOL_GROUP]).astype(yc_ref.dtype)


def _pool_branch(h, w_pool, pool_w, pool_scale, hist, tm, tiles_per_seq, pos_offset, emit_hist):
    rows = h.shape[0]
    assert not emit_hist or rows == tm
    kern = functools.partial(_pool_branch_kernel, tm=tm, tiles_per_seq=tiles_per_seq,
                             pos_offset=pos_offset)
    ng = len(POOL_WINDOWS)
    n_out = 2 if emit_hist else 1
    return pl.pallas_call(
        kern,
        grid=(rows // tm,),
        in_specs=[pl.BlockSpec((tm, D_MODEL), lambda i: (i, 0)),
                  pl.BlockSpec((D_MODEL, D_POOL), lambda i: (0, 0)),
                  pl.BlockSpec((ng, POOL_GROUP, POOL_GROUP), lambda i: (0, 0, 0)),
                  pl.BlockSpec((1, D_POOL), lambda i: (0, 0)),
                  pl.BlockSpec((HIST, D_POOL), lambda i: (0, 0))],
        out_specs=[pl.BlockSpec((tm, D_POOL), lambda i: (i, 0)),
                   pl.BlockSpec((HIST, D_POOL), lambda i: (0, 0))][:n_out],
        out_shape=[jax.ShapeDtypeStruct((rows, D_POOL), BF),
                   jax.ShapeDtypeStruct((HIST, D_POOL), F32)][:n_out],
        scratch_shapes=[pltpu.VMEM((HIST, D_POOL), F32)],
        name="pool_branch",
        compiler_params=_params("arbitrary"),
    )(h, w_pool, pool_w, pool_scale, hist)


def _merge_kernel(h_ref, ya_ref, yb_ref, yc_ref, wg0_ref, wg1_ref, wg2_ref, wa_ref, wb_ref, wc_ref,
                  o_ref):
    h = h_ref[...]
    m = jax.nn.sigmoid(_dot(h, wg0_ref[...])) * _dot(ya_ref[...], wa_ref[...])
    m += jax.nn.sigmoid(_dot(h, wg1_ref[...])) * _dot(yb_ref[...], wb_ref[...])
    m += jax.nn.sigmoid(_dot(h, wg2_ref[...])) * _dot(yc_ref[...], wc_ref[...])
    o_ref[...] = m.astype(o_ref.dtype)


def _merge(h, ya, yb, yc, w_gate, wa, wb, wc, tm, tn):
    rows = h.shape[0]
    ncb = D_MODEL // tn
    row = lambda width: pl.BlockSpec((tm, width), lambda i, j: (i, 0))
    col = lambda depth, off: pl.BlockSpec((depth, tn), lambda i, j: (0, j + off))
    return pl.pallas_call(
        _merge_kernel,
        grid=(rows // tm, ncb),
        in_specs=[row(D_MODEL), row(D_CONV), row(N_HEADS * V_HEAD), row(D_POOL),
                  col(D_MODEL, 0), col(D_MODEL, ncb), col(D_MODEL, 2 * ncb),
                  col(D_CONV, 0), col(N_HEADS * V_HEAD, 0), col(D_POOL, 0)],
        out_specs=pl.BlockSpec((tm, tn), lambda i, j: (i, j)),
        out_shape=jax.ShapeDtypeStruct((rows, D_MODEL), BF),
        name="merge",
        compiler_params=_params("parallel", "arbitrary"),
    )(h, ya, yb, yc, w_gate, w_gate, w_gate, wa, wb, wc)


def _oproj_kernel(m_ref, w_ref, x_ref, g_ref, xo_ref, h2_ref):
    x = x_ref[...] + _dot(m_ref[...], w_ref[...])
    xo_ref[...] = x
    ms = jnp.mean(x * x, axis=-1, keepdims=True)
    h2_ref[...] = (x * lax.rsqrt(ms + EPS) * g_ref[...]).astype(h2_ref.dtype)


def _oproj(merged, w_o, x, g, tm):
    rows = x.shape[0]
    return pl.pallas_call(
        _oproj_kernel,
        grid=(rows // tm,),
        in_specs=[pl.BlockSpec((tm, D_MODEL), lambda i: (i, 0)),
                  pl.BlockSpec((D_MODEL, D_MODEL), lambda i: (0, 0)),
                  pl.BlockSpec((tm, D_MODEL), lambda i: (i, 0)),
                  pl.BlockSpec((1, D_MODEL), lambda i: (0, 0))],
        out_specs=[pl.BlockSpec((tm, D_MODEL), lambda i: (i, 0)),
                   pl.BlockSpec((tm, D_MODEL), lambda i: (i, 0))],
        out_shape=[jax.ShapeDtypeStruct((rows, D_MODEL), F32),
                   jax.ShapeDtypeStruct((rows, D_MODEL), BF)],
        name="oproj",
        compiler_params=_params("parallel"),
    )(merged, w_o, x, g)


def _mlp_kernel(h2_ref, wu_ref, wd_ref, x_ref, o_ref):
    @pl.when(pl.program_id(1) == 0)
    def _():
        o_ref[...] = x_ref[...]

    a = jnp.maximum(_dot(h2_ref[...], wu_ref[...]), 0.0)
    o_ref[...] += _dot((a * a).astype(BF), wd_ref[...])


def _mlp(h2, w_up, w_down, x, tm, tf):
    rows = x.shape[0]
    return pl.pallas_call(
        _mlp_kernel,
        grid=(rows // tm, D_FF // tf),
        in_specs=[pl.BlockSpec((tm, D_MODEL), lambda i, k: (i, 0)),
                  pl.BlockSpec((D_MODEL, tf), lambda i, k: (0, k)),
                  pl.BlockSpec((tf, D_MODEL), lambda i, k: (k, 0)),
                  pl.BlockSpec((tm, D_MODEL), lambda i, k: (i, 0))],
        out_specs=pl.BlockSpec((tm, D_MODEL), lambda i, k: (i, 0)),
        out_shape=jax.ShapeDtypeStruct((rows, D_MODEL), F32),
        name="mlp",
        compiler_params=_params("parallel", "arbitrary"),
    )(h2, w_up, w_down, x)


def _spread_rope(a):
    z = jnp.zeros(a.shape[:-1] + (HALF_ROPE,), a.dtype)
    return jnp.concatenate([a[..., :HALF_ROPE], z, a[..., HALF_ROPE:], z], axis=-1)


def _pad_head(a):
    return jnp.concatenate([a[..., :QK_NOPE], _spread_rope(a[..., QK_NOPE:])], axis=-1)


def _layer_params(l, attn_norm, w_in, conv_w, q_lat_norm, kv_lat_norm, w_uq, w_ukv, q_norm, k_norm,
                  pool_w, pool_scale, w_branch_a, w_branch_b, w_branch_c, w_o, mlp_norm, w_up,
                  w_down):
    wi = w_in[l].astype(BF)
    o_q = 3 * D_CONV
    o_kv = o_q + Q_LORA
    o_kr = o_kv + KV_LORA
    o_pool = o_kr + QK_ROPE
    o_gate = o_pool + D_POOL
    w_ukv_h = w_ukv[l].astype(BF).reshape(KV_LORA, N_HEADS, QK_NOPE + V_HEAD)
    return dict(
        attn_norm=attn_norm[l][None],
        w_a=wi[:, :o_q],
        w_lat=jnp.concatenate([wi[:, o_q:o_kr], _spread_rope(wi[:, o_kr:o_pool])], axis=1),
        w_pool=wi[:, o_pool:o_gate],
        w_gate=wi[:, o_gate:],
        conv_w=conv_w[l],
        q_lat_norm=q_lat_norm[l][None],
        kv_lat_norm=kv_lat_norm[l][None],
        w_uq=_pad_head(w_uq[l].astype(BF).reshape(Q_LORA, N_HEADS, QK_HEAD)).reshape(
            Q_LORA, N_HEADS * HEAD_PAD),
        w_uk=w_ukv_h[:, :, :QK_NOPE].reshape(KV_LORA, N_HEADS * QK_NOPE),
        w_uv=w_ukv_h[:, :, QK_NOPE:].reshape(KV_LORA, N_HEADS * V_HEAD),
        gq=_pad_head(q_norm[l] * (QK_HEAD ** -0.5 * LOG2_E))[None],
        gk_nope=k_norm[l][None, :QK_NOPE],
        gk_rope=_spread_rope(k_norm[l][QK_NOPE:])[None],
        pool_w=pool_w[l].astype(BF),
        pool_scale=pool_scale[l][None],
        wa=w_branch_a[l].astype(BF),
        wb=w_branch_b[l].astype(BF),
        wc=w_branch_c[l].astype(BF),
        w_o=w_o[l].astype(BF),
        mlp_norm=mlp_norm[l][None],
        w_up=w_up[l].astype(BF),
        w_down=w_down[l].astype(BF),
    )


def _rope_tables(total):
    pos = jnp.arange(total, dtype=F32)
    inv = ROPE_THETA ** (-jnp.arange(0, QK_ROPE, 2, dtype=F32) / QK_ROPE)
    ang = pos[:, None] * inv[None, :]
    cos, sin = jnp.cos(ang), jnp.sin(ang)
    z = jnp.zeros_like(cos)
    return (jnp.concatenate([cos, z, cos, z], axis=1),
            jnp.concatenate([-sin, z, sin, z], axis=1))


def _layer(x, p, cos_t, sin_t, prefix, *, batch, seq, finish):
    is_meta = prefix is None
    tm = min(512, seq)
    tps = seq // tm
    tm_big = min(1024, seq)
    if is_meta:
        conv_hist = jnp.zeros((HIST, D_CONV), F32)
        pool_hist = jnp.zeros((HIST, D_POOL), F32)
    else:
        conv_hist, pool_hist, kp, vp = prefix

    h = _rmsnorm(x, p['attn_norm'], tm)
    ya, *conv_tail = _conv_branch(h, p['w_a'], p['conv_w'], conv_hist, tm_big, 512,
                                  seq // tm_big, is_meta)
    yc, *pool_tail = _pool_branch(h, p['w_pool'], p['pool_w'], p['pool_scale'], pool_hist, tm, tps,
                                  0 if is_meta else N_META, is_meta)
    qn, kvn, kr = _latent(h, p['w_lat'], p['q_lat_norm'], p['kv_lat_norm'], tm)
    k, v = _kv_proj(kvn, p['w_uk'], p['w_uv'], kr, p['gk_nope'], p['gk_rope'], cos_t, sin_t,
                    tm, 4, tps)
    out_prefix = (conv_tail[0], pool_tail[0], k, v) if is_meta else None
    if not finish:
        return None, out_prefix
    q = _q_proj(qn, p['w_uq'], p['gq'], cos_t, sin_t, tm, 4, tps)
    if is_meta:
        yb = _attention_meta(q, k, v)
    else:
        pad = ((0, 128 - N_META), (0, 0))
        per_batch = lambda a: a.reshape(batch, seq, a.shape[-1])
        yb = _attention_tiles(per_batch(q), per_batch(k), per_batch(v), jnp.pad(kp, pad),
                              jnp.pad(vp, pad), batch, seq, 512).reshape(batch * seq, -1)
    merged = _merge(h, ya, yb, yc, p['w_gate'], p['wa'], p['wb'], p['wc'], tm_big, 256)
    x_mid, h2 = _oproj(merged, p['w_o'], x, p['mlp_norm'], min(256, seq))
    return _mlp(h2, p['w_up'], p['w_down'], x_mid, tm_big, 512), out_prefix


def kernel(x, meta_tokens, attn_norm, w_in, conv_w, q_lat_norm, kv_lat_norm, w_uq, w_ukv, q_norm, k_norm, pool_w, pool_scale, w_branch_a, w_branch_b, w_branch_c, w_o, mlp_norm, w_up, w_down):
    batch, seq, d = x.shape
    depth = w_in.shape[0]
    cos_t, sin_t = _rope_tables(N_META + seq)
    xm = meta_tokens.astype(F32)
    xr = x.reshape(batch * seq, d)
    for l in range(depth):
        p = _layer_params(l, attn_norm, w_in, conv_w, q_lat_norm, kv_lat_norm, w_uq, w_ukv, q_norm,
                          k_norm, pool_w, pool_scale, w_branch_a, w_branch_b, w_branch_c, w_o,
                          mlp_norm, w_up, w_down)
        last = l == depth - 1
        xm, prefix = _layer(xm, p, cos_t[:N_META], sin_t[:N_META], None,
                            batch=1, seq=N_META, finish=not last)
        xr, _ = _layer(xr, p, cos_t[N_META:], sin_t[N_META:], prefix,
                       batch=batch, seq=seq, finish=True)
    return xr.reshape(batch, seq, d)
```

```python
import functools

import jax
import jax.numpy as jnp
from jax import lax
from jax.experimental import pallas as pl
from jax.experimental.pallas import tpu as pltpu

D_MODEL = 2048
N_META = 16
EPS = 1e-6
D_CONV = 1024
N_HEADS = 16
QK_NOPE = 128
QK_ROPE = 64
QK_HEAD = QK_NOPE + QK_ROPE
HALF_ROPE = QK_ROPE // 2
V_HEAD = 128
Q_LORA = 512
KV_LORA = 512
ROPE_THETA = 10000.0
D_POOL = 1024
POOL_WINDOWS = (2, 4, 8, 16)
POOL_GROUP = D_POOL // len(POOL_WINDOWS)
D_FF = 4 * D_MODEL
HIST = 16
HEAD_PAD = 256
HEAD_W = HEAD_PAD + 128
D_LAT = Q_LORA + KV_LORA + 256
VMEM_LIMIT = 56 * 2**20
NEG_BIG = -1e30
LOG2_E = 1.4426950408889634

BF = jnp.bfloat16
F32 = jnp.float32


def _dot(a, b):
    return jnp.dot(a, b, preferred_element_type=F32)


def _dot_nt(a, b):
    return lax.dot_general(a, b, (((1,), (1,)), ((), ())), preferred_element_type=F32)


def _params(*sem):
    return pltpu.CompilerParams(dimension_semantics=sem, vmem_limit_bytes=VMEM_LIMIT)


def _rmsnorm_kernel(x_ref, g_ref, o_ref):
    x = x_ref[...]
    ms = jnp.mean(x * x, axis=-1, keepdims=True)
    o_ref[...] = (x * lax.rsqrt(ms + EPS) * g_ref[...]).astype(o_ref.dtype)


def _rmsnorm(x, g, tm):
    rows, d = x.shape
    return pl.pallas_call(
        _rmsnorm_kernel,
        grid=(rows // tm,),
        in_specs=[pl.BlockSpec((tm, d), lambda i: (i, 0)),
                  pl.BlockSpec((1, d), lambda i: (0, 0))],
        out_specs=pl.BlockSpec((tm, d), lambda i: (i, 0)),
        out_shape=jax.ShapeDtypeStruct((rows, d), BF),
        name="rmsnorm",
        compiler_params=_params("parallel"),
    )(x, g)


def _conv_branch_kernel(h_ref, wu_ref, wb_ref, wc_ref, cw_ref, hist_ref, ya_ref, *rest,
                        tm, tiles_per_seq):
    carry_ref = rest[-1]
    i, j = pl.program_id(0), pl.program_id(1)
    h = h_ref[...]
    cu = _dot(h, wc_ref[...]) * _dot(h, wu_ref[...])

    @pl.when(i % tiles_per_seq == 0)
    def _():
        carry_ref[j] = hist_ref[...]

    ext = jnp.concatenate([carry_ref[j], cu], axis=0)
    tail = cu[tm - HIST:]
    carry_ref[j] = tail
    if len(rest) == 2:
        rest[0][...] = tail
    cw = cw_ref[...]
    y = (cw[2:3] * cu + cw[1:2] * ext[HIST - 1:HIST - 1 + tm] + cw[0:1] * ext[HIST - 2:HIST - 2 + tm])
    ya_ref[...] = (_dot(h, wb_ref[...]) * y).astype(ya_ref.dtype)


def _conv_branch(h, w_a, conv_w, hist, tm, tc, tiles_per_seq, emit_hist):
    rows = h.shape[0]
    ncb = D_CONV // tc
    assert not emit_hist or rows == tm
    kern = functools.partial(_conv_branch_kernel, tm=tm, tiles_per_seq=tiles_per_seq)
    n_out = 2 if emit_hist else 1
    return pl.pallas_call(
        kern,
        grid=(rows // tm, ncb),
        in_specs=[pl.BlockSpec((tm, D_MODEL), lambda i, j: (i, 0)),
                  pl.BlockSpec((D_MODEL, tc), lambda i, j: (0, j)),
                  pl.BlockSpec((D_MODEL, tc), lambda i, j: (0, j + ncb)),
                  pl.BlockSpec((D_MODEL, tc), lambda i, j: (0, j + 2 * ncb)),
                  pl.BlockSpec((3, tc), lambda i, j: (0, j)),
                  pl.BlockSpec((HIST, tc), lambda i, j: (0, j))],
        out_specs=[pl.BlockSpec((tm, tc), lambda i, j: (i, j)),
                   pl.BlockSpec((HIST, tc), lambda i, j: (0, j))][:n_out],
        out_shape=[jax.ShapeDtypeStruct((rows, D_CONV), BF),
                   jax.ShapeDtypeStruct((HIST, D_CONV), F32)][:n_out],
        scratch_shapes=[pltpu.VMEM((ncb, HIST, tc), F32)],
        name="conv_branch",
        compiler_params=_params("arbitrary", "arbitrary"),
    )(h, w_a, w_a, w_a, conv_w, hist)


def _latent_kernel(h_ref, w_ref, gq_ref, gkv_ref, qn_ref, kvn_ref, kr_ref):
    acc = _dot(h_ref[...], w_ref[...])

    def norm(a, g):
        ms = jnp.mean(a * a, axis=-1, keepdims=True)
        return a * lax.rsqrt(ms + EPS) * g

    qn_ref[...] = norm(acc[:, :Q_LORA], gq_ref[...]).astype(qn_ref.dtype)
    kvn_ref[...] = norm(acc[:, Q_LORA:Q_LORA + KV_LORA], gkv_ref[...]).astype(kvn_ref.dtype)
    kr_ref[...] = acc[:, Q_LORA + KV_LORA:]


def _latent(h, w_lat, gq, gkv, tm):
    rows = h.shape[0]
    return pl.pallas_call(
        _latent_kernel,
        grid=(rows // tm,),
        in_specs=[pl.BlockSpec((tm, D_MODEL), lambda i: (i, 0)),
                  pl.BlockSpec((D_MODEL, D_LAT), lambda i: (0, 0)),
                  pl.BlockSpec((1, Q_LORA), lambda i: (0, 0)),
                  pl.BlockSpec((1, KV_LORA), lambda i: (0, 0))],
        out_specs=[pl.BlockSpec((tm, Q_LORA), lambda i: (i, 0)),
                   pl.BlockSpec((tm, KV_LORA), lambda i: (i, 0)),
                   pl.BlockSpec((tm, 256), lambda i: (i, 0))],
        out_shape=[jax.ShapeDtypeStruct((rows, Q_LORA), BF),
                   jax.ShapeDtypeStruct((rows, KV_LORA), BF),
                   jax.ShapeDtypeStruct((rows, 256), F32)],
        name="latent",
        compiler_params=_params("parallel"),
    )(h, w_lat, gq, gkv)


def _row_sumsq(a):
    return _dot((a * a).astype(BF), jnp.ones((a.shape[1], 128), BF))


def _q_kernel(qn_ref, w_ref, g_ref, cos_ref, sin_ref, q_ref, acc_ref, ss_ref, *, heads, rb):
    acc_ref[...] = _dot(qn_ref[...], w_ref[...])
    for hd in range(heads):
        ss_ref[:, hd * 128:(hd + 1) * 128] = _row_sumsq(
            acc_ref[:, hd * HEAD_W:hd * HEAD_W + HEAD_PAD])
    g = g_ref[...]
    for r in range(0, acc_ref.shape[0], rb):
        rs = slice(r, r + rb)
        cos_g, sin_g = cos_ref[rs, :], sin_ref[rs, :]
        for hd in range(heads):
            c0 = hd * HEAD_W
            inv = lax.rsqrt(ss_ref[rs, hd * 128:(hd + 1) * 128] * (1.0 / QK_HEAD) + EPS)
            q_ref[rs, hd * HEAD_PAD:hd * HEAD_PAD + QK_NOPE] = (
                acc_ref[rs, c0:c0 + QK_NOPE] * (inv * g)).astype(q_ref.dtype)
            rot = (acc_ref[rs, c0 + QK_NOPE:c0 + HEAD_PAD] * cos_g
                   + acc_ref[rs, c0 + HEAD_PAD:c0 + HEAD_W] * sin_g)
            q_ref[rs, hd * HEAD_PAD + QK_NOPE:(hd + 1) * HEAD_PAD] = (rot * inv).astype(q_ref.dtype)


def _q_proj(qn, w_uq, g_nope, cos_g, sin_g, tm, heads, tiles_per_seq):
    rows = qn.shape[0]
    kern = functools.partial(_q_kernel, heads=heads, rb=min(64, tm))
    return pl.pallas_call(
        kern,
        grid=(rows // tm, N_HEADS // heads),
        in_specs=[pl.BlockSpec((tm, Q_LORA), lambda i, j: (i, 0)),
                  pl.BlockSpec((Q_LORA, heads * HEAD_W), lambda i, j: (0, j)),
                  pl.BlockSpec((1, QK_NOPE), lambda i, j: (0, 0)),
                  pl.BlockSpec((tm, 128), lambda i, j: (i % tiles_per_seq, 0)),
                  pl.BlockSpec((tm, 128), lambda i, j: (i % tiles_per_seq, 0))],
        out_specs=pl.BlockSpec((tm, heads * HEAD_PAD), lambda i, j: (i, j)),
        out_shape=jax.ShapeDtypeStruct((rows, N_HEADS * HEAD_PAD), BF),
        scratch_shapes=[pltpu.VMEM((tm, heads * HEAD_W), F32),
                        pltpu.VMEM((tm, heads * 128), F32)],
        name="q_proj",
        compiler_params=_params("parallel", "parallel"),
    )(qn, w_uq, g_nope, cos_g, sin_g)


def _kv_kernel(kvn_ref, wk_ref, wv_ref, kr_ref, gn_ref, cos_ref, sin_ref, k_ref, v_ref,
               acc_ref, ss_ref, *, heads, rb):
    kvn = kvn_ref[...]
    acc_ref[...] = _dot(kvn, wk_ref[...])
    v_ref[...] = _dot(kvn, wv_ref[...]).astype(v_ref.dtype)
    blk = (lax.broadcasted_iota(jnp.int32, (256, 256), 0) // 128
           == lax.broadcasted_iota(jnp.int32, (256, 256), 1) // 128)
    pair_ones = jnp.where(blk, 1.0, 0.0).astype(BF)
    for hd in range(0, heads, 2):
        a = acc_ref[:, hd * QK_NOPE:(hd + 2) * QK_NOPE]
        ss_ref[:, hd * 128:(hd + 2) * 128] = _dot((a * a).astype(BF), pair_ones)
    ss_ref[:, heads * 128:] = _row_sumsq(kr_ref[:, :128])
    gn = gn_ref[...]
    for r in range(0, acc_ref.shape[0], rb):
        rs = slice(r, r + rb)
        kr_rot = kr_ref[rs, :128] * cos_ref[rs, :] + kr_ref[rs, 128:] * sin_ref[rs, :]
        kr_ss = ss_ref[rs, heads * 128:]
        for hd in range(heads):
            ss = ss_ref[rs, hd * 128:(hd + 1) * 128] + kr_ss
            inv = lax.rsqrt(ss * (1.0 / QK_HEAD) + EPS)
            k_ref[rs, hd * HEAD_PAD:hd * HEAD_PAD + QK_NOPE] = (
                acc_ref[rs, hd * QK_NOPE:(hd + 1) * QK_NOPE] * (inv * gn)).astype(k_ref.dtype)
            k_ref[rs, hd * HEAD_PAD + QK_NOPE:(hd + 1) * HEAD_PAD] = (kr_rot * inv).astype(
                k_ref.dtype)


def _kv_proj(kvn, w_uk, w_uv, kr, gn, cos_t, sin_t, tm, heads, tiles_per_seq):
    rows = kvn.shape[0]
    kern = functools.partial(_kv_kernel, heads=heads, rb=min(64, tm))
    return pl.pallas_call(
        kern,
        grid=(rows // tm, N_HEADS // heads),
        in_specs=[pl.BlockSpec((tm, KV_LORA), lambda i, j: (i, 0)),
                  pl.BlockSpec((KV_LORA, heads * QK_NOPE), lambda i, j: (0, j)),
                  pl.BlockSpec((KV_LORA, heads * V_HEAD), lambda i, j: (0, j)),
                  pl.BlockSpec((tm, 256), lambda i, j: (i, 0)),
                  pl.BlockSpec((1, 128), lambda i, j: (0, 0)),
                  pl.BlockSpec((tm, 128), lambda i, j: (i % tiles_per_seq, 0)),
                  pl.BlockSpec((tm, 128), lambda i, j: (i % tiles_per_seq, 0))],
        out_specs=[pl.BlockSpec((tm, heads * HEAD_PAD), lambda i, j: (i, j)),
                   pl.BlockSpec((tm, heads * V_HEAD), lambda i, j: (i, j))],
        out_shape=[jax.ShapeDtypeStruct((rows, N_HEADS * HEAD_PAD), BF),
                   jax.ShapeDtypeStruct((rows, N_HEADS * V_HEAD), BF)],
        scratch_shapes=[pltpu.VMEM((tm, heads * QK_NOPE), F32),
                        pltpu.VMEM((tm, (heads + 1) * 128), F32)],
        name="kv_proj",
        compiler_params=_params("parallel", "parallel"),
    )(kvn, w_uk, w_uv, kr, gn, cos_t, sin_t)


def _attn_tile_kernel(q_ref, k_ref, v_ref, kp_ref, vp_ref, *rest, q_tile, tq, tc, rb, n_prefix):
    o_ref, s_refs, p_refs, a_refs, m_ref, l_ref, acc_ref = rest[-7:]
    depth = len(s_refs)
    d0 = q_tile * tq
    items = [("full", c * tc, 0) for c in range(d0 // tc)]
    items += [("diag", d0 + h * tc, h * tc) for h in range(tq // tc)]
    items += [("prefix", 0, 0)]

    def scores(item, s_ref):
        kind, k0, r0 = item
        q = q_ref[r0:, :]
        if kind == "prefix":
            s_ref[:, :128] = _dot_nt(q, kp_ref[...])
        else:
            s_ref[r0:, :] = _dot_nt(q, k_ref[k0:k0 + tc, :])

    def softmax(item, s_ref, p_ref, a_ref):
        kind, k0, r0 = item
        width = 128 if kind == "prefix" else tc
        for r in range(r0, tq, rb):
            rs = slice(r, r + rb)
            s = s_ref[rs, :width]
            col = lax.broadcasted_iota(jnp.int32, (rb, width), 1)
            if kind == "prefix":
                s = jnp.where(col < n_prefix, s, NEG_BIG)
            elif kind == "diag" and r < r0 + tc:
                row = lax.broadcasted_iota(jnp.int32, (rb, width), 0) + (r - r0)
                s = jnp.where(col <= row, s, NEG_BIG)
            m_old = m_ref[rs, :]
            m_new = jnp.maximum(m_old, jnp.max(s, axis=-1, keepdims=True))
            alpha = jnp.exp2(m_old - m_new)
            p = jnp.exp2(s - jnp.concatenate([m_new] * (width // 128), axis=1))
            l_ref[rs, :] = alpha * l_ref[rs, :] + jnp.sum(p, axis=-1, keepdims=True)
            m_ref[rs, :] = m_new
            a_ref[rs, :] = alpha
            p_ref[rs, :width] = p.astype(p_ref.dtype)

    def accumulate(item, p_ref, a_ref):
        kind, k0, r0 = item
        if kind == "prefix":
            pv = _dot(p_ref[:, :128], vp_ref[...])
        else:
            pv = _dot(p_ref[r0:, :], v_ref[k0:k0 + tc, :])
        acc_ref[r0:, :] = a_ref[r0:, :] * acc_ref[r0:, :] + pv

    m_ref[...] = jnp.full(m_ref.shape, NEG_BIG, F32)
    l_ref[...] = jnp.zeros(l_ref.shape, F32)
    acc_ref[...] = jnp.zeros(acc_ref.shape, F32)
    n = len(items)
    for t in range(n + 2):
        if t < n:
            scores(items[t], s_refs[t % depth])
        if 1 <= t <= n:
            softmax(items[t - 1], s_refs[(t - 1) % depth], p_refs[(t - 1) % depth],
                    a_refs[(t - 1) % depth])
        if t >= 2:
            accumulate(items[t - 2], p_refs[(t - 2) % depth], a_refs[(t - 2) % depth])
    o_ref[...] = (acc_ref[...] / l_ref[...]).astype(o_ref.dtype)


def _attention_tiles(q, k, v, kp, vp, batch, seq, tq):
    nq = seq // tq
    tc = tq // 2
    depth = 3
    out = None
    for i in range(nq):
        keys = (i + 1) * tq
        kern = functools.partial(_attn_tile_kernel, q_tile=i, tq=tq, tc=tc, rb=64,
                                 n_prefix=N_META)
        in_specs = [pl.BlockSpec((None, tq, HEAD_PAD), lambda b, h, i=i: (b, i, h)),
                    pl.BlockSpec((None, keys, HEAD_PAD), lambda b, h: (b, 0, h)),
                    pl.BlockSpec((None, keys, V_HEAD), lambda b, h: (b, 0, h)),
                    pl.BlockSpec((128, HEAD_PAD), lambda b, h: (0, h)),
                    pl.BlockSpec((128, V_HEAD), lambda b, h: (0, h))]
        args = [q, k, v, kp, vp]
        if out is not None:
            in_specs.append(pl.BlockSpec(memory_space=pl.ANY))
            args.append(out)
        out = pl.pallas_call(
            kern,
            grid=(batch, N_HEADS),
            in_specs=in_specs,
            out_specs=pl.BlockSpec((None, tq, V_HEAD), lambda b, h, i=i: (b, i, h)),
            out_shape=jax.ShapeDtypeStruct((batch, seq, N_HEADS * V_HEAD), BF),
            scratch_shapes=[[pltpu.VMEM((tq, tc), F32)] * depth,
                            [pltpu.VMEM((tq, tc), BF)] * depth,
                            [pltpu.VMEM((tq, 128), F32)] * depth,
                            pltpu.VMEM((tq, 128), F32), pltpu.VMEM((tq, 128), F32),
                            pltpu.VMEM((tq, V_HEAD), F32)],
            input_output_aliases={5: 0} if len(args) == 6 else {},
            name=f"attention_q{i}",
            compiler_params=_params("parallel", "parallel"),
        )(*args)
    return out


def _attn_meta_kernel(q_ref, k_ref, v_ref, o_ref):
    s = _dot_nt(q_ref[...], k_ref[...])
    row = lax.broadcasted_iota(jnp.int32, s.shape, 0)
    col = lax.broadcasted_iota(jnp.int32, s.shape, 1)
    s = jnp.where(row >= col, s, NEG_BIG)
    p = jnp.exp2(s - jnp.max(s, axis=-1, keepdims=True))
    l = jnp.sum(p, axis=-1, keepdims=True)
    o_ref[...] = (_dot(p.astype(BF), v_ref[...]) / l).astype(o_ref.dtype)


def _attention_meta(q, k, v):
    return pl.pallas_call(
        _attn_meta_kernel,
        grid=(N_HEADS,),
        in_specs=[pl.BlockSpec((N_META, HEAD_PAD), lambda h: (0, h)),
                  pl.BlockSpec((N_META, HEAD_PAD), lambda h: (0, h)),
                  pl.BlockSpec((N_META, V_HEAD), lambda h: (0, h))],
        out_specs=pl.BlockSpec((N_META, V_HEAD), lambda h: (0, h)),
        out_shape=jax.ShapeDtypeStruct((N_META, N_HEADS * V_HEAD), BF),
        name="attention_meta",
        compiler_params=_params("parallel"),
    )(q, k, v)


def _pool_branch_kernel(h_ref, w_ref, pw_ref, ps_ref, hist_ref, yc_ref, *rest,
                        tm, tiles_per_seq, pos_offset):
    carry_ref = rest[-1]
    i = pl.program_id(0)
    pin = _dot(h_ref[...], w_ref[...])

    @pl.when(i % tiles_per_seq == 0)
    def _():
        carry_ref[...] = hist_ref[...]

    ext = jnp.concatenate([carry_ref[...], pin], axis=0)
    tail = pin[tm - HIST:]
    carry_ref[...] = tail
    if len(rest) == 2:
        rest[0][...] = tail
    n_ext = tm + HIST
    seen = (lax.broadcasted_iota(jnp.int32, (tm, 1), 0)
            + ((i % tiles_per_seq) * tm + pos_offset + 1)).astype(F32)
    for g, w in enumerate(POOL_WINDOWS):
        xg = ext[:, g * POOL_GROUP:(g + 1) * POOL_GROUP]
        s, span = xg, 1
        while span < w:
            s = s[span:] + s[:s.shape[0] - span]
            span *= 2
        s = s[n_ext - (w - 1) - tm:]
        pooled = s / jnp.minimum(seen, float(w)) - xg[HIST:]
        mixed = _dot(pooled.astype(BF), pw_ref[g])
        yc_ref[:, g * POOL_GROUP:(g + 1) * POOL_GROUP] = (
            mixed * ps_ref[:, g * POOL_GROUP:(g + 1) * POOL_GROUP]).astype(yc_ref.dtype)


def _pool_branch(h, w_pool, pool_w, pool_scale, hist, tm, tiles_per_seq, pos_offset, emit_hist):
    rows = h.shape[0]
    assert not emit_hist or rows == tm
    kern = functools.partial(_pool_branch_kernel, tm=tm, tiles_per_seq=tiles_per_seq,
                             pos_offset=pos_offset)
    ng = len(POOL_WINDOWS)
    n_out = 2 if emit_hist else 1
    return pl.pallas_call(
        kern,
        grid=(rows // tm,),
        in_specs=[pl.BlockSpec((tm, D_MODEL), lambda i: (i, 0)),
                  pl.BlockSpec((D_MODEL, D_POOL), lambda i: (0, 0)),
                  pl.BlockSpec((ng, POOL_GROUP, POOL_GROUP), lambda i: (0, 0, 0)),
                  pl.BlockSpec((1, D_POOL), lambda i: (0, 0)),
                  pl.BlockSpec((HIST, D_POOL), lambda i: (0, 0))],
        out_specs=[pl.BlockSpec((tm, D_POOL), lambda i: (i, 0)),
                   pl.BlockSpec((HIST, D_POOL), lambda i: (0, 0))][:n_out],
        out_shape=[jax.ShapeDtypeStruct((rows, D_POOL), BF),
                   jax.ShapeDtypeStruct((HIST, D_POOL), F32)][:n_out],
        scratch_shapes=[pltpu.VMEM((HIST, D_POOL), F32)],
        name="pool_branch",
        compiler_params=_params("arbitrary"),
    )(h, w_pool, pool_w, pool_scale, hist)


def _merge_kernel(h_ref, ya_ref, yb_ref, yc_ref, wg0_ref, wg1_ref, wg2_ref, wa_ref, wb_ref, wc_ref,
                  o_ref):
    h = h_ref[...]
    m = jax.nn.sigmoid(_dot(h, wg0_ref[...])) * _dot(ya_ref[...], wa_ref[...])
    m += jax.nn.sigmoid(_dot(h, wg1_ref[...])) * _dot(yb_ref[...], wb_ref[...])
    m += jax.nn.sigmoid(_dot(h, wg2_ref[...])) * _dot(yc_ref[...], wc_ref[...])
    o_ref[...] = m.astype(o_ref.dtype)


def _merge(h, ya, yb, yc, w_gate, wa, wb, wc, tm, tn):
    rows = h.shape[0]
    ncb = D_MODEL // tn
    row = lambda width: pl.BlockSpec((tm, width), lambda i, j: (i, 0))
    col = lambda depth, off: pl.BlockSpec((depth, tn), lambda i, j: (0, j + off))
    return pl.pallas_call(
        _merge_kernel,
        grid=(rows // tm, ncb),
        in_specs=[row(D_MODEL), row(D_CONV), row(N_HEADS * V_HEAD), row(D_POOL),
                  col(D_MODEL, 0), col(D_MODEL, ncb), col(D_MODEL, 2 * ncb),
                  col(D_CONV, 0), col(N_HEADS * V_HEAD, 0), col(D_POOL, 0)],
        out_specs=pl.BlockSpec((tm, tn), lambda i, j: (i, j)),
        out_shape=jax.ShapeDtypeStruct((rows, D_MODEL), BF),
        name="merge",
        compiler_params=_params("parallel", "arbitrary"),
    )(h, ya, yb, yc, w_gate, w_gate, w_gate, wa, wb, wc)


def _oproj_kernel(m_ref, w_ref, x_ref, g_ref, xo_ref, h2_ref):
    x = x_ref[...] + _dot(m_ref[...], w_ref[...])
    xo_ref[...] = x
    ms = jnp.mean(x * x, axis=-1, keepdims=True)
    h2_ref[...] = (x * lax.rsqrt(ms + EPS) * g_ref[...]).astype(h2_ref.dtype)


def _oproj(merged, w_o, x, g, tm):
    rows = x.shape[0]
    return pl.pallas_call(
        _oproj_kernel,
        grid=(rows // tm,),
        in_specs=[pl.BlockSpec((tm, D_MODEL), lambda i: (i, 0)),
                  pl.BlockSpec((D_MODEL, D_MODEL), lambda i: (0, 0)),
                  pl.BlockSpec((tm, D_MODEL), lambda i: (i, 0)),
                  pl.BlockSpec((1, D_MODEL), lambda i: (0, 0))],
        out_specs=[pl.BlockSpec((tm, D_MODEL), lambda i: (i, 0)),
                   pl.BlockSpec((tm, D_MODEL), lambda i: (i, 0))],
        out_shape=[jax.ShapeDtypeStruct((rows, D_MODEL), F32),
                   jax.ShapeDtypeStruct((rows, D_MODEL), BF)],
        name="oproj",
        compiler_params=_params("parallel"),
    )(merged, w_o, x, g)


def _mlp_kernel(h2_ref, wu_ref, wd_ref, x_ref, o_ref):
    @pl.when(pl.program_id(1) == 0)
    def _():
        o_ref[...] = x_ref[...]

    a = jnp.maximum(_dot(h2_ref[...], wu_ref[...]), 0.0)
    o_ref[...] += _dot((a * a).astype(BF), wd_ref[...])


def _mlp(h2, w_up, w_down, x, tm, tf):
    rows = x.shape[0]
    return pl.pallas_call(
        _mlp_kernel,
        grid=(rows // tm, D_FF // tf),
        in_specs=[pl.BlockSpec((tm, D_MODEL), lambda i, k: (i, 0)),
                  pl.BlockSpec((D_MODEL, tf), lambda i, k: (0, k)),
                  pl.BlockSpec((tf, D_MODEL), lambda i, k: (k, 0)),
                  pl.BlockSpec((tm, D_MODEL), lambda i, k: (i, 0))],
        out_specs=pl.BlockSpec((tm, D_MODEL), lambda i, k: (i, 0)),
        out_shape=jax.ShapeDtypeStruct((rows, D_MODEL), F32),
        name="mlp",
        compiler_params=_params("parallel", "arbitrary"),
    )(h2, w_up, w_down, x)


def _spread_rope(a):
    z = jnp.zeros(a.shape[:-1] + (HALF_ROPE,), a.dtype)
    return jnp.concatenate([a[..., :HALF_ROPE], z, a[..., HALF_ROPE:], z], axis=-1)


def _swap_halves(a):
    return jnp.concatenate([a[..., HALF_ROPE:], a[..., :HALF_ROPE]], axis=-1)


def _widen_head(a):
    rope = a[..., QK_NOPE:]
    return jnp.concatenate([a[..., :QK_NOPE], _spread_rope(rope), _spread_rope(_swap_halves(rope))],
                           axis=-1)


def _rope_gain_tables(gain, cos_t, sin_t):
    rope_gain = gain[QK_NOPE:]
    return (cos_t * _spread_rope(rope_gain)[None], sin_t * _spread_rope(_swap_halves(rope_gain))[None])


def _layer_params(l, cos_t, sin_t, attn_norm, w_in, conv_w, q_lat_norm, kv_lat_norm, w_uq, w_ukv,
                  q_norm, k_norm, pool_w, pool_scale, w_branch_a, w_branch_b, w_branch_c, w_o,
                  mlp_norm, w_up, w_down):
    cut = lambda a, b: w_in[l, :, a:b].astype(BF)
    gq = q_norm[l] * (QK_HEAD ** -0.5 * LOG2_E)
    q_cos, q_sin = _rope_gain_tables(gq, cos_t, sin_t)
    k_cos, k_sin = _rope_gain_tables(k_norm[l], cos_t, sin_t)
    o_q = 3 * D_CONV
    o_kv = o_q + Q_LORA
    o_kr = o_kv + KV_LORA
    o_pool = o_kr + QK_ROPE
    o_gate = o_pool + D_POOL
    w_kr = cut(o_kr, o_pool)
    w_ukv_h = w_ukv[l].astype(BF).reshape(KV_LORA, N_HEADS, QK_NOPE + V_HEAD)
    return dict(
        attn_norm=attn_norm[l][None],
        w_a=cut(0, o_q),
        w_lat=jnp.concatenate([cut(o_q, o_kr), _spread_rope(w_kr),
                               _spread_rope(_swap_halves(w_kr))], axis=1),
        w_pool=cut(o_pool, o_gate),
        w_gate=cut(o_gate, w_in.shape[-1]),
        conv_w=conv_w[l],
        q_lat_norm=q_lat_norm[l][None],
        kv_lat_norm=kv_lat_norm[l][None],
        w_uq=_widen_head(w_uq[l].astype(BF).reshape(Q_LORA, N_HEADS, QK_HEAD)).reshape(
            Q_LORA, N_HEADS * HEAD_W),
        w_uk=w_ukv_h[:, :, :QK_NOPE].reshape(KV_LORA, N_HEADS * QK_NOPE),
        w_uv=w_ukv_h[:, :, QK_NOPE:].reshape(KV_LORA, N_HEADS * V_HEAD),
        gq_nope=gq[None, :QK_NOPE],
        gk_nope=k_norm[l][None, :QK_NOPE],
        q_rope=(q_cos, q_sin),
        k_rope=(k_cos, k_sin),
        pool_w=pool_w[l].astype(BF),
        pool_scale=pool_scale[l][None],
        wa=w_branch_a[l].astype(BF),
        wb=w_branch_b[l].astype(BF),
        wc=w_branch_c[l].astype(BF),
        w_o=w_o[l].astype(BF),
        mlp_norm=mlp_norm[l][None],
        w_up=w_up[l].astype(BF),
        w_down=w_down[l].astype(BF),
    )


def _rope_tables(total):
    pos = jnp.arange(total, dtype=F32)
    inv = ROPE_THETA ** (-jnp.arange(0, QK_ROPE, 2, dtype=F32) / QK_ROPE)
    ang = pos[:, None] * inv[None, :]
    cos, sin = jnp.cos(ang), jnp.sin(ang)
    z = jnp.zeros_like(cos)
    return (jnp.concatenate([cos, z, cos, z], axis=1),
            jnp.concatenate([-sin, z, sin, z], axis=1))


def _layer(x, p, positions, prefix, *, batch, seq, finish):
    is_meta = prefix is None
    q_cos, q_sin = (t[positions] for t in p['q_rope'])
    k_cos, k_sin = (t[positions] for t in p['k_rope'])
    tm = min(512, seq)
    tps = seq // tm
    tm_big = min(1024, seq)
    if is_meta:
        conv_hist = jnp.zeros((HIST, D_CONV), F32)
        pool_hist = jnp.zeros((HIST, D_POOL), F32)
    else:
        conv_hist, pool_hist, kp, vp = prefix

    h = _rmsnorm(x, p['attn_norm'], tm)
    ya, *conv_tail = _conv_branch(h, p['w_a'], p['conv_w'], conv_hist, tm_big, 512,
                                  seq // tm_big, is_meta)
    yc, *pool_tail = _pool_branch(h, p['w_pool'], p['pool_w'], p['pool_scale'], pool_hist, tm, tps,
                                  0 if is_meta else N_META, is_meta)
    qn, kvn, kr = _latent(h, p['w_lat'], p['q_lat_norm'], p['kv_lat_norm'], tm)
    k, v = _kv_proj(kvn, p['w_uk'], p['w_uv'], kr, p['gk_nope'], k_cos, k_sin, tm, 4, tps)
    out_prefix = (conv_tail[0], pool_tail[0], k, v) if is_meta else None
    if not finish:
        return None, out_prefix
    q = _q_proj(qn, p['w_uq'], p['gq_nope'], q_cos, q_sin, tm, 4, tps)
    if is_meta:
        yb = _attention_meta(q, k, v)
    else:
        pad = ((0, 128 - N_META), (0, 0))
        per_batch = lambda a: a.reshape(batch, seq, a.shape[-1])
        yb = _attention_tiles(per_batch(q), per_batch(k), per_batch(v), jnp.pad(kp, pad),
                              jnp.pad(vp, pad), batch, seq, 512).reshape(batch * seq, -1)
    merged = _merge(h, ya, yb, yc, p['w_gate'], p['wa'], p['wb'], p['wc'], tm_big, 256)
    x_mid, h2 = _oproj(merged, p['w_o'], x, p['mlp_norm'], min(256, seq))
    return _mlp(h2, p['w_up'], p['w_down'], x_mid, tm_big, 512), out_prefix


def kernel(x, meta_tokens, attn_norm, w_in, conv_w, q_lat_norm, kv_lat_norm, w_uq, w_ukv, q_norm, k_norm, pool_w, pool_scale, w_branch_a, w_branch_b, w_branch_c, w_o, mlp_norm, w_up, w_down):
    batch, seq, d = x.shape
    depth = w_in.shape[0]
    cos_t, sin_t = _rope_tables(N_META + seq)
    xm = meta_tokens.astype(F32)
    xr = x.reshape(batch * seq, d)
    for l in range(depth):
        p = _layer_params(l, cos_t, sin_t, attn_norm, w_in, conv_w, q_lat_norm, kv_lat_norm, w_uq,
                          w_ukv, q_norm, k_norm, pool_w, pool_scale, w_branch_a, w_branch_b,
                          w_branch_c, w_o, mlp_norm, w_up, w_down)
        last = l == depth - 1
        xm, prefix = _layer(xm, p, slice(0, N_META), None, batch=1, seq=N_META, finish=not last)
        xr, _ = _layer(xr, p, slice(N_META, None), prefix, batch=batch, seq=seq, finish=True)
    return xr.reshape(batch, seq, d)
```

```python
import functools

import jax
import jax.numpy as jnp
from jax import lax
from jax.experimental import pallas as pl
from jax.experimental.pallas import tpu as pltpu

D_MODEL = 2048
N_META = 16
EPS = 1e-6
D_CONV = 1024
N_HEADS = 16
QK_NOPE = 128
QK_ROPE = 64
QK_HEAD = QK_NOPE + QK_ROPE
HALF_ROPE = QK_ROPE // 2
V_HEAD = 128
Q_LORA = 512
KV_LORA = 512
ROPE_THETA = 10000.0
D_POOL = 1024
POOL_WINDOWS = (2, 4, 8, 16)
POOL_GROUP = D_POOL // len(POOL_WINDOWS)
D_FF = 4 * D_MODEL
HIST = 16
HEAD_PAD = 256
HEAD_W = HEAD_PAD + 128
D_LAT = Q_LORA + KV_LORA + 256
VMEM_LIMIT = 56 * 2**20
NEG_BIG = -1e30
LOG2_E = 1.4426950408889634

BF = jnp.bfloat16
F32 = jnp.float32


def _dot(a, b):
    return jnp.dot(a, b, preferred_element_type=F32)


def _dot_nt(a, b):
    return lax.dot_general(a, b, (((1,), (1,)), ((), ())), preferred_element_type=F32)


def _params(*sem):
    return pltpu.CompilerParams(dimension_semantics=sem, vmem_limit_bytes=VMEM_LIMIT)


def _rmsnorm_kernel(x_ref, g_ref, o_ref):
    x = x_ref[...]
    ms = jnp.mean(x * x, axis=-1, keepdims=True)
    o_ref[...] = (x * lax.rsqrt(ms + EPS) * g_ref[...]).astype(o_ref.dtype)


def _rmsnorm(x, g, tm):
    rows, d = x.shape
    return pl.pallas_call(
        _rmsnorm_kernel,
        grid=(rows // tm,),
        in_specs=[pl.BlockSpec((tm, d), lambda i: (i, 0)),
                  pl.BlockSpec((1, d), lambda i: (0, 0))],
        out_specs=pl.BlockSpec((tm, d), lambda i: (i, 0)),
        out_shape=jax.ShapeDtypeStruct((rows, d), BF),
        name="rmsnorm",
        compiler_params=_params("parallel"),
    )(x, g)


def _conv_branch_kernel(h_ref, wu_ref, wb_ref, wc_ref, cw_ref, hist_ref, ya_ref, *rest,
                        tm, tiles_per_seq):
    carry_ref = rest[-1]
    i, j = pl.program_id(0), pl.program_id(1)
    h = h_ref[...]
    cu = _dot(h, wc_ref[...]) * _dot(h, wu_ref[...])

    @pl.when(i % tiles_per_seq == 0)
    def _():
        carry_ref[j] = hist_ref[...]

    ext = jnp.concatenate([carry_ref[j], cu], axis=0)
    tail = cu[tm - HIST:]
    carry_ref[j] = tail
    if len(rest) == 2:
        rest[0][...] = tail
    cw = cw_ref[...]
    y = (cw[2:3] * cu + cw[1:2] * ext[HIST - 1:HIST - 1 + tm] + cw[0:1] * ext[HIST - 2:HIST - 2 + tm])
    ya_ref[...] = (_dot(h, wb_ref[...]) * y).astype(ya_ref.dtype)


def _conv_branch(h, w_a, conv_w, hist, tm, tc, tiles_per_seq, emit_hist):
    rows = h.shape[0]
    ncb = D_CONV // tc
    assert not emit_hist or rows == tm
    kern = functools.partial(_conv_branch_kernel, tm=tm, tiles_per_seq=tiles_per_seq)
    n_out = 2 if emit_hist else 1
    return pl.pallas_call(
        kern,
        grid=(rows // tm, ncb),
        in_specs=[pl.BlockSpec((tm, D_MODEL), lambda i, j: (i, 0)),
                  pl.BlockSpec((D_MODEL, tc), lambda i, j: (0, j)),
                  pl.BlockSpec((D_MODEL, tc), lambda i, j: (0, j + ncb)),
                  pl.BlockSpec((D_MODEL, tc), lambda i, j: (0, j + 2 * ncb)),
                  pl.BlockSpec((3, tc), lambda i, j: (0, j)),
                  pl.BlockSpec((HIST, tc), lambda i, j: (0, j))],
        out_specs=[pl.BlockSpec((tm, tc), lambda i, j: (i, j)),
                   pl.BlockSpec((HIST, tc), lambda i, j: (0, j))][:n_out],
        out_shape=[jax.ShapeDtypeStruct((rows, D_CONV), BF),
                   jax.ShapeDtypeStruct((HIST, D_CONV), F32)][:n_out],
        scratch_shapes=[pltpu.VMEM((ncb, HIST, tc), F32)],
        name="conv_branch",
        compiler_params=_params("arbitrary", "arbitrary"),
    )(h, w_a, w_a, w_a, conv_w, hist)


def _latent_kernel(h_ref, w_ref, gq_ref, gkv_ref, qn_ref, kvn_ref, kr_ref):
    acc = _dot(h_ref[...], w_ref[...])

    def norm(a, g):
        ms = jnp.mean(a * a, axis=-1, keepdims=True)
        return a * lax.rsqrt(ms + EPS) * g

    qn_ref[...] = norm(acc[:, :Q_LORA], gq_ref[...]).astype(qn_ref.dtype)
    kvn_ref[...] = norm(acc[:, Q_LORA:Q_LORA + KV_LORA], gkv_ref[...]).astype(kvn_ref.dtype)
    kr_ref[...] = acc[:, Q_LORA + KV_LORA:]


def _latent(h, w_lat, gq, gkv, tm):
    rows = h.shape[0]
    return pl.pallas_call(
        _latent_kernel,
        grid=(rows // tm,),
        in_specs=[pl.BlockSpec((tm, D_MODEL), lambda i: (i, 0)),
                  pl.BlockSpec((D_MODEL, D_LAT), lambda i: (0, 0)),
                  pl.BlockSpec((1, Q_LORA), lambda i: (0, 0)),
                  pl.BlockSpec((1, KV_LORA), lambda i: (0, 0))],
        out_specs=[pl.BlockSpec((tm, Q_LORA), lambda i: (i, 0)),
                   pl.BlockSpec((tm, KV_LORA), lambda i: (i, 0)),
                   pl.BlockSpec((tm, 256), lambda i: (i, 0))],
        out_shape=[jax.ShapeDtypeStruct((rows, Q_LORA), BF),
                   jax.ShapeDtypeStruct((rows, KV_LORA), BF),
                   jax.ShapeDtypeStruct((rows, 256), F32)],
        name="latent",
        compiler_params=_params("parallel"),
    )(h, w_lat, gq, gkv)


def _row_sumsq(a):
    return _dot((a * a).astype(BF), jnp.ones((a.shape[1], 128), BF))


def _q_kernel(qn_ref, w_ref, g_ref, cos_ref, sin_ref, q_ref, acc_ref, ss_ref, *, heads, rb):
    acc_ref[...] = _dot(qn_ref[...], w_ref[...])
    for hd in range(heads):
        ss_ref[:, hd * 128:(hd + 1) * 128] = _row_sumsq(
            acc_ref[:, hd * HEAD_W:hd * HEAD_W + HEAD_PAD])
    g = g_ref[...]
    for r in range(0, acc_ref.shape[0], rb):
        rs = slice(r, r + rb)
        cos_g, sin_g = cos_ref[rs, :], sin_ref[rs, :]
        for hd in range(heads):
            c0 = hd * HEAD_W
            inv = lax.rsqrt(ss_ref[rs, hd * 128:(hd + 1) * 128] * (1.0 / QK_HEAD) + EPS)
            q_ref[rs, hd * HEAD_PAD:hd * HEAD_PAD + QK_NOPE] = (
                acc_ref[rs, c0:c0 + QK_NOPE] * (inv * g)).astype(q_ref.dtype)
            rot = (acc_ref[rs, c0 + QK_NOPE:c0 + HEAD_PAD] * cos_g
                   + acc_ref[rs, c0 + HEAD_PAD:c0 + HEAD_W] * sin_g)
            q_ref[rs, hd * HEAD_PAD + QK_NOPE:(hd + 1) * HEAD_PAD] = (rot * inv).astype(q_ref.dtype)


def _q_proj(qn, w_uq, g_nope, cos_g, sin_g, tm, heads, tiles_per_seq):
    rows = qn.shape[0]
    kern = functools.partial(_q_kernel, heads=heads, rb=min(64, tm))
    return pl.pallas_call(
        kern,
        grid=(rows // tm, N_HEADS // heads),
        in_specs=[pl.BlockSpec((tm, Q_LORA), lambda i, j: (i, 0)),
                  pl.BlockSpec((Q_LORA, heads * HEAD_W), lambda i, j: (0, j)),
                  pl.BlockSpec((1, QK_NOPE), lambda i, j: (0, 0)),
                  pl.BlockSpec((tm, 128), lambda i, j: (i % tiles_per_seq, 0)),
                  pl.BlockSpec((tm, 128), lambda i, j: (i % tiles_per_seq, 0))],
        out_specs=pl.BlockSpec((tm, heads * HEAD_PAD), lambda i, j: (i, j)),
        out_shape=jax.ShapeDtypeStruct((rows, N_HEADS * HEAD_PAD), BF),
        scratch_shapes=[pltpu.VMEM((tm, heads * HEAD_W), F32),
                        pltpu.VMEM((tm, heads * 128), F32)],
        name="q_proj",
        compiler_params=_params("parallel", "parallel"),
    )(qn, w_uq, g_nope, cos_g, sin_g)


def _kv_kernel(kvn_ref, wk_ref, wv_ref, kr_ref, gn_ref, cos_ref, sin_ref, k_ref, v_ref,
               acc_ref, ss_ref, *, heads, rb):
    kvn = kvn_ref[...]
    acc_ref[...] = _dot(kvn, wk_ref[...])
    v_ref[...] = _dot(kvn, wv_ref[...]).astype(v_ref.dtype)
    blk = (lax.broadcasted_iota(jnp.int32, (256, 256), 0) // 128
           == lax.broadcasted_iota(jnp.int32, (256, 256), 1) // 128)
    pair_ones = jnp.where(blk, 1.0, 0.0).astype(BF)
    for hd in range(0, heads, 2):
        a = acc_ref[:, hd * QK_NOPE:(hd + 2) * QK_NOPE]
        ss_ref[:, hd * 128:(hd + 2) * 128] = _dot((a * a).astype(BF), pair_ones)
    ss_ref[:, heads * 128:] = _row_sumsq(kr_ref[:, :128])
    gn = gn_ref[...]
    for r in range(0, acc_ref.shape[0], rb):
        rs = slice(r, r + rb)
        kr_rot = kr_ref[rs, :128] * cos_ref[rs, :] + kr_ref[rs, 128:] * sin_ref[rs, :]
        kr_ss = ss_ref[rs, heads * 128:]
        for hd in range(heads):
            ss = ss_ref[rs, hd * 128:(hd + 1) * 128] + kr_ss
            inv = lax.rsqrt(ss * (1.0 / QK_HEAD) + EPS)
            k_ref[rs, hd * HEAD_PAD:hd * HEAD_PAD + QK_NOPE] = (
                acc_ref[rs, hd * QK_NOPE:(hd + 1) * QK_NOPE] * (inv * gn)).astype(k_ref.dtype)
            k_ref[rs, hd * HEAD_PAD + QK_NOPE:(hd + 1) * HEAD_PAD] = (kr_rot * inv).astype(
                k_ref.dtype)


def _kv_proj(kvn, w_uk, w_uv, kr, gn, cos_t, sin_t, tm, heads, tiles_per_seq):
    rows = kvn.shape[0]
    kern = functools.partial(_kv_kernel, heads=heads, rb=min(64, tm))
    return pl.pallas_call(
        kern,
        grid=(rows // tm, N_HEADS // heads),
        in_specs=[pl.BlockSpec((tm, KV_LORA), lambda i, j: (i, 0)),
                  pl.BlockSpec((KV_LORA, heads * QK_NOPE), lambda i, j: (0, j)),
                  pl.BlockSpec((KV_LORA, heads * V_HEAD), lambda i, j: (0, j)),
                  pl.BlockSpec((tm, 256), lambda i, j: (i, 0)),
                  pl.BlockSpec((1, 128), lambda i, j: (0, 0)),
                  pl.BlockSpec((tm, 128), lambda i, j: (i % tiles_per_seq, 0)),
                  pl.BlockSpec((tm, 128), lambda i, j: (i % tiles_per_seq, 0))],
        out_specs=[pl.BlockSpec((tm, heads * HEAD_PAD), lambda i, j: (i, j)),
                   pl.BlockSpec((tm, heads * V_HEAD), lambda i, j: (i, j))],
        out_shape=[jax.ShapeDtypeStruct((rows, N_HEADS * HEAD_PAD), BF),
                   jax.ShapeDtypeStruct((rows, N_HEADS * V_HEAD), BF)],
        scratch_shapes=[pltpu.VMEM((tm, heads * QK_NOPE), F32),
                        pltpu.VMEM((tm, (heads + 1) * 128), F32)],
        name="kv_proj",
        compiler_params=_params("parallel", "parallel"),
    )(kvn, w_uk, w_uv, kr, gn, cos_t, sin_t)


def _attn_tile_kernel(q_ref, k_ref, v_ref, kp_ref, vp_ref, *rest, q_tile, tq, tc, td, rb,
                      n_prefix):
    _, o_ref, head_bufs = rest
    d0 = q_tile * tq
    items = [("full", c * tc, tc, 0) for c in range(d0 // tc)]
    items += [("diag", d0 + h * td, td, h * td) for h in range(tq // td)]
    items += [("prefix", 0, 128, 0)]

    def scores(hd, item, s_ref):
        kind, k0, width, r0 = item
        q = q_ref[r0:, hd * HEAD_PAD:(hd + 1) * HEAD_PAD]
        if kind == "prefix":
            keys = kp_ref[:, hd * HEAD_PAD:(hd + 1) * HEAD_PAD]
        else:
            keys = k_ref[k0:k0 + width, hd * HEAD_PAD:(hd + 1) * HEAD_PAD]
        s_ref[r0:, :width] = _dot_nt(q, keys)

    def softmax(item, s_ref, p_ref, a_ref, m_ref, l_ref):
        kind, k0, width, r0 = item
        for r in range(r0, tq, rb):
            rs = slice(r, r + rb)
            s = s_ref[rs, :width]
            col = lax.broadcasted_iota(jnp.int32, (rb, width), 1)
            if kind == "prefix":
                s = jnp.where(col < n_prefix, s, NEG_BIG)
            elif kind == "diag" and r < r0 + width:
                row = lax.broadcasted_iota(jnp.int32, (rb, width), 0) + (r - r0)
                s = jnp.where(col <= row, s, NEG_BIG)
            m_old = m_ref[rs, :]
            m_new = jnp.maximum(m_old, jnp.max(s, axis=-1, keepdims=True))
            alpha = jnp.exp2(m_old - m_new)
            p = jnp.exp2(s - jnp.concatenate([m_new] * (width // 128), axis=1))
            l_ref[rs, :] = alpha * l_ref[rs, :] + jnp.sum(p, axis=-1, keepdims=True)
            m_ref[rs, :] = m_new
            a_ref[rs, :] = alpha
            p_ref[rs, :width] = p.astype(p_ref.dtype)

    def accumulate(hd, item, p_ref, a_ref, acc_ref):
        kind, k0, width, r0 = item
        if kind == "prefix":
            values = vp_ref[:, hd * V_HEAD:(hd + 1) * V_HEAD]
        else:
            values = v_ref[k0:k0 + width, hd * V_HEAD:(hd + 1) * V_HEAD]
        acc_ref[r0:, :] = a_ref[r0:, :] * acc_ref[r0:, :] + _dot(p_ref[r0:, :width], values)

    for s_refs, p_refs, a_refs, m_ref, l_ref, acc_ref in head_bufs:
        m_ref[...] = jnp.full(m_ref.shape, NEG_BIG, F32)
        l_ref[...] = jnp.zeros(l_ref.shape, F32)
        acc_ref[...] = jnp.zeros(acc_ref.shape, F32)
    n = len(items)
    for t in range(n + 2):
        for hd, (s_refs, p_refs, a_refs, m_ref, l_ref, acc_ref) in enumerate(head_bufs):
            depth = len(s_refs)
            if t < n:
                scores(hd, items[t], s_refs[t % depth])
            if 1 <= t <= n:
                u = (t - 1) % depth
                softmax(items[t - 1], s_refs[u], p_refs[u], a_refs[u], m_ref, l_ref)
            if t >= 2:
                u = (t - 2) % depth
                accumulate(hd, items[t - 2], p_refs[u], a_refs[u], acc_ref)
    for hd, (_, _, _, _, l_ref, acc_ref) in enumerate(head_bufs):
        o_ref[:, hd * V_HEAD:(hd + 1) * V_HEAD] = (acc_ref[...] / l_ref[...]).astype(o_ref.dtype)


def _attention_tiles(q, k, v, kp, vp, batch, seq, tq, tc, td, heads):
    nq = seq // tq
    depth = 3
    stat = pltpu.VMEM((tq, 128), F32)
    head_bufs = [([pltpu.VMEM((tq, tc), F32)] * depth, [pltpu.VMEM((tq, tc), BF)] * depth,
                  [stat] * depth, stat, stat, pltpu.VMEM((tq, V_HEAD), F32))] * heads
    out = jnp.zeros((batch, seq, N_HEADS * V_HEAD), BF)
    for i in range(nq):
        keys = (i + 1) * tq
        kern = functools.partial(_attn_tile_kernel, q_tile=i, tq=tq, tc=tc, td=td, rb=64,
                                 n_prefix=N_META)
        out = pl.pallas_call(
            kern,
            grid=(batch, N_HEADS // heads),
            in_specs=[pl.BlockSpec((None, tq, heads * HEAD_PAD), lambda b, h, i=i: (b, i, h)),
                      pl.BlockSpec((None, keys, heads * HEAD_PAD), lambda b, h: (b, 0, h)),
                      pl.BlockSpec((None, keys, heads * V_HEAD), lambda b, h: (b, 0, h)),
                      pl.BlockSpec((128, heads * HEAD_PAD), lambda b, h: (0, h)),
                      pl.BlockSpec((128, heads * V_HEAD), lambda b, h: (0, h)),
                      pl.BlockSpec(memory_space=pl.ANY)],
            out_specs=pl.BlockSpec((None, tq, heads * V_HEAD), lambda b, h, i=i: (b, i, h)),
            out_shape=jax.ShapeDtypeStruct(out.shape, out.dtype),
            scratch_shapes=[head_bufs],
            input_output_aliases={5: 0},
            name=f"attention_q{i}",
            compiler_params=_params("parallel", "parallel"),
        )(q, k, v, kp, vp, out)
    return out


def _attn_meta_kernel(q_ref, k_ref, v_ref, o_ref):
    s = _dot_nt(q_ref[...], k_ref[...])
    row = lax.broadcasted_iota(jnp.int32, s.shape, 0)
    col = lax.broadcasted_iota(jnp.int32, s.shape, 1)
    s = jnp.where(row >= col, s, NEG_BIG)
    p = jnp.exp2(s - jnp.max(s, axis=-1, keepdims=True))
    l = jnp.sum(p, axis=-1, keepdims=True)
    o_ref[...] = (_dot(p.astype(BF), v_ref[...]) / l).astype(o_ref.dtype)


def _attention_meta(q, k, v):
    return pl.pallas_call(
        _attn_meta_kernel,
        grid=(N_HEADS,),
        in_specs=[pl.BlockSpec((N_META, HEAD_PAD), lambda h: (0, h)),
                  pl.BlockSpec((N_META, HEAD_PAD), lambda h: (0, h)),
                  pl.BlockSpec((N_META, V_HEAD), lambda h: (0, h))],
        out_specs=pl.BlockSpec((N_META, V_HEAD), lambda h: (0, h)),
        out_shape=jax.ShapeDtypeStruct((N_META, N_HEADS * V_HEAD), BF),
        name="attention_meta",
        compiler_params=_params("parallel"),
    )(q, k, v)


def _pool_branch_kernel(h_ref, w_ref, pw_ref, ps_ref, hist_ref, yc_ref, *rest,
                        tm, tiles_per_seq, pos_offset):
    carry_ref = rest[-1]
    i = pl.program_id(0)
    pin = _dot(h_ref[...], w_ref[...])

    @pl.when(i % tiles_per_seq == 0)
    def _():
        carry_ref[...] = hist_ref[...]

    ext = jnp.concatenate([carry_ref[...], pin], axis=0)
    tail = pin[tm - HIST:]
    carry_ref[...] = tail
    if len(rest) == 2:
        rest[0][...] = tail
    n_ext = tm + HIST
    seen = (lax.broadcasted_iota(jnp.int32, (tm, 1), 0)
            + ((i % tiles_per_seq) * tm + pos_offset + 1)).astype(F32)
    for g, w in enumerate(POOL_WINDOWS):
        xg = ext[:, g * POOL_GROUP:(g + 1) * POOL_GROUP]
        s, span = xg, 1
        while span < w:
            s = s[span:] + s[:s.shape[0] - span]
            span *= 2
        s = s[n_ext - (w - 1) - tm:]
        pooled = s / jnp.minimum(seen, float(w)) - xg[HIST:]
        mixed = _dot(pooled.astype(BF), pw_ref[g])
        yc_ref[:, g * POOL_GROUP:(g + 1) * POOL_GROUP] = (
            mixed * ps_ref[:, g * POOL_GROUP:(g + 1) * POOL_GROUP]).astype(yc_ref.dtype)


def _pool_branch(h, w_pool, pool_w, pool_scale, hist, tm, tiles_per_seq, pos_offset, emit_hist):
    rows = h.shape[0]
    assert not emit_hist or rows == tm
    kern = functools.partial(_pool_branch_kernel, tm=tm, tiles_per_seq=tiles_per_seq,
                             pos_offset=pos_offset)
    ng = len(POOL_WINDOWS)
    n_out = 2 if emit_hist else 1
    return pl.pallas_call(
        kern,
        grid=(rows // tm,),
        in_specs=[pl.BlockSpec((tm, D_MODEL), lambda i: (i, 0)),
                  pl.BlockSpec((D_MODEL, D_POOL), lambda i: (0, 0)),
                  pl.BlockSpec((ng, POOL_GROUP, POOL_GROUP), lambda i: (0, 0, 0)),
                  pl.BlockSpec((1, D_POOL), lambda i: (0, 0)),
                  pl.BlockSpec((HIST, D_POOL), lambda i: (0, 0))],
        out_specs=[pl.BlockSpec((tm, D_POOL), lambda i: (i, 0)),
                   pl.BlockSpec((HIST, D_POOL), lambda i: (0, 0))][:n_out],
        out_shape=[jax.ShapeDtypeStruct((rows, D_POOL), BF),
                   jax.ShapeDtypeStruct((HIST, D_POOL), F32)][:n_out],
        scratch_shapes=[pltpu.VMEM((HIST, D_POOL), F32)],
        name="pool_branch",
        compiler_params=_params("arbitrary"),
    )(h, w_pool, pool_w, pool_scale, hist)


def _merge_kernel(h_ref, ya_ref, yb_ref, yc_ref, wg0_ref, wg1_ref, wg2_ref, wa_ref, wb_ref, wc_ref,
                  o_ref):
    h = h_ref[...]
    m = jax.nn.sigmoid(_dot(h, wg0_ref[...])) * _dot(ya_ref[...], wa_ref[...])
    m += jax.nn.sigmoid(_dot(h, wg1_ref[...])) * _dot(yb_ref[...], wb_ref[...])
    m += jax.nn.sigmoid(_dot(h, wg2_ref[...])) * _dot(yc_ref[...], wc_ref[...])
    o_ref[...] = m.astype(o_ref.dtype)


def _merge(h, ya, yb, yc, w_gate, wa, wb, wc, tm, tn):
    rows = h.shape[0]
    ncb = D_MODEL // tn
    row = lambda width: pl.BlockSpec((tm, width), lambda i, j: (i, 0))
    col = lambda depth, off: pl.BlockSpec((depth, tn), lambda i, j: (0, j + off))
    return pl.pallas_call(
        _merge_kernel,
        grid=(rows // tm, ncb),
        in_specs=[row(D_MODEL), row(D_CONV), row(N_HEADS * V_HEAD), row(D_POOL),
                  col(D_MODEL, 0), col(D_MODEL, ncb), col(D_MODEL, 2 * ncb),
                  col(D_CONV, 0), col(N_HEADS * V_HEAD, 0), col(D_POOL, 0)],
        out_specs=pl.BlockSpec((tm, tn), lambda i, j: (i, j)),
        out_shape=jax.ShapeDtypeStruct((rows, D_MODEL), BF),
        name="merge",
        compiler_params=_params("parallel", "arbitrary"),
    )(h, ya, yb, yc, w_gate, w_gate, w_gate, wa, wb, wc)


def _oproj_kernel(m_ref, w_ref, x_ref, g_ref, xo_ref, h2_ref):
    x = x_ref[...] + _dot(m_ref[...], w_ref[...])
    xo_ref[...] = x
    ms = jnp.mean(x * x, axis=-1, keepdims=True)
    h2_ref[...] = (x * lax.rsqrt(ms + EPS) * g_ref[...]).astype(h2_ref.dtype)


def _oproj(merged, w_o, x, g, tm):
    rows = x.shape[0]
    return pl.pallas_call(
        _oproj_kernel,
        grid=(rows // tm,),
        in_specs=[pl.BlockSpec((tm, D_MODEL), lambda i: (i, 0)),
                  pl.BlockSpec((D_MODEL, D_MODEL), lambda i: (0, 0)),
                  pl.BlockSpec((tm, D_MODEL), lambda i: (i, 0)),
                  pl.BlockSpec((1, D_MODEL), lambda i: (0, 0))],
        out_specs=[pl.BlockSpec((tm, D_MODEL), lambda i: (i, 0)),
                   pl.BlockSpec((tm, D_MODEL), lambda i: (i, 0))],
        out_shape=[jax.ShapeDtypeStruct((rows, D_MODEL), F32),
                   jax.ShapeDtypeStruct((rows, D_MODEL), BF)],
        name="oproj",
        compiler_params=_params("parallel"),
    )(merged, w_o, x, g)


def _mlp_kernel(h2_ref, wu_ref, wd_ref, x_ref, o_ref):
    @pl.when(pl.program_id(1) == 0)
    def _():
        o_ref[...] = x_ref[...]

    a = jnp.maximum(_dot(h2_ref[...], wu_ref[...]), 0.0)
    o_ref[...] += _dot((a * a).astype(BF), wd_ref[...])


def _mlp(h2, w_up, w_down, x, tm, tf):
    rows = x.shape[0]
    return pl.pallas_call(
        _mlp_kernel,
        grid=(rows // tm, D_FF // tf),
        in_specs=[pl.BlockSpec((tm, D_MODEL), lambda i, k: (i, 0)),
                  pl.BlockSpec((D_MODEL, tf), lambda i, k: (0, k)),
                  pl.BlockSpec((tf, D_MODEL), lambda i, k: (k, 0)),
                  pl.BlockSpec((tm, D_MODEL), lambda i, k: (i, 0))],
        out_specs=pl.BlockSpec((tm, D_MODEL), lambda i, k: (i, 0)),
        out_shape=jax.ShapeDtypeStruct((rows, D_MODEL), F32),
        name="mlp",
        compiler_params=_params("parallel", "arbitrary"),
    )(h2, w_up, w_down, x)


def _spread_rope(a):
    z = jnp.zeros(a.shape[:-1] + (HALF_ROPE,), a.dtype)
    return jnp.concatenate([a[..., :HALF_ROPE], z, a[..., HALF_ROPE:], z], axis=-1)


def _swap_halves(a):
    return jnp.concatenate([a[..., HALF_ROPE:], a[..., :HALF_ROPE]], axis=-1)


def _widen_head(a):
    rope = a[..., QK_NOPE:]
    return jnp.concatenate([a[..., :QK_NOPE], _spread_rope(rope), _spread_rope(_swap_halves(rope))],
                           axis=-1)


def _rope_gain_tables(gain, cos_t, sin_t):
    rope_gain = gain[QK_NOPE:]
    return (cos_t * _spread_rope(rope_gain)[None], sin_t * _spread_rope(_swap_halves(rope_gain))[None])


def _layer_params(l, cos_t, sin_t, attn_norm, w_in, conv_w, q_lat_norm, kv_lat_norm, w_uq, w_ukv,
                  q_norm, k_norm, pool_w, pool_scale, w_branch_a, w_branch_b, w_branch_c, w_o,
                  mlp_norm, w_up, w_down):
    cut = lambda a, b: w_in[l, :, a:b].astype(BF)
    gq = q_norm[l] * (QK_HEAD ** -0.5 * LOG2_E)
    q_cos, q_sin = _rope_gain_tables(gq, cos_t, sin_t)
    k_cos, k_sin = _rope_gain_tables(k_norm[l], cos_t, sin_t)
    o_q = 3 * D_CONV
    o_kv = o_q + Q_LORA
    o_kr = o_kv + KV_LORA
    o_pool = o_kr + QK_ROPE
    o_gate = o_pool + D_POOL
    w_kr = cut(o_kr, o_pool)
    w_ukv_h = w_ukv[l].astype(BF).reshape(KV_LORA, N_HEADS, QK_NOPE + V_HEAD)
    return dict(
        attn_norm=attn_norm[l][None],
        w_a=cut(0, o_q),
        w_lat=jnp.concatenate([cut(o_q, o_kr), _spread_rope(w_kr),
                               _spread_rope(_swap_halves(w_kr))], axis=1),
        w_pool=cut(o_pool, o_gate),
        w_gate=cut(o_gate, w_in.shape[-1]),
        conv_w=conv_w[l],
        q_lat_norm=q_lat_norm[l][None],
        kv_lat_norm=kv_lat_norm[l][None],
        w_uq=_widen_head(w_uq[l].astype(BF).reshape(Q_LORA, N_HEADS, QK_HEAD)).reshape(
            Q_LORA, N_HEADS * HEAD_W),
        w_uk=w_ukv_h[:, :, :QK_NOPE].reshape(KV_LORA, N_HEADS * QK_NOPE),
        w_uv=w_ukv_h[:, :, QK_NOPE:].reshape(KV_LORA, N_HEADS * V_HEAD),
        gq_nope=gq[None, :QK_NOPE],
        gk_nope=k_norm[l][None, :QK_NOPE],
        q_rope=(q_cos, q_sin),
        k_rope=(k_cos, k_sin),
        pool_w=pool_w[l].astype(BF),
        pool_scale=pool_scale[l][None],
        wa=w_branch_a[l].astype(BF),
        wb=w_branch_b[l].astype(BF),
        wc=w_branch_c[l].astype(BF),
        w_o=w_o[l].astype(BF),
        mlp_norm=mlp_norm[l][None],
        w_up=w_up[l].astype(BF),
        w_down=w_down[l].astype(BF),
    )


def _rope_tables(total):
    pos = jnp.arange(total, dtype=F32)
    inv = ROPE_THETA ** (-jnp.arange(0, QK_ROPE, 2, dtype=F32) / QK_ROPE)
    ang = pos[:, None] * inv[None, :]
    cos, sin = jnp.cos(ang), jnp.sin(ang)
    z = jnp.zeros_like(cos)
    return (jnp.concatenate([cos, z, cos, z], axis=1),
            jnp.concatenate([-sin, z, sin, z], axis=1))


def _layer(x, p, positions, prefix, *, batch, seq, finish):
    is_meta = prefix is None
    q_cos, q_sin = (t[positions] for t in p['q_rope'])
    k_cos, k_sin = (t[positions] for t in p['k_rope'])
    tm = min(512, seq)
    tps = seq // tm
    tm_big = min(1024, seq)
    if is_meta:
        conv_hist = jnp.zeros((HIST, D_CONV), F32)
        pool_hist = jnp.zeros((HIST, D_POOL), F32)
    else:
        conv_hist, pool_hist, kp, vp = prefix

    h = _rmsnorm(x, p['attn_norm'], tm)
    ya, *conv_tail = _conv_branch(h, p['w_a'], p['conv_w'], conv_hist, tm_big, 512,
                                  seq // tm_big, is_meta)
    yc, *pool_tail = _pool_branch(h, p['w_pool'], p['pool_w'], p['pool_scale'], pool_hist, tm, tps,
                                  0 if is_meta else N_META, is_meta)
    qn, kvn, kr = _latent(h, p['w_lat'], p['q_lat_norm'], p['kv_lat_norm'], tm)
    k, v = _kv_proj(kvn, p['w_uk'], p['w_uv'], kr, p['gk_nope'], k_cos, k_sin, tm, 4, tps)
    out_prefix = (conv_tail[0], pool_tail[0], k, v) if is_meta else None
    if not finish:
        return None, out_prefix
    q = _q_proj(qn, p['w_uq'], p['gq_nope'], q_cos, q_sin, tm, 4, tps)
    if is_meta:
        yb = _attention_meta(q, k, v)
    else:
        pad = ((0, 128 - N_META), (0, 0))
        per_batch = lambda a: a.reshape(batch, seq, a.shape[-1])
        yb = _attention_tiles(per_batch(q), per_batch(k), per_batch(v), jnp.pad(kp, pad),
                              jnp.pad(vp, pad), batch, seq, 512, 512, 256, 2).reshape(
                                  batch * seq, -1)
    merged = _merge(h, ya, yb, yc, p['w_gate'], p['wa'], p['wb'], p['wc'], tm_big, 256)
    x_mid, h2 = _oproj(merged, p['w_o'], x, p['mlp_norm'], min(256, seq))
    return _mlp(h2, p['w_up'], p['w_down'], x_mid, tm_big, 512), out_prefix


def kernel(x, meta_tokens, attn_norm, w_in, conv_w, q_lat_norm, kv_lat_norm, w_uq, w_ukv, q_norm, k_norm, pool_w, pool_scale, w_branch_a, w_branch_b, w_branch_c, w_o, mlp_norm, w_up, w_down):
    batch, seq, d = x.shape
    depth = w_in.shape[0]
    cos_t, sin_t = _rope_tables(N_META + seq)
    xm = meta_tokens.astype(F32)
    xr = x.reshape(batch * seq, d)
    for l in range(depth):
        p = _layer_params(l, cos_t, sin_t, attn_norm, w_in, conv_w, q_lat_norm, kv_lat_norm, w_uq,
                          w_ukv, q_norm, k_norm, pool_w, pool_scale, w_branch_a, w_branch_b,
                          w_branch_c, w_o, mlp_norm, w_up, w_down)
        last = l == depth - 1
        xm, prefix = _layer(xm, p, slice(0, N_META), None, batch=1, seq=N_META, finish=not last)
        xr, _ = _layer(xr, p, slice(N_META, None), prefix, batch=batch, seq=seq, finish=True)
    return xr.reshape(batch, seq, d)
```

```python
import functools

import jax
import jax.numpy as jnp
from jax import lax
from jax.experimental import pallas as pl
from jax.experimental.pallas import tpu as pltpu

D_MODEL = 2048
N_META = 16
EPS = 1e-6
D_CONV = 1024
N_HEADS = 16
QK_NOPE = 128
QK_ROPE = 64
QK_HEAD = QK_NOPE + QK_ROPE
HALF_ROPE = QK_ROPE // 2
V_HEAD = 128
Q_LORA = 512
KV_LORA = 512
ROPE_THETA = 10000.0
D_POOL = 1024
POOL_WINDOWS = (2, 4, 8, 16)
POOL_GROUP = D_POOL // len(POOL_WINDOWS)
D_FF = 4 * D_MODEL
HIST = 16
HEAD_PAD = 256
HEAD_W = HEAD_PAD + 128
D_LAT = Q_LORA + KV_LORA + 256
VMEM_LIMIT = 56 * 2**20
NEG_BIG = -1e30
LOG2_E = 1.4426950408889634

BF = jnp.bfloat16
F32 = jnp.float32


def _dot(a, b):
    return jnp.dot(a, b, preferred_element_type=F32)


def _dot_nt(a, b):
    return lax.dot_general(a, b, (((1,), (1,)), ((), ())), preferred_element_type=F32)


def _layer_spec(l, shape, index_map):
    return pl.BlockSpec((None,) + shape, lambda *g: (l,) + index_map(*g))


def _params(*sem):
    return pltpu.CompilerParams(dimension_semantics=sem, vmem_limit_bytes=VMEM_LIMIT)


def _rmsnorm_kernel(x_ref, g_ref, o_ref):
    x = x_ref[...]
    ms = jnp.mean(x * x, axis=-1, keepdims=True)
    o_ref[...] = (x * lax.rsqrt(ms + EPS) * g_ref[...]).astype(o_ref.dtype)


def _rmsnorm(x, g, tm):
    rows, d = x.shape
    return pl.pallas_call(
        _rmsnorm_kernel,
        grid=(rows // tm,),
        in_specs=[pl.BlockSpec((tm, d), lambda i: (i, 0)),
                  pl.BlockSpec((1, d), lambda i: (0, 0))],
        out_specs=pl.BlockSpec((tm, d), lambda i: (i, 0)),
        out_shape=jax.ShapeDtypeStruct((rows, d), BF),
        name="rmsnorm",
        compiler_params=_params("parallel"),
    )(x, g)


def _conv_branch_kernel(h_ref, wu_ref, wb_ref, wc_ref, cw_ref, hist_ref, ya_ref, *rest,
                        tm, tiles_per_seq):
    carry_ref = rest[-1]
    i, j = pl.program_id(0), pl.program_id(1)
    h = h_ref[...]
    cu = _dot(h, wc_ref[...]) * _dot(h, wu_ref[...])

    @pl.when(i % tiles_per_seq == 0)
    def _():
        carry_ref[j] = hist_ref[...]

    ext = jnp.concatenate([carry_ref[j], cu], axis=0)
    tail = cu[tm - HIST:]
    carry_ref[j] = tail
    if len(rest) == 2:
        rest[0][...] = tail
    cw = cw_ref[...]
    y = (cw[2:3] * cu + cw[1:2] * ext[HIST - 1:HIST - 1 + tm] + cw[0:1] * ext[HIST - 2:HIST - 2 + tm])
    ya_ref[...] = (_dot(h, wb_ref[...]) * y).astype(ya_ref.dtype)


def _conv_branch(h, w_in, l, conv_w, hist, tm, tc, tiles_per_seq, emit_hist):
    rows = h.shape[0]
    ncb = D_CONV // tc
    assert not emit_hist or rows == tm
    kern = functools.partial(_conv_branch_kernel, tm=tm, tiles_per_seq=tiles_per_seq)
    n_out = 2 if emit_hist else 1
    return pl.pallas_call(
        kern,
        grid=(rows // tm, ncb),
        in_specs=[pl.BlockSpec((tm, D_MODEL), lambda i, j: (i, 0)),
                  _layer_spec(l, (D_MODEL, tc), lambda i, j: (0, j)),
                  _layer_spec(l, (D_MODEL, tc), lambda i, j: (0, j + ncb)),
                  _layer_spec(l, (D_MODEL, tc), lambda i, j: (0, j + 2 * ncb)),
                  pl.BlockSpec((3, tc), lambda i, j: (0, j)),
                  pl.BlockSpec((HIST, tc), lambda i, j: (0, j))],
        out_specs=[pl.BlockSpec((tm, tc), lambda i, j: (i, j)),
                   pl.BlockSpec((HIST, tc), lambda i, j: (0, j))][:n_out],
        out_shape=[jax.ShapeDtypeStruct((rows, D_CONV), BF),
                   jax.ShapeDtypeStruct((HIST, D_CONV), F32)][:n_out],
        scratch_shapes=[pltpu.VMEM((ncb, HIST, tc), F32)],
        name="conv_branch",
        compiler_params=_params("arbitrary", "arbitrary"),
    )(h, w_in, w_in, w_in, conv_w, hist)


def _latent_kernel(h_ref, w_ref, gq_ref, gkv_ref, qn_ref, kvn_ref, kr_ref):
    acc = _dot(h_ref[...], w_ref[...])

    def norm(a, g):
        ms = jnp.mean(a * a, axis=-1, keepdims=True)
        return a * lax.rsqrt(ms + EPS) * g

    qn_ref[...] = norm(acc[:, :Q_LORA], gq_ref[...]).astype(qn_ref.dtype)
    kvn_ref[...] = norm(acc[:, Q_LORA:Q_LORA + KV_LORA], gkv_ref[...]).astype(kvn_ref.dtype)
    kr_ref[...] = acc[:, Q_LORA + KV_LORA:]


def _latent(h, w_lat, l, gq, gkv, tm):
    rows = h.shape[0]
    return pl.pallas_call(
        _latent_kernel,
        grid=(rows // tm,),
        in_specs=[pl.BlockSpec((tm, D_MODEL), lambda i: (i, 0)),
                  _layer_spec(l, (D_MODEL, D_LAT), lambda i: (0, 0)),
                  pl.BlockSpec((1, Q_LORA), lambda i: (0, 0)),
                  pl.BlockSpec((1, KV_LORA), lambda i: (0, 0))],
        out_specs=[pl.BlockSpec((tm, Q_LORA), lambda i: (i, 0)),
                   pl.BlockSpec((tm, KV_LORA), lambda i: (i, 0)),
                   pl.BlockSpec((tm, 256), lambda i: (i, 0))],
        out_shape=[jax.ShapeDtypeStruct((rows, Q_LORA), BF),
                   jax.ShapeDtypeStruct((rows, KV_LORA), BF),
                   jax.ShapeDtypeStruct((rows, 256), F32)],
        name="latent",
        compiler_params=_params("parallel"),
    )(h, w_lat, gq, gkv)


def _row_sumsq(a):
    return _dot((a * a).astype(BF), jnp.ones((a.shape[1], 128), BF))


def _q_kernel(qn_ref, w_ref, g_ref, cos_ref, sin_ref, q_ref, acc_ref, ss_ref, *, heads, rb):
    acc_ref[...] = _dot(qn_ref[...], w_ref[...])
    for hd in range(heads):
        ss_ref[:, hd * 128:(hd + 1) * 128] = _row_sumsq(
            acc_ref[:, hd * HEAD_W:hd * HEAD_W + HEAD_PAD])
    g = g_ref[...]
    for r in range(0, acc_ref.shape[0], rb):
        rs = slice(r, r + rb)
        cos_g, sin_g = cos_ref[rs, :], sin_ref[rs, :]
        for hd in range(heads):
            c0 = hd * HEAD_W
            inv = lax.rsqrt(ss_ref[rs, hd * 128:(hd + 1) * 128] * (1.0 / QK_HEAD) + EPS)
            q_ref[rs, hd * HEAD_PAD:hd * HEAD_PAD + QK_NOPE] = (
                acc_ref[rs, c0:c0 + QK_NOPE] * (inv * g)).astype(q_ref.dtype)
            rot = (acc_ref[rs, c0 + QK_NOPE:c0 + HEAD_PAD] * cos_g
                   + acc_ref[rs, c0 + HEAD_PAD:c0 + HEAD_W] * sin_g)
            q_ref[rs, hd * HEAD_PAD + QK_NOPE:(hd + 1) * HEAD_PAD] = (rot * inv).astype(q_ref.dtype)


def _q_proj(qn, w_uq, l, g_nope, cos_g, sin_g, tm, heads, tiles_per_seq):
    rows = qn.shape[0]
    kern = functools.partial(_q_kernel, heads=heads, rb=min(64, tm))
    return pl.pallas_call(
        kern,
        grid=(rows // tm, N_HEADS // heads),
        in_specs=[pl.BlockSpec((tm, Q_LORA), lambda i, j: (i, 0)),
                  _layer_spec(l, (Q_LORA, heads * HEAD_W), lambda i, j: (0, j)),
                  pl.BlockSpec((1, QK_NOPE), lambda i, j: (0, 0)),
                  pl.BlockSpec((tm, 128), lambda i, j: (i % tiles_per_seq, 0)),
                  pl.BlockSpec((tm, 128), lambda i, j: (i % tiles_per_seq, 0))],
        out_specs=pl.BlockSpec((tm, heads * HEAD_PAD), lambda i, j: (i, j)),
        out_shape=jax.ShapeDtypeStruct((rows, N_HEADS * HEAD_PAD), BF),
        scratch_shapes=[pltpu.VMEM((tm, heads * HEAD_W), F32),
                        pltpu.VMEM((tm, heads * 128), F32)],
        name="q_proj",
        compiler_params=_params("parallel", "parallel"),
    )(qn, w_uq, g_nope, cos_g, sin_g)


def _kv_kernel(kvn_ref, wk_ref, wv_ref, kr_ref, gn_ref, cos_ref, sin_ref, k_ref, v_ref,
               acc_ref, ss_ref, *, heads, rb):
    kvn = kvn_ref[...]
    acc_ref[...] = _dot(kvn, wk_ref[...])
    v_ref[...] = _dot(kvn, wv_ref[...]).astype(v_ref.dtype)
    blk = (lax.broadcasted_iota(jnp.int32, (256, 256), 0) // 128
           == lax.broadcasted_iota(jnp.int32, (256, 256), 1) // 128)
    pair_ones = jnp.where(blk, 1.0, 0.0).astype(BF)
    for hd in range(0, heads, 2):
        a = acc_ref[:, hd * QK_NOPE:(hd + 2) * QK_NOPE]
        ss_ref[:, hd * 128:(hd + 2) * 128] = _dot((a * a).astype(BF), pair_ones)
    ss_ref[:, heads * 128:] = _row_sumsq(kr_ref[:, :128])
    gn = gn_ref[...]
    for r in range(0, acc_ref.shape[0], rb):
        rs = slice(r, r + rb)
        kr_rot = kr_ref[rs, :128] * cos_ref[rs, :] + kr_ref[rs, 128:] * sin_ref[rs, :]
        kr_ss = ss_ref[rs, heads * 128:]
        for hd in range(heads):
            ss = ss_ref[rs, hd * 128:(hd + 1) * 128] + kr_ss
            inv = lax.rsqrt(ss * (1.0 / QK_HEAD) + EPS)
            k_ref[rs, hd * HEAD_PAD:hd * HEAD_PAD + QK_NOPE] = (
                acc_ref[rs, hd * QK_NOPE:(hd + 1) * QK_NOPE] * (inv * gn)).astype(k_ref.dtype)
            k_ref[rs, hd * HEAD_PAD + QK_NOPE:(hd + 1) * HEAD_PAD] = (kr_rot * inv).astype(
                k_ref.dtype)


def _kv_proj(kvn, w_uk, w_uv, l, kr, gn, cos_t, sin_t, tm, heads, tiles_per_seq):
    rows = kvn.shape[0]
    kern = functools.partial(_kv_kernel, heads=heads, rb=min(64, tm))
    return pl.pallas_call(
        kern,
        grid=(rows // tm, N_HEADS // heads),
        in_specs=[pl.BlockSpec((tm, KV_LORA), lambda i, j: (i, 0)),
                  _layer_spec(l, (KV_LORA, heads * QK_NOPE), lambda i, j: (0, j)),
                  _layer_spec(l, (KV_LORA, heads * V_HEAD), lambda i, j: (0, j)),
                  pl.BlockSpec((tm, 256), lambda i, j: (i, 0)),
                  pl.BlockSpec((1, 128), lambda i, j: (0, 0)),
                  pl.BlockSpec((tm, 128), lambda i, j: (i % tiles_per_seq, 0)),
                  pl.BlockSpec((tm, 128), lambda i, j: (i % tiles_per_seq, 0))],
        out_specs=[pl.BlockSpec((tm, heads * HEAD_PAD), lambda i, j: (i, j)),
                   pl.BlockSpec((tm, heads * V_HEAD), lambda i, j: (i, j))],
        out_shape=[jax.ShapeDtypeStruct((rows, N_HEADS * HEAD_PAD), BF),
                   jax.ShapeDtypeStruct((rows, N_HEADS * V_HEAD), BF)],
        scratch_shapes=[pltpu.VMEM((tm, heads * QK_NOPE), F32),
                        pltpu.VMEM((tm, (heads + 1) * 128), F32)],
        name="kv_proj",
        compiler_params=_params("parallel", "parallel"),
    )(kvn, w_uk, w_uv, kr, gn, cos_t, sin_t)


def _attn_tile_kernel(q_ref, k_ref, v_ref, kp_ref, vp_ref, *rest, q_tile, tq, tc, td, rb,
                      n_prefix):
    _, o_ref, head_bufs = rest
    d0 = q_tile * tq
    items = [("full", c * tc, tc, 0) for c in range(d0 // tc)]
    items += [("diag", d0 + h * td, td, h * td) for h in range(tq // td)]
    items += [("prefix", 0, 128, 0)]

    def scores(hd, item, s_ref):
        kind, k0, width, r0 = item
        q = q_ref[r0:, hd * HEAD_PAD:(hd + 1) * HEAD_PAD]
        if kind == "prefix":
            keys = kp_ref[:, hd * HEAD_PAD:(hd + 1) * HEAD_PAD]
        else:
            keys = k_ref[k0:k0 + width, hd * HEAD_PAD:(hd + 1) * HEAD_PAD]
        s_ref[r0:, :width] = _dot_nt(q, keys)

    def softmax(item, s_ref, p_ref, a_ref, m_ref, l_ref):
        kind, k0, width, r0 = item
        for r in range(r0, tq, rb):
            rs = slice(r, r + rb)
            s = s_ref[rs, :width]
            col = lax.broadcasted_iota(jnp.int32, (rb, width), 1)
            if kind == "prefix":
                s = jnp.where(col < n_prefix, s, NEG_BIG)
            elif kind == "diag" and r < r0 + width:
                row = lax.broadcasted_iota(jnp.int32, (rb, width), 0) + (r - r0)
                s = jnp.where(col <= row, s, NEG_BIG)
            m_old = m_ref[rs, :]
            m_new = jnp.maximum(m_old, jnp.max(s, axis=-1, keepdims=True))
            alpha = jnp.exp2(m_old - m_new)
            p = jnp.exp2(s - jnp.concatenate([m_new] * (width // 128), axis=1))
            l_ref[rs, :] = alpha * l_ref[rs, :] + jnp.sum(p, axis=-1, keepdims=True)
            m_ref[rs, :] = m_new
            a_ref[rs, :] = alpha
            p_ref[rs, :width] = p.astype(p_ref.dtype)

    def accumulate(hd, item, p_ref, a_ref, acc_ref):
        kind, k0, width, r0 = item
        if kind == "prefix":
            values = vp_ref[:, hd * V_HEAD:(hd + 1) * V_HEAD]
        else:
            values = v_ref[k0:k0 + width, hd * V_HEAD:(hd + 1) * V_HEAD]
        acc_ref[r0:, :] = a_ref[r0:, :] * acc_ref[r0:, :] + _dot(p_ref[r0:, :width], values)

    for s_refs, p_refs, a_refs, m_ref, l_ref, acc_ref in head_bufs:
        m_ref[...] = jnp.full(m_ref.shape, NEG_BIG, F32)
        l_ref[...] = jnp.zeros(l_ref.shape, F32)
        acc_ref[...] = jnp.zeros(acc_ref.shape, F32)
    n = len(items)
    for t in range(n + 2):
        for hd, (s_refs, p_refs, a_refs, m_ref, l_ref, acc_ref) in enumerate(head_bufs):
            depth = len(s_refs)
            if t < n:
                scores(hd, items[t], s_refs[t % depth])
            if 1 <= t <= n:
                u = (t - 1) % depth
                softmax(items[t - 1], s_refs[u], p_refs[u], a_refs[u], m_ref, l_ref)
            if t >= 2:
                u = (t - 2) % depth
                accumulate(hd, items[t - 2], p_refs[u], a_refs[u], acc_ref)
    for hd, (_, _, _, _, l_ref, acc_ref) in enumerate(head_bufs):
        o_ref[:, hd * V_HEAD:(hd + 1) * V_HEAD] = (acc_ref[...] / l_ref[...]).astype(o_ref.dtype)


def _attention_tiles(q, k, v, kp, vp, batch, seq, tq, tc, td, heads):
    nq = seq // tq
    depth = 3
    stat = pltpu.VMEM((tq, 128), F32)
    head_bufs = [([pltpu.VMEM((tq, tc), F32)] * depth, [pltpu.VMEM((tq, tc), BF)] * depth,
                  [stat] * depth, stat, stat, pltpu.VMEM((tq, V_HEAD), F32))] * heads
    out = jnp.zeros((batch, seq, N_HEADS * V_HEAD), BF)
    for i in range(nq):
        keys = (i + 1) * tq
        kern = functools.partial(_attn_tile_kernel, q_tile=i, tq=tq, tc=tc, td=td, rb=64,
                                 n_prefix=N_META)
        out = pl.pallas_call(
            kern,
            grid=(batch, N_HEADS // heads),
            in_specs=[pl.BlockSpec((None, tq, heads * HEAD_PAD), lambda b, h, i=i: (b, i, h)),
                      pl.BlockSpec((None, keys, heads * HEAD_PAD), lambda b, h: (b, 0, h)),
                      pl.BlockSpec((None, keys, heads * V_HEAD), lambda b, h: (b, 0, h)),
                      pl.BlockSpec((128, heads * HEAD_PAD), lambda b, h: (0, h)),
                      pl.BlockSpec((128, heads * V_HEAD), lambda b, h: (0, h)),
                      pl.BlockSpec(memory_space=pl.ANY)],
            out_specs=pl.BlockSpec((None, tq, heads * V_HEAD), lambda b, h, i=i: (b, i, h)),
            out_shape=jax.ShapeDtypeStruct(out.shape, out.dtype),
            scratch_shapes=[head_bufs],
            input_output_aliases={5: 0},
            name=f"attention_q{i}",
            compiler_params=_params("parallel", "parallel"),
        )(q, k, v, kp, vp, out)
    return out


def _attn_meta_kernel(q_ref, k_ref, v_ref, o_ref):
    s = _dot_nt(q_ref[...], k_ref[...])
    row = lax.broadcasted_iota(jnp.int32, s.shape, 0)
    col = lax.broadcasted_iota(jnp.int32, s.shape, 1)
    s = jnp.where(row >= col, s, NEG_BIG)
    p = jnp.exp2(s - jnp.max(s, axis=-1, keepdims=True))
    l = jnp.sum(p, axis=-1, keepdims=True)
    o_ref[...] = (_dot(p.astype(BF), v_ref[...]) / l).astype(o_ref.dtype)


def _attention_meta(q, k, v):
    return pl.pallas_call(
        _attn_meta_kernel,
        grid=(N_HEADS,),
        in_specs=[pl.BlockSpec((N_META, HEAD_PAD), lambda h: (0, h)),
                  pl.BlockSpec((N_META, HEAD_PAD), lambda h: (0, h)),
                  pl.BlockSpec((N_META, V_HEAD), lambda h: (0, h))],
        out_specs=pl.BlockSpec((N_META, V_HEAD), lambda h: (0, h)),
        out_shape=jax.ShapeDtypeStruct((N_META, N_HEADS * V_HEAD), BF),
        name="attention_meta",
        compiler_params=_params("parallel"),
    )(q, k, v)


def _pool_branch_kernel(h_ref, w_ref, pw_ref, ps_ref, hist_ref, yc_ref, *rest,
                        tm, tiles_per_seq, pos_offset):
    carry_ref = rest[-1]
    i = pl.program_id(0)
    pin = _dot(h_ref[...], w_ref[...])

    @pl.when(i % tiles_per_seq == 0)
    def _():
        carry_ref[...] = hist_ref[...]

    ext = jnp.concatenate([carry_ref[...], pin], axis=0)
    tail = pin[tm - HIST:]
    carry_ref[...] = tail
    if len(rest) == 2:
        rest[0][...] = tail
    n_ext = tm + HIST
    seen = (lax.broadcasted_iota(jnp.int32, (tm, 1), 0)
            + ((i % tiles_per_seq) * tm + pos_offset + 1)).astype(F32)
    for g, w in enumerate(POOL_WINDOWS):
        xg = ext[:, g * POOL_GROUP:(g + 1) * POOL_GROUP]
        s, span = xg, 1
        while span < w:
            s = s[span:] + s[:s.shape[0] - span]
            span *= 2
        s = s[n_ext - (w - 1) - tm:]
        pooled = s / jnp.minimum(seen, float(w)) - xg[HIST:]
        mixed = _dot(pooled.astype(BF), pw_ref[g])
        yc_ref[:, g * POOL_GROUP:(g + 1) * POOL_GROUP] = (
            mixed * ps_ref[:, g * POOL_GROUP:(g + 1) * POOL_GROUP]).astype(yc_ref.dtype)


def _pool_branch(h, w_pool, pool_w, l, pool_scale, hist, tm, tiles_per_seq, pos_offset,
                 emit_hist):
    rows = h.shape[0]
    assert not emit_hist or rows == tm
    kern = functools.partial(_pool_branch_kernel, tm=tm, tiles_per_seq=tiles_per_seq,
                             pos_offset=pos_offset)
    ng = len(POOL_WINDOWS)
    n_out = 2 if emit_hist else 1
    return pl.pallas_call(
        kern,
        grid=(rows // tm,),
        in_specs=[pl.BlockSpec((tm, D_MODEL), lambda i: (i, 0)),
                  _layer_spec(l, (D_MODEL, D_POOL), lambda i: (0, 0)),
                  _layer_spec(l, (ng, POOL_GROUP, POOL_GROUP), lambda i: (0, 0, 0)),
                  pl.BlockSpec((1, D_POOL), lambda i: (0, 0)),
                  pl.BlockSpec((HIST, D_POOL), lambda i: (0, 0))],
        out_specs=[pl.BlockSpec((tm, D_POOL), lambda i: (i, 0)),
                   pl.BlockSpec((HIST, D_POOL), lambda i: (0, 0))][:n_out],
        out_shape=[jax.ShapeDtypeStruct((rows, D_POOL), BF),
                   jax.ShapeDtypeStruct((HIST, D_POOL), F32)][:n_out],
        scratch_shapes=[pltpu.VMEM((HIST, D_POOL), F32)],
        name="pool_branch",
        compiler_params=_params("arbitrary"),
    )(h, w_pool, pool_w, pool_scale, hist)


def _merge_kernel(h_ref, ya_ref, yb_ref, yc_ref, wg0_ref, wg1_ref, wg2_ref, wa_ref, wb_ref, wc_ref,
                  o_ref):
    h = h_ref[...]
    m = jax.nn.sigmoid(_dot(h, wg0_ref[...])) * _dot(ya_ref[...], wa_ref[...])
    m += jax.nn.sigmoid(_dot(h, wg1_ref[...])) * _dot(yb_ref[...], wb_ref[...])
    m += jax.nn.sigmoid(_dot(h, wg2_ref[...])) * _dot(yc_ref[...], wc_ref[...])
    o_ref[...] = m.astype(o_ref.dtype)


def _merge(h, ya, yb, yc, w_gate, wa, wb, wc, l, tm, tn):
    rows = h.shape[0]
    ncb = D_MODEL // tn
    row = lambda width: pl.BlockSpec((tm, width), lambda i, j: (i, 0))
    col = lambda depth, off: _layer_spec(l, (depth, tn), lambda i, j: (0, j + off))
    return pl.pallas_call(
        _merge_kernel,
        grid=(rows // tm, ncb),
        in_specs=[row(D_MODEL), row(D_CONV), row(N_HEADS * V_HEAD), row(D_POOL),
                  col(D_MODEL, 0), col(D_MODEL, ncb), col(D_MODEL, 2 * ncb),
                  col(D_CONV, 0), col(N_HEADS * V_HEAD, 0), col(D_POOL, 0)],
        out_specs=pl.BlockSpec((tm, tn), lambda i, j: (i, j)),
        out_shape=jax.ShapeDtypeStruct((rows, D_MODEL), BF),
        name="merge",
        compiler_params=_params("parallel", "arbitrary"),
    )(h, ya, yb, yc, w_gate, w_gate, w_gate, wa, wb, wc)


def _oproj_kernel(m_ref, w_ref, x_ref, g_ref, xo_ref, h2_ref):
    x = x_ref[...] + _dot(m_ref[...], w_ref[...])
    xo_ref[...] = x
    ms = jnp.mean(x * x, axis=-1, keepdims=True)
    h2_ref[...] = (x * lax.rsqrt(ms + EPS) * g_ref[...]).astype(h2_ref.dtype)


def _oproj(merged, w_o, l, x, g, tm):
    rows = x.shape[0]
    return pl.pallas_call(
        _oproj_kernel,
        grid=(rows // tm,),
        in_specs=[pl.BlockSpec((tm, D_MODEL), lambda i: (i, 0)),
                  _layer_spec(l, (D_MODEL, D_MODEL), lambda i: (0, 0)),
                  pl.BlockSpec((tm, D_MODEL), lambda i: (i, 0)),
                  pl.BlockSpec((1, D_MODEL), lambda i: (0, 0))],
        out_specs=[pl.BlockSpec((tm, D_MODEL), lambda i: (i, 0)),
                   pl.BlockSpec((tm, D_MODEL), lambda i: (i, 0))],
        out_shape=[jax.ShapeDtypeStruct((rows, D_MODEL), F32),
                   jax.ShapeDtypeStruct((rows, D_MODEL), BF)],
        name="oproj",
        compiler_params=_params("parallel"),
    )(merged, w_o, x, g)


def _mlp_kernel(h2_ref, wu_ref, wd_ref, x_ref, o_ref):
    @pl.when(pl.program_id(1) == 0)
    def _():
        o_ref[...] = x_ref[...]

    a = jnp.maximum(_dot(h2_ref[...], wu_ref[...]), 0.0)
    o_ref[...] += _dot((a * a).astype(BF), wd_ref[...])


def _mlp(h2, w_up, w_down, l, x, tm, tf):
    rows = x.shape[0]
    return pl.pallas_call(
        _mlp_kernel,
        grid=(rows // tm, D_FF // tf),
        in_specs=[pl.BlockSpec((tm, D_MODEL), lambda i, k: (i, 0)),
                  _layer_spec(l, (D_MODEL, tf), lambda i, k: (0, k)),
                  _layer_spec(l, (tf, D_MODEL), lambda i, k: (k, 0)),
                  pl.BlockSpec((tm, D_MODEL), lambda i, k: (i, 0))],
        out_specs=pl.BlockSpec((tm, D_MODEL), lambda i, k: (i, 0)),
        out_shape=jax.ShapeDtypeStruct((rows, D_MODEL), F32),
        name="mlp",
        compiler_params=_params("parallel", "arbitrary"),
    )(h2, w_up, w_down, x)


def _spread_rope(a):
    z = jnp.zeros(a.shape[:-1] + (HALF_ROPE,), a.dtype)
    return jnp.concatenate([a[..., :HALF_ROPE], z, a[..., HALF_ROPE:], z], axis=-1)


def _swap_halves(a):
    return jnp.concatenate([a[..., HALF_ROPE:], a[..., :HALF_ROPE]], axis=-1)


def _widen_head(a):
    rope = a[..., QK_NOPE:]
    return jnp.concatenate([a[..., :QK_NOPE], _spread_rope(rope), _spread_rope(_swap_halves(rope))],
                           axis=-1)


def _rope_gain_tables(gain, cos_t, sin_t):
    rope_gain = gain[QK_NOPE:]
    return (cos_t * _spread_rope(rope_gain)[None], sin_t * _spread_rope(_swap_halves(rope_gain))[None])


def _stacked_weights(w_in, w_uq, w_ukv, pool_w, w_branch_a, w_branch_b, w_branch_c, w_o, w_up,
                     w_down):
    depth = w_in.shape[0]
    wi = w_in.astype(BF)
    o_q = 3 * D_CONV
    o_kr = o_q + Q_LORA + KV_LORA
    o_pool = o_kr + QK_ROPE
    o_gate = o_pool + D_POOL
    w_kr = wi[:, :, o_kr:o_pool]
    w_ukv_h = w_ukv.astype(BF).reshape(depth, KV_LORA, N_HEADS, QK_NOPE + V_HEAD)
    return dict(
        w_in=wi,
        w_lat=jnp.concatenate([wi[:, :, o_q:o_kr], _spread_rope(w_kr),
                               _spread_rope(_swap_halves(w_kr))], axis=2),
        w_pool=wi[:, :, o_pool:o_gate],
        w_gate=wi[:, :, o_gate:],
        w_uq=_widen_head(w_uq.astype(BF).reshape(depth, Q_LORA, N_HEADS, QK_HEAD)).reshape(
            depth, Q_LORA, N_HEADS * HEAD_W),
        w_uk=w_ukv_h[..., :QK_NOPE].reshape(depth, KV_LORA, N_HEADS * QK_NOPE),
        w_uv=w_ukv_h[..., QK_NOPE:].reshape(depth, KV_LORA, N_HEADS * V_HEAD),
        pool_w=pool_w.astype(BF),
        wa=w_branch_a.astype(BF),
        wb=w_branch_b.astype(BF),
        wc=w_branch_c.astype(BF),
        w_o=w_o.astype(BF),
        w_up=w_up.astype(BF),
        w_down=w_down.astype(BF),
    )


def _layer_vectors(l, cos_t, sin_t, attn_norm, conv_w, q_lat_norm, kv_lat_norm, q_norm, k_norm,
                   pool_scale, mlp_norm):
    gq = q_norm[l] * (QK_HEAD ** -0.5 * LOG2_E)
    return dict(
        attn_norm=attn_norm[l][None],
        conv_w=conv_w[l],
        q_lat_norm=q_lat_norm[l][None],
        kv_lat_norm=kv_lat_norm[l][None],
        gq_nope=gq[None, :QK_NOPE],
        gk_nope=k_norm[l][None, :QK_NOPE],
        q_rope=_rope_gain_tables(gq, cos_t, sin_t),
        k_rope=_rope_gain_tables(k_norm[l], cos_t, sin_t),
        pool_scale=pool_scale[l][None],
        mlp_norm=mlp_norm[l][None],
    )


def _rope_tables(total):
    pos = jnp.arange(total, dtype=F32)
    inv = ROPE_THETA ** (-jnp.arange(0, QK_ROPE, 2, dtype=F32) / QK_ROPE)
    ang = pos[:, None] * inv[None, :]
    cos, sin = jnp.cos(ang), jnp.sin(ang)
    z = jnp.zeros_like(cos)
    return (jnp.concatenate([cos, z, cos, z], axis=1),
            jnp.concatenate([-sin, z, sin, z], axis=1))


def _layer(x, w, l, p, positions, prefix, *, batch, seq, finish):
    is_meta = prefix is None
    q_cos, q_sin = (t[positions] for t in p['q_rope'])
    k_cos, k_sin = (t[positions] for t in p['k_rope'])
    tm = min(512, seq)
    tps = seq // tm
    tm_big = min(1024, seq)
    if is_meta:
        conv_hist = jnp.zeros((HIST, D_CONV), F32)
        pool_hist = jnp.zeros((HIST, D_POOL), F32)
    else:
        conv_hist, pool_hist, kp, vp = prefix

    h = _rmsnorm(x, p['attn_norm'], tm)
    ya, *conv_tail = _conv_branch(h, w['w_in'], l, p['conv_w'], conv_hist, tm_big, 512,
                                  seq // tm_big, is_meta)
    yc, *pool_tail = _pool_branch(h, w['w_pool'], w['pool_w'], l, p['pool_scale'], pool_hist, tm,
                                  tps, 0 if is_meta else N_META, is_meta)
    qn, kvn, kr = _latent(h, w['w_lat'], l, p['q_lat_norm'], p['kv_lat_norm'], tm)
    k, v = _kv_proj(kvn, w['w_uk'], w['w_uv'], l, kr, p['gk_nope'], k_cos, k_sin, tm, 4, tps)
    out_prefix = (conv_tail[0], pool_tail[0], k, v) if is_meta else None
    if not finish:
        return None, out_prefix
    q = _q_proj(qn, w['w_uq'], l, p['gq_nope'], q_cos, q_sin, tm, 4, tps)
    if is_meta:
        yb = _attention_meta(q, k, v)
    else:
        pad = ((0, 128 - N_META), (0, 0))
        per_batch = lambda a: a.reshape(batch, seq, a.shape[-1])
        yb = _attention_tiles(per_batch(q), per_batch(k), per_batch(v), jnp.pad(kp, pad),
                              jnp.pad(vp, pad), batch, seq, 512, 512, 256, 2).reshape(
                                  batch * seq, -1)
    merged = _merge(h, ya, yb, yc, w['w_gate'], w['wa'], w['wb'], w['wc'], l, tm_big, 256)
    x_mid, h2 = _oproj(merged, w['w_o'], l, x, p['mlp_norm'], min(256, seq))
    return _mlp(h2, w['w_up'], w['w_down'], l, x_mid, tm_big, 512), out_prefix


def kernel(x, meta_tokens, attn_norm, w_in, conv_w, q_lat_norm, kv_lat_norm, w_uq, w_ukv, q_norm, k_norm, pool_w, pool_scale, w_branch_a, w_branch_b, w_branch_c, w_o, mlp_norm, w_up, w_down):
    batch, seq, d = x.shape
    depth = w_in.shape[0]
    cos_t, sin_t = _rope_tables(N_META + seq)
    xm = meta_tokens.astype(F32)
    xr = x.reshape(batch * seq, d)
    w = _stacked_weights(w_in, w_uq, w_ukv, pool_w, w_branch_a, w_branch_b, w_branch_c, w_o, w_up,
                         w_down)
    for l in range(depth):
        p = _layer_vectors(l, cos_t, sin_t, attn_norm, conv_w, q_lat_norm, kv_lat_norm, q_norm,
                           k_norm, pool_scale, mlp_norm)
        last = l == depth - 1
        xm, prefix = _layer(xm, w, l, p, slice(0, N_META), None, batch=1, seq=N_META,
                            finish=not last)
        xr, _ = _layer(xr, w, l, p, slice(N_META, None), prefix, batch=batch, seq=seq, finish=True)
    return xr.reshape(batch, seq, d)
```

```python
import functools

import jax
import jax.numpy as jnp
from jax import lax
from jax.experimental import pallas as pl
from jax.experimental.pallas import tpu as pltpu

D_MODEL = 2048
N_META = 16
EPS = 1e-6
D_CONV = 1024
N_HEADS = 16
QK_NOPE = 128
QK_ROPE = 64
QK_HEAD = QK_NOPE + QK_ROPE
HALF_ROPE = QK_ROPE // 2
V_HEAD = 128
Q_LORA = 512
KV_LORA = 512
ROPE_THETA = 10000.0
D_POOL = 1024
POOL_WINDOWS = (2, 4, 8, 16)
POOL_GROUP = D_POOL // len(POOL_WINDOWS)
D_FF = 4 * D_MODEL
HIST = 16
HEAD_PAD = 256
HEAD_W = HEAD_PAD + 128
D_LAT = Q_LORA + KV_LORA + 256
VMEM_LIMIT = 56 * 2**20
NEG_BIG = -1e30
LOG2_E = 1.4426950408889634

BF = jnp.bfloat16
F32 = jnp.float32


def _dot(a, b):
    return jnp.dot(a, b, preferred_element_type=F32)


def _dot_nt(a, b):
    return lax.dot_general(a, b, (((1,), (1,)), ((), ())), preferred_element_type=F32)


def _layer_spec(l, shape, index_map):
    return pl.BlockSpec((None,) + shape, lambda *g: (l,) + index_map(*g))


def _params(*sem):
    return pltpu.CompilerParams(dimension_semantics=sem, vmem_limit_bytes=VMEM_LIMIT)


def _rms_norm(x, g):
    ms = jnp.mean(x * x, axis=-1, keepdims=True)
    return x * lax.rsqrt(ms + EPS) * g


def _conv_branch_kernel(h_ref, wu_ref, wb_ref, wc_ref, cw_ref, hist_ref, ya_ref, *rest,
                        tm, tiles_per_seq):
    carry_ref = rest[-1]
    i, j = pl.program_id(0), pl.program_id(1)
    h = h_ref[...]
    cu = _dot(h, wc_ref[...]) * _dot(h, wu_ref[...])

    @pl.when(i % tiles_per_seq == 0)
    def _():
        carry_ref[j] = hist_ref[...]

    ext = jnp.concatenate([carry_ref[j], cu], axis=0)
    tail = cu[tm - HIST:]
    carry_ref[j] = tail
    if len(rest) == 2:
        rest[0][...] = tail
    cw = cw_ref[...]
    y = (cw[2:3] * cu + cw[1:2] * ext[HIST - 1:HIST - 1 + tm] + cw[0:1] * ext[HIST - 2:HIST - 2 + tm])
    ya_ref[...] = (_dot(h, wb_ref[...]) * y).astype(ya_ref.dtype)


def _conv_branch(h, w_in, l, conv_w, hist, tm, tc, tiles_per_seq, emit_hist):
    rows = h.shape[0]
    ncb = D_CONV // tc
    assert not emit_hist or rows == tm
    kern = functools.partial(_conv_branch_kernel, tm=tm, tiles_per_seq=tiles_per_seq)
    n_out = 2 if emit_hist else 1
    return pl.pallas_call(
        kern,
        grid=(rows // tm, ncb),
        in_specs=[pl.BlockSpec((tm, D_MODEL), lambda i, j: (i, 0)),
                  _layer_spec(l, (D_MODEL, tc), lambda i, j: (0, j)),
                  _layer_spec(l, (D_MODEL, tc), lambda i, j: (0, j + ncb)),
                  _layer_spec(l, (D_MODEL, tc), lambda i, j: (0, j + 2 * ncb)),
                  pl.BlockSpec((3, tc), lambda i, j: (0, j)),
                  pl.BlockSpec((HIST, tc), lambda i, j: (0, j))],
        out_specs=[pl.BlockSpec((tm, tc), lambda i, j: (i, j)),
                   pl.BlockSpec((HIST, tc), lambda i, j: (0, j))][:n_out],
        out_shape=[jax.ShapeDtypeStruct((rows, D_CONV), BF),
                   jax.ShapeDtypeStruct((HIST, D_CONV), F32)][:n_out],
        scratch_shapes=[pltpu.VMEM((ncb, HIST, tc), F32)],
        name="conv_branch",
        compiler_params=_params("arbitrary", "arbitrary"),
    )(h, w_in, w_in, w_in, conv_w, hist)


def _latent_kernel(x_ref, g_ref, w_ref, gq_ref, gkv_ref, h_ref, qn_ref, kvn_ref, kr_ref):
    h = _rms_norm(x_ref[...], g_ref[...]).astype(h_ref.dtype)
    h_ref[...] = h
    acc = _dot(h, w_ref[...])
    qn_ref[...] = _rms_norm(acc[:, :Q_LORA], gq_ref[...]).astype(qn_ref.dtype)
    kvn_ref[...] = _rms_norm(acc[:, Q_LORA:Q_LORA + KV_LORA], gkv_ref[...]).astype(kvn_ref.dtype)
    kr_ref[...] = acc[:, Q_LORA + KV_LORA:]


def _latent(x, g, w_lat, l, gq, gkv, tm):
    rows = x.shape[0]
    row = lambda width: pl.BlockSpec((tm, width), lambda i: (i, 0))
    vec = lambda width: pl.BlockSpec((1, width), lambda i: (0, 0))
    return pl.pallas_call(
        _latent_kernel,
        grid=(rows // tm,),
        in_specs=[row(D_MODEL), vec(D_MODEL), _layer_spec(l, (D_MODEL, D_LAT), lambda i: (0, 0)),
                  vec(Q_LORA), vec(KV_LORA)],
        out_specs=[row(D_MODEL), row(Q_LORA), row(KV_LORA), row(256)],
        out_shape=[jax.ShapeDtypeStruct((rows, D_MODEL), BF),
                   jax.ShapeDtypeStruct((rows, Q_LORA), BF),
                   jax.ShapeDtypeStruct((rows, KV_LORA), BF),
                   jax.ShapeDtypeStruct((rows, 256), F32)],
        name="latent",
        compiler_params=_params("parallel"),
    )(x, g, w_lat, gq, gkv)


def _row_sumsq(a):
    return _dot((a * a).astype(BF), jnp.ones((a.shape[1], 128), BF))


def _q_kernel(qn_ref, w_ref, g_ref, cos_ref, sin_ref, q_ref, acc_ref, ss_ref, *, heads, rb):
    acc_ref[...] = _dot(qn_ref[...], w_ref[...])
    for hd in range(heads):
        ss_ref[:, hd * 128:(hd + 1) * 128] = _row_sumsq(
            acc_ref[:, hd * HEAD_W:hd * HEAD_W + HEAD_PAD])
    g = g_ref[...]
    for r in range(0, acc_ref.shape[0], rb):
        rs = slice(r, r + rb)
        cos_g, sin_g = cos_ref[rs, :], sin_ref[rs, :]
        for hd in range(heads):
            c0 = hd * HEAD_W
            inv = lax.rsqrt(ss_ref[rs, hd * 128:(hd + 1) * 128] * (1.0 / QK_HEAD) + EPS)
            q_ref[rs, hd * HEAD_PAD:hd * HEAD_PAD + QK_NOPE] = (
                acc_ref[rs, c0:c0 + QK_NOPE] * (inv * g)).astype(q_ref.dtype)
            rot = (acc_ref[rs, c0 + QK_NOPE:c0 + HEAD_PAD] * cos_g
                   + acc_ref[rs, c0 + HEAD_PAD:c0 + HEAD_W] * sin_g)
            q_ref[rs, hd * HEAD_PAD + QK_NOPE:(hd + 1) * HEAD_PAD] = (rot * inv).astype(q_ref.dtype)


def _q_proj(qn, w_uq, l, g_nope, cos_g, sin_g, tm, heads, tiles_per_seq):
    rows = qn.shape[0]
    kern = functools.partial(_q_kernel, heads=heads, rb=min(64, tm))
    return pl.pallas_call(
        kern,
        grid=(rows // tm, N_HEADS // heads),
        in_specs=[pl.BlockSpec((tm, Q_LORA), lambda i, j: (i, 0)),
                  _layer_spec(l, (Q_LORA, heads * HEAD_W), lambda i, j: (0, j)),
                  pl.BlockSpec((1, QK_NOPE), lambda i, j: (0, 0)),
                  pl.BlockSpec((tm, 128), lambda i, j: (i % tiles_per_seq, 0)),
                  pl.BlockSpec((tm, 128), lambda i, j: (i % tiles_per_seq, 0))],
        out_specs=pl.BlockSpec((tm, heads * HEAD_PAD), lambda i, j: (i, j)),
        out_shape=jax.ShapeDtypeStruct((rows, N_HEADS * HEAD_PAD), BF),
        scratch_shapes=[pltpu.VMEM((tm, heads * HEAD_W), F32),
                        pltpu.VMEM((tm, heads * 128), F32)],
        name="q_proj",
        compiler_params=_params("parallel", "parallel"),
    )(qn, w_uq, g_nope, cos_g, sin_g)


def _kv_kernel(kvn_ref, wk_ref, wv_ref, kr_ref, gn_ref, cos_ref, sin_ref, k_ref, v_ref,
               acc_ref, ss_ref, *, heads, rb):
    kvn = kvn_ref[...]
    acc_ref[...] = _dot(kvn, wk_ref[...])
    v_ref[...] = _dot(kvn, wv_ref[...]).astype(v_ref.dtype)
    blk = (lax.broadcasted_iota(jnp.int32, (256, 256), 0) // 128
           == lax.broadcasted_iota(jnp.int32, (256, 256), 1) // 128)
    pair_ones = jnp.where(blk, 1.0, 0.0).astype(BF)
    for hd in range(0, heads, 2):
        a = acc_ref[:, hd * QK_NOPE:(hd + 2) * QK_NOPE]
        ss_ref[:, hd * 128:(hd + 2) * 128] = _dot((a * a).astype(BF), pair_ones)
    ss_ref[:, heads * 128:] = _row_sumsq(kr_ref[:, :128])
    gn = gn_ref[...]
    for r in range(0, acc_ref.shape[0], rb):
        rs = slice(r, r + rb)
        kr_rot = kr_ref[rs, :128] * cos_ref[rs, :] + kr_ref[rs, 128:] * sin_ref[rs, :]
        kr_ss = ss_ref[rs, heads * 128:]
        for hd in range(heads):
            ss = ss_ref[rs, hd * 128:(hd + 1) * 128] + kr_ss
            inv = lax.rsqrt(ss * (1.0 / QK_HEAD) + EPS)
            k_ref[rs, hd * HEAD_PAD:hd * HEAD_PAD + QK_NOPE] = (
                acc_ref[rs, hd * QK_NOPE:(hd + 1) * QK_NOPE] * (inv * gn)).astype(k_ref.dtype)
            k_ref[rs, hd * HEAD_PAD + QK_NOPE:(hd + 1) * HEAD_PAD] = (kr_rot * inv).astype(
                k_ref.dtype)


def _kv_proj(kvn, w_uk, w_uv, l, kr, gn, cos_t, sin_t, tm, heads, tiles_per_seq):
    rows = kvn.shape[0]
    kern = functools.partial(_kv_kernel, heads=heads, rb=min(64, tm))
    return pl.pallas_call(
        kern,
        grid=(rows // tm, N_HEADS // heads),
        in_specs=[pl.BlockSpec((tm, KV_LORA), lambda i, j: (i, 0)),
                  _layer_spec(l, (KV_LORA, heads * QK_NOPE), lambda i, j: (0, j)),
                  _layer_spec(l, (KV_LORA, heads * V_HEAD), lambda i, j: (0, j)),
                  pl.BlockSpec((tm, 256), lambda i, j: (i, 0)),
                  pl.BlockSpec((1, 128), lambda i, j: (0, 0)),
                  pl.BlockSpec((tm, 128), lambda i, j: (i % tiles_per_seq, 0)),
                  pl.BlockSpec((tm, 128), lambda i, j: (i % tiles_per_seq, 0))],
        out_specs=[pl.BlockSpec((tm, heads * HEAD_PAD), lambda i, j: (i, j)),
                   pl.BlockSpec((tm, heads * V_HEAD), lambda i, j: (i, j))],
        out_shape=[jax.ShapeDtypeStruct((rows, N_HEADS * HEAD_PAD), BF),
                   jax.ShapeDtypeStruct((rows, N_HEADS * V_HEAD), BF)],
        scratch_shapes=[pltpu.VMEM((tm, heads * QK_NOPE), F32),
                        pltpu.VMEM((tm, (heads + 1) * 128), F32)],
        name="kv_proj",
        compiler_params=_params("parallel", "parallel"),
    )(kvn, w_uk, w_uv, kr, gn, cos_t, sin_t)


def _attn_tile_kernel(q_ref, k_ref, v_ref, kp_ref, vp_ref, *rest, q_tile, tq, tc, td, rb,
                      n_prefix):
    _, o_ref, head_bufs = rest
    d0 = q_tile * tq
    items = [("full", c * tc, tc, 0) for c in range(d0 // tc)]
    items += [("diag", d0 + h * td, td, h * td) for h in range(tq // td)]
    items += [("prefix", 0, 128, 0)]

    def scores(hd, item, s_ref):
        kind, k0, width, r0 = item
        q = q_ref[r0:, hd * HEAD_PAD:(hd + 1) * HEAD_PAD]
        if kind == "prefix":
            keys = kp_ref[:, hd * HEAD_PAD:(hd + 1) * HEAD_PAD]
        else:
            keys = k_ref[k0:k0 + width, hd * HEAD_PAD:(hd + 1) * HEAD_PAD]
        s_ref[r0:, :width] = _dot_nt(q, keys)

    def softmax(item, s_ref, p_ref, a_ref, m_ref, l_ref):
        kind, k0, width, r0 = item
        for r in range(r0, tq, rb):
            rs = slice(r, r + rb)
            s = s_ref[rs, :width]
            col = lax.broadcasted_iota(jnp.int32, (rb, width), 1)
            if kind == "prefix":
                s = jnp.where(col < n_prefix, s, NEG_BIG)
            elif kind == "diag" and r < r0 + width:
                row = lax.broadcasted_iota(jnp.int32, (rb, width), 0) + (r - r0)
                s = jnp.where(col <= row, s, NEG_BIG)
            m_old = m_ref[rs, :]
            m_new = jnp.maximum(m_old, jnp.max(s, axis=-1, keepdims=True))
            alpha = jnp.exp2(m_old - m_new)
            p = jnp.exp2(s - jnp.concatenate([m_new] * (width // 128), axis=1))
            l_ref[rs, :] = alpha * l_ref[rs, :] + jnp.sum(p, axis=-1, keepdims=True)
            m_ref[rs, :] = m_new
            a_ref[rs, :] = alpha
            p_ref[rs, :width] = p.astype(p_ref.dtype)

    def accumulate(hd, item, p_ref, a_ref, acc_ref):
        kind, k0, width, r0 = item
        if kind == "prefix":
            values = vp_ref[:, hd * V_HEAD:(hd + 1) * V_HEAD]
        else:
            values = v_ref[k0:k0 + width, hd * V_HEAD:(hd + 1) * V_HEAD]
        acc_ref[r0:, :] = a_ref[r0:, :] * acc_ref[r0:, :] + _dot(p_ref[r0:, :width], values)

    for s_refs, p_refs, a_refs, m_ref, l_ref, acc_ref in head_bufs:
        m_ref[...] = jnp.full(m_ref.shape, NEG_BIG, F32)
        l_ref[...] = jnp.zeros(l_ref.shape, F32)
        acc_ref[...] = jnp.zeros(acc_ref.shape, F32)
    n = len(items)
    for t in range(n + 2):
        for hd, (s_refs, p_refs, a_refs, m_ref, l_ref, acc_ref) in enumerate(head_bufs):
            depth = len(s_refs)
            if t < n:
                scores(hd, items[t], s_refs[t % depth])
            if 1 <= t <= n:
                u = (t - 1) % depth
                softmax(items[t - 1], s_refs[u], p_refs[u], a_refs[u], m_ref, l_ref)
            if t >= 2:
                u = (t - 2) % depth
                accumulate(hd, items[t - 2], p_refs[u], a_refs[u], acc_ref)
    for hd, (_, _, _, _, l_ref, acc_ref) in enumerate(head_bufs):
        o_ref[:, hd * V_HEAD:(hd + 1) * V_HEAD] = (acc_ref[...] / l_ref[...]).astype(o_ref.dtype)


def _attention_tiles(q, k, v, kp, vp, batch, seq, tq, tc, td, heads):
    nq = seq // tq
    depth = 3
    stat = pltpu.VMEM((tq, 128), F32)
    head_bufs = [([pltpu.VMEM((tq, tc), F32)] * depth, [pltpu.VMEM((tq, tc), BF)] * depth,
                  [stat] * depth, stat, stat, pltpu.VMEM((tq, V_HEAD), F32))] * heads
    out = jnp.zeros((batch, seq, N_HEADS * V_HEAD), BF)
    for i in range(nq):
        keys = (i + 1) * tq
        kern = functools.partial(_attn_tile_kernel, q_tile=i, tq=tq, tc=tc, td=td, rb=64,
                                 n_prefix=N_META)
        out = pl.pallas_call(
            kern,
            grid=(batch, N_HEADS // heads),
            in_specs=[pl.BlockSpec((None, tq, heads * HEAD_PAD), lambda b, h, i=i: (b, i, h)),
                      pl.BlockSpec((None, keys, heads * HEAD_PAD), lambda b, h: (b, 0, h)),
                      pl.BlockSpec((None, keys, heads * V_HEAD), lambda b, h: (b, 0, h)),
                      pl.BlockSpec((128, heads * HEAD_PAD), lambda b, h: (0, h)),
                      pl.BlockSpec((128, heads * V_HEAD), lambda b, h: (0, h)),
                      pl.BlockSpec(memory_space=pl.ANY)],
            out_specs=pl.BlockSpec((None, tq, heads * V_HEAD), lambda b, h, i=i: (b, i, h)),
            out_shape=jax.ShapeDtypeStruct(out.shape, out.dtype),
            scratch_shapes=[head_bufs],
            input_output_aliases={5: 0},
            name=f"attention_q{i}",
            compiler_params=_params("parallel", "parallel"),
        )(q, k, v, kp, vp, out)
    return out


def _attn_meta_kernel(q_ref, k_ref, v_ref, o_ref):
    s = _dot_nt(q_ref[...], k_ref[...])
    row = lax.broadcasted_iota(jnp.int32, s.shape, 0)
    col = lax.broadcasted_iota(jnp.int32, s.shape, 1)
    s = jnp.where(row >= col, s, NEG_BIG)
    p = jnp.exp2(s - jnp.max(s, axis=-1, keepdims=True))
    l = jnp.sum(p, axis=-1, keepdims=True)
    o_ref[...] = (_dot(p.astype(BF), v_ref[...]) / l).astype(o_ref.dtype)


def _attention_meta(q, k, v):
    return pl.pallas_call(
        _attn_meta_kernel,
        grid=(N_HEADS,),
        in_specs=[pl.BlockSpec((N_META, HEAD_PAD), lambda h: (0, h)),
                  pl.BlockSpec((N_META, HEAD_PAD), lambda h: (0, h)),
                  pl.BlockSpec((N_META, V_HEAD), lambda h: (0, h))],
        out_specs=pl.BlockSpec((N_META, V_HEAD), lambda h: (0, h)),
        out_shape=jax.ShapeDtypeStruct((N_META, N_HEADS * V_HEAD), BF),
        name="attention_meta",
        compiler_params=_params("parallel"),
    )(q, k, v)


def _pool_branch_kernel(h_ref, w_ref, pw_ref, ps_ref, hist_ref, yc_ref, *rest,
                        tm, tiles_per_seq, pos_offset):
    carry_ref = rest[-1]
    i = pl.program_id(0)
    pin = _dot(h_ref[...], w_ref[...])

    @pl.when(i % tiles_per_seq == 0)
    def _():
        carry_ref[...] = hist_ref[...]

    ext = jnp.concatenate([carry_ref[...], pin], axis=0)
    tail = pin[tm - HIST:]
    carry_ref[...] = tail
    if len(rest) == 2:
        rest[0][...] = tail
    n_ext = tm + HIST
    seen = (lax.broadcasted_iota(jnp.int32, (tm, 1), 0)
            + ((i % tiles_per_seq) * tm + pos_offset + 1)).astype(F32)
    for g, w in enumerate(POOL_WINDOWS):
        xg = ext[:, g * POOL_GROUP:(g + 1) * POOL_GROUP]
        s, span = xg, 1
        while span < w:
            s = s[span:] + s[:s.shape[0] - span]
            span *= 2
        s = s[n_ext - (w - 1) - tm:]
        pooled = s / jnp.minimum(seen, float(w)) - xg[HIST:]
        mixed = _dot(pooled.astype(BF), pw_ref[g])
        yc_ref[:, g * POOL_GROUP:(g + 1) * POOL_GROUP] = (
            mixed * ps_ref[:, g * POOL_GROUP:(g + 1) * POOL_GROUP]).astype(yc_ref.dtype)


def _pool_branch(h, w_pool, pool_w, l, pool_scale, hist, tm, tiles_per_seq, pos_offset,
                 emit_hist):
    rows = h.shape[0]
    assert not emit_hist or rows == tm
    kern = functools.partial(_pool_branch_kernel, tm=tm, tiles_per_seq=tiles_per_seq,
                             pos_offset=pos_offset)
    ng = len(POOL_WINDOWS)
    n_out = 2 if emit_hist else 1
    return pl.pallas_call(
        kern,
        grid=(rows // tm,),
        in_specs=[pl.BlockSpec((tm, D_MODEL), lambda i: (i, 0)),
                  _layer_spec(l, (D_MODEL, D_POOL), lambda i: (0, 0)),
                  _layer_spec(l, (ng, POOL_GROUP, POOL_GROUP), lambda i: (0, 0, 0)),
                  pl.BlockSpec((1, D_POOL), lambda i: (0, 0)),
                  pl.BlockSpec((HIST, D_POOL), lambda i: (0, 0))],
        out_specs=[pl.BlockSpec((tm, D_POOL), lambda i: (i, 0)),
                   pl.BlockSpec((HIST, D_POOL), lambda i: (0, 0))][:n_out],
        out_shape=[jax.ShapeDtypeStruct((rows, D_POOL), BF),
                   jax.ShapeDtypeStruct((HIST, D_POOL), F32)][:n_out],
        scratch_shapes=[pltpu.VMEM((HIST, D_POOL), F32)],
        name="pool_branch",
        compiler_params=_params("arbitrary"),
    )(h, w_pool, pool_w, pool_scale, hist)


def _merge_kernel(h_ref, ya_ref, yb_ref, yc_ref, wg0_ref, wg1_ref, wg2_ref, wa_ref, wb_ref, wc_ref,
                  o_ref):
    h = h_ref[...]
    m = jax.nn.sigmoid(_dot(h, wg0_ref[...])) * _dot(ya_ref[...], wa_ref[...])
    m += jax.nn.sigmoid(_dot(h, wg1_ref[...])) * _dot(yb_ref[...], wb_ref[...])
    m += jax.nn.sigmoid(_dot(h, wg2_ref[...])) * _dot(yc_ref[...], wc_ref[...])
    o_ref[...] = m.astype(o_ref.dtype)


def _merge(h, ya, yb, yc, w_gate, wa, wb, wc, l, tm, tn):
    rows = h.shape[0]
    ncb = D_MODEL // tn
    row = lambda width: pl.BlockSpec((tm, width), lambda i, j: (i, 0))
    col = lambda depth, off: _layer_spec(l, (depth, tn), lambda i, j: (0, j + off))
    return pl.pallas_call(
        _merge_kernel,
        grid=(rows // tm, ncb),
        in_specs=[row(D_MODEL), row(D_CONV), row(N_HEADS * V_HEAD), row(D_POOL),
                  col(D_MODEL, 0), col(D_MODEL, ncb), col(D_MODEL, 2 * ncb),
                  col(D_CONV, 0), col(N_HEADS * V_HEAD, 0), col(D_POOL, 0)],
        out_specs=pl.BlockSpec((tm, tn), lambda i, j: (i, j)),
        out_shape=jax.ShapeDtypeStruct((rows, D_MODEL), BF),
        name="merge",
        compiler_params=_params("parallel", "arbitrary"),
    )(h, ya, yb, yc, w_gate, w_gate, w_gate, wa, wb, wc)


def _oproj_kernel(m_ref, w_ref, x_ref, g_ref, xo_ref, h2_ref):
    x = x_ref[...] + _dot(m_ref[...], w_ref[...])
    xo_ref[...] = x
    h2_ref[...] = _rms_norm(x, g_ref[...]).astype(h2_ref.dtype)


def _oproj(merged, w_o, l, x, g, tm):
    rows = x.shape[0]
    return pl.pallas_call(
        _oproj_kernel,
        grid=(rows // tm,),
        in_specs=[pl.BlockSpec((tm, D_MODEL), lambda i: (i, 0)),
                  _layer_spec(l, (D_MODEL, D_MODEL), lambda i: (0, 0)),
                  pl.BlockSpec((tm, D_MODEL), lambda i: (i, 0)),
                  pl.BlockSpec((1, D_MODEL), lambda i: (0, 0))],
        out_specs=[pl.BlockSpec((tm, D_MODEL), lambda i: (i, 0)),
                   pl.BlockSpec((tm, D_MODEL), lambda i: (i, 0))],
        out_shape=[jax.ShapeDtypeStruct((rows, D_MODEL), F32),
                   jax.ShapeDtypeStruct((rows, D_MODEL), BF)],
        name="oproj",
        compiler_params=_params("parallel"),
    )(merged, w_o, x, g)


def _mlp_kernel(h2_ref, wu_ref, wd_ref, x_ref, o_ref):
    @pl.when(pl.program_id(1) == 0)
    def _():
        o_ref[...] = x_ref[...]

    a = jnp.maximum(_dot(h2_ref[...], wu_ref[...]), 0.0)
    o_ref[...] += _dot((a * a).astype(BF), wd_ref[...])


def _mlp(h2, w_up, w_down, l, x, tm, tf):
    rows = x.shape[0]
    return pl.pallas_call(
        _mlp_kernel,
        grid=(rows // tm, D_FF // tf),
        in_specs=[pl.BlockSpec((tm, D_MODEL), lambda i, k: (i, 0)),
                  _layer_spec(l, (D_MODEL, tf), lambda i, k: (0, k)),
                  _layer_spec(l, (tf, D_MODEL), lambda i, k: (k, 0)),
                  pl.BlockSpec((tm, D_MODEL), lambda i, k: (i, 0))],
        out_specs=pl.BlockSpec((tm, D_MODEL), lambda i, k: (i, 0)),
        out_shape=jax.ShapeDtypeStruct((rows, D_MODEL), F32),
        name="mlp",
        compiler_params=_params("parallel", "arbitrary"),
    )(h2, w_up, w_down, x)


def _spread_rope(a):
    z = jnp.zeros(a.shape[:-1] + (HALF_ROPE,), a.dtype)
    return jnp.concatenate([a[..., :HALF_ROPE], z, a[..., HALF_ROPE:], z], axis=-1)


def _swap_halves(a):
    return jnp.concatenate([a[..., HALF_ROPE:], a[..., :HALF_ROPE]], axis=-1)


def _widen_head(a):
    rope = a[..., QK_NOPE:]
    return jnp.concatenate([a[..., :QK_NOPE], _spread_rope(rope), _spread_rope(_swap_halves(rope))],
                           axis=-1)


def _rope_gain_tables(gain, cos_t, sin_t):
    rope_gain = gain[QK_NOPE:]
    return (cos_t * _spread_rope(rope_gain)[None], sin_t * _spread_rope(_swap_halves(rope_gain))[None])


def _stacked_weights(w_in, w_uq, w_ukv, pool_w, w_branch_a, w_branch_b, w_branch_c, w_o, w_up,
                     w_down):
    depth = w_in.shape[0]
    wi = w_in.astype(BF)
    o_q = 3 * D_CONV
    o_kr = o_q + Q_LORA + KV_LORA
    o_pool = o_kr + QK_ROPE
    o_gate = o_pool + D_POOL
    w_kr = wi[:, :, o_kr:o_pool]
    w_ukv_h = w_ukv.astype(BF).reshape(depth, KV_LORA, N_HEADS, QK_NOPE + V_HEAD)
    return dict(
        w_in=wi,
        w_lat=jnp.concatenate([wi[:, :, o_q:o_kr], _spread_rope(w_kr),
                               _spread_rope(_swap_halves(w_kr))], axis=2),
        w_pool=wi[:, :, o_pool:o_gate],
        w_gate=wi[:, :, o_gate:],
        w_uq=_widen_head(w_uq.astype(BF).reshape(depth, Q_LORA, N_HEADS, QK_HEAD)).reshape(
            depth, Q_LORA, N_HEADS * HEAD_W),
        w_uk=w_ukv_h[..., :QK_NOPE].reshape(depth, KV_LORA, N_HEADS * QK_NOPE),
        w_uv=w_ukv_h[..., QK_NOPE:].reshape(depth, KV_LORA, N_HEADS * V_HEAD),
        pool_w=pool_w.astype(BF),
        wa=w_branch_a.astype(BF),
        wb=w_branch_b.astype(BF),
        wc=w_branch_c.astype(BF),
        w_o=w_o.astype(BF),
        w_up=w_up.astype(BF),
        w_down=w_down.astype(BF),
    )


def _layer_vectors(l, cos_t, sin_t, attn_norm, conv_w, q_lat_norm, kv_lat_norm, q_norm, k_norm,
                   pool_scale, mlp_norm):
    gq = q_norm[l] * (QK_HEAD ** -0.5 * LOG2_E)
    return dict(
        attn_norm=attn_norm[l][None],
        conv_w=conv_w[l],
        q_lat_norm=q_lat_norm[l][None],
        kv_lat_norm=kv_lat_norm[l][None],
        gq_nope=gq[None, :QK_NOPE],
        gk_nope=k_norm[l][None, :QK_NOPE],
        q_rope=_rope_gain_tables(gq, cos_t, sin_t),
        k_rope=_rope_gain_tables(k_norm[l], cos_t, sin_t),
        pool_scale=pool_scale[l][None],
        mlp_norm=mlp_norm[l][None],
    )


def _rope_tables(total):
    pos = jnp.arange(total, dtype=F32)
    inv = ROPE_THETA ** (-jnp.arange(0, QK_ROPE, 2, dtype=F32) / QK_ROPE)
    ang = pos[:, None] * inv[None, :]
    cos, sin = jnp.cos(ang), jnp.sin(ang)
    z = jnp.zeros_like(cos)
    return (jnp.concatenate([cos, z, cos, z], axis=1),
            jnp.concatenate([-sin, z, sin, z], axis=1))


def _layer(x, w, l, p, positions, prefix, *, batch, seq, finish):
    is_meta = prefix is None
    q_cos, q_sin = (t[positions] for t in p['q_rope'])
    k_cos, k_sin = (t[positions] for t in p['k_rope'])
    tm = min(512, seq)
    tps = seq // tm
    tm_big = min(1024, seq)
    if is_meta:
        conv_hist = jnp.zeros((HIST, D_CONV), F32)
        pool_hist = jnp.zeros((HIST, D_POOL), F32)
    else:
        conv_hist, pool_hist, kp, vp = prefix

    h, qn, kvn, kr = _latent(x, p['attn_norm'], w['w_lat'], l, p['q_lat_norm'], p['kv_lat_norm'],
                             tm)
    ya, *conv_tail = _conv_branch(h, w['w_in'], l, p['conv_w'], conv_hist, tm_big, 512,
                                  seq // tm_big, is_meta)
    yc, *pool_tail = _pool_branch(h, w['w_pool'], w['pool_w'], l, p['pool_scale'], pool_hist, tm,
                                  tps, 0 if is_meta else N_META, is_meta)
    k, v = _kv_proj(kvn, w['w_uk'], w['w_uv'], l, kr, p['gk_nope'], k_cos, k_sin, tm, N_HEADS,
                    tps)
    out_prefix = (conv_tail[0], pool_tail[0], k, v) if is_meta else None
    if not finish:
        return None, out_prefix
    q = _q_proj(qn, w['w_uq'], l, p['gq_nope'], q_cos, q_sin, tm, N_HEADS, tps)
    if is_meta:
        yb = _attention_meta(q, k, v)
    else:
        pad = ((0, 128 - N_META), (0, 0))
        per_batch = lambda a: a.reshape(batch, seq, a.shape[-1])
        yb = _attention_tiles(per_batch(q), per_batch(k), per_batch(v), jnp.pad(kp, pad),
                              jnp.pad(vp, pad), batch, seq, 512, 512, 256, 2).reshape(
                                  batch * seq, -1)
    merged = _merge(h, ya, yb, yc, w['w_gate'], w['wa'], w['wb'], w['wc'], l, tm_big, 256)
    x_mid, h2 = _oproj(merged, w['w_o'], l, x, p['mlp_norm'], min(256, seq))
    return _mlp(h2, w['w_up'], w['w_down'], l, x_mid, tm_big, 512), out_prefix


def kernel(x, meta_tokens, attn_norm, w_in, conv_w, q_lat_norm, kv_lat_norm, w_uq, w_ukv, q_norm, k_norm, pool_w, pool_scale, w_branch_a, w_branch_b, w_branch_c, w_o, mlp_norm, w_up, w_down):
    batch, seq, d = x.shape
    depth = w_in.shape[0]
    cos_t, sin_t = _rope_tables(N_META + seq)
    xm = meta_tokens.astype(F32)
    xr = x.reshape(batch * seq, d)
    w = _stacked_weights(w_in, w_uq, w_ukv, pool_w, w_branch_a, w_branch_b, w_branch_c, w_o, w_up,
                         w_down)
    for l in range(depth):
        p = _layer_vectors(l, cos_t, sin_t, attn_norm, conv_w, q_lat_norm, kv_lat_norm, q_norm,
                           k_norm, pool_scale, mlp_norm)
        last = l == depth - 1
        xm, prefix = _layer(xm, w, l, p, slice(0, N_META), None, batch=1, seq=N_META,
                            finish=not last)
        xr, _ = _layer(xr, w, l, p, slice(N_META, None), prefix, batch=batch, seq=seq, finish=True)
    return xr.reshape(batch, seq, d)
```

```python
import functools

import jax
import jax.numpy as jnp
from jax import lax
from jax.experimental import pallas as pl
from jax.experimental.pallas import tpu as pltpu

D_MODEL = 2048
N_META = 16
EPS = 1e-6
D_CONV = 1024
N_HEADS = 16
QK_NOPE = 128
QK_ROPE = 64
QK_HEAD = QK_NOPE + QK_ROPE
HALF_ROPE = QK_ROPE // 2
V_HEAD = 128
Q_LORA = 512
KV_LORA = 512
ROPE_THETA = 10000.0
D_POOL = 1024
POOL_WINDOWS = (2, 4, 8, 16)
POOL_GROUP = D_POOL // len(POOL_WINDOWS)
D_FF = 4 * D_MODEL
HIST = 16
HEAD_PAD = 256
D_LAT = Q_LORA + KV_LORA + 256
VMEM_LIMIT = 56 * 2**20
NEG_BIG = -1e30
LOG2_E = 1.4426950408889634

BF = jnp.bfloat16
F32 = jnp.float32


def _dot(a, b):
    return jnp.dot(a, b, preferred_element_type=F32)


def _dot_nt(a, b):
    return lax.dot_general(a, b, (((1,), (1,)), ((), ())), preferred_element_type=F32)


def _layer_spec(l, shape, index_map):
    return pl.BlockSpec((None,) + shape, lambda *g: (l,) + index_map(*g))


def _params(*sem):
    return pltpu.CompilerParams(dimension_semantics=sem, vmem_limit_bytes=VMEM_LIMIT)


def _rms_norm(x, g):
    ms = jnp.mean(x * x, axis=-1, keepdims=True)
    return x * lax.rsqrt(ms + EPS) * g


def _conv_branch_kernel(h_ref, wu_ref, wb_ref, wc_ref, cw_ref, hist_ref, ya_ref, *rest,
                        tm, tiles_per_seq):
    carry_ref = rest[-1]
    i, j = pl.program_id(0), pl.program_id(1)
    h = h_ref[...]
    cu = _dot(h, wc_ref[...]) * _dot(h, wu_ref[...])

    @pl.when(i % tiles_per_seq == 0)
    def _():
        carry_ref[j] = hist_ref[...]

    ext = jnp.concatenate([carry_ref[j], cu], axis=0)
    tail = cu[tm - HIST:]
    carry_ref[j] = tail
    if len(rest) == 2:
        rest[0][...] = tail
    cw = cw_ref[...]
    y = (cw[2:3] * cu + cw[1:2] * ext[HIST - 1:HIST - 1 + tm] + cw[0:1] * ext[HIST - 2:HIST - 2 + tm])
    ya_ref[...] = (_dot(h, wb_ref[...]) * y).astype(ya_ref.dtype)


def _conv_branch(h, w_in, l, conv_w, hist, tm, tc, tiles_per_seq, emit_hist):
    rows = h.shape[0]
    ncb = D_CONV // tc
    assert not emit_hist or rows == tm
    kern = functools.partial(_conv_branch_kernel, tm=tm, tiles_per_seq=tiles_per_seq)
    n_out = 2 if emit_hist else 1
    return pl.pallas_call(
        kern,
        grid=(rows // tm, ncb),
        in_specs=[pl.BlockSpec((tm, D_MODEL), lambda i, j: (i, 0)),
                  _layer_spec(l, (D_MODEL, tc), lambda i, j: (0, j)),
                  _layer_spec(l, (D_MODEL, tc), lambda i, j: (0, j + ncb)),
                  _layer_spec(l, (D_MODEL, tc), lambda i, j: (0, j + 2 * ncb)),
                  pl.BlockSpec((3, tc), lambda i, j: (0, j)),
                  pl.BlockSpec((HIST, tc), lambda i, j: (0, j))],
        out_specs=[pl.BlockSpec((tm, tc), lambda i, j: (i, j)),
                   pl.BlockSpec((HIST, tc), lambda i, j: (0, j))][:n_out],
        out_shape=[jax.ShapeDtypeStruct((rows, D_CONV), BF),
                   jax.ShapeDtypeStruct((HIST, D_CONV), F32)][:n_out],
        scratch_shapes=[pltpu.VMEM((ncb, HIST, tc), F32)],
        name="conv_branch",
        compiler_params=_params("arbitrary", "arbitrary"),
    )(h, w_in, w_in, w_in, conv_w, hist)


def _latent_kernel(x_ref, g_ref, w_ref, gq_ref, gkv_ref, h_ref, qn_ref, kvn_ref, kr_ref):
    h = _rms_norm(x_ref[...], g_ref[...]).astype(h_ref.dtype)
    h_ref[...] = h
    acc = _dot(h, w_ref[...])
    qn_ref[...] = _rms_norm(acc[:, :Q_LORA], gq_ref[...]).astype(qn_ref.dtype)
    kvn_ref[...] = _rms_norm(acc[:, Q_LORA:Q_LORA + KV_LORA], gkv_ref[...]).astype(kvn_ref.dtype)
    kr_ref[...] = acc[:, Q_LORA + KV_LORA:]


def _latent(x, g, w_lat, l, gq, gkv, tm):
    rows = x.shape[0]
    row = lambda width: pl.BlockSpec((tm, width), lambda i: (i, 0))
    vec = lambda width: pl.BlockSpec((1, width), lambda i: (0, 0))
    return pl.pallas_call(
        _latent_kernel,
        grid=(rows // tm,),
        in_specs=[row(D_MODEL), vec(D_MODEL), _layer_spec(l, (D_MODEL, D_LAT), lambda i: (0, 0)),
                  vec(Q_LORA), vec(KV_LORA)],
        out_specs=[row(D_MODEL), row(Q_LORA), row(KV_LORA), row(256)],
        out_shape=[jax.ShapeDtypeStruct((rows, D_MODEL), BF),
                   jax.ShapeDtypeStruct((rows, Q_LORA), BF),
                   jax.ShapeDtypeStruct((rows, KV_LORA), BF),
                   jax.ShapeDtypeStruct((rows, 256), F32)],
        name="latent",
        compiler_params=_params("parallel"),
    )(x, g, w_lat, gq, gkv)


def _row_sumsq(a, n_real):
    keep = lax.broadcasted_iota(jnp.int32, (a.shape[1], 128), 0) < n_real
    return _dot((a * a).astype(BF), jnp.where(keep, 1.0, 0.0).astype(BF))


def _q_kernel(qn_ref, w_ref, g_ref, trig_ref, q_ref, acc_ref, ss_ref, *, heads, rb):
    acc_ref[...] = _dot(qn_ref[...], w_ref[...])
    for hd in range(heads):
        ss_ref[:, hd * 128:(hd + 1) * 128] = _row_sumsq(
            acc_ref[:, hd * HEAD_PAD:(hd + 1) * HEAD_PAD], QK_HEAD)
    g = g_ref[...]
    for r in range(0, acc_ref.shape[0], rb):
        rs = slice(r, r + rb)
        trig = trig_ref[rs, :]
        for hd in range(heads):
            c0 = hd * HEAD_PAD
            inv = lax.rsqrt(ss_ref[rs, hd * 128:(hd + 1) * 128] * (1.0 / QK_HEAD) + EPS)
            q_ref[rs, c0:c0 + QK_NOPE] = (
                acc_ref[rs, c0:c0 + QK_NOPE] * (inv * g)).astype(q_ref.dtype)
            q_ref[rs, c0 + QK_NOPE:c0 + HEAD_PAD] = (
                acc_ref[rs, c0 + QK_NOPE:c0 + HEAD_PAD] * (inv * trig)).astype(q_ref.dtype)


def _q_proj(qn, w_uq, l, g_nope, trig, tm, heads, tiles_per_seq):
    rows = qn.shape[0]
    kern = functools.partial(_q_kernel, heads=heads, rb=min(64, tm))
    return pl.pallas_call(
        kern,
        grid=(rows // tm, N_HEADS // heads),
        in_specs=[pl.BlockSpec((tm, Q_LORA), lambda i, j: (i, 0)),
                  _layer_spec(l, (Q_LORA, heads * HEAD_PAD), lambda i, j: (0, j)),
                  pl.BlockSpec((1, QK_NOPE), lambda i, j: (0, 0)),
                  pl.BlockSpec((tm, 128), lambda i, j: (i % tiles_per_seq, 0))],
        out_specs=pl.BlockSpec((tm, heads * HEAD_PAD), lambda i, j: (i, j)),
        out_shape=jax.ShapeDtypeStruct((rows, N_HEADS * HEAD_PAD), BF),
        scratch_shapes=[pltpu.VMEM((tm, heads * HEAD_PAD), F32),
                        pltpu.VMEM((tm, heads * 128), F32)],
        name="q_proj",
        compiler_params=_params("parallel", "parallel"),
    )(qn, w_uq, g_nope, trig)


def _kv_kernel(kvn_ref, wk_ref, wv_ref, kr_ref, gn_ref, cos_ref, sin_ref, k_ref, v_ref,
               acc_ref, ss_ref, *, heads, rb):
    kvn = kvn_ref[...]
    acc_ref[...] = _dot(kvn, wk_ref[...])
    v_ref[...] = _dot(kvn, wv_ref[...]).astype(v_ref.dtype)
    blk = (lax.broadcasted_iota(jnp.int32, (256, 256), 0) // 128
           == lax.broadcasted_iota(jnp.int32, (256, 256), 1) // 128)
    pair_ones = jnp.where(blk, 1.0, 0.0).astype(BF)
    for hd in range(0, heads, 2):
        a = acc_ref[:, hd * QK_NOPE:(hd + 2) * QK_NOPE]
        ss_ref[:, hd * 128:(hd + 2) * 128] = _dot((a * a).astype(BF), pair_ones)
    ss_ref[:, heads * 128:] = _row_sumsq(kr_ref[:, :128], QK_ROPE)
    gn = gn_ref[...]
    for r in range(0, acc_ref.shape[0], rb):
        rs = slice(r, r + rb)
        kr_rot = kr_ref[rs, :128] * cos_ref[rs, :] + kr_ref[rs, 128:] * sin_ref[rs, :]
        kr_ss = ss_ref[rs, heads * 128:]
        for hd in range(heads):
            ss = ss_ref[rs, hd * 128:(hd + 1) * 128] + kr_ss
            inv = lax.rsqrt(ss * (1.0 / QK_HEAD) + EPS)
            k_ref[rs, hd * HEAD_PAD:hd * HEAD_PAD + QK_NOPE] = (
                acc_ref[rs, hd * QK_NOPE:(hd + 1) * QK_NOPE] * (inv * gn)).astype(k_ref.dtype)
            k_ref[rs, hd * HEAD_PAD + QK_NOPE:(hd + 1) * HEAD_PAD] = (kr_rot * inv).astype(
                k_ref.dtype)


def _kv_proj(kvn, w_uk, w_uv, l, kr, gn, cos_t, sin_t, tm, heads, tiles_per_seq):
    rows = kvn.shape[0]
    kern = functools.partial(_kv_kernel, heads=heads, rb=min(64, tm))
    return pl.pallas_call(
        kern,
        grid=(rows // tm, N_HEADS // heads),
        in_specs=[pl.BlockSpec((tm, KV_LORA), lambda i, j: (i, 0)),
                  _layer_spec(l, (KV_LORA, heads * QK_NOPE), lambda i, j: (0, j)),
                  _layer_spec(l, (KV_LORA, heads * V_HEAD), lambda i, j: (0, j)),
                  pl.BlockSpec((tm, 256), lambda i, j: (i, 0)),
                  pl.BlockSpec((1, 128), lambda i, j: (0, 0)),
                  pl.BlockSpec((tm, 128), lambda i, j: (i % tiles_per_seq, 0)),
                  pl.BlockSpec((tm, 128), lambda i, j: (i % tiles_per_seq, 0))],
        out_specs=[pl.BlockSpec((tm, heads * HEAD_PAD), lambda i, j: (i, j)),
                   pl.BlockSpec((tm, heads * V_HEAD), lambda i, j: (i, j))],
        out_shape=[jax.ShapeDtypeStruct((rows, N_HEADS * HEAD_PAD), BF),
                   jax.ShapeDtypeStruct((rows, N_HEADS * V_HEAD), BF)],
        scratch_shapes=[pltpu.VMEM((tm, heads * QK_NOPE), F32),
                        pltpu.VMEM((tm, (heads + 1) * 128), F32)],
        name="kv_proj",
        compiler_params=_params("parallel", "parallel"),
    )(kvn, w_uk, w_uv, kr, gn, cos_t, sin_t)


def _attn_tile_kernel(q_ref, k_ref, v_ref, kp_ref, vp_ref, *rest, q_tile, tq, tc, td, rb,
                      n_prefix):
    _, o_ref, head_bufs = rest
    d0 = q_tile * tq
    items = [("full", c * tc, tc, 0) for c in range(d0 // tc)]
    items += [("diag", d0 + h * td, td, h * td) for h in range(tq // td)]
    items += [("prefix", 0, 128, 0)]

    def scores(hd, item, s_ref):
        kind, k0, width, r0 = item
        q = q_ref[r0:, hd * HEAD_PAD:(hd + 1) * HEAD_PAD]
        if kind == "prefix":
            keys = kp_ref[:, hd * HEAD_PAD:(hd + 1) * HEAD_PAD]
        else:
            keys = k_ref[k0:k0 + width, hd * HEAD_PAD:(hd + 1) * HEAD_PAD]
        s_ref[r0:, :width] = _dot_nt(q, keys)

    def softmax(item, s_ref, p_ref, a_ref, m_ref, l_ref):
        kind, k0, width, r0 = item
        for r in range(r0, tq, rb):
            rs = slice(r, r + rb)
            s = s_ref[rs, :width]
            col = lax.broadcasted_iota(jnp.int32, (rb, width), 1)
            if kind == "prefix":
                s = jnp.where(col < n_prefix, s, NEG_BIG)
            elif kind == "diag" and r < r0 + width:
                row = lax.broadcasted_iota(jnp.int32, (rb, width), 0) + (r - r0)
                s = jnp.where(col <= row, s, NEG_BIG)
            m_old = m_ref[rs, :]
            m_new = jnp.maximum(m_old, jnp.max(s, axis=-1, keepdims=True))
            alpha = jnp.exp2(m_old - m_new)
            p = jnp.exp2(s - jnp.concatenate([m_new] * (width // 128), axis=1))
            l_ref[rs, :] = alpha * l_ref[rs, :] + jnp.sum(p, axis=-1, keepdims=True)
            m_ref[rs, :] = m_new
            a_ref[rs, :] = alpha
            p_ref[rs, :width] = p.astype(p_ref.dtype)

    def accumulate(hd, item, p_ref, a_ref, acc_ref):
        kind, k0, width, r0 = item
        if kind == "prefix":
            values = vp_ref[:, hd * V_HEAD:(hd + 1) * V_HEAD]
        else:
            values = v_ref[k0:k0 + width, hd * V_HEAD:(hd + 1) * V_HEAD]
        acc_ref[r0:, :] = a_ref[r0:, :] * acc_ref[r0:, :] + _dot(p_ref[r0:, :width], values)

    for s_refs, p_refs, a_refs, m_ref, l_ref, acc_ref in head_bufs:
        m_ref[...] = jnp.full(m_ref.shape, NEG_BIG, F32)
        l_ref[...] = jnp.zeros(l_ref.shape, F32)
        acc_ref[...] = jnp.zeros(acc_ref.shape, F32)
    n = len(items)
    for t in range(n + 2):
        for hd, (s_refs, p_refs, a_refs, m_ref, l_ref, acc_ref) in enumerate(head_bufs):
            depth = len(s_refs)
            if t < n:
                scores(hd, items[t], s_refs[t % depth])
            if 1 <= t <= n:
                u = (t - 1) % depth
                softmax(items[t - 1], s_refs[u], p_refs[u], a_refs[u], m_ref, l_ref)
            if t >= 2:
                u = (t - 2) % depth
                accumulate(hd, items[t - 2], p_refs[u], a_refs[u], acc_ref)
    for hd, (_, _, _, _, l_ref, acc_ref) in enumerate(head_bufs):
        o_ref[:, hd * V_HEAD:(hd + 1) * V_HEAD] = (acc_ref[...] / l_ref[...]).astype(o_ref.dtype)


def _attention_tiles(q, k, v, kp, vp, batch, seq, tq, tc, td, heads):
    nq = seq // tq
    depth = 3
    stat = pltpu.VMEM((tq, 128), F32)
    head_bufs = [([pltpu.VMEM((tq, tc), F32)] * depth, [pltpu.VMEM((tq, tc), BF)] * depth,
                  [stat] * depth, stat, stat, pltpu.VMEM((tq, V_HEAD), F32))] * heads
    out = jnp.zeros((batch, seq, N_HEADS * V_HEAD), BF)
    for i in range(nq):
        keys = (i + 1) * tq
        kern = functools.partial(_attn_tile_kernel, q_tile=i, tq=tq, tc=tc, td=td, rb=64,
                                 n_prefix=N_META)
        out = pl.pallas_call(
            kern,
            grid=(batch, N_HEADS // heads),
            in_specs=[pl.BlockSpec((None, tq, heads * HEAD_PAD), lambda b, h, i=i: (b, i, h)),
                      pl.BlockSpec((None, keys, heads * HEAD_PAD), lambda b, h: (b, 0, h)),
                      pl.BlockSpec((None, keys, heads * V_HEAD), lambda b, h: (b, 0, h)),
                      pl.BlockSpec((128, heads * HEAD_PAD), lambda b, h: (0, h)),
                      pl.BlockSpec((128, heads * V_HEAD), lambda b, h: (0, h)),
                      pl.BlockSpec(memory_space=pl.ANY)],
            out_specs=pl.BlockSpec((None, tq, heads * V_HEAD), lambda b, h, i=i: (b, i, h)),
            out_shape=jax.ShapeDtypeStruct(out.shape, out.dtype),
            scratch_shapes=[head_bufs],
            input_output_aliases={5: 0},
            name=f"attention_q{i}",
            compiler_params=_params("parallel", "parallel"),
        )(q, k, v, kp, vp, out)
    return out


def _attn_meta_kernel(q_ref, k_ref, v_ref, o_ref):
    s = _dot_nt(q_ref[...], k_ref[...])
    row = lax.broadcasted_iota(jnp.int32, s.shape, 0)
    col = lax.broadcasted_iota(jnp.int32, s.shape, 1)
    s = jnp.where(row >= col, s, NEG_BIG)
    p = jnp.exp2(s - jnp.max(s, axis=-1, keepdims=True))
    l = jnp.sum(p, axis=-1, keepdims=True)
    o_ref[...] = (_dot(p.astype(BF), v_ref[...]) / l).astype(o_ref.dtype)


def _attention_meta(q, k, v):
    return pl.pallas_call(
        _attn_meta_kernel,
        grid=(N_HEADS,),
        in_specs=[pl.BlockSpec((N_META, HEAD_PAD), lambda h: (0, h)),
                  pl.BlockSpec((N_META, HEAD_PAD), lambda h: (0, h)),
                  pl.BlockSpec((N_META, V_HEAD), lambda h: (0, h))],
        out_specs=pl.BlockSpec((N_META, V_HEAD), lambda h: (0, h)),
        out_shape=jax.ShapeDtypeStruct((N_META, N_HEADS * V_HEAD), BF),
        name="attention_meta",
        compiler_params=_params("parallel"),
    )(q, k, v)


def _pool_branch_kernel(h_ref, w_ref, pw_ref, ps_ref, hist_ref, yc_ref, *rest,
                        tm, tiles_per_seq, pos_offset):
    carry_ref = rest[-1]
    i = pl.program_id(0)
    pin = _dot(h_ref[...], w_ref[...])

    @pl.when(i % tiles_per_seq == 0)
    def _():
        carry_ref[...] = hist_ref[...]

    ext = jnp.concatenate([carry_ref[...], pin], axis=0)
    tail = pin[tm - HIST:]
    carry_ref[...] = tail
    if len(rest) == 2:
        rest[0][...] = tail
    n_ext = tm + HIST
    seen = (lax.broadcasted_iota(jnp.int32, (tm, 1), 0)
            + ((i % tiles_per_seq) * tm + pos_offset + 1)).astype(F32)
    for g, w in enumerate(POOL_WINDOWS):
        xg = ext[:, g * POOL_GROUP:(g + 1) * POOL_GROUP]
        s, span = xg, 1
        while span < w:
            s = s[span:] + s[:s.shape[0] - span]
            span *= 2
        s = s[n_ext - (w - 1) - tm:]
        pooled = s / jnp.minimum(seen, float(w)) - xg[HIST:]
        mixed = _dot(pooled.astype(BF), pw_ref[g])
        yc_ref[:, g * POOL_GROUP:(g + 1) * POOL_GROUP] = (
            mixed * ps_ref[:, g * POOL_GROUP:(g + 1) * POOL_GROUP]).astype(yc_ref.dtype)


def _pool_branch(h, w_pool, pool_w, l, pool_scale, hist, tm, tiles_per_seq, pos_offset,
                 emit_hist):
    rows = h.shape[0]
    assert not emit_hist or rows == tm
    kern = functools.partial(_pool_branch_kernel, tm=tm, tiles_per_seq=tiles_per_seq,
                             pos_offset=pos_offset)
    ng = len(POOL_WINDOWS)
    n_out = 2 if emit_hist else 1
    return pl.pallas_call(
        kern,
        grid=(rows // tm,),
        in_specs=[pl.BlockSpec((tm, D_MODEL), lambda i: (i, 0)),
                  _layer_spec(l, (D_MODEL, D_POOL), lambda i: (0, 0)),
                  _layer_spec(l, (ng, POOL_GROUP, POOL_GROUP), lambda i: (0, 0, 0)),
                  pl.BlockSpec((1, D_POOL), lambda i: (0, 0)),
                  pl.BlockSpec((HIST, D_POOL), lambda i: (0, 0))],
        out_specs=[pl.BlockSpec((tm, D_POOL), lambda i: (i, 0)),
                   pl.BlockSpec((HIST, D_POOL), lambda i: (0, 0))][:n_out],
        out_shape=[jax.ShapeDtypeStruct((rows, D_POOL), BF),
                   jax.ShapeDtypeStruct((HIST, D_POOL), F32)][:n_out],
        scratch_shapes=[pltpu.VMEM((HIST, D_POOL), F32)],
        name="pool_branch",
        compiler_params=_params("arbitrary"),
    )(h, w_pool, pool_w, pool_scale, hist)


def _merge_kernel(h_ref, ya_ref, yb_ref, yc_ref, wg0_ref, wg1_ref, wg2_ref, wa_ref, wb_ref, wc_ref,
                  o_ref):
    h = h_ref[...]
    m = jax.nn.sigmoid(_dot(h, wg0_ref[...])) * _dot(ya_ref[...], wa_ref[...])
    m += jax.nn.sigmoid(_dot(h, wg1_ref[...])) * _dot(yb_ref[...], wb_ref[...])
    m += jax.nn.sigmoid(_dot(h, wg2_ref[...])) * _dot(yc_ref[...], wc_ref[...])
    o_ref[...] = m.astype(o_ref.dtype)


def _merge(h, ya, yb, yc, w_gate, wa, wb, wc, l, tm, tn):
    rows = h.shape[0]
    ncb = D_MODEL // tn
    row = lambda width: pl.BlockSpec((tm, width), lambda i, j: (i, 0))
    col = lambda depth, off: _layer_spec(l, (depth, tn), lambda i, j: (0, j + off))
    return pl.pallas_call(
        _merge_kernel,
        grid=(rows // tm, ncb),
        in_specs=[row(D_MODEL), row(D_CONV), row(N_HEADS * V_HEAD), row(D_POOL),
                  col(D_MODEL, 0), col(D_MODEL, ncb), col(D_MODEL, 2 * ncb),
                  col(D_CONV, 0), col(N_HEADS * V_HEAD, 0), col(D_POOL, 0)],
        out_specs=pl.BlockSpec((tm, tn), lambda i, j: (i, j)),
        out_shape=jax.ShapeDtypeStruct((rows, D_MODEL), BF),
        name="merge",
        compiler_params=_params("parallel", "arbitrary"),
    )(h, ya, yb, yc, w_gate, w_gate, w_gate, wa, wb, wc)


def _oproj_kernel(m_ref, w_ref, x_ref, g_ref, xo_ref, h2_ref):
    x = x_ref[...] + _dot(m_ref[...], w_ref[...])
    xo_ref[...] = x
    h2_ref[...] = _rms_norm(x, g_ref[...]).astype(h2_ref.dtype)


def _oproj(merged, w_o, l, x, g, tm):
    rows = x.shape[0]
    return pl.pallas_call(
        _oproj_kernel,
        grid=(rows // tm,),
        in_specs=[pl.BlockSpec((tm, D_MODEL), lambda i: (i, 0)),
                  _layer_spec(l, (D_MODEL, D_MODEL), lambda i: (0, 0)),
                  pl.BlockSpec((tm, D_MODEL), lambda i: (i, 0)),
                  pl.BlockSpec((1, D_MODEL), lambda i: (0, 0))],
        out_specs=[pl.BlockSpec((tm, D_MODEL), lambda i: (i, 0)),
                   pl.BlockSpec((tm, D_MODEL), lambda i: (i, 0))],
        out_shape=[jax.ShapeDtypeStruct((rows, D_MODEL), F32),
                   jax.ShapeDtypeStruct((rows, D_MODEL), BF)],
        name="oproj",
        compiler_params=_params("parallel"),
    )(merged, w_o, x, g)


def _mlp_kernel(h2_ref, wu_ref, wd_ref, x_ref, o_ref):
    @pl.when(pl.program_id(1) == 0)
    def _():
        o_ref[...] = x_ref[...]

    a = jnp.maximum(_dot(h2_ref[...], wu_ref[...]), 0.0)
    o_ref[...] += _dot((a * a).astype(BF), wd_ref[...])


def _mlp(h2, w_up, w_down, l, x, tm, tf):
    rows = x.shape[0]
    return pl.pallas_call(
        _mlp_kernel,
        grid=(rows // tm, D_FF // tf),
        in_specs=[pl.BlockSpec((tm, D_MODEL), lambda i, k: (i, 0)),
                  _layer_spec(l, (D_MODEL, tf), lambda i, k: (0, k)),
                  _layer_spec(l, (tf, D_MODEL), lambda i, k: (k, 0)),
                  pl.BlockSpec((tm, D_MODEL), lambda i, k: (i, 0))],
        out_specs=pl.BlockSpec((tm, D_MODEL), lambda i, k: (i, 0)),
        out_shape=jax.ShapeDtypeStruct((rows, D_MODEL), F32),
        name="mlp",
        compiler_params=_params("parallel", "arbitrary"),
    )(h2, w_up, w_down, x)


def _swap_halves(a):
    return jnp.concatenate([a[..., HALF_ROPE:], a[..., :HALF_ROPE]], axis=-1)


def _twice(a):
    return jnp.concatenate([a, a], axis=-1)


def _widen_head(a):
    rope = a[..., QK_NOPE:]
    return jnp.concatenate([a[..., :QK_NOPE], rope, _swap_halves(rope)], axis=-1)


def _rope_gain_tables(gain, cos_t, sin_t):
    rope_gain = gain[QK_NOPE:]
    return cos_t * rope_gain[None], sin_t * _swap_halves(rope_gain)[None]


def _stacked_weights(w_in, w_uq, w_ukv, pool_w, w_branch_a, w_branch_b, w_branch_c, w_o, w_up,
                     w_down):
    depth = w_in.shape[0]
    wi = w_in.astype(BF)
    o_q = 3 * D_CONV
    o_kr = o_q + Q_LORA + KV_LORA
    o_pool = o_kr + QK_ROPE
    o_gate = o_pool + D_POOL
    w_kr = wi[:, :, o_kr:o_pool]
    w_ukv_h = w_ukv.astype(BF).reshape(depth, KV_LORA, N_HEADS, QK_NOPE + V_HEAD)
    return dict(
        w_in=wi,
        w_lat=jnp.concatenate([wi[:, :, o_q:o_kr], _twice(w_kr), _twice(_swap_halves(w_kr))],
                              axis=2),
        w_pool=wi[:, :, o_pool:o_gate],
        w_gate=wi[:, :, o_gate:],
        w_uq=_widen_head(w_uq.astype(BF).reshape(depth, Q_LORA, N_HEADS, QK_HEAD)).reshape(
            depth, Q_LORA, N_HEADS * HEAD_PAD),
        w_uk=w_ukv_h[..., :QK_NOPE].reshape(depth, KV_LORA, N_HEADS * QK_NOPE),
        w_uv=w_ukv_h[..., QK_NOPE:].reshape(depth, KV_LORA, N_HEADS * V_HEAD),
        pool_w=pool_w.astype(BF),
        wa=w_branch_a.astype(BF),
        wb=w_branch_b.astype(BF),
        wc=w_branch_c.astype(BF),
        w_o=w_o.astype(BF),
        w_up=w_up.astype(BF),
        w_down=w_down.astype(BF),
    )


def _layer_vectors(l, cos_t, sin_t, attn_norm, conv_w, q_lat_norm, kv_lat_norm, q_norm, k_norm,
                   pool_scale, mlp_norm):
    gq = q_norm[l] * (QK_HEAD ** -0.5 * LOG2_E)
    q_cos, q_sin = _rope_gain_tables(gq, cos_t, sin_t)
    k_cos, k_sin = _rope_gain_tables(k_norm[l], cos_t, sin_t)
    return dict(
        attn_norm=attn_norm[l][None],
        conv_w=conv_w[l],
        q_lat_norm=q_lat_norm[l][None],
        kv_lat_norm=kv_lat_norm[l][None],
        gq_nope=gq[None, :QK_NOPE],
        gk_nope=k_norm[l][None, :QK_NOPE],
        q_trig=jnp.concatenate([q_cos, q_sin], axis=1),
        k_rope=(_twice(k_cos), _twice(k_sin)),
        pool_scale=pool_scale[l][None],
        mlp_norm=mlp_norm[l][None],
    )


def _rope_tables(total):
    pos = jnp.arange(total, dtype=F32)
    inv = ROPE_THETA ** (-jnp.arange(0, QK_ROPE, 2, dtype=F32) / QK_ROPE)
    ang = pos[:, None] * inv[None, :]
    cos, sin = jnp.cos(ang), jnp.sin(ang)
    return _twice(cos), jnp.concatenate([-sin, sin], axis=1)


def _layer(x, w, l, p, positions, prefix, *, batch, seq, finish):
    is_meta = prefix is None
    q_trig = p['q_trig'][positions]
    k_cos, k_sin = (t[positions] for t in p['k_rope'])
    tm = min(512, seq)
    tps = seq // tm
    tm_big = min(1024, seq)
    if is_meta:
        conv_hist = jnp.zeros((HIST, D_CONV), F32)
        pool_hist = jnp.zeros((HIST, D_POOL), F32)
    else:
        conv_hist, pool_hist, kp, vp = prefix

    h, qn, kvn, kr = _latent(x, p['attn_norm'], w['w_lat'], l, p['q_lat_norm'], p['kv_lat_norm'],
                             tm)
    ya, *conv_tail = _conv_branch(h, w['w_in'], l, p['conv_w'], conv_hist, tm_big, 512,
                                  seq // tm_big, is_meta)
    yc, *pool_tail = _pool_branch(h, w['w_pool'], w['pool_w'], l, p['pool_scale'], pool_hist, tm,
                                  tps, 0 if is_meta else N_META, is_meta)
    k, v = _kv_proj(kvn, w['w_uk'], w['w_uv'], l, kr, p['gk_nope'], k_cos, k_sin, tm, N_HEADS,
                    tps)
    out_prefix = (conv_tail[0], pool_tail[0], k, v) if is_meta else None
    if not finish:
        return None, out_prefix
    q = _q_proj(qn, w['w_uq'], l, p['gq_nope'], q_trig, tm, N_HEADS, tps)
    if is_meta:
        yb = _attention_meta(q, k, v)
    else:
        pad = ((0, 128 - N_META), (0, 0))
        per_batch = lambda a: a.reshape(batch, seq, a.shape[-1])
        yb = _attention_tiles(per_batch(q), per_batch(k), per_batch(v), jnp.pad(kp, pad),
                              jnp.pad(vp, pad), batch, seq, 512, 512, 256, 2).reshape(
                                  batch * seq, -1)
    merged = _merge(h, ya, yb, yc, w['w_gate'], w['wa'], w['wb'], w['wc'], l, tm_big, 256)
    x_mid, h2 = _oproj(merged, w['w_o'], l, x, p['mlp_norm'], min(256, seq))
    return _mlp(h2, w['w_up'], w['w_down'], l, x_mid, tm_big, 512), out_prefix


def kernel(x, meta_tokens, attn_norm, w_in, conv_w, q_lat_norm, kv_lat_norm, w_uq, w_ukv, q_norm, k_norm, pool_w, pool_scale, w_branch_a, w_branch_b, w_branch_c, w_o, mlp_norm, w_up, w_down):
    batch, seq, d = x.shape
    depth = w_in.shape[0]
    cos_t, sin_t = _rope_tables(N_META + seq)
    xm = meta_tokens.astype(F32)
    xr = x.reshape(batch * seq, d)
    w = _stacked_weights(w_in, w_uq, w_ukv, pool_w, w_branch_a, w_branch_b, w_branch_c, w_o, w_up,
                         w_down)
    for l in range(depth):
        p = _layer_vectors(l, cos_t, sin_t, attn_norm, conv_w, q_lat_norm, kv_lat_norm, q_norm,
                           k_norm, pool_scale, mlp_norm)
        last = l == depth - 1
        xm, prefix = _layer(xm, w, l, p, slice(0, N_META), None, batch=1, seq=N_META,
                            finish=not last)
        xr, _ = _layer(xr, w, l, p, slice(N_META, None), prefix, batch=batch, seq=seq, finish=True)
    return xr.reshape(batch, seq, d)
```

```python
import functools

import jax
import jax.numpy as jnp
from jax import lax
from jax.experimental import pallas as pl
from jax.experimental.pallas import tpu as pltpu

D_MODEL = 2048
N_META = 16
EPS = 1e-6
D_CONV = 1024
N_HEADS = 16
QK_NOPE = 128
QK_ROPE = 64
QK_HEAD = QK_NOPE + QK_ROPE
HALF_ROPE = QK_ROPE // 2
V_HEAD = 128
Q_LORA = 512
KV_LORA = 512
ROPE_THETA = 10000.0
D_POOL = 1024
POOL_WINDOWS = (2, 4, 8, 16)
POOL_GROUP = D_POOL // len(POOL_WINDOWS)
D_FF = 4 * D_MODEL
HIST = 16
HEAD_PAD = 256
D_LAT = Q_LORA + KV_LORA + 256
VMEM_LIMIT = 56 * 2**20
NEG_BIG = -1e30
LOG2_E = 1.4426950408889634

BF = jnp.bfloat16
F32 = jnp.float32


def _dot(a, b):
    return jnp.dot(a, b, preferred_element_type=F32)


def _dot_nt(a, b):
    return lax.dot_general(a, b, (((1,), (1,)), ((), ())), preferred_element_type=F32)


def _layer_spec(l, shape, index_map):
    return pl.BlockSpec((None,) + shape, lambda *g: (l,) + index_map(*g))


def _params(*sem):
    return pltpu.CompilerParams(dimension_semantics=sem, vmem_limit_bytes=VMEM_LIMIT)


def _rms_norm(x, g):
    ms = jnp.mean(x * x, axis=-1, keepdims=True)
    return x * lax.rsqrt(ms + EPS) * g


def _conv_branch_kernel(h_ref, wu_ref, wb_ref, wc_ref, cw_ref, hist_ref, ya_ref, *rest,
                        tm, tiles_per_seq):
    carry_ref = rest[-1]
    i, j = pl.program_id(0), pl.program_id(1)
    h = h_ref[...]
    cu = _dot(h, wc_ref[...]) * _dot(h, wu_ref[...])

    @pl.when(i % tiles_per_seq == 0)
    def _():
        carry_ref[j] = hist_ref[...]

    ext = jnp.concatenate([carry_ref[j], cu], axis=0)
    tail = cu[tm - HIST:]
    carry_ref[j] = tail
    if len(rest) == 2:
        rest[0][...] = tail
    cw = cw_ref[...]
    y = (cw[2:3] * cu + cw[1:2] * ext[HIST - 1:HIST - 1 + tm] + cw[0:1] * ext[HIST - 2:HIST - 2 + tm])
    ya_ref[...] = (_dot(h, wb_ref[...]) * y).astype(ya_ref.dtype)


def _conv_branch(h, w_in, l, conv_w, hist, tm, tc, tiles_per_seq, emit_hist):
    rows = h.shape[0]
    ncb = D_CONV // tc
    assert not emit_hist or rows == tm
    kern = functools.partial(_conv_branch_kernel, tm=tm, tiles_per_seq=tiles_per_seq)
    n_out = 2 if emit_hist else 1
    return pl.pallas_call(
        kern,
        grid=(rows // tm, ncb),
        in_specs=[pl.BlockSpec((tm, D_MODEL), lambda i, j: (i, 0)),
                  _layer_spec(l, (D_MODEL, tc), lambda i, j: (0, j)),
                  _layer_spec(l, (D_MODEL, tc), lambda i, j: (0, j + ncb)),
                  _layer_spec(l, (D_MODEL, tc), lambda i, j: (0, j + 2 * ncb)),
                  pl.BlockSpec((3, tc), lambda i, j: (0, j)),
                  pl.BlockSpec((HIST, tc), lambda i, j: (0, j))],
        out_specs=[pl.BlockSpec((tm, tc), lambda i, j: (i, j)),
                   pl.BlockSpec((HIST, tc), lambda i, j: (0, j))][:n_out],
        out_shape=[jax.ShapeDtypeStruct((rows, D_CONV), BF),
                   jax.ShapeDtypeStruct((HIST, D_CONV), F32)][:n_out],
        scratch_shapes=[pltpu.VMEM((ncb, HIST, tc), F32)],
        name="conv_branch",
        compiler_params=_params("arbitrary", "arbitrary"),
    )(h, w_in, w_in, w_in, conv_w, hist)


def _latent_kernel(x_ref, g_ref, w_ref, gq_ref, gkv_ref, h_ref, qn_ref, kvn_ref, kr_ref):
    h = _rms_norm(x_ref[...], g_ref[...]).astype(h_ref.dtype)
    h_ref[...] = h
    acc = _dot(h, w_ref[...])
    qn_ref[...] = _rms_norm(acc[:, :Q_LORA], gq_ref[...]).astype(qn_ref.dtype)
    kvn_ref[...] = _rms_norm(acc[:, Q_LORA:Q_LORA + KV_LORA], gkv_ref[...]).astype(kvn_ref.dtype)
    kr_ref[...] = acc[:, Q_LORA + KV_LORA:]


def _latent(x, g, w_lat, l, gq, gkv, tm):
    rows = x.shape[0]
    row = lambda width: pl.BlockSpec((tm, width), lambda i: (i, 0))
    vec = lambda width: pl.BlockSpec((1, width), lambda i: (0, 0))
    return pl.pallas_call(
        _latent_kernel,
        grid=(rows // tm,),
        in_specs=[row(D_MODEL), vec(D_MODEL), _layer_spec(l, (D_MODEL, D_LAT), lambda i: (0, 0)),
                  vec(Q_LORA), vec(KV_LORA)],
        out_specs=[row(D_MODEL), row(Q_LORA), row(KV_LORA), row(256)],
        out_shape=[jax.ShapeDtypeStruct((rows, D_MODEL), BF),
                   jax.ShapeDtypeStruct((rows, Q_LORA), BF),
                   jax.ShapeDtypeStruct((rows, KV_LORA), BF),
                   jax.ShapeDtypeStruct((rows, 256), F32)],
        name="latent",
        compiler_params=_params("parallel"),
    )(x, g, w_lat, gq, gkv)


def _row_sumsq(a, n_real):
    keep = lax.broadcasted_iota(jnp.int32, (a.shape[1], 128), 0) < n_real
    return _dot((a * a).astype(BF), jnp.where(keep, 1.0, 0.0).astype(BF))


def _q_kernel(qn_ref, w_ref, g_ref, trig_ref, q_ref, acc_ref, ss_ref, *, heads, rb):
    acc_ref[...] = _dot(qn_ref[...], w_ref[...])
    for hd in range(heads):
        ss_ref[:, hd * 128:(hd + 1) * 128] = _row_sumsq(
            acc_ref[:, hd * HEAD_PAD:(hd + 1) * HEAD_PAD], QK_HEAD)
    g = g_ref[...]
    for r in range(0, acc_ref.shape[0], rb):
        rs = slice(r, r + rb)
        trig = trig_ref[rs, :]
        for hd in range(heads):
            c0 = hd * HEAD_PAD
            inv = lax.rsqrt(ss_ref[rs, hd * 128:(hd + 1) * 128] * (1.0 / QK_HEAD) + EPS)
            q_ref[rs, c0:c0 + QK_NOPE] = (
                acc_ref[rs, c0:c0 + QK_NOPE] * (inv * g)).astype(q_ref.dtype)
            q_ref[rs, c0 + QK_NOPE:c0 + HEAD_PAD] = (
                acc_ref[rs, c0 + QK_NOPE:c0 + HEAD_PAD] * (inv * trig)).astype(q_ref.dtype)


def _q_proj(qn, w_uq, l, g_nope, trig, tm, heads, tiles_per_seq):
    rows = qn.shape[0]
    kern = functools.partial(_q_kernel, heads=heads, rb=min(64, tm))
    return pl.pallas_call(
        kern,
        grid=(rows // tm, N_HEADS // heads),
        in_specs=[pl.BlockSpec((tm, Q_LORA), lambda i, j: (i, 0)),
                  _layer_spec(l, (Q_LORA, heads * HEAD_PAD), lambda i, j: (0, j)),
                  pl.BlockSpec((1, QK_NOPE), lambda i, j: (0, 0)),
                  pl.BlockSpec((tm, 128), lambda i, j: (i % tiles_per_seq, 0))],
        out_specs=pl.BlockSpec((tm, heads * HEAD_PAD), lambda i, j: (i, j)),
        out_shape=jax.ShapeDtypeStruct((rows, N_HEADS * HEAD_PAD), BF),
        scratch_shapes=[pltpu.VMEM((tm, heads * HEAD_PAD), F32),
                        pltpu.VMEM((tm, heads * 128), F32)],
        name="q_proj",
        compiler_params=_params("parallel", "parallel"),
    )(qn, w_uq, g_nope, trig)


def _kv_kernel(kvn_ref, wk_ref, wv_ref, kr_ref, gn_ref, cos_ref, sin_ref, k_ref, v_ref,
               acc_ref, ss_ref, *, heads, rb):
    kvn = kvn_ref[...]
    acc_ref[...] = _dot(kvn, wk_ref[...])
    v_ref[...] = _dot(kvn, wv_ref[...]).astype(v_ref.dtype)
    blk = (lax.broadcasted_iota(jnp.int32, (256, 256), 0) // 128
           == lax.broadcasted_iota(jnp.int32, (256, 256), 1) // 128)
    pair_ones = jnp.where(blk, 1.0, 0.0).astype(BF)
    for hd in range(0, heads, 2):
        a = acc_ref[:, hd * QK_NOPE:(hd + 2) * QK_NOPE]
        ss_ref[:, hd * 128:(hd + 2) * 128] = _dot((a * a).astype(BF), pair_ones)
    ss_ref[:, heads * 128:] = _row_sumsq(kr_ref[:, :128], QK_ROPE)
    gn = gn_ref[...]
    for r in range(0, acc_ref.shape[0], rb):
        rs = slice(r, r + rb)
        kr_rot = kr_ref[rs, :128] * cos_ref[rs, :] + kr_ref[rs, 128:] * sin_ref[rs, :]
        kr_ss = ss_ref[rs, heads * 128:]
        for hd in range(heads):
            ss = ss_ref[rs, hd * 128:(hd + 1) * 128] + kr_ss
            inv = lax.rsqrt(ss * (1.0 / QK_HEAD) + EPS)
            k_ref[rs, hd * HEAD_PAD:hd * HEAD_PAD + QK_NOPE] = (
                acc_ref[rs, hd * QK_NOPE:(hd + 1) * QK_NOPE] * (inv * gn)).astype(k_ref.dtype)
            k_ref[rs, hd * HEAD_PAD + QK_NOPE:(hd + 1) * HEAD_PAD] = (kr_rot * inv).astype(
                k_ref.dtype)


def _kv_proj(kvn, w_uk, w_uv, l, kr, gn, cos_t, sin_t, tm, heads, tiles_per_seq):
    rows = kvn.shape[0]
    kern = functools.partial(_kv_kernel, heads=heads, rb=min(64, tm))
    return pl.pallas_call(
        kern,
        grid=(rows // tm, N_HEADS // heads),
        in_specs=[pl.BlockSpec((tm, KV_LORA), lambda i, j: (i, 0)),
                  _layer_spec(l, (KV_LORA, heads * QK_NOPE), lambda i, j: (0, j)),
                  _layer_spec(l, (KV_LORA, heads * V_HEAD), lambda i, j: (0, j)),
                  pl.BlockSpec((tm, 256), lambda i, j: (i, 0)),
                  pl.BlockSpec((1, 128), lambda i, j: (0, 0)),
                  pl.BlockSpec((tm, 128), lambda i, j: (i % tiles_per_seq, 0)),
                  pl.BlockSpec((tm, 128), lambda i, j: (i % tiles_per_seq, 0))],
        out_specs=[pl.BlockSpec((tm, heads * HEAD_PAD), lambda i, j: (i, j)),
                   pl.BlockSpec((tm, heads * V_HEAD), lambda i, j: (i, j))],
        out_shape=[jax.ShapeDtypeStruct((rows, N_HEADS * HEAD_PAD), BF),
                   jax.ShapeDtypeStruct((rows, N_HEADS * V_HEAD), BF)],
        scratch_shapes=[pltpu.VMEM((tm, heads * QK_NOPE), F32),
                        pltpu.VMEM((tm, (heads + 1) * 128), F32)],
        name="kv_proj",
        compiler_params=_params("parallel", "parallel"),
    )(kvn, w_uk, w_uv, kr, gn, cos_t, sin_t)


def _attn_tile_kernel(q_ref, k_ref, v_ref, kp_ref, vp_ref, *rest, q_tile, tq, tc, td, rb,
                      n_prefix):
    _, o_ref, head_bufs = rest
    d0 = q_tile * tq
    items = [None] + [c * tc for c in range(d0 // tc)]
    bands = range(tq // td)
    qcols = lambda hd: slice(hd * HEAD_PAD, (hd + 1) * HEAD_PAD)
    vcols = lambda hd: slice(hd * V_HEAD, (hd + 1) * V_HEAD)

    def scores(hd, k0, s_ref):
        if k0 is not None:
            s_ref[:, :tc] = _dot_nt(q_ref[:, qcols(hd)], k_ref[k0:k0 + tc, qcols(hd)])
            return
        s_ref[:, tq:tq + 128] = _dot_nt(q_ref[:, qcols(hd)], kp_ref[:, qcols(hd)])
        for g in bands:
            seen = (g + 1) * td
            s_ref[g * td:seen, :seen] = _dot_nt(q_ref[g * td:seen, qcols(hd)],
                                                k_ref[d0:d0 + seen, qcols(hd)])

    def softmax_rows(rs, pieces, p_ref, a_ref, m_ref, l_ref):
        s = jnp.concatenate([x for _, x in pieces], axis=1)
        m_old = m_ref[rs, :]
        m_new = jnp.maximum(m_old, jnp.max(s, axis=-1, keepdims=True))
        alpha = jnp.exp2(m_old - m_new)
        p = jnp.exp2(s - jnp.concatenate([m_new] * (s.shape[1] // 128), axis=1))
        l_ref[rs, :] = alpha * l_ref[rs, :] + jnp.sum(p, axis=-1, keepdims=True)
        m_ref[rs, :] = m_new
        a_ref[rs, :] = alpha
        at = 0
        for cols, x in pieces:
            p_ref[rs, cols] = p[:, at:at + x.shape[1]].astype(p_ref.dtype)
            at += x.shape[1]

    def softmax(k0, s_ref, p_ref, a_ref, m_ref, l_ref):
        for r in range(0, tq, rb):
            rs = slice(r, r + rb)
            if k0 is not None:
                pieces = [(slice(0, tc), s_ref[rs, :tc])]
            else:
                g = r // td
                open_cols, last = slice(0, g * td), slice(g * td, (g + 1) * td)
                row = lax.broadcasted_iota(jnp.int32, (rb, td), 0) + (r - g * td)
                col = lax.broadcasted_iota(jnp.int32, (rb, td), 1)
                colp = lax.broadcasted_iota(jnp.int32, (rb, 128), 1)
                pieces = [(open_cols, s_ref[rs, open_cols])] if g else []
                pieces += [(last, jnp.where(col <= row, s_ref[rs, last], NEG_BIG)),
                           (slice(tq, tq + 128),
                            jnp.where(colp < n_prefix, s_ref[rs, tq:tq + 128], NEG_BIG))]
            softmax_rows(rs, pieces, p_ref, a_ref, m_ref, l_ref)

    def accumulate(hd, k0, p_ref, a_ref, acc_ref):
        if k0 is not None:
            acc_ref[...] = a_ref[...] * acc_ref[...] + _dot(p_ref[:, :tc],
                                                            v_ref[k0:k0 + tc, vcols(hd)])
            return
        for g in bands:
            rows, seen = slice(g * td, (g + 1) * td), (g + 1) * td
            acc_ref[rows, :] = (a_ref[rows, :] * acc_ref[rows, :]
                                + _dot(p_ref[rows, :seen], v_ref[d0:d0 + seen, vcols(hd)])
                                + _dot(p_ref[rows, tq:tq + 128], vp_ref[:, vcols(hd)]))

    for s_refs, p_refs, a_refs, m_ref, l_ref, acc_ref in head_bufs:
        m_ref[...] = jnp.full(m_ref.shape, NEG_BIG, F32)
        l_ref[...] = jnp.zeros(l_ref.shape, F32)
        acc_ref[...] = jnp.zeros(acc_ref.shape, F32)
    n = len(items)
    for t in range(n + 2):
        for hd, (s_refs, p_refs, a_refs, m_ref, l_ref, acc_ref) in enumerate(head_bufs):
            depth = len(s_refs)
            if t < n:
                scores(hd, items[t], s_refs[t % depth])
            if 1 <= t <= n:
                u = (t - 1) % depth
                softmax(items[t - 1], s_refs[u], p_refs[u], a_refs[u], m_ref, l_ref)
            if t >= 2:
                u = (t - 2) % depth
                accumulate(hd, items[t - 2], p_refs[u], a_refs[u], acc_ref)
    for hd, (_, _, _, _, l_ref, acc_ref) in enumerate(head_bufs):
        o_ref[:, hd * V_HEAD:(hd + 1) * V_HEAD] = (acc_ref[...] / l_ref[...]).astype(o_ref.dtype)


def _attention_tiles(q, k, v, kp, vp, batch, seq, tq, tc, td, heads):
    nq = seq // tq
    depth = 3
    stat = pltpu.VMEM((tq, 128), F32)
    width = max(tc, tq + 128)
    head_bufs = [([pltpu.VMEM((tq, width), F32)] * depth, [pltpu.VMEM((tq, width), BF)] * depth,
                  [stat] * depth, stat, stat, pltpu.VMEM((tq, V_HEAD), F32))] * heads
    out = jnp.zeros((batch, seq, N_HEADS * V_HEAD), BF)
    for i in range(nq):
        keys = (i + 1) * tq
        kern = functools.partial(_attn_tile_kernel, q_tile=i, tq=tq, tc=tc, td=td, rb=64,
                                 n_prefix=N_META)
        out = pl.pallas_call(
            kern,
            grid=(batch, N_HEADS // heads),
            in_specs=[pl.BlockSpec((None, tq, heads * HEAD_PAD), lambda b, h, i=i: (b, i, h)),
                      pl.BlockSpec((None, keys, heads * HEAD_PAD), lambda b, h: (b, 0, h)),
                      pl.BlockSpec((None, keys, heads * V_HEAD), lambda b, h: (b, 0, h)),
                      pl.BlockSpec((128, heads * HEAD_PAD), lambda b, h: (0, h)),
                      pl.BlockSpec((128, heads * V_HEAD), lambda b, h: (0, h)),
                      pl.BlockSpec(memory_space=pl.ANY)],
            out_specs=pl.BlockSpec((None, tq, heads * V_HEAD), lambda b, h, i=i: (b, i, h)),
            out_shape=jax.ShapeDtypeStruct(out.shape, out.dtype),
            scratch_shapes=[head_bufs],
            input_output_aliases={5: 0},
            name=f"attention_q{i}",
            compiler_params=_params("parallel", "parallel"),
        )(q, k, v, kp, vp, out)
    return out


def _attn_meta_kernel(q_ref, k_ref, v_ref, o_ref):
    s = _dot_nt(q_ref[...], k_ref[...])
    row = lax.broadcasted_iota(jnp.int32, s.shape, 0)
    col = lax.broadcasted_iota(jnp.int32, s.shape, 1)
    s = jnp.where(row >= col, s, NEG_BIG)
    p = jnp.exp2(s - jnp.max(s, axis=-1, keepdims=True))
    l = jnp.sum(p, axis=-1, keepdims=True)
    o_ref[...] = (_dot(p.astype(BF), v_ref[...]) / l).astype(o_ref.dtype)


def _attention_meta(q, k, v):
    return pl.pallas_call(
        _attn_meta_kernel,
        grid=(N_HEADS,),
        in_specs=[pl.BlockSpec((N_META, HEAD_PAD), lambda h: (0, h)),
                  pl.BlockSpec((N_META, HEAD_PAD), lambda h: (0, h)),
                  pl.BlockSpec((N_META, V_HEAD), lambda h: (0, h))],
        out_specs=pl.BlockSpec((N_META, V_HEAD), lambda h: (0, h)),
        out_shape=jax.ShapeDtypeStruct((N_META, N_HEADS * V_HEAD), BF),
        name="attention_meta",
        compiler_params=_params("parallel"),
    )(q, k, v)


def _pool_branch_kernel(h_ref, w_ref, pw_ref, ps_ref, hist_ref, yc_ref, *rest,
                        tm, tiles_per_seq, pos_offset):
    carry_ref = rest[-1]
    i = pl.program_id(0)
    pin = _dot(h_ref[...], w_ref[...])

    @pl.when(i % tiles_per_seq == 0)
    def _():
        carry_ref[...] = hist_ref[...]

    ext = jnp.concatenate([carry_ref[...], pin], axis=0)
    tail = pin[tm - HIST:]
    carry_ref[...] = tail
    if len(rest) == 2:
        rest[0][...] = tail
    n_ext = tm + HIST
    seen = (lax.broadcasted_iota(jnp.int32, (tm, 1), 0)
            + ((i % tiles_per_seq) * tm + pos_offset + 1)).astype(F32)
    for g, w in enumerate(POOL_WINDOWS):
        xg = ext[:, g * POOL_GROUP:(g + 1) * POOL_GROUP]
        s, span = xg, 1
        while span < w:
            s = s[span:] + s[:s.shape[0] - span]
            span *= 2
        s = s[n_ext - (w - 1) - tm:]
        pooled = s / jnp.minimum(seen, float(w)) - xg[HIST:]
        mixed = _dot(pooled.astype(BF), pw_ref[g])
        yc_ref[:, g * POOL_GROUP:(g + 1) * POOL_GROUP] = (
            mixed * ps_ref[:, g * POOL_GROUP:(g + 1) * POOL_GROUP]).astype(yc_ref.dtype)


def _pool_branch(h, w_pool, pool_w, l, pool_scale, hist, tm, tiles_per_seq, pos_offset,
                 emit_hist):
    rows = h.shape[0]
    assert not emit_hist or rows == tm
    kern = functools.partial(_pool_branch_kernel, tm=tm, tiles_per_seq=tiles_per_seq,
                             pos_offset=pos_offset)
    ng = len(POOL_WINDOWS)
    n_out = 2 if emit_hist else 1
    return pl.pallas_call(
        kern,
        grid=(rows // tm,),
        in_specs=[pl.BlockSpec((tm, D_MODEL), lambda i: (i, 0)),
                  _layer_spec(l, (D_MODEL, D_POOL), lambda i: (0, 0)),
                  _layer_spec(l, (ng, POOL_GROUP, POOL_GROUP), lambda i: (0, 0, 0)),
                  pl.BlockSpec((1, D_POOL), lambda i: (0, 0)),
                  pl.BlockSpec((HIST, D_POOL), lambda i: (0, 0))],
        out_specs=[pl.BlockSpec((tm, D_POOL), lambda i: (i, 0)),
                   pl.BlockSpec((HIST, D_POOL), lambda i: (0, 0))][:n_out],
        out_shape=[jax.ShapeDtypeStruct((rows, D_POOL), BF),
                   jax.ShapeDtypeStruct((HIST, D_POOL), F32)][:n_out],
        scratch_shapes=[pltpu.VMEM((HIST, D_POOL), F32)],
        name="pool_branch",
        compiler_params=_params("arbitrary"),
    )(h, w_pool, pool_w, pool_scale, hist)


def _merge_kernel(h_ref, ya_ref, yb_ref, yc_ref, wg0_ref, wg1_ref, wg2_ref, wa_ref, wb_ref, wc_ref,
                  o_ref):
    h = h_ref[...]
    m = jax.nn.sigmoid(_dot(h, wg0_ref[...])) * _dot(ya_ref[...], wa_ref[...])
    m += jax.nn.sigmoid(_dot(h, wg1_ref[...])) * _dot(yb_ref[...], wb_ref[...])
    m += jax.nn.sigmoid(_dot(h, wg2_ref[...])) * _dot(yc_ref[...], wc_ref[...])
    o_ref[...] = m.astype(o_ref.dtype)


def _merge(h, ya, yb, yc, w_gate, wa, wb, wc, l, tm, tn):
    rows = h.shape[0]
    ncb = D_MODEL // tn
    row = lambda width: pl.BlockSpec((tm, width), lambda i, j: (i, 0))
    col = lambda depth, off: _layer_spec(l, (depth, tn), lambda i, j: (0, j + off))
    return pl.pallas_call(
        _merge_kernel,
        grid=(rows // tm, ncb),
        in_specs=[row(D_MODEL), row(D_CONV), row(N_HEADS * V_HEAD), row(D_POOL),
                  col(D_MODEL, 0), col(D_MODEL, ncb), col(D_MODEL, 2 * ncb),
                  col(D_CONV, 0), col(N_HEADS * V_HEAD, 0), col(D_POOL, 0)],
        out_specs=pl.BlockSpec((tm, tn), lambda i, j: (i, j)),
        out_shape=jax.ShapeDtypeStruct((rows, D_MODEL), BF),
        name="merge",
        compiler_params=_params("parallel", "arbitrary"),
    )(h, ya, yb, yc, w_gate, w_gate, w_gate, wa, wb, wc)


def _oproj_kernel(m_ref, w_ref, x_ref, g_ref, xo_ref, h2_ref):
    x = x_ref[...] + _dot(m_ref[...], w_ref[...])
    xo_ref[...] = x
    h2_ref[...] = _rms_norm(x, g_ref[...]).astype(h2_ref.dtype)


def _oproj(merged, w_o, l, x, g, tm):
    rows = x.shape[0]
    return pl.pallas_call(
        _oproj_kernel,
        grid=(rows // tm,),
        in_specs=[pl.BlockSpec((tm, D_MODEL), lambda i: (i, 0)),
                  _layer_spec(l, (D_MODEL, D_MODEL), lambda i: (0, 0)),
                  pl.BlockSpec((tm, D_MODEL), lambda i: (i, 0)),
                  pl.BlockSpec((1, D_MODEL), lambda i: (0, 0))],
        out_specs=[pl.BlockSpec((tm, D_MODEL), lambda i: (i, 0)),
                   pl.BlockSpec((tm, D_MODEL), lambda i: (i, 0))],
        out_shape=[jax.ShapeDtypeStruct((rows, D_MODEL), F32),
                   jax.ShapeDtypeStruct((rows, D_MODEL), BF)],
        name="oproj",
        compiler_params=_params("parallel"),
    )(merged, w_o, x, g)


def _mlp_kernel(h2_ref, wu_ref, wd_ref, x_ref, o_ref):
    @pl.when(pl.program_id(1) == 0)
    def _():
        o_ref[...] = x_ref[...]

    a = jnp.maximum(_dot(h2_ref[...], wu_ref[...]), 0.0)
    o_ref[...] += _dot((a * a).astype(BF), wd_ref[...])


def _mlp(h2, w_up, w_down, l, x, tm, tf):
    rows = x.shape[0]
    return pl.pallas_call(
        _mlp_kernel,
        grid=(rows // tm, D_FF // tf),
        in_specs=[pl.BlockSpec((tm, D_MODEL), lambda i, k: (i, 0)),
                  _layer_spec(l, (D_MODEL, tf), lambda i, k: (0, k)),
                  _layer_spec(l, (tf, D_MODEL), lambda i, k: (k, 0)),
                  pl.BlockSpec((tm, D_MODEL), lambda i, k: (i, 0))],
        out_specs=pl.BlockSpec((tm, D_MODEL), lambda i, k: (i, 0)),
        out_shape=jax.ShapeDtypeStruct((rows, D_MODEL), F32),
        name="mlp",
        compiler_params=_params("parallel", "arbitrary"),
    )(h2, w_up, w_down, x)


def _swap_halves(a):
    return jnp.concatenate([a[..., HALF_ROPE:], a[..., :HALF_ROPE]], axis=-1)


def _twice(a):
    return jnp.concatenate([a, a], axis=-1)


def _widen_head(a):
    rope = a[..., QK_NOPE:]
    return jnp.concatenate([a[..., :QK_NOPE], rope, _swap_halves(rope)], axis=-1)


def _rope_gain_tables(gain, cos_t, sin_t):
    rope_gain = gain[QK_NOPE:]
    return cos_t * rope_gain[None], sin_t * _swap_halves(rope_gain)[None]


def _stacked_weights(w_in, w_uq, w_ukv, pool_w, w_branch_a, w_branch_b, w_branch_c, w_o, w_up,
                     w_down):
    depth = w_in.shape[0]
    wi = w_in.astype(BF)
    o_q = 3 * D_CONV
    o_kr = o_q + Q_LORA + KV_LORA
    o_pool = o_kr + QK_ROPE
    o_gate = o_pool + D_POOL
    w_kr = wi[:, :, o_kr:o_pool]
    w_ukv_h = w_ukv.astype(BF).reshape(depth, KV_LORA, N_HEADS, QK_NOPE + V_HEAD)
    return dict(
        w_in=wi,
        w_lat=jnp.concatenate([wi[:, :, o_q:o_kr], _twice(w_kr), _twice(_swap_halves(w_kr))],
                              axis=2),
        w_pool=wi[:, :, o_pool:o_gate],
        w_gate=wi[:, :, o_gate:],
        w_uq=_widen_head(w_uq.astype(BF).reshape(depth, Q_LORA, N_HEADS, QK_HEAD)).reshape(
            depth, Q_LORA, N_HEADS * HEAD_PAD),
        w_uk=w_ukv_h[..., :QK_NOPE].reshape(depth, KV_LORA, N_HEADS * QK_NOPE),
        w_uv=w_ukv_h[..., QK_NOPE:].reshape(depth, KV_LORA, N_HEADS * V_HEAD),
        pool_w=pool_w.astype(BF),
        wa=w_branch_a.astype(BF),
        wb=w_branch_b.astype(BF),
        wc=w_branch_c.astype(BF),
        w_o=w_o.astype(BF),
        w_up=w_up.astype(BF),
        w_down=w_down.astype(BF),
    )


def _layer_vectors(l, cos_t, sin_t, attn_norm, conv_w, q_lat_norm, kv_lat_norm, q_norm, k_norm,
                   pool_scale, mlp_norm):
    gq = q_norm[l] * (QK_HEAD ** -0.5 * LOG2_E)
    q_cos, q_sin = _rope_gain_tables(gq, cos_t, sin_t)
    k_cos, k_sin = _rope_gain_tables(k_norm[l], cos_t, sin_t)
    return dict(
        attn_norm=attn_norm[l][None],
        conv_w=conv_w[l],
        q_lat_norm=q_lat_norm[l][None],
        kv_lat_norm=kv_lat_norm[l][None],
        gq_nope=gq[None, :QK_NOPE],
        gk_nope=k_norm[l][None, :QK_NOPE],
        q_trig=jnp.concatenate([q_cos, q_sin], axis=1),
        k_rope=(_twice(k_cos), _twice(k_sin)),
        pool_scale=pool_scale[l][None],
        mlp_norm=mlp_norm[l][None],
    )


def _rope_tables(total):
    pos = jnp.arange(total, dtype=F32)
    inv = ROPE_THETA ** (-jnp.arange(0, QK_ROPE, 2, dtype=F32) / QK_ROPE)
    ang = pos[:, None] * inv[None, :]
    cos, sin = jnp.cos(ang), jnp.sin(ang)
    return _twice(cos), jnp.concatenate([-sin, sin], axis=1)


def _layer(x, w, l, p, positions, prefix, *, batch, seq, finish):
    is_meta = prefix is None
    q_trig = p['q_trig'][positions]
    k_cos, k_sin = (t[positions] for t in p['k_rope'])
    tm = min(512, seq)
    tps = seq // tm
    tm_big = min(1024, seq)
    if is_meta:
        conv_hist = jnp.zeros((HIST, D_CONV), F32)
        pool_hist = jnp.zeros((HIST, D_POOL), F32)
    else:
        conv_hist, pool_hist, kp, vp = prefix

    h, qn, kvn, kr = _latent(x, p['attn_norm'], w['w_lat'], l, p['q_lat_norm'], p['kv_lat_norm'],
                             tm)
    ya, *conv_tail = _conv_branch(h, w['w_in'], l, p['conv_w'], conv_hist, tm_big, 512,
                                  seq // tm_big, is_meta)
    yc, *pool_tail = _pool_branch(h, w['w_pool'], w['pool_w'], l, p['pool_scale'], pool_hist, tm,
                                  tps, 0 if is_meta else N_META, is_meta)
    k, v = _kv_proj(kvn, w['w_uk'], w['w_uv'], l, kr, p['gk_nope'], k_cos, k_sin, tm, N_HEADS,
                    tps)
    out_prefix = (conv_tail[0], pool_tail[0], k, v) if is_meta else None
    if not finish:
        return None, out_prefix
    q = _q_proj(qn, w['w_uq'], l, p['gq_nope'], q_trig, tm, N_HEADS, tps)
    if is_meta:
        yb = _attention_meta(q, k, v)
    else:
        pad = ((0, 128 - N_META), (0, 0))
        per_batch = lambda a: a.reshape(batch, seq, a.shape[-1])
        yb = _attention_tiles(per_batch(q), per_batch(k), per_batch(v), jnp.pad(kp, pad),
                              jnp.pad(vp, pad), batch, seq, 512, 512, 256, 2).reshape(
                                  batch * seq, -1)
    merged = _merge(h, ya, yb, yc, w['w_gate'], w['wa'], w['wb'], w['wc'], l, tm_big, 256)
    x_mid, h2 = _oproj(merged, w['w_o'], l, x, p['mlp_norm'], min(256, seq))
    return _mlp(h2, w['w_up'], w['w_down'], l, x_mid, tm_big, 512), out_prefix


def kernel(x, meta_tokens, attn_norm, w_in, conv_w, q_lat_norm, kv_lat_norm, w_uq, w_ukv, q_norm, k_norm, pool_w, pool_scale, w_branch_a, w_branch_b, w_branch_c, w_o, mlp_norm, w_up, w_down):
    batch, seq, d = x.shape
    depth = w_in.shape[0]
    cos_t, sin_t = _rope_tables(N_META + seq)
    xm = meta_tokens.astype(F32)
    xr = x.reshape(batch * seq, d)
    w = _stacked_weights(w_in, w_uq, w_ukv, pool_w, w_branch_a, w_branch_b, w_branch_c, w_o, w_up,
                         w_down)
    for l in range(depth):
        p = _layer_vectors(l, cos_t, sin_t, attn_norm, conv_w, q_lat_norm, kv_lat_norm, q_norm,
                           k_norm, pool_scale, mlp_norm)
        last = l == depth - 1
        xm, prefix = _layer(xm, w, l, p, slice(0, N_META), None, batch=1, seq=N_META,
                            finish=not last)
        xr, _ = _layer(xr, w, l, p, slice(N_META, None), prefix, batch=batch, seq=seq, finish=True)
    return xr.reshape(batch, seq, d)
```

```python
import functools

import jax
import jax.numpy as jnp
from jax import lax
from jax.experimental import pallas as pl
from jax.experimental.pallas import tpu as pltpu

D_MODEL = 2048
N_META = 16
EPS = 1e-6
D_CONV = 1024
N_HEADS = 16
QK_NOPE = 128
QK_ROPE = 64
QK_HEAD = QK_NOPE + QK_ROPE
HALF_ROPE = QK_ROPE // 2
V_HEAD = 128
Q_LORA = 512
KV_LORA = 512
ROPE_THETA = 10000.0
D_POOL = 1024
POOL_WINDOWS = (2, 4, 8, 16)
POOL_GROUP = D_POOL // len(POOL_WINDOWS)
D_FF = 4 * D_MODEL
HIST = 16
HEAD_PAD = 256
D_LAT = Q_LORA + KV_LORA + 256
VMEM_LIMIT = 56 * 2**20
NEG_BIG = -1e30
LOG2_E = 1.4426950408889634

BF = jnp.bfloat16
F32 = jnp.float32


def _dot(a, b):
    return jnp.dot(a, b, preferred_element_type=F32)


def _dot_nt(a, b):
    return lax.dot_general(a, b, (((1,), (1,)), ((), ())), preferred_element_type=F32)


def _layer_spec(l, shape, index_map):
    return pl.BlockSpec((None,) + shape, lambda *g: (l,) + index_map(*g))


def _params(*sem):
    return pltpu.CompilerParams(dimension_semantics=sem, vmem_limit_bytes=VMEM_LIMIT)


def _rms_norm(x, g):
    ms = jnp.mean(x * x, axis=-1, keepdims=True)
    return x * lax.rsqrt(ms + EPS) * g


def _conv_branch_kernel(h_ref, wu_ref, wb_ref, wc_ref, cw_ref, hist_ref, ya_ref, *rest,
                        tm, tiles_per_seq):
    carry_ref = rest[-1]
    i, j = pl.program_id(0), pl.program_id(1)
    h = h_ref[...]
    cu = _dot(h, wc_ref[...]) * _dot(h, wu_ref[...])

    @pl.when(i % tiles_per_seq == 0)
    def _():
        carry_ref[j] = hist_ref[...]

    ext = jnp.concatenate([carry_ref[j], cu], axis=0)
    tail = cu[tm - HIST:]
    carry_ref[j] = tail
    if len(rest) == 2:
        rest[0][...] = tail
    cw = cw_ref[...]
    y = (cw[2:3] * cu + cw[1:2] * ext[HIST - 1:HIST - 1 + tm] + cw[0:1] * ext[HIST - 2:HIST - 2 + tm])
    ya_ref[...] = (_dot(h, wb_ref[...]) * y).astype(ya_ref.dtype)


def _conv_branch(h, w_in, l, conv_w, hist, tm, tc, tiles_per_seq, emit_hist):
    rows = h.shape[0]
    ncb = D_CONV // tc
    assert not emit_hist or rows == tm
    kern = functools.partial(_conv_branch_kernel, tm=tm, tiles_per_seq=tiles_per_seq)
    n_out = 2 if emit_hist else 1
    return pl.pallas_call(
        kern,
        grid=(rows // tm, ncb),
        in_specs=[pl.BlockSpec((tm, D_MODEL), lambda i, j: (i, 0)),
                  _layer_spec(l, (D_MODEL, tc), lambda i, j: (0, j)),
                  _layer_spec(l, (D_MODEL, tc), lambda i, j: (0, j + ncb)),
                  _layer_spec(l, (D_MODEL, tc), lambda i, j: (0, j + 2 * ncb)),
                  pl.BlockSpec((3, tc), lambda i, j: (0, j)),
                  pl.BlockSpec((HIST, tc), lambda i, j: (0, j))],
        out_specs=[pl.BlockSpec((tm, tc), lambda i, j: (i, j)),
                   pl.BlockSpec((HIST, tc), lambda i, j: (0, j))][:n_out],
        out_shape=[jax.ShapeDtypeStruct((rows, D_CONV), BF),
                   jax.ShapeDtypeStruct((HIST, D_CONV), F32)][:n_out],
        scratch_shapes=[pltpu.VMEM((ncb, HIST, tc), F32)],
        name="conv_branch",
        compiler_params=_params("arbitrary", "arbitrary"),
    )(h, w_in, w_in, w_in, conv_w, hist)


def _latent_kernel(x_ref, g_ref, w_ref, gq_ref, gkv_ref, h_ref, qn_ref, kvn_ref, kr_ref):
    h = _rms_norm(x_ref[...], g_ref[...]).astype(h_ref.dtype)
    h_ref[...] = h
    acc = _dot(h, w_ref[...])
    qn_ref[...] = _rms_norm(acc[:, :Q_LORA], gq_ref[...]).astype(qn_ref.dtype)
    kvn_ref[...] = _rms_norm(acc[:, Q_LORA:Q_LORA + KV_LORA], gkv_ref[...]).astype(kvn_ref.dtype)
    kr_ref[...] = acc[:, Q_LORA + KV_LORA:]


def _latent(x, g, w_lat, l, gq, gkv, tm):
    rows = x.shape[0]
    row = lambda width: pl.BlockSpec((tm, width), lambda i: (i, 0))
    vec = lambda width: pl.BlockSpec((1, width), lambda i: (0, 0))
    return pl.pallas_call(
        _latent_kernel,
        grid=(rows // tm,),
        in_specs=[row(D_MODEL), vec(D_MODEL), _layer_spec(l, (D_MODEL, D_LAT), lambda i: (0, 0)),
                  vec(Q_LORA), vec(KV_LORA)],
        out_specs=[row(D_MODEL), row(Q_LORA), row(KV_LORA), row(256)],
        out_shape=[jax.ShapeDtypeStruct((rows, D_MODEL), BF),
                   jax.ShapeDtypeStruct((rows, Q_LORA), BF),
                   jax.ShapeDtypeStruct((rows, KV_LORA), BF),
                   jax.ShapeDtypeStruct((rows, 256), F32)],
        name="latent",
        compiler_params=_params("parallel"),
    )(x, g, w_lat, gq, gkv)


def _row_sumsq(a, n_real):
    keep = lax.broadcasted_iota(jnp.int32, (a.shape[1], 128), 0) < n_real
    return _dot((a * a).astype(BF), jnp.where(keep, 1.0, 0.0).astype(BF))


def _q_kernel(qn_ref, w_ref, g_ref, trig_ref, q_ref, acc_ref, ss_ref, *, heads, rb):
    acc_ref[...] = _dot(qn_ref[...], w_ref[...])
    for hd in range(heads):
        ss_ref[:, hd * 128:(hd + 1) * 128] = _row_sumsq(
            acc_ref[:, hd * HEAD_PAD:(hd + 1) * HEAD_PAD], QK_HEAD)
    g = g_ref[...]
    for r in range(0, acc_ref.shape[0], rb):
        rs = slice(r, r + rb)
        trig = trig_ref[rs, :]
        for hd in range(heads):
            c0 = hd * HEAD_PAD
            inv = lax.rsqrt(ss_ref[rs, hd * 128:(hd + 1) * 128] * (1.0 / QK_HEAD) + EPS)
            q_ref[rs, c0:c0 + QK_NOPE] = (
                acc_ref[rs, c0:c0 + QK_NOPE] * (inv * g)).astype(q_ref.dtype)
            q_ref[rs, c0 + QK_NOPE:c0 + HEAD_PAD] = (
                acc_ref[rs, c0 + QK_NOPE:c0 + HEAD_PAD] * (inv * trig)).astype(q_ref.dtype)


def _q_proj(qn, w_uq, l, g_nope, trig, tm, heads, tiles_per_seq):
    rows = qn.shape[0]
    kern = functools.partial(_q_kernel, heads=heads, rb=min(64, tm))
    return pl.pallas_call(
        kern,
        grid=(rows // tm, N_HEADS // heads),
        in_specs=[pl.BlockSpec((tm, Q_LORA), lambda i, j: (i, 0)),
                  _layer_spec(l, (Q_LORA, heads * HEAD_PAD), lambda i, j: (0, j)),
                  pl.BlockSpec((1, QK_NOPE), lambda i, j: (0, 0)),
                  pl.BlockSpec((tm, 128), lambda i, j: (i % tiles_per_seq, 0))],
        out_specs=pl.BlockSpec((tm, heads * HEAD_PAD), lambda i, j: (i, j)),
        out_shape=jax.ShapeDtypeStruct((rows, N_HEADS * HEAD_PAD), BF),
        scratch_shapes=[pltpu.VMEM((tm, heads * HEAD_PAD), F32),
                        pltpu.VMEM((tm, heads * 128), F32)],
        name="q_proj",
        compiler_params=_params("parallel", "parallel"),
    )(qn, w_uq, g_nope, trig)


def _kv_kernel(kvn_ref, wk_ref, wv_ref, kr_ref, gn_ref, cos_ref, sin_ref, k_ref, v_ref,
               acc_ref, ss_ref, *, heads, rb):
    kvn = kvn_ref[...]
    acc_ref[...] = _dot(kvn, wk_ref[...])
    v_ref[...] = _dot(kvn, wv_ref[...]).astype(v_ref.dtype)
    blk = (lax.broadcasted_iota(jnp.int32, (256, 256), 0) // 128
           == lax.broadcasted_iota(jnp.int32, (256, 256), 1) // 128)
    pair_ones = jnp.where(blk, 1.0, 0.0).astype(BF)
    for hd in range(0, heads, 2):
        a = acc_ref[:, hd * QK_NOPE:(hd + 2) * QK_NOPE]
        ss_ref[:, hd * 128:(hd + 2) * 128] = _dot((a * a).astype(BF), pair_ones)
    ss_ref[:, heads * 128:] = _row_sumsq(kr_ref[:, :128], QK_ROPE)
    gn = gn_ref[...]
    for r in range(0, acc_ref.shape[0], rb):
        rs = slice(r, r + rb)
        kr_rot = kr_ref[rs, :128] * cos_ref[rs, :] + kr_ref[rs, 128:] * sin_ref[rs, :]
        kr_ss = ss_ref[rs, heads * 128:]
        for hd in range(heads):
            ss = ss_ref[rs, hd * 128:(hd + 1) * 128] + kr_ss
            inv = lax.rsqrt(ss * (1.0 / QK_HEAD) + EPS)
            k_ref[rs, hd * HEAD_PAD:hd * HEAD_PAD + QK_NOPE] = (
                acc_ref[rs, hd * QK_NOPE:(hd + 1) * QK_NOPE] * (inv * gn)).astype(k_ref.dtype)
            k_ref[rs, hd * HEAD_PAD + QK_NOPE:(hd + 1) * HEAD_PAD] = (kr_rot * inv).astype(
                k_ref.dtype)


def _kv_proj(kvn, w_uk, w_uv, l, kr, gn, cos_t, sin_t, tm, heads, tiles_per_seq):
    rows = kvn.shape[0]
    kern = functools.partial(_kv_kernel, heads=heads, rb=min(64, tm))
    return pl.pallas_call(
        kern,
        grid=(rows // tm, N_HEADS // heads),
        in_specs=[pl.BlockSpec((tm, KV_LORA), lambda i, j: (i, 0)),
                  _layer_spec(l, (KV_LORA, heads * QK_NOPE), lambda i, j: (0, j)),
                  _layer_spec(l, (KV_LORA, heads * V_HEAD), lambda i, j: (0, j)),
                  pl.BlockSpec((tm, 256), lambda i, j: (i, 0)),
                  pl.BlockSpec((1, 128), lambda i, j: (0, 0)),
                  pl.BlockSpec((tm, 128), lambda i, j: (i % tiles_per_seq, 0)),
                  pl.BlockSpec((tm, 128), lambda i, j: (i % tiles_per_seq, 0))],
        out_specs=[pl.BlockSpec((tm, heads * HEAD_PAD), lambda i, j: (i, j)),
                   pl.BlockSpec((tm, heads * V_HEAD), lambda i, j: (i, j))],
        out_shape=[jax.ShapeDtypeStruct((rows, N_HEADS * HEAD_PAD), BF),
                   jax.ShapeDtypeStruct((rows, N_HEADS * V_HEAD), BF)],
        scratch_shapes=[pltpu.VMEM((tm, heads * QK_NOPE), F32),
                        pltpu.VMEM((tm, (heads + 1) * 128), F32)],
        name="kv_proj",
        compiler_params=_params("parallel", "parallel"),
    )(kvn, w_uk, w_uv, kr, gn, cos_t, sin_t)


def _attn_kernel(q_ref, k_ref, v_ref, kp_ref, vp_ref, o_ref, head_bufs, *, tq, tc, td, rb,
                 n_prefix):
    for q_tile in range(q_ref.shape[0] // tq):
        _attn_tile(q_ref, k_ref, v_ref, kp_ref, vp_ref, o_ref, head_bufs, q_tile=q_tile, tq=tq,
                   tc=tc, td=td, rb=rb, n_prefix=n_prefix)


def _attn_tile(q_ref, k_ref, v_ref, kp_ref, vp_ref, o_ref, head_bufs, *, q_tile, tq, tc, td, rb,
               n_prefix):
    d0 = q_tile * tq
    items = [None] + [c * tc for c in range(d0 // tc)]
    bands = range(tq // td)
    qcols = lambda hd: slice(hd * HEAD_PAD, (hd + 1) * HEAD_PAD)
    vcols = lambda hd: slice(hd * V_HEAD, (hd + 1) * V_HEAD)
    tile = slice(d0, d0 + tq)

    def scores(hd, k0, s_ref):
        if k0 is not None:
            s_ref[:, :tc] = _dot_nt(q_ref[tile, qcols(hd)], k_ref[k0:k0 + tc, qcols(hd)])
            return
        s_ref[:, tq:tq + 128] = _dot_nt(q_ref[tile, qcols(hd)], kp_ref[:, qcols(hd)])
        for g in bands:
            seen = (g + 1) * td
            s_ref[g * td:seen, :seen] = _dot_nt(q_ref[d0 + g * td:d0 + seen, qcols(hd)],
                                                k_ref[d0:d0 + seen, qcols(hd)])

    def softmax_rows(rs, pieces, p_ref, a_ref, m_ref, l_ref):
        s = jnp.concatenate([x for _, x in pieces], axis=1)
        m_old = m_ref[rs, :]
        m_new = jnp.maximum(m_old, jnp.max(s, axis=-1, keepdims=True))
        alpha = jnp.exp2(m_old - m_new)
        p = jnp.exp2(s - jnp.concatenate([m_new] * (s.shape[1] // 128), axis=1))
        l_ref[rs, :] = alpha * l_ref[rs, :] + jnp.sum(p, axis=-1, keepdims=True)
        m_ref[rs, :] = m_new
        a_ref[rs, :] = alpha
        at = 0
        for cols, x in pieces:
            p_ref[rs, cols] = p[:, at:at + x.shape[1]].astype(p_ref.dtype)
            at += x.shape[1]

    def softmax(k0, s_ref, p_ref, a_ref, m_ref, l_ref):
        for r in range(0, tq, rb):
            rs = slice(r, r + rb)
            if k0 is not None:
                pieces = [(slice(0, tc), s_ref[rs, :tc])]
            else:
                g = r // td
                open_cols, last = slice(0, g * td), slice(g * td, (g + 1) * td)
                row = lax.broadcasted_iota(jnp.int32, (rb, td), 0) + (r - g * td)
                col = lax.broadcasted_iota(jnp.int32, (rb, td), 1)
                colp = lax.broadcasted_iota(jnp.int32, (rb, 128), 1)
                pieces = [(open_cols, s_ref[rs, open_cols])] if g else []
                pieces += [(last, jnp.where(col <= row, s_ref[rs, last], NEG_BIG)),
                           (slice(tq, tq + 128),
                            jnp.where(colp < n_prefix, s_ref[rs, tq:tq + 128], NEG_BIG))]
            softmax_rows(rs, pieces, p_ref, a_ref, m_ref, l_ref)

    def accumulate(hd, k0, p_ref, a_ref, acc_ref):
        if k0 is not None:
            acc_ref[...] = a_ref[...] * acc_ref[...] + _dot(p_ref[:, :tc],
                                                            v_ref[k0:k0 + tc, vcols(hd)])
            return
        for g in bands:
            rows, seen = slice(g * td, (g + 1) * td), (g + 1) * td
            acc_ref[rows, :] = (a_ref[rows, :] * acc_ref[rows, :]
                                + _dot(p_ref[rows, :seen], v_ref[d0:d0 + seen, vcols(hd)])
                                + _dot(p_ref[rows, tq:tq + 128], vp_ref[:, vcols(hd)]))

    for s_refs, p_refs, a_refs, m_ref, l_ref, acc_ref in head_bufs:
        m_ref[...] = jnp.full(m_ref.shape, NEG_BIG, F32)
        l_ref[...] = jnp.zeros(l_ref.shape, F32)
        acc_ref[...] = jnp.zeros(acc_ref.shape, F32)
    n = len(items)
    for t in range(n + 2):
        for hd, (s_refs, p_refs, a_refs, m_ref, l_ref, acc_ref) in enumerate(head_bufs):
            depth = len(s_refs)
            if t < n:
                scores(hd, items[t], s_refs[t % depth])
            if 1 <= t <= n:
                u = (t - 1) % depth
                softmax(items[t - 1], s_refs[u], p_refs[u], a_refs[u], m_ref, l_ref)
            if t >= 2:
                u = (t - 2) % depth
                accumulate(hd, items[t - 2], p_refs[u], a_refs[u], acc_ref)
    for hd, (_, _, _, _, l_ref, acc_ref) in enumerate(head_bufs):
        o_ref[tile, vcols(hd)] = (acc_ref[...] / l_ref[...]).astype(o_ref.dtype)


def _attention(q, k, v, kp, vp, batch, seq, tq, tc, td, heads):
    depth = 3
    stat = pltpu.VMEM((tq, 128), F32)
    width = max(tc, tq + 128)
    head_bufs = [([pltpu.VMEM((tq, width), F32)] * depth, [pltpu.VMEM((tq, width), BF)] * depth,
                  [stat] * depth, stat, stat, pltpu.VMEM((tq, V_HEAD), F32))] * heads
    kern = functools.partial(_attn_kernel, tq=tq, tc=tc, td=td, rb=64, n_prefix=N_META)
    return pl.pallas_call(
        kern,
        grid=(batch, N_HEADS // heads),
        in_specs=[pl.BlockSpec((None, seq, heads * HEAD_PAD), lambda b, h: (b, 0, h)),
                  pl.BlockSpec((None, seq, heads * HEAD_PAD), lambda b, h: (b, 0, h)),
                  pl.BlockSpec((None, seq, heads * V_HEAD), lambda b, h: (b, 0, h)),
                  pl.BlockSpec((128, heads * HEAD_PAD), lambda b, h: (0, h)),
                  pl.BlockSpec((128, heads * V_HEAD), lambda b, h: (0, h))],
        out_specs=pl.BlockSpec((None, seq, heads * V_HEAD), lambda b, h: (b, 0, h)),
        out_shape=jax.ShapeDtypeStruct((batch, seq, N_HEADS * V_HEAD), BF),
        scratch_shapes=[head_bufs],
        name="attention",
        compiler_params=_params("parallel", "parallel"),
    )(q, k, v, kp, vp)


def _attn_meta_kernel(q_ref, k_ref, v_ref, o_ref):
    s = _dot_nt(q_ref[...], k_ref[...])
    row = lax.broadcasted_iota(jnp.int32, s.shape, 0)
    col = lax.broadcasted_iota(jnp.int32, s.shape, 1)
    s = jnp.where(row >= col, s, NEG_BIG)
    p = jnp.exp2(s - jnp.max(s, axis=-1, keepdims=True))
    l = jnp.sum(p, axis=-1, keepdims=True)
    o_ref[...] = (_dot(p.astype(BF), v_ref[...]) / l).astype(o_ref.dtype)


def _attention_meta(q, k, v):
    return pl.pallas_call(
        _attn_meta_kernel,
        grid=(N_HEADS,),
        in_specs=[pl.BlockSpec((N_META, HEAD_PAD), lambda h: (0, h)),
                  pl.BlockSpec((N_META, HEAD_PAD), lambda h: (0, h)),
                  pl.BlockSpec((N_META, V_HEAD), lambda h: (0, h))],
        out_specs=pl.BlockSpec((N_META, V_HEAD), lambda h: (0, h)),
        out_shape=jax.ShapeDtypeStruct((N_META, N_HEADS * V_HEAD), BF),
        name="attention_meta",
        compiler_params=_params("parallel"),
    )(q, k, v)


def _pool_branch_kernel(h_ref, w_ref, pw_ref, ps_ref, hist_ref, yc_ref, *rest,
                        tm, tiles_per_seq, pos_offset):
    carry_ref = rest[-1]
    i = pl.program_id(0)
    pin = _dot(h_ref[...], w_ref[...])

    @pl.when(i % tiles_per_seq == 0)
    def _():
        carry_ref[...] = hist_ref[...]

    ext = jnp.concatenate([carry_ref[...], pin], axis=0)
    tail = pin[tm - HIST:]
    carry_ref[...] = tail
    if len(rest) == 2:
        rest[0][...] = tail
    n_ext = tm + HIST
    seen = (lax.broadcasted_iota(jnp.int32, (tm, 1), 0)
            + ((i % tiles_per_seq) * tm + pos_offset + 1)).astype(F32)
    for g, w in enumerate(POOL_WINDOWS):
        xg = ext[:, g * POOL_GROUP:(g + 1) * POOL_GROUP]
        s, span = xg, 1
        while span < w:
            s = s[span:] + s[:s.shape[0] - span]
            span *= 2
        s = s[n_ext - (w - 1) - tm:]
        pooled = s / jnp.minimum(seen, float(w)) - xg[HIST:]
        mixed = _dot(pooled.astype(BF), pw_ref[g])
        yc_ref[:, g * POOL_GROUP:(g + 1) * POOL_GROUP] = (
            mixed * ps_ref[:, g * POOL_GROUP:(g + 1) * POOL_GROUP]).astype(yc_ref.dtype)


def _pool_branch(h, w_pool, pool_w, l, pool_scale, hist, tm, tiles_per_seq, pos_offset,
                 emit_hist):
    rows = h.shape[0]
    assert not emit_hist or rows == tm
    kern = functools.partial(_pool_branch_kernel, tm=tm, tiles_per_seq=tiles_per_seq,
                             pos_offset=pos_offset)
    ng = len(POOL_WINDOWS)
    n_out = 2 if emit_hist else 1
    return pl.pallas_call(
        kern,
        grid=(rows // tm,),
        in_specs=[pl.BlockSpec((tm, D_MODEL), lambda i: (i, 0)),
                  _layer_spec(l, (D_MODEL, D_POOL), lambda i: (0, 0)),
                  _layer_spec(l, (ng, POOL_GROUP, POOL_GROUP), lambda i: (0, 0, 0)),
                  pl.BlockSpec((1, D_POOL), lambda i: (0, 0)),
                  pl.BlockSpec((HIST, D_POOL), lambda i: (0, 0))],
        out_specs=[pl.BlockSpec((tm, D_POOL), lambda i: (i, 0)),
                   pl.BlockSpec((HIST, D_POOL), lambda i: (0, 0))][:n_out],
        out_shape=[jax.ShapeDtypeStruct((rows, D_POOL), BF),
                   jax.ShapeDtypeStruct((HIST, D_POOL), F32)][:n_out],
        scratch_shapes=[pltpu.VMEM((HIST, D_POOL), F32)],
        name="pool_branch",
        compiler_params=_params("arbitrary"),
    )(h, w_pool, pool_w, pool_scale, hist)


def _merge_kernel(h_ref, ya_ref, yb_ref, yc_ref, wg0_ref, wg1_ref, wg2_ref, wa_ref, wb_ref, wc_ref,
                  o_ref):
    h = h_ref[...]
    m = jax.nn.sigmoid(_dot(h, wg0_ref[...])) * _dot(ya_ref[...], wa_ref[...])
    m += jax.nn.sigmoid(_dot(h, wg1_ref[...])) * _dot(yb_ref[...], wb_ref[...])
    m += jax.nn.sigmoid(_dot(h, wg2_ref[...])) * _dot(yc_ref[...], wc_ref[...])
    o_ref[...] = m.astype(o_ref.dtype)


def _merge(h, ya, yb, yc, w_gate, wa, wb, wc, l, tm, tn):
    rows = h.shape[0]
    ncb = D_MODEL // tn
    row = lambda width: pl.BlockSpec((tm, width), lambda i, j: (i, 0))
    col = lambda depth, off: _layer_spec(l, (depth, tn), lambda i, j: (0, j + off))
    return pl.pallas_call(
        _merge_kernel,
        grid=(rows // tm, ncb),
        in_specs=[row(D_MODEL), row(D_CONV), row(N_HEADS * V_HEAD), row(D_POOL),
                  col(D_MODEL, 0), col(D_MODEL, ncb), col(D_MODEL, 2 * ncb),
                  col(D_CONV, 0), col(N_HEADS * V_HEAD, 0), col(D_POOL, 0)],
        out_specs=pl.BlockSpec((tm, tn), lambda i, j: (i, j)),
        out_shape=jax.ShapeDtypeStruct((rows, D_MODEL), BF),
        name="merge",
        compiler_params=_params("parallel", "arbitrary"),
    )(h, ya, yb, yc, w_gate, w_gate, w_gate, wa, wb, wc)


def _oproj_kernel(m_ref, w_ref, x_ref, g_ref, xo_ref, h2_ref):
    x = x_ref[...] + _dot(m_ref[...], w_ref[...])
    xo_ref[...] = x
    h2_ref[...] = _rms_norm(x, g_ref[...]).astype(h2_ref.dtype)


def _oproj(merged, w_o, l, x, g, tm):
    rows = x.shape[0]
    return pl.pallas_call(
        _oproj_kernel,
        grid=(rows // tm,),
        in_specs=[pl.BlockSpec((tm, D_MODEL), lambda i: (i, 0)),
                  _layer_spec(l, (D_MODEL, D_MODEL), lambda i: (0, 0)),
                  pl.BlockSpec((tm, D_MODEL), lambda i: (i, 0)),
                  pl.BlockSpec((1, D_MODEL), lambda i: (0, 0))],
        out_specs=[pl.BlockSpec((tm, D_MODEL), lambda i: (i, 0)),
                   pl.BlockSpec((tm, D_MODEL), lambda i: (i, 0))],
        out_shape=[jax.ShapeDtypeStruct((rows, D_MODEL), F32),
                   jax.ShapeDtypeStruct((rows, D_MODEL), BF)],
        name="oproj",
        compiler_params=_params("parallel"),
    )(merged, w_o, x, g)


def _mlp_kernel(h2_ref, wu_ref, wd_ref, x_ref, o_ref):
    @pl.when(pl.program_id(1) == 0)
    def _():
        o_ref[...] = x_ref[...]

    a = jnp.maximum(_dot(h2_ref[...], wu_ref[...]), 0.0)
    o_ref[...] += _dot((a * a).astype(BF), wd_ref[...])


def _mlp(h2, w_up, w_down, l, x, tm, tf):
    rows = x.shape[0]
    return pl.pallas_call(
        _mlp_kernel,
        grid=(rows // tm, D_FF // tf),
        in_specs=[pl.BlockSpec((tm, D_MODEL), lambda i, k: (i, 0)),
                  _layer_spec(l, (D_MODEL, tf), lambda i, k: (0, k)),
                  _layer_spec(l, (tf, D_MODEL), lambda i, k: (k, 0)),
                  pl.BlockSpec((tm, D_MODEL), lambda i, k: (i, 0))],
        out_specs=pl.BlockSpec((tm, D_MODEL), lambda i, k: (i, 0)),
        out_shape=jax.ShapeDtypeStruct((rows, D_MODEL), F32),
        name="mlp",
        compiler_params=_params("parallel", "arbitrary"),
    )(h2, w_up, w_down, x)


def _swap_halves(a):
    return jnp.concatenate([a[..., HALF_ROPE:], a[..., :HALF_ROPE]], axis=-1)


def _twice(a):
    return jnp.concatenate([a, a], axis=-1)


def _widen_head(a):
    rope = a[..., QK_NOPE:]
    return jnp.concatenate([a[..., :QK_NOPE], rope, _swap_halves(rope)], axis=-1)


def _rope_gain_tables(gain, cos_t, sin_t):
    rope_gain = gain[QK_NOPE:]
    return cos_t * rope_gain[None], sin_t * _swap_halves(rope_gain)[None]


def _stacked_weights(w_in, w_uq, w_ukv, pool_w, w_branch_a, w_branch_b, w_branch_c, w_o, w_up,
                     w_down):
    depth = w_in.shape[0]
    wi = w_in.astype(BF)
    o_q = 3 * D_CONV
    o_kr = o_q + Q_LORA + KV_LORA
    o_pool = o_kr + QK_ROPE
    o_gate = o_pool + D_POOL
    w_kr = wi[:, :, o_kr:o_pool]
    w_ukv_h = w_ukv.astype(BF).reshape(depth, KV_LORA, N_HEADS, QK_NOPE + V_HEAD)
    return dict(
        w_in=wi,
        w_lat=jnp.concatenate([wi[:, :, o_q:o_kr], _twice(w_kr), _twice(_swap_halves(w_kr))],
                              axis=2),
        w_pool=wi[:, :, o_pool:o_gate],
        w_gate=wi[:, :, o_gate:],
        w_uq=_widen_head(w_uq.astype(BF).reshape(depth, Q_LORA, N_HEADS, QK_HEAD)).reshape(
            depth, Q_LORA, N_HEADS * HEAD_PAD),
        w_uk=w_ukv_h[..., :QK_NOPE].reshape(depth, KV_LORA, N_HEADS * QK_NOPE),
        w_uv=w_ukv_h[..., QK_NOPE:].reshape(depth, KV_LORA, N_HEADS * V_HEAD),
        pool_w=pool_w.astype(BF),
        wa=w_branch_a.astype(BF),
        wb=w_branch_b.astype(BF),
        wc=w_branch_c.astype(BF),
        w_o=w_o.astype(BF),
        w_up=w_up.astype(BF),
        w_down=w_down.astype(BF),
    )


def _layer_vectors(l, cos_t, sin_t, attn_norm, conv_w, q_lat_norm, kv_lat_norm, q_norm, k_norm,
                   pool_scale, mlp_norm):
    gq = q_norm[l] * (QK_HEAD ** -0.5 * LOG2_E)
    q_cos, q_sin = _rope_gain_tables(gq, cos_t, sin_t)
    k_cos, k_sin = _rope_gain_tables(k_norm[l], cos_t, sin_t)
    return dict(
        attn_norm=attn_norm[l][None],
        conv_w=conv_w[l],
        q_lat_norm=q_lat_norm[l][None],
        kv_lat_norm=kv_lat_norm[l][None],
        gq_nope=gq[None, :QK_NOPE],
        gk_nope=k_norm[l][None, :QK_NOPE],
        q_trig=jnp.concatenate([q_cos, q_sin], axis=1),
        k_rope=(_twice(k_cos), _twice(k_sin)),
        pool_scale=pool_scale[l][None],
        mlp_norm=mlp_norm[l][None],
    )


def _rope_tables(total):
    pos = jnp.arange(total, dtype=F32)
    inv = ROPE_THETA ** (-jnp.arange(0, QK_ROPE, 2, dtype=F32) / QK_ROPE)
    ang = pos[:, None] * inv[None, :]
    cos, sin = jnp.cos(ang), jnp.sin(ang)
    return _twice(cos), jnp.concatenate([-sin, sin], axis=1)


def _layer(x, w, l, p, positions, prefix, *, batch, seq, finish):
    is_meta = prefix is None
    q_trig = p['q_trig'][positions]
    k_cos, k_sin = (t[positions] for t in p['k_rope'])
    tm = min(512, seq)
    tps = seq // tm
    tm_big = min(1024, seq)
    if is_meta:
        conv_hist = jnp.zeros((HIST, D_CONV), F32)
        pool_hist = jnp.zeros((HIST, D_POOL), F32)
    else:
        conv_hist, pool_hist, kp, vp = prefix

    h, qn, kvn, kr = _latent(x, p['attn_norm'], w['w_lat'], l, p['q_lat_norm'], p['kv_lat_norm'],
                             tm)
    ya, *conv_tail = _conv_branch(h, w['w_in'], l, p['conv_w'], conv_hist, tm_big, 512,
                                  seq // tm_big, is_meta)
    yc, *pool_tail = _pool_branch(h, w['w_pool'], w['pool_w'], l, p['pool_scale'], pool_hist, tm,
                                  tps, 0 if is_meta else N_META, is_meta)
    k, v = _kv_proj(kvn, w['w_uk'], w['w_uv'], l, kr, p['gk_nope'], k_cos, k_sin, tm, N_HEADS,
                    tps)
    out_prefix = (conv_tail[0], pool_tail[0], k, v) if is_meta else None
    if not finish:
        return None, out_prefix
    q = _q_proj(qn, w['w_uq'], l, p['gq_nope'], q_trig, tm, N_HEADS, tps)
    if is_meta:
        yb = _attention_meta(q, k, v)
    else:
        pad = ((0, 128 - N_META), (0, 0))
        per_batch = lambda a: a.reshape(batch, seq, a.shape[-1])
        yb = _attention(per_batch(q), per_batch(k), per_batch(v), jnp.pad(kp, pad),
                        jnp.pad(vp, pad), batch, seq, 512, 512, 256, 1).reshape(batch * seq, -1)
    merged = _merge(h, ya, yb, yc, w['w_gate'], w['wa'], w['wb'], w['wc'], l, tm_big, 256)
    x_mid, h2 = _oproj(merged, w['w_o'], l, x, p['mlp_norm'], min(256, seq))
    return _mlp(h2, w['w_up'], w['w_down'], l, x_mid, tm_big, 512), out_prefix


def kernel(x, meta_tokens, attn_norm, w_in, conv_w, q_lat_norm, kv_lat_norm, w_uq, w_ukv, q_norm, k_norm, pool_w, pool_scale, w_branch_a, w_branch_b, w_branch_c, w_o, mlp_norm, w_up, w_down):
    batch, seq, d = x.shape
    depth = w_in.shape[0]
    cos_t, sin_t = _rope_tables(N_META + seq)
    xm = meta_tokens.astype(F32)
    xr = x.reshape(batch * seq, d)
    w = _stacked_weights(w_in, w_uq, w_ukv, pool_w, w_branch_a, w_branch_b, w_branch_c, w_o, w_up,
                         w_down)
    for l in range(depth):
        p = _layer_vectors(l, cos_t, sin_t, attn_norm, conv_w, q_lat_norm, kv_lat_norm, q_norm,
                           k_norm, pool_scale, mlp_norm)
        last = l == depth - 1
        xm, prefix = _layer(xm, w, l, p, slice(0, N_META), None, batch=1, seq=N_META,
                            finish=not last)
        xr, _ = _layer(xr, w, l, p, slice(N_META, None), prefix, batch=batch, seq=seq, finish=True)
    return xr.reshape(batch, seq, d)
```

```python
import functools

import jax
import jax.numpy as jnp
from jax import lax
from jax.experimental import pallas as pl
from jax.experimental.pallas import tpu as pltpu

D_MODEL = 2048
N_META = 16
EPS = 1e-6
D_CONV = 1024
N_HEADS = 16
QK_NOPE = 128
QK_ROPE = 64
QK_HEAD = QK_NOPE + QK_ROPE
HALF_ROPE = QK_ROPE // 2
V_HEAD = 128
Q_LORA = 512
KV_LORA = 512
ROPE_THETA = 10000.0
D_POOL = 1024
POOL_WINDOWS = (2, 4, 8, 16)
POOL_GROUP = D_POOL // len(POOL_WINDOWS)
D_FF = 4 * D_MODEL
HIST = 16
HEAD_PAD = 256
D_LAT = Q_LORA + KV_LORA + 256
VMEM_LIMIT = 56 * 2**20
NEG_BIG = -1e30
LOG2_E = 1.4426950408889634

BF = jnp.bfloat16
F32 = jnp.float32


def _dot(a, b):
    return jnp.dot(a, b, preferred_element_type=F32)


def _dot_nt(a, b):
    return lax.dot_general(a, b, (((1,), (1,)), ((), ())), preferred_element_type=F32)


def _layer_spec(l, shape, index_map):
    return pl.BlockSpec((None,) + shape, lambda *g: (l,) + index_map(*g))


def _params(*sem):
    return pltpu.CompilerParams(dimension_semantics=sem, vmem_limit_bytes=VMEM_LIMIT)


def _rms_norm(x, g):
    ms = jnp.mean(x * x, axis=-1, keepdims=True)
    return x * lax.rsqrt(ms + EPS) * g


def _conv_branch_kernel(h_ref, wu_ref, wb_ref, wc_ref, cw_ref, hist_ref, ya_ref, *rest,
                        tm, tiles_per_seq):
    carry_ref = rest[-1]
    i, j = pl.program_id(0), pl.program_id(1)
    h = h_ref[...]
    cu = _dot(h, wc_ref[...]) * _dot(h, wu_ref[...])

    @pl.when(i % tiles_per_seq == 0)
    def _():
        carry_ref[j] = hist_ref[...]

    ext = jnp.concatenate([carry_ref[j], cu], axis=0)
    tail = cu[tm - HIST:]
    carry_ref[j] = tail
    if len(rest) == 2:
        rest[0][...] = tail
    cw = cw_ref[...]
    y = (cw[2:3] * cu + cw[1:2] * ext[HIST - 1:HIST - 1 + tm] + cw[0:1] * ext[HIST - 2:HIST - 2 + tm])
    ya_ref[...] = (_dot(h, wb_ref[...]) * y).astype(ya_ref.dtype)


def _conv_branch(h, w_in, l, conv_w, hist, tm, tc, tiles_per_seq, emit_hist):
    rows = h.shape[0]
    ncb = D_CONV // tc
    assert not emit_hist or rows == tm
    kern = functools.partial(_conv_branch_kernel, tm=tm, tiles_per_seq=tiles_per_seq)
    n_out = 2 if emit_hist else 1
    return pl.pallas_call(
        kern,
        grid=(rows // tm, ncb),
        in_specs=[pl.BlockSpec((tm, D_MODEL), lambda i, j: (i, 0)),
                  _layer_spec(l, (D_MODEL, tc), lambda i, j: (0, j)),
                  _layer_spec(l, (D_MODEL, tc), lambda i, j: (0, j + ncb)),
                  _layer_spec(l, (D_MODEL, tc), lambda i, j: (0, j + 2 * ncb)),
                  pl.BlockSpec((3, tc), lambda i, j: (0, j)),
                  pl.BlockSpec((HIST, tc), lambda i, j: (0, j))],
        out_specs=[pl.BlockSpec((tm, tc), lambda i, j: (i, j)),
                   pl.BlockSpec((HIST, tc), lambda i, j: (0, j))][:n_out],
        out_shape=[jax.ShapeDtypeStruct((rows, D_CONV), BF),
                   jax.ShapeDtypeStruct((HIST, D_CONV), F32)][:n_out],
        scratch_shapes=[pltpu.VMEM((ncb, HIST, tc), F32)],
        name="conv_branch",
        compiler_params=_params("arbitrary", "arbitrary"),
    )(h, w_in, w_in, w_in, conv_w, hist)


def _latent_kernel(x_ref, g_ref, w_ref, gq_ref, gkv_ref, h_ref, qn_ref, kvn_ref, kr_ref):
    h = _rms_norm(x_ref[...], g_ref[...]).astype(h_ref.dtype)
    h_ref[...] = h
    acc = _dot(h, w_ref[...])
    qn_ref[...] = _rms_norm(acc[:, :Q_LORA], gq_ref[...]).astype(qn_ref.dtype)
    kvn_ref[...] = _rms_norm(acc[:, Q_LORA:Q_LORA + KV_LORA], gkv_ref[...]).astype(kvn_ref.dtype)
    kr_ref[...] = acc[:, Q_LORA + KV_LORA:]


def _latent(x, g, w_lat, l, gq, gkv, tm):
    rows = x.shape[0]
    row = lambda width: pl.BlockSpec((tm, width), lambda i: (i, 0))
    vec = lambda width: pl.BlockSpec((1, width), lambda i: (0, 0))
    return pl.pallas_call(
        _latent_kernel,
        grid=(rows // tm,),
        in_specs=[row(D_MODEL), vec(D_MODEL), _layer_spec(l, (D_MODEL, D_LAT), lambda i: (0, 0)),
                  vec(Q_LORA), vec(KV_LORA)],
        out_specs=[row(D_MODEL), row(Q_LORA), row(KV_LORA), row(256)],
        out_shape=[jax.ShapeDtypeStruct((rows, D_MODEL), BF),
                   jax.ShapeDtypeStruct((rows, Q_LORA), BF),
                   jax.ShapeDtypeStruct((rows, KV_LORA), BF),
                   jax.ShapeDtypeStruct((rows, 256), F32)],
        name="latent",
        compiler_params=_params("parallel"),
    )(x, g, w_lat, gq, gkv)


def _row_sumsq(a, n_real):
    keep = lax.broadcasted_iota(jnp.int32, (a.shape[1], 128), 0) < n_real
    return _dot((a * a).astype(BF), jnp.where(keep, 1.0, 0.0).astype(BF))


def _q_kernel(qn_ref, w_ref, g_ref, trig_ref, q_ref, acc_ref, ss_ref, *, heads, rb):
    acc_ref[...] = _dot(qn_ref[...], w_ref[...])
    for hd in range(heads):
        ss_ref[:, hd * 128:(hd + 1) * 128] = _row_sumsq(
            acc_ref[:, hd * HEAD_PAD:(hd + 1) * HEAD_PAD], QK_HEAD)
    g = g_ref[...]
    for r in range(0, acc_ref.shape[0], rb):
        rs = slice(r, r + rb)
        trig = trig_ref[rs, :]
        for hd in range(heads):
            c0 = hd * HEAD_PAD
            inv = lax.rsqrt(ss_ref[rs, hd * 128:(hd + 1) * 128] * (1.0 / QK_HEAD) + EPS)
            q_ref[rs, c0:c0 + QK_NOPE] = (
                acc_ref[rs, c0:c0 + QK_NOPE] * (inv * g)).astype(q_ref.dtype)
            q_ref[rs, c0 + QK_NOPE:c0 + HEAD_PAD] = (
                acc_ref[rs, c0 + QK_NOPE:c0 + HEAD_PAD] * (inv * trig)).astype(q_ref.dtype)


def _q_proj(qn, w_uq, l, g_nope, trig, tm, heads, tiles_per_seq):
    rows = qn.shape[0]
    kern = functools.partial(_q_kernel, heads=heads, rb=min(64, tm))
    return pl.pallas_call(
        kern,
        grid=(rows // tm, N_HEADS // heads),
        in_specs=[pl.BlockSpec((tm, Q_LORA), lambda i, j: (i, 0)),
                  _layer_spec(l, (Q_LORA, heads * HEAD_PAD), lambda i, j: (0, j)),
                  pl.BlockSpec((1, QK_NOPE), lambda i, j: (0, 0)),
                  pl.BlockSpec((tm, 128), lambda i, j: (i % tiles_per_seq, 0))],
        out_specs=pl.BlockSpec((tm, heads * HEAD_PAD), lambda i, j: (i, j)),
        out_shape=jax.ShapeDtypeStruct((rows, N_HEADS * HEAD_PAD), BF),
        scratch_shapes=[pltpu.VMEM((tm, heads * HEAD_PAD), F32),
                        pltpu.VMEM((tm, heads * 128), F32)],
        name="q_proj",
        compiler_params=_params("parallel", "parallel"),
    )(qn, w_uq, g_nope, trig)


def _kv_kernel(kvn_ref, wk_ref, wv_ref, kr_ref, gn_ref, cos_ref, sin_ref, k_ref, v_ref,
               acc_ref, ss_ref, *, heads, rb):
    kvn = kvn_ref[...]
    acc_ref[...] = _dot(kvn, wk_ref[...])
    v_ref[...] = _dot(kvn, wv_ref[...]).astype(v_ref.dtype)
    blk = (lax.broadcasted_iota(jnp.int32, (256, 256), 0) // 128
           == lax.broadcasted_iota(jnp.int32, (256, 256), 1) // 128)
    pair_ones = jnp.where(blk, 1.0, 0.0).astype(BF)
    for hd in range(0, heads, 2):
        a = acc_ref[:, hd * QK_NOPE:(hd + 2) * QK_NOPE]
        ss_ref[:, hd * 128:(hd + 2) * 128] = _dot((a * a).astype(BF), pair_ones)
    ss_ref[:, heads * 128:] = _row_sumsq(kr_ref[:, :128], QK_ROPE)
    gn = gn_ref[...]
    for r in range(0, acc_ref.shape[0], rb):
        rs = slice(r, r + rb)
        kr_rot = kr_ref[rs, :128] * cos_ref[rs, :] + kr_ref[rs, 128:] * sin_ref[rs, :]
        kr_ss = ss_ref[rs, heads * 128:]
        for hd in range(heads):
            ss = ss_ref[rs, hd * 128:(hd + 1) * 128] + kr_ss
            inv = lax.rsqrt(ss * (1.0 / QK_HEAD) + EPS)
            k_ref[rs, hd * HEAD_PAD:hd * HEAD_PAD + QK_NOPE] = (
                acc_ref[rs, hd * QK_NOPE:(hd + 1) * QK_NOPE] * (inv * gn)).astype(k_ref.dtype)
            k_ref[rs, hd * HEAD_PAD + QK_NOPE:(hd + 1) * HEAD_PAD] = (kr_rot * inv).astype(
                k_ref.dtype)


def _kv_proj(kvn, w_uk, w_uv, l, kr, gn, cos_t, sin_t, tm, heads, tiles_per_seq):
    rows = kvn.shape[0]
    kern = functools.partial(_kv_kernel, heads=heads, rb=min(64, tm))
    return pl.pallas_call(
        kern,
        grid=(rows // tm, N_HEADS // heads),
        in_specs=[pl.BlockSpec((tm, KV_LORA), lambda i, j: (i, 0)),
                  _layer_spec(l, (KV_LORA, heads * QK_NOPE), lambda i, j: (0, j)),
                  _layer_spec(l, (KV_LORA, heads * V_HEAD), lambda i, j: (0, j)),
                  pl.BlockSpec((tm, 256), lambda i, j: (i, 0)),
                  pl.BlockSpec((1, 128), lambda i, j: (0, 0)),
                  pl.BlockSpec((tm, 128), lambda i, j: (i % tiles_per_seq, 0)),
                  pl.BlockSpec((tm, 128), lambda i, j: (i % tiles_per_seq, 0))],
        out_specs=[pl.BlockSpec((tm, heads * HEAD_PAD), lambda i, j: (i, j)),
                   pl.BlockSpec((tm, heads * V_HEAD), lambda i, j: (i, j))],
        out_shape=[jax.ShapeDtypeStruct((rows, N_HEADS * HEAD_PAD), BF),
                   jax.ShapeDtypeStruct((rows, N_HEADS * V_HEAD), BF)],
        scratch_shapes=[pltpu.VMEM((tm, heads * QK_NOPE), F32),
                        pltpu.VMEM((tm, (heads + 1) * 128), F32)],
        name="kv_proj",
        compiler_params=_params("parallel", "parallel"),
    )(kvn, w_uk, w_uv, kr, gn, cos_t, sin_t)


def _attn_kernel(q_ref, k_ref, v_ref, kp_ref, vp_ref, o_ref, head_bufs, *, tq, tc, td, rb,
                 n_prefix):
    for q_tile in range(q_ref.shape[0] // tq):
        _attn_tile(q_ref, k_ref, v_ref, kp_ref, vp_ref, o_ref, head_bufs, q_tile=q_tile, tq=tq,
                   tc=tc, td=td, rb=rb, n_prefix=n_prefix)


def _attn_tile(q_ref, k_ref, v_ref, kp_ref, vp_ref, o_ref, head_bufs, *, q_tile, tq, tc, td, rb,
               n_prefix):
    d0 = q_tile * tq
    items = [None] + [c * tc for c in range(d0 // tc)]
    bands = range(tq // td)
    qcols = lambda hd: slice(hd * HEAD_PAD, (hd + 1) * HEAD_PAD)
    vcols = lambda hd: slice(hd * V_HEAD, (hd + 1) * V_HEAD)
    tile = slice(d0, d0 + tq)

    def scores(hd, k0, s_ref):
        if k0 is not None:
            s_ref[:, :tc] = _dot_nt(q_ref[tile, qcols(hd)], k_ref[k0:k0 + tc, qcols(hd)])
            return
        s_ref[:, tq:tq + 128] = _dot_nt(q_ref[tile, qcols(hd)], kp_ref[:, qcols(hd)])
        for g in bands:
            seen = (g + 1) * td
            s_ref[g * td:seen, :seen] = _dot_nt(q_ref[d0 + g * td:d0 + seen, qcols(hd)],
                                                k_ref[d0:d0 + seen, qcols(hd)])

    def softmax_rows(rs, pieces, p_ref, a_ref, m_ref, l_ref):
        s = jnp.concatenate([x for _, x in pieces], axis=1)
        m_old = m_ref[rs, :]
        m_new = jnp.maximum(m_old, jnp.max(s, axis=-1, keepdims=True))
        alpha = jnp.exp2(m_old - m_new)
        p = jnp.exp2(s - jnp.concatenate([m_new] * (s.shape[1] // 128), axis=1))
        l_ref[rs, :] = alpha * l_ref[rs, :] + jnp.sum(p, axis=-1, keepdims=True)
        m_ref[rs, :] = m_new
        a_ref[rs, :] = alpha
        at = 0
        for cols, x in pieces:
            p_ref[rs, cols] = p[:, at:at + x.shape[1]].astype(p_ref.dtype)
            at += x.shape[1]

    def softmax(k0, s_ref, p_ref, a_ref, m_ref, l_ref):
        for r in range(0, tq, rb):
            rs = slice(r, r + rb)
            if k0 is not None:
                pieces = [(slice(0, tc), s_ref[rs, :tc])]
            else:
                g = r // td
                open_cols, last = slice(0, g * td), slice(g * td, (g + 1) * td)
                row = lax.broadcasted_iota(jnp.int32, (rb, td), 0) + (r - g * td)
                col = lax.broadcasted_iota(jnp.int32, (rb, td), 1)
                colp = lax.broadcasted_iota(jnp.int32, (rb, 128), 1)
                pieces = [(open_cols, s_ref[rs, open_cols])] if g else []
                pieces += [(last, jnp.where(col <= row, s_ref[rs, last], NEG_BIG)),
                           (slice(tq, tq + 128),
                            jnp.where(colp < n_prefix, s_ref[rs, tq:tq + 128], NEG_BIG))]
            softmax_rows(rs, pieces, p_ref, a_ref, m_ref, l_ref)

    def accumulate(hd, k0, p_ref, a_ref, acc_ref):
        if k0 is not None:
            acc_ref[...] = a_ref[...] * acc_ref[...] + _dot(p_ref[:, :tc],
                                                            v_ref[k0:k0 + tc, vcols(hd)])
            return
        for g in bands:
            rows, seen = slice(g * td, (g + 1) * td), (g + 1) * td
            acc_ref[rows, :] = (a_ref[rows, :] * acc_ref[rows, :]
                                + _dot(p_ref[rows, :seen], v_ref[d0:d0 + seen, vcols(hd)])
                                + _dot(p_ref[rows, tq:tq + 128], vp_ref[:, vcols(hd)]))

    for s_refs, p_refs, a_refs, m_ref, l_ref, acc_ref in head_bufs:
        m_ref[...] = jnp.full(m_ref.shape, NEG_BIG, F32)
        l_ref[...] = jnp.zeros(l_ref.shape, F32)
        acc_ref[...] = jnp.zeros(acc_ref.shape, F32)
    n = len(items)
    for t in range(n + 2):
        for hd, (s_refs, p_refs, a_refs, m_ref, l_ref, acc_ref) in enumerate(head_bufs):
            depth = len(s_refs)
            if t < n:
                scores(hd, items[t], s_refs[t % depth])
            if 1 <= t <= n:
                u = (t - 1) % depth
                softmax(items[t - 1], s_refs[u], p_refs[u], a_refs[u], m_ref, l_ref)
            if t >= 2:
                u = (t - 2) % depth
                accumulate(hd, items[t - 2], p_refs[u], a_refs[u], acc_ref)
    for hd, (_, _, _, _, l_ref, acc_ref) in enumerate(head_bufs):
        o_ref[tile, vcols(hd)] = (acc_ref[...] / l_ref[...]).astype(o_ref.dtype)


def _attention(q, k, v, kp, vp, batch, seq, tq, tc, td, heads):
    depth = 3
    stat = pltpu.VMEM((tq, 128), F32)
    width = max(tc, tq + 128)
    head_bufs = [([pltpu.VMEM((tq, width), F32)] * depth, [pltpu.VMEM((tq, width), BF)] * depth,
                  [stat] * depth, stat, stat, pltpu.VMEM((tq, V_HEAD), F32))] * heads
    kern = functools.partial(_attn_kernel, tq=tq, tc=tc, td=td, rb=64, n_prefix=N_META)
    return pl.pallas_call(
        kern,
        grid=(batch, N_HEADS // heads),
        in_specs=[pl.BlockSpec((None, seq, heads * HEAD_PAD), lambda b, h: (b, 0, h)),
                  pl.BlockSpec((None, seq, heads * HEAD_PAD), lambda b, h: (b, 0, h)),
                  pl.BlockSpec((None, seq, heads * V_HEAD), lambda b, h: (b, 0, h)),
                  pl.BlockSpec((128, heads * HEAD_PAD), lambda b, h: (0, h)),
                  pl.BlockSpec((128, heads * V_HEAD), lambda b, h: (0, h))],
        out_specs=pl.BlockSpec((None, seq, heads * V_HEAD), lambda b, h: (b, 0, h)),
        out_shape=jax.ShapeDtypeStruct((batch, seq, N_HEADS * V_HEAD), BF),
        scratch_shapes=[head_bufs],
        name="attention",
        compiler_params=_params("parallel", "parallel"),
    )(q, k, v, kp, vp)


def _attn_meta_kernel(q_ref, k_ref, v_ref, o_ref):
    s = _dot_nt(q_ref[...], k_ref[...])
    row = lax.broadcasted_iota(jnp.int32, s.shape, 0)
    col = lax.broadcasted_iota(jnp.int32, s.shape, 1)
    s = jnp.where(row >= col, s, NEG_BIG)
    p = jnp.exp2(s - jnp.max(s, axis=-1, keepdims=True))
    l = jnp.sum(p, axis=-1, keepdims=True)
    o_ref[...] = (_dot(p.astype(BF), v_ref[...]) / l).astype(o_ref.dtype)


def _attention_meta(q, k, v):
    return pl.pallas_call(
        _attn_meta_kernel,
        grid=(N_HEADS,),
        in_specs=[pl.BlockSpec((N_META, HEAD_PAD), lambda h: (0, h)),
                  pl.BlockSpec((N_META, HEAD_PAD), lambda h: (0, h)),
                  pl.BlockSpec((N_META, V_HEAD), lambda h: (0, h))],
        out_specs=pl.BlockSpec((N_META, V_HEAD), lambda h: (0, h)),
        out_shape=jax.ShapeDtypeStruct((N_META, N_HEADS * V_HEAD), BF),
        name="attention_meta",
        compiler_params=_params("parallel"),
    )(q, k, v)


def _pool_branch_kernel(h_ref, w_ref, pw_ref, ps_ref, hist_ref, yc_ref, *rest,
                        tm, tiles_per_seq, pos_offset):
    carry_ref = rest[-1]
    i = pl.program_id(0)
    pin = _dot(h_ref[...], w_ref[...])

    @pl.when(i % tiles_per_seq == 0)
    def _():
        carry_ref[...] = hist_ref[...]

    ext = jnp.concatenate([carry_ref[...], pin], axis=0)
    tail = pin[tm - HIST:]
    carry_ref[...] = tail
    if len(rest) == 2:
        rest[0][...] = tail
    n_ext = tm + HIST
    seen = (lax.broadcasted_iota(jnp.int32, (tm, 1), 0)
            + ((i % tiles_per_seq) * tm + pos_offset + 1)).astype(F32)
    for g, w in enumerate(POOL_WINDOWS):
        xg = ext[:, g * POOL_GROUP:(g + 1) * POOL_GROUP]
        s, span = xg, 1
        while span < w:
            s = s[span:] + s[:s.shape[0] - span]
            span *= 2
        s = s[n_ext - (w - 1) - tm:]
        pooled = s / jnp.minimum(seen, float(w)) - xg[HIST:]
        mixed = _dot(pooled.astype(BF), pw_ref[g])
        yc_ref[:, g * POOL_GROUP:(g + 1) * POOL_GROUP] = (
            mixed * ps_ref[:, g * POOL_GROUP:(g + 1) * POOL_GROUP]).astype(yc_ref.dtype)


def _pool_branch(h, w_pool, pool_w, l, pool_scale, hist, tm, tiles_per_seq, pos_offset,
                 emit_hist):
    rows = h.shape[0]
    assert not emit_hist or rows == tm
    kern = functools.partial(_pool_branch_kernel, tm=tm, tiles_per_seq=tiles_per_seq,
                             pos_offset=pos_offset)
    ng = len(POOL_WINDOWS)
    n_out = 2 if emit_hist else 1
    return pl.pallas_call(
        kern,
        grid=(rows // tm,),
        in_specs=[pl.BlockSpec((tm, D_MODEL), lambda i: (i, 0)),
                  _layer_spec(l, (D_MODEL, D_POOL), lambda i: (0, 0)),
                  _layer_spec(l, (ng, POOL_GROUP, POOL_GROUP), lambda i: (0, 0, 0)),
                  pl.BlockSpec((1, D_POOL), lambda i: (0, 0)),
                  pl.BlockSpec((HIST, D_POOL), lambda i: (0, 0))],
        out_specs=[pl.BlockSpec((tm, D_POOL), lambda i: (i, 0)),
                   pl.BlockSpec((HIST, D_POOL), lambda i: (0, 0))][:n_out],
        out_shape=[jax.ShapeDtypeStruct((rows, D_POOL), BF),
                   jax.ShapeDtypeStruct((HIST, D_POOL), F32)][:n_out],
        scratch_shapes=[pltpu.VMEM((HIST, D_POOL), F32)],
        name="pool_branch",
        compiler_params=_params("arbitrary"),
    )(h, w_pool, pool_w, pool_scale, hist)


def _merge_kernel(h_ref, ya_ref, yb_ref, yc_ref, wg0_ref, wg1_ref, wg2_ref, wa_ref, wb_ref, wc_ref,
                  o_ref):
    h = h_ref[...]
    m = jax.nn.sigmoid(_dot(h, wg0_ref[...])) * _dot(ya_ref[...], wa_ref[...])
    m += jax.nn.sigmoid(_dot(h, wg1_ref[...])) * _dot(yb_ref[...], wb_ref[...])
    m += jax.nn.sigmoid(_dot(h, wg2_ref[...])) * _dot(yc_ref[...], wc_ref[...])
    o_ref[...] = m.astype(o_ref.dtype)


def _merge(h, ya, yb, yc, w_gate, wa, wb, wc, l, tm, tn):
    rows = h.shape[0]
    ncb = D_MODEL // tn
    row = lambda width: pl.BlockSpec((tm, width), lambda i, j: (i, 0))
    col = lambda depth, off: _layer_spec(l, (depth, tn), lambda i, j: (0, j + off))
    return pl.pallas_call(
        _merge_kernel,
        grid=(rows // tm, ncb),
        in_specs=[row(D_MODEL), row(D_CONV), row(N_HEADS * V_HEAD), row(D_POOL),
                  col(D_MODEL, 0), col(D_MODEL, ncb), col(D_MODEL, 2 * ncb),
                  col(D_CONV, 0), col(N_HEADS * V_HEAD, 0), col(D_POOL, 0)],
        out_specs=pl.BlockSpec((tm, tn), lambda i, j: (i, j)),
        out_shape=jax.ShapeDtypeStruct((rows, D_MODEL), BF),
        name="merge",
        compiler_params=_params("parallel", "arbitrary"),
    )(h, ya, yb, yc, w_gate, w_gate, w_gate, wa, wb, wc)


def _oproj_kernel(m_ref, w_ref, x_ref, g_ref, xo_ref, h2_ref):
    x = x_ref[...] + _dot(m_ref[...], w_ref[...])
    xo_ref[...] = x
    h2_ref[...] = _rms_norm(x, g_ref[...]).astype(h2_ref.dtype)


def _oproj(merged, w_o, l, x, g, tm):
    rows = x.shape[0]
    return pl.pallas_call(
        _oproj_kernel,
        grid=(rows // tm,),
        in_specs=[pl.BlockSpec((tm, D_MODEL), lambda i: (i, 0)),
                  _layer_spec(l, (D_MODEL, D_MODEL), lambda i: (0, 0)),
                  pl.BlockSpec((tm, D_MODEL), lambda i: (i, 0)),
                  pl.BlockSpec((1, D_MODEL), lambda i: (0, 0))],
        out_specs=[pl.BlockSpec((tm, D_MODEL), lambda i: (i, 0)),
                   pl.BlockSpec((tm, D_MODEL), lambda i: (i, 0))],
        out_shape=[jax.ShapeDtypeStruct((rows, D_MODEL), F32),
                   jax.ShapeDtypeStruct((rows, D_MODEL), BF)],
        name="oproj",
        compiler_params=_params("parallel"),
    )(merged, w_o, x, g)


def _mlp_kernel(h2_ref, wu_ref, wd_ref, x_ref, o_ref):
    @pl.when(pl.program_id(1) == 0)
    def _():
        o_ref[...] = x_ref[...]

    a = jnp.maximum(_dot(h2_ref[...], wu_ref[...]), 0.0)
    o_ref[...] += _dot((a * a).astype(BF), wd_ref[...])


def _mlp(h2, w_up, w_down, l, x, tm, tf):
    rows = x.shape[0]
    return pl.pallas_call(
        _mlp_kernel,
        grid=(rows // tm, D_FF // tf),
        in_specs=[pl.BlockSpec((tm, D_MODEL), lambda i, k: (i, 0)),
                  _layer_spec(l, (D_MODEL, tf), lambda i, k: (0, k)),
                  _layer_spec(l, (tf, D_MODEL), lambda i, k: (k, 0)),
                  pl.BlockSpec((tm, D_MODEL), lambda i, k: (i, 0))],
        out_specs=pl.BlockSpec((tm, D_MODEL), lambda i, k: (i, 0)),
        out_shape=jax.ShapeDtypeStruct((rows, D_MODEL), F32),
        name="mlp",
        compiler_params=_params("parallel", "arbitrary"),
    )(h2, w_up, w_down, x)


def _swap_halves(a):
    return jnp.concatenate([a[..., HALF_ROPE:], a[..., :HALF_ROPE]], axis=-1)


def _twice(a):
    return jnp.concatenate([a, a], axis=-1)


def _widen_head(a):
    rope = a[..., QK_NOPE:]
    return jnp.concatenate([a[..., :QK_NOPE], rope, _swap_halves(rope)], axis=-1)


def _rope_gain_tables(gain, cos_t, sin_t):
    rope_gain = gain[QK_NOPE:]
    return cos_t * rope_gain[None], sin_t * _swap_halves(rope_gain)[None]


def _stacked_weights(w_in, w_uq, w_ukv, pool_w, w_branch_a, w_branch_b, w_branch_c, w_o, w_up,
                     w_down):
    depth = w_in.shape[0]
    wi = w_in.astype(BF)
    o_q = 3 * D_CONV
    o_kr = o_q + Q_LORA + KV_LORA
    o_pool = o_kr + QK_ROPE
    o_gate = o_pool + D_POOL
    w_kr = wi[:, :, o_kr:o_pool]
    w_ukv_h = w_ukv.astype(BF).reshape(depth, KV_LORA, N_HEADS, QK_NOPE + V_HEAD)
    return dict(
        w_in=wi,
        w_lat=jnp.concatenate([wi[:, :, o_q:o_kr], _twice(w_kr), _twice(_swap_halves(w_kr))],
                              axis=2),
        w_pool=wi[:, :, o_pool:o_gate],
        w_gate=wi[:, :, o_gate:],
        w_uq=_widen_head(w_uq.astype(BF).reshape(depth, Q_LORA, N_HEADS, QK_HEAD)).reshape(
            depth, Q_LORA, N_HEADS * HEAD_PAD),
        w_uk=w_ukv_h[..., :QK_NOPE].reshape(depth, KV_LORA, N_HEADS * QK_NOPE),
        w_uv=w_ukv_h[..., QK_NOPE:].reshape(depth, KV_LORA, N_HEADS * V_HEAD),
        pool_w=pool_w.astype(BF),
        wa=w_branch_a.astype(BF),
        wb=w_branch_b.astype(BF),
        wc=w_branch_c.astype(BF),
        w_o=w_o.astype(BF),
        w_up=w_up.astype(BF),
        w_down=w_down.astype(BF),
    )


def _layer_vectors(l, cos_t, sin_t, attn_norm, conv_w, q_lat_norm, kv_lat_norm, q_norm, k_norm,
                   pool_scale, mlp_norm):
    gq = q_norm[l] * (QK_HEAD ** -0.5 * LOG2_E)
    q_cos, q_sin = _rope_gain_tables(gq, cos_t, sin_t)
    k_cos, k_sin = _rope_gain_tables(k_norm[l], cos_t, sin_t)
    return dict(
        attn_norm=attn_norm[l][None],
        conv_w=conv_w[l],
        q_lat_norm=q_lat_norm[l][None],
        kv_lat_norm=kv_lat_norm[l][None],
        gq_nope=gq[None, :QK_NOPE],
        gk_nope=k_norm[l][None, :QK_NOPE],
        q_trig=jnp.concatenate([q_cos, q_sin], axis=1),
        k_rope=(_twice(k_cos), _twice(k_sin)),
        pool_scale=pool_scale[l][None],
        mlp_norm=mlp_norm[l][None],
    )


def _rope_tables(total):
    pos = jnp.arange(total, dtype=F32)
    inv = ROPE_THETA ** (-jnp.arange(0, QK_ROPE, 2, dtype=F32) / QK_ROPE)
    ang = pos[:, None] * inv[None, :]
    cos, sin = jnp.cos(ang), jnp.sin(ang)
    return _twice(cos), jnp.concatenate([-sin, sin], axis=1)


def _layer(x, w, l, p, positions, prefix, *, batch, seq, finish):
    is_meta = prefix is None
    q_trig = p['q_trig'][positions]
    k_cos, k_sin = (t[positions] for t in p['k_rope'])
    tm = min(512, seq)
    tps = seq // tm
    tm_big = min(1024, seq)
    if is_meta:
        conv_hist = jnp.zeros((HIST, D_CONV), F32)
        pool_hist = jnp.zeros((HIST, D_POOL), F32)
    else:
        conv_hist, pool_hist, kp, vp = prefix

    h, qn, kvn, kr = _latent(x, p['attn_norm'], w['w_lat'], l, p['q_lat_norm'], p['kv_lat_norm'],
                             tm)
    ya, *conv_tail = _conv_branch(h, w['w_in'], l, p['conv_w'], conv_hist, tm_big, 512,
                                  seq // tm_big, is_meta)
    yc, *pool_tail = _pool_branch(h, w['w_pool'], w['pool_w'], l, p['pool_scale'], pool_hist,
                                  tm_big, seq // tm_big, 0 if is_meta else N_META, is_meta)
    k, v = _kv_proj(kvn, w['w_uk'], w['w_uv'], l, kr, p['gk_nope'], k_cos, k_sin, tm, N_HEADS,
                    tps)
    out_prefix = (conv_tail[0], pool_tail[0], k, v) if is_meta else None
    if not finish:
        return None, out_prefix
    q = _q_proj(qn, w['w_uq'], l, p['gq_nope'], q_trig, tm, N_HEADS, tps)
    if is_meta:
        yb = _attention_meta(q, k, v)
    else:
        pad = ((0, 128 - N_META), (0, 0))
        per_batch = lambda a: a.reshape(batch, seq, a.shape[-1])
        yb = _attention(per_batch(q), per_batch(k), per_batch(v), jnp.pad(kp, pad),
                        jnp.pad(vp, pad), batch, seq, 512, 512, 256, 1).reshape(batch * seq, -1)
    merged = _merge(h, ya, yb, yc, w['w_gate'], w['wa'], w['wb'], w['wc'], l, tm_big, 512)
    x_mid, h2 = _oproj(merged, w['w_o'], l, x, p['mlp_norm'], min(256, seq))
    return _mlp(h2, w['w_up'], w['w_down'], l, x_mid, tm_big, 512), out_prefix


def kernel(x, meta_tokens, attn_norm, w_in, conv_w, q_lat_norm, kv_lat_norm, w_uq, w_ukv, q_norm, k_norm, pool_w, pool_scale, w_branch_a, w_branch_b, w_branch_c, w_o, mlp_norm, w_up, w_down):
    batch, seq, d = x.shape
    depth = w_in.shape[0]
    cos_t, sin_t = _rope_tables(N_META + seq)
    xm = meta_tokens.astype(F32)
    xr = x.reshape(batch * seq, d)
    w = _stacked_weights(w_in, w_uq, w_ukv, pool_w, w_branch_a, w_branch_b, w_branch_c, w_o, w_up,
                         w_down)
    for l in range(depth):
        p = _layer_vectors(l, cos_t, sin_t, attn_norm, conv_w, q_lat_norm, kv_lat_norm, q_norm,
                           k_norm, pool_scale, mlp_norm)
        last = l == depth - 1
        xm, prefix = _layer(xm, w, l, p, slice(0, N_META), None, batch=1, seq=N_META,
                            finish=not last)
        xr, _ = _layer(xr, w, l, p, slice(N_META, None), prefix, batch=batch, seq=seq, finish=True)
    return xr.reshape(batch, seq, d)
```

```python
import functools

import jax
import jax.numpy as jnp
from jax import lax
from jax.experimental import pallas as pl
from jax.experimental.pallas import tpu as pltpu

D_MODEL = 2048
N_META = 16
EPS = 1e-6
D_CONV = 1024
N_HEADS = 16
QK_NOPE = 128
QK_ROPE = 64
QK_HEAD = QK_NOPE + QK_ROPE
HALF_ROPE = QK_ROPE // 2
V_HEAD = 128
Q_LORA = 512
KV_LORA = 512
ROPE_THETA = 10000.0
D_POOL = 1024
POOL_WINDOWS = (2, 4, 8, 16)
POOL_GROUP = D_POOL // len(POOL_WINDOWS)
D_FF = 4 * D_MODEL
HIST = 16
HEAD_PAD = 256
D_LAT = Q_LORA + KV_LORA + 256
VMEM_LIMIT = 56 * 2**20
NEG_BIG = -1e30
LOG2_E = 1.4426950408889634

BF = jnp.bfloat16
F32 = jnp.float32


def _dot(a, b):
    return jnp.dot(a, b, preferred_element_type=F32)


def _dot_nt(a, b):
    return lax.dot_general(a, b, (((1,), (1,)), ((), ())), preferred_element_type=F32)


def _layer_spec(l, shape, index_map):
    return pl.BlockSpec((None,) + shape, lambda *g: (l,) + index_map(*g))


def _params(*sem):
    return pltpu.CompilerParams(dimension_semantics=sem, vmem_limit_bytes=VMEM_LIMIT)


def _rms_norm(x, g):
    ms = jnp.mean(x * x, axis=-1, keepdims=True)
    return x * lax.rsqrt(ms + EPS) * g


def _conv_branch_kernel(h_ref, wu_ref, wb_ref, wc_ref, cw_ref, hist_ref, ya_ref, *rest,
                        tm, tiles_per_seq):
    carry_ref = rest[-1]
    i, j = pl.program_id(0), pl.program_id(1)
    h = h_ref[...]
    cu = _dot(h, wc_ref[...]) * _dot(h, wu_ref[...])

    @pl.when(i % tiles_per_seq == 0)
    def _():
        carry_ref[j] = hist_ref[...]

    ext = jnp.concatenate([carry_ref[j], cu], axis=0)
    tail = cu[tm - HIST:]
    carry_ref[j] = tail
    if len(rest) == 2:
        rest[0][...] = tail
    cw = cw_ref[...]
    y = (cw[2:3] * cu + cw[1:2] * ext[HIST - 1:HIST - 1 + tm] + cw[0:1] * ext[HIST - 2:HIST - 2 + tm])
    ya_ref[...] = (_dot(h, wb_ref[...]) * y).astype(ya_ref.dtype)


def _conv_branch(h, w_in, l, conv_w, hist, tm, tc, tiles_per_seq, emit_hist):
    rows = h.shape[0]
    ncb = D_CONV // tc
    assert not emit_hist or rows == tm
    kern = functools.partial(_conv_branch_kernel, tm=tm, tiles_per_seq=tiles_per_seq)
    n_out = 2 if emit_hist else 1
    return pl.pallas_call(
        kern,
        grid=(rows // tm, ncb),
        in_specs=[pl.BlockSpec((tm, D_MODEL), lambda i, j: (i, 0)),
                  _layer_spec(l, (D_MODEL, tc), lambda i, j: (0, j)),
                  _layer_spec(l, (D_MODEL, tc), lambda i, j: (0, j + ncb)),
                  _layer_spec(l, (D_MODEL, tc), lambda i, j: (0, j + 2 * ncb)),
                  pl.BlockSpec((3, tc), lambda i, j: (0, j)),
                  pl.BlockSpec((HIST, tc), lambda i, j: (0, j))],
        out_specs=[pl.BlockSpec((tm, tc), lambda i, j: (i, j)),
                   pl.BlockSpec((HIST, tc), lambda i, j: (0, j))][:n_out],
        out_shape=[jax.ShapeDtypeStruct((rows, D_CONV), BF),
                   jax.ShapeDtypeStruct((HIST, D_CONV), F32)][:n_out],
        scratch_shapes=[pltpu.VMEM((ncb, HIST, tc), F32)],
        name="conv_branch",
        compiler_params=_params("arbitrary", "arbitrary"),
    )(h, w_in, w_in, w_in, conv_w, hist)


def _latent_kernel(x_ref, g_ref, w_ref, gq_ref, gkv_ref, h_ref, qn_ref, kvn_ref, kr_ref):
    h = _rms_norm(x_ref[...], g_ref[...]).astype(h_ref.dtype)
    h_ref[...] = h
    acc = _dot(h, w_ref[...])
    qn_ref[...] = _rms_norm(acc[:, :Q_LORA], gq_ref[...]).astype(qn_ref.dtype)
    kvn_ref[...] = _rms_norm(acc[:, Q_LORA:Q_LORA + KV_LORA], gkv_ref[...]).astype(kvn_ref.dtype)
    kr_ref[...] = acc[:, Q_LORA + KV_LORA:]


def _latent(x, g, w_lat, l, gq, gkv, tm):
    rows = x.shape[0]
    row = lambda width: pl.BlockSpec((tm, width), lambda i: (i, 0))
    vec = lambda width: pl.BlockSpec((1, width), lambda i: (0, 0))
    return pl.pallas_call(
        _latent_kernel,
        grid=(rows // tm,),
        in_specs=[row(D_MODEL), vec(D_MODEL), _layer_spec(l, (D_MODEL, D_LAT), lambda i: (0, 0)),
                  vec(Q_LORA), vec(KV_LORA)],
        out_specs=[row(D_MODEL), row(Q_LORA), row(KV_LORA), row(256)],
        out_shape=[jax.ShapeDtypeStruct((rows, D_MODEL), BF),
                   jax.ShapeDtypeStruct((rows, Q_LORA), BF),
                   jax.ShapeDtypeStruct((rows, KV_LORA), BF),
                   jax.ShapeDtypeStruct((rows, 256), F32)],
        name="latent",
        compiler_params=_params("parallel"),
    )(x, g, w_lat, gq, gkv)


def _row_sumsq(a, n_real):
    keep = lax.broadcasted_iota(jnp.int32, (a.shape[1], 128), 0) < n_real
    return _dot((a * a).astype(BF), jnp.where(keep, 1.0, 0.0).astype(BF))


def _q_kernel(qn_ref, w_ref, g_ref, trig_ref, q_ref, acc_ref, ss_ref, *, heads, rb):
    acc_ref[...] = _dot(qn_ref[...], w_ref[...])
    for hd in range(heads):
        ss_ref[:, hd * 128:(hd + 1) * 128] = _row_sumsq(
            acc_ref[:, hd * HEAD_PAD:(hd + 1) * HEAD_PAD], QK_HEAD)
    g = g_ref[...]
    for r in range(0, acc_ref.shape[0], rb):
        rs = slice(r, r + rb)
        trig = trig_ref[rs, :]
        for hd in range(heads):
            c0 = hd * HEAD_PAD
            inv = lax.rsqrt(ss_ref[rs, hd * 128:(hd + 1) * 128] * (1.0 / QK_HEAD) + EPS)
            q_ref[rs, c0:c0 + QK_NOPE] = (
                acc_ref[rs, c0:c0 + QK_NOPE] * (inv * g)).astype(q_ref.dtype)
            q_ref[rs, c0 + QK_NOPE:c0 + HEAD_PAD] = (
                acc_ref[rs, c0 + QK_NOPE:c0 + HEAD_PAD] * (inv * trig)).astype(q_ref.dtype)


def _q_proj(qn, w_uq, l, g_nope, trig, tm, heads, tiles_per_seq):
    rows = qn.shape[0]
    kern = functools.partial(_q_kernel, heads=heads, rb=min(64, tm))
    return pl.pallas_call(
        kern,
        grid=(rows // tm, N_HEADS // heads),
        in_specs=[pl.BlockSpec((tm, Q_LORA), lambda i, j: (i, 0)),
                  _layer_spec(l, (Q_LORA, heads * HEAD_PAD), lambda i, j: (0, j)),
                  pl.BlockSpec((1, QK_NOPE), lambda i, j: (0, 0)),
                  pl.BlockSpec((tm, 128), lambda i, j: (i % tiles_per_seq, 0))],
        out_specs=pl.BlockSpec((tm, heads * HEAD_PAD), lambda i, j: (i, j)),
        out_shape=jax.ShapeDtypeStruct((rows, N_HEADS * HEAD_PAD), BF),
        scratch_shapes=[pltpu.VMEM((tm, heads * HEAD_PAD), F32),
                        pltpu.VMEM((tm, heads * 128), F32)],
        name="q_proj",
        compiler_params=_params("parallel", "parallel"),
    )(qn, w_uq, g_nope, trig)


def _kv_kernel(kvn_ref, wk_ref, wv_ref, kr_ref, gn_ref, cos_ref, sin_ref, k_ref, v_ref,
               acc_ref, ss_ref, *, heads, rb):
    kvn = kvn_ref[...]
    acc_ref[...] = _dot(kvn, wk_ref[...])
    v_ref[...] = _dot(kvn, wv_ref[...]).astype(v_ref.dtype)
    blk = (lax.broadcasted_iota(jnp.int32, (256, 256), 0) // 128
           == lax.broadcasted_iota(jnp.int32, (256, 256), 1) // 128)
    pair_ones = jnp.where(blk, 1.0, 0.0).astype(BF)
    for hd in range(0, heads, 2):
        a = acc_ref[:, hd * QK_NOPE:(hd + 2) * QK_NOPE]
        ss_ref[:, hd * 128:(hd + 2) * 128] = _dot((a * a).astype(BF), pair_ones)
    ss_ref[:, heads * 128:] = _row_sumsq(kr_ref[:, :128], QK_ROPE)
    gn = gn_ref[...]
    for r in range(0, acc_ref.shape[0], rb):
        rs = slice(r, r + rb)
        kr_rot = kr_ref[rs, :128] * cos_ref[rs, :] + kr_ref[rs, 128:] * sin_ref[rs, :]
        kr_ss = ss_ref[rs, heads * 128:]
        for hd in range(heads):
            ss = ss_ref[rs, hd * 128:(hd + 1) * 128] + kr_ss
            inv = lax.rsqrt(ss * (1.0 / QK_HEAD) + EPS)
            k_ref[rs, hd * HEAD_PAD:hd * HEAD_PAD + QK_NOPE] = (
                acc_ref[rs, hd * QK_NOPE:(hd + 1) * QK_NOPE] * (inv * gn)).astype(k_ref.dtype)
            k_ref[rs, hd * HEAD_PAD + QK_NOPE:(hd + 1) * HEAD_PAD] = (kr_rot * inv).astype(
                k_ref.dtype)


def _kv_proj(kvn, w_uk, w_uv, l, kr, gn, cos_t, sin_t, tm, heads, tiles_per_seq):
    rows = kvn.shape[0]
    kern = functools.partial(_kv_kernel, heads=heads, rb=min(64, tm))
    return pl.pallas_call(
        kern,
        grid=(rows // tm, N_HEADS // heads),
        in_specs=[pl.BlockSpec((tm, KV_LORA), lambda i, j: (i, 0)),
                  _layer_spec(l, (KV_LORA, heads * QK_NOPE), lambda i, j: (0, j)),
                  _layer_spec(l, (KV_LORA, heads * V_HEAD), lambda i, j: (0, j)),
                  pl.BlockSpec((tm, 256), lambda i, j: (i, 0)),
                  pl.BlockSpec((1, 128), lambda i, j: (0, 0)),
                  pl.BlockSpec((tm, 128), lambda i, j: (i % tiles_per_seq, 0)),
                  pl.BlockSpec((tm, 128), lambda i, j: (i % tiles_per_seq, 0))],
        out_specs=[pl.BlockSpec((tm, heads * HEAD_PAD), lambda i, j: (i, j)),
                   pl.BlockSpec((tm, heads * V_HEAD), lambda i, j: (i, j))],
        out_shape=[jax.ShapeDtypeStruct((rows, N_HEADS * HEAD_PAD), BF),
                   jax.ShapeDtypeStruct((rows, N_HEADS * V_HEAD), BF)],
        scratch_shapes=[pltpu.VMEM((tm, heads * QK_NOPE), F32),
                        pltpu.VMEM((tm, (heads + 1) * 128), F32)],
        name="kv_proj",
        compiler_params=_params("parallel", "parallel"),
    )(kvn, w_uk, w_uv, kr, gn, cos_t, sin_t)


def _attn_kernel(q_ref, k_ref, v_ref, kp_ref, vp_ref, o_ref, head_bufs, *, tq, tc, td, rb,
                 n_prefix):
    for q_tile in range(q_ref.shape[0] // tq):
        _attn_tile(q_ref, k_ref, v_ref, kp_ref, vp_ref, o_ref, head_bufs, q_tile=q_tile, tq=tq,
                   tc=tc, td=td, rb=rb, n_prefix=n_prefix)


def _attn_tile(q_ref, k_ref, v_ref, kp_ref, vp_ref, o_ref, head_bufs, *, q_tile, tq, tc, td, rb,
               n_prefix):
    d0 = q_tile * tq
    items = [None] + [c * tc for c in range(d0 // tc)]
    bands = range(tq // td)
    qcols = lambda hd: slice(hd * HEAD_PAD, (hd + 1) * HEAD_PAD)
    vcols = lambda hd: slice(hd * V_HEAD, (hd + 1) * V_HEAD)
    tile = slice(d0, d0 + tq)

    def scores(hd, k0, s_ref):
        if k0 is not None:
            s_ref[:, :tc] = _dot_nt(q_ref[tile, qcols(hd)], k_ref[k0:k0 + tc, qcols(hd)])
            return
        s_ref[:, tq:tq + 128] = _dot_nt(q_ref[tile, qcols(hd)], kp_ref[:, qcols(hd)])
        for g in bands:
            seen = (g + 1) * td
            s_ref[g * td:seen, :seen] = _dot_nt(q_ref[d0 + g * td:d0 + seen, qcols(hd)],
                                                k_ref[d0:d0 + seen, qcols(hd)])

    def softmax_rows(rs, pieces, p_ref, a_ref, m_ref, l_ref):
        s = jnp.concatenate([x for _, x in pieces], axis=1)
        m_old = m_ref[rs, :]
        m_new = jnp.maximum(m_old, jnp.max(s, axis=-1, keepdims=True))
        alpha = jnp.exp2(m_old - m_new)
        p = jnp.exp2(s - jnp.concatenate([m_new] * (s.shape[1] // 128), axis=1))
        l_ref[rs, :] = alpha * l_ref[rs, :] + jnp.sum(p, axis=-1, keepdims=True)
        m_ref[rs, :] = m_new
        a_ref[rs, :] = alpha
        at = 0
        for cols, x in pieces:
            p_ref[rs, cols] = p[:, at:at + x.shape[1]].astype(p_ref.dtype)
            at += x.shape[1]

    def softmax(k0, s_ref, p_ref, a_ref, m_ref, l_ref):
        for r in range(0, tq, rb):
            rs = slice(r, r + rb)
            if k0 is not None:
                pieces = [(slice(0, tc), s_ref[rs, :tc])]
            else:
                g = r // td
                open_cols, last = slice(0, g * td), slice(g * td, (g + 1) * td)
                row = lax.broadcasted_iota(jnp.int32, (rb, td), 0) + (r - g * td)
                col = lax.broadcasted_iota(jnp.int32, (rb, td), 1)
                colp = lax.broadcasted_iota(jnp.int32, (rb, 128), 1)
                pieces = [(open_cols, s_ref[rs, open_cols])] if g else []
                pieces += [(last, jnp.where(col <= row, s_ref[rs, last], NEG_BIG)),
                           (slice(tq, tq + 128),
                            jnp.where(colp < n_prefix, s_ref[rs, tq:tq + 128], NEG_BIG))]
            softmax_rows(rs, pieces, p_ref, a_ref, m_ref, l_ref)

    def accumulate(hd, k0, p_ref, a_ref, acc_ref):
        if k0 is not None:
            acc_ref[...] = a_ref[...] * acc_ref[...] + _dot(p_ref[:, :tc],
                                                            v_ref[k0:k0 + tc, vcols(hd)])
            return
        for g in bands:
            rows, seen = slice(g * td, (g + 1) * td), (g + 1) * td
            acc_ref[rows, :] = (a_ref[rows, :] * acc_ref[rows, :]
                                + _dot(p_ref[rows, :seen], v_ref[d0:d0 + seen, vcols(hd)])
                                + _dot(p_ref[rows, tq:tq + 128], vp_ref[:, vcols(hd)]))

    for s_refs, p_refs, a_refs, m_ref, l_ref, acc_ref in head_bufs:
        m_ref[...] = jnp.full(m_ref.shape, NEG_BIG, F32)
        l_ref[...] = jnp.zeros(l_ref.shape, F32)
        acc_ref[...] = jnp.zeros(acc_ref.shape, F32)
    n = len(items)
    for t in range(n + 2):
        for hd, (s_refs, p_refs, a_refs, m_ref, l_ref, acc_ref) in enumerate(head_bufs):
            depth = len(s_refs)
            if t < n:
                scores(hd, items[t], s_refs[t % depth])
            if 1 <= t <= n:
                u = (t - 1) % depth
                softmax(items[t - 1], s_refs[u], p_refs[u], a_refs[u], m_ref, l_ref)
            if t >= 2:
                u = (t - 2) % depth
                accumulate(hd, items[t - 2], p_refs[u], a_refs[u], acc_ref)
    for hd, (_, _, _, _, l_ref, acc_ref) in enumerate(head_bufs):
        o_ref[tile, vcols(hd)] = (acc_ref[...] / l_ref[...]).astype(o_ref.dtype)


def _attention(q, k, v, kp, vp, batch, seq, tq, tc, td, heads):
    depth = 3
    stat = pltpu.VMEM((tq, 128), F32)
    width = max(tc, tq + 128)
    head_bufs = [([pltpu.VMEM((tq, width), F32)] * depth, [pltpu.VMEM((tq, width), BF)] * depth,
                  [stat] * depth, stat, stat, pltpu.VMEM((tq, V_HEAD), F32))] * heads
    kern = functools.partial(_attn_kernel, tq=tq, tc=tc, td=td, rb=64, n_prefix=N_META)
    return pl.pallas_call(
        kern,
        grid=(batch, N_HEADS // heads),
        in_specs=[pl.BlockSpec((None, seq, heads * HEAD_PAD), lambda b, h: (b, 0, h)),
                  pl.BlockSpec((None, seq, heads * HEAD_PAD), lambda b, h: (b, 0, h)),
                  pl.BlockSpec((None, seq, heads * V_HEAD), lambda b, h: (b, 0, h)),
                  pl.BlockSpec((128, heads * HEAD_PAD), lambda b, h: (0, h)),
                  pl.BlockSpec((128, heads * V_HEAD), lambda b, h: (0, h))],
        out_specs=pl.BlockSpec((None, seq, heads * V_HEAD), lambda b, h: (b, 0, h)),
        out_shape=jax.ShapeDtypeStruct((batch, seq, N_HEADS * V_HEAD), BF),
        scratch_shapes=[head_bufs],
        name="attention",
        compiler_params=_params("parallel", "parallel"),
    )(q, k, v, kp, vp)


def _attn_meta_kernel(q_ref, k_ref, v_ref, o_ref):
    s = _dot_nt(q_ref[...], k_ref[...])
    row = lax.broadcasted_iota(jnp.int32, s.shape, 0)
    col = lax.broadcasted_iota(jnp.int32, s.shape, 1)
    s = jnp.where(row >= col, s, NEG_BIG)
    p = jnp.exp2(s - jnp.max(s, axis=-1, keepdims=True))
    l = jnp.sum(p, axis=-1, keepdims=True)
    o_ref[...] = (_dot(p.astype(BF), v_ref[...]) / l).astype(o_ref.dtype)


def _attention_meta(q, k, v):
    return pl.pallas_call(
        _attn_meta_kernel,
        grid=(N_HEADS,),
        in_specs=[pl.BlockSpec((N_META, HEAD_PAD), lambda h: (0, h)),
                  pl.BlockSpec((N_META, HEAD_PAD), lambda h: (0, h)),
                  pl.BlockSpec((N_META, V_HEAD), lambda h: (0, h))],
        out_specs=pl.BlockSpec((N_META, V_HEAD), lambda h: (0, h)),
        out_shape=jax.ShapeDtypeStruct((N_META, N_HEADS * V_HEAD), BF),
        name="attention_meta",
        compiler_params=_params("parallel"),
    )(q, k, v)


def _pool_branch_kernel(h_ref, w_ref, pw_ref, ps_ref, hist_ref, yc_ref, *rest,
                        tm, tiles_per_seq, pos_offset):
    carry_ref = rest[-1]
    i = pl.program_id(0)
    pin = _dot(h_ref[...], w_ref[...])

    @pl.when(i % tiles_per_seq == 0)
    def _():
        carry_ref[...] = hist_ref[...]

    ext = jnp.concatenate([carry_ref[...], pin], axis=0)
    tail = pin[tm - HIST:]
    carry_ref[...] = tail
    if len(rest) == 2:
        rest[0][...] = tail
    n_ext = tm + HIST
    seen = (lax.broadcasted_iota(jnp.int32, (tm, 1), 0)
            + ((i % tiles_per_seq) * tm + pos_offset + 1)).astype(F32)
    for g, w in enumerate(POOL_WINDOWS):
        xg = ext[:, g * POOL_GROUP:(g + 1) * POOL_GROUP]
        s, span = xg, 1
        while span < w:
            s = s[span:] + s[:s.shape[0] - span]
            span *= 2
        s = s[n_ext - (w - 1) - tm:]
        pooled = s / jnp.minimum(seen, float(w)) - xg[HIST:]
        mixed = _dot(pooled.astype(BF), pw_ref[g])
        yc_ref[:, g * POOL_GROUP:(g + 1) * POOL_GROUP] = (
            mixed * ps_ref[:, g * POOL_GROUP:(g + 1) * POOL_GROUP]).astype(yc_ref.dtype)


def _pool_branch(h, w_pool, pool_w, l, pool_scale, hist, tm, tiles_per_seq, pos_offset,
                 emit_hist):
    rows = h.shape[0]
    assert not emit_hist or rows == tm
    kern = functools.partial(_pool_branch_kernel, tm=tm, tiles_per_seq=tiles_per_seq,
                             pos_offset=pos_offset)
    ng = len(POOL_WINDOWS)
    n_out = 2 if emit_hist else 1
    return pl.pallas_call(
        kern,
        grid=(rows // tm,),
        in_specs=[pl.BlockSpec((tm, D_MODEL), lambda i: (i, 0)),
                  _layer_spec(l, (D_MODEL, D_POOL), lambda i: (0, 0)),
                  _layer_spec(l, (ng, POOL_GROUP, POOL_GROUP), lambda i: (0, 0, 0)),
                  pl.BlockSpec((1, D_POOL), lambda i: (0, 0)),
                  pl.BlockSpec((HIST, D_POOL), lambda i: (0, 0))],
        out_specs=[pl.BlockSpec((tm, D_POOL), lambda i: (i, 0)),
                   pl.BlockSpec((HIST, D_POOL), lambda i: (0, 0))][:n_out],
        out_shape=[jax.ShapeDtypeStruct((rows, D_POOL), BF),
                   jax.ShapeDtypeStruct((HIST, D_POOL), F32)][:n_out],
        scratch_shapes=[pltpu.VMEM((HIST, D_POOL), F32)],
        name="pool_branch",
        compiler_params=_params("arbitrary"),
    )(h, w_pool, pool_w, pool_scale, hist)


def _merge_kernel(h_ref, ya_ref, yb_ref, yc_ref, wg0_ref, wg1_ref, wg2_ref, wa_ref, wb_ref, wc_ref,
                  o_ref):
    h = h_ref[...]
    m = jax.nn.sigmoid(_dot(h, wg0_ref[...])) * _dot(ya_ref[...], wa_ref[...])
    m += jax.nn.sigmoid(_dot(h, wg1_ref[...])) * _dot(yb_ref[...], wb_ref[...])
    m += jax.nn.sigmoid(_dot(h, wg2_ref[...])) * _dot(yc_ref[...], wc_ref[...])
    o_ref[...] = m.astype(o_ref.dtype)


def _merge(h, ya, yb, yc, w_gate, wa, wb, wc, l, tm, tn):
    rows = h.shape[0]
    ncb = D_MODEL // tn
    row = lambda width: pl.BlockSpec((tm, width), lambda i, j: (i, 0))
    col = lambda depth, off: _layer_spec(l, (depth, tn), lambda i, j: (0, j + off))
    return pl.pallas_call(
        _merge_kernel,
        grid=(rows // tm, ncb),
        in_specs=[row(D_MODEL), row(D_CONV), row(N_HEADS * V_HEAD), row(D_POOL),
                  col(D_MODEL, 0), col(D_MODEL, ncb), col(D_MODEL, 2 * ncb),
                  col(D_CONV, 0), col(N_HEADS * V_HEAD, 0), col(D_POOL, 0)],
        out_specs=pl.BlockSpec((tm, tn), lambda i, j: (i, j)),
        out_shape=jax.ShapeDtypeStruct((rows, D_MODEL), BF),
        name="merge",
        compiler_params=_params("parallel", "arbitrary"),
    )(h, ya, yb, yc, w_gate, w_gate, w_gate, wa, wb, wc)


def _oproj_kernel(m_ref, w_ref, x_ref, g_ref, xo_ref, h2_ref):
    x = x_ref[...] + _dot(m_ref[...], w_ref[...])
    xo_ref[...] = x
    h2_ref[...] = _rms_norm(x, g_ref[...]).astype(h2_ref.dtype)


def _oproj(merged, w_o, l, x, g, tm):
    rows = x.shape[0]
    return pl.pallas_call(
        _oproj_kernel,
        grid=(rows // tm,),
        in_specs=[pl.BlockSpec((tm, D_MODEL), lambda i: (i, 0)),
                  _layer_spec(l, (D_MODEL, D_MODEL), lambda i: (0, 0)),
                  pl.BlockSpec((tm, D_MODEL), lambda i: (i, 0)),
                  pl.BlockSpec((1, D_MODEL), lambda i: (0, 0))],
        out_specs=[pl.BlockSpec((tm, D_MODEL), lambda i: (i, 0)),
                   pl.BlockSpec((tm, D_MODEL), lambda i: (i, 0))],
        out_shape=[jax.ShapeDtypeStruct((rows, D_MODEL), F32),
                   jax.ShapeDtypeStruct((rows, D_MODEL), BF)],
        name="oproj",
        compiler_params=_params("parallel"),
    )(merged, w_o, x, g)


def _mlp_kernel(h2_ref, wu_ref, wd_ref, x_hbm, o_ref, sem, *, tm):
    first = pl.program_id(1) == 0
    rows = pl.ds(pl.multiple_of(pl.program_id(0) * tm, tm), tm)
    residual_copy = pltpu.make_async_copy(x_hbm.at[rows], o_ref, sem)

    @pl.when(first)
    def _():
        residual_copy.start()

    a = jnp.maximum(_dot(h2_ref[...], wu_ref[...]), 0.0)
    a = (a * a).astype(BF)

    @pl.when(first)
    def _():
        residual_copy.wait()

    o_ref[...] += _dot(a, wd_ref[...])


def _mlp(h2, w_up, w_down, l, x, tm, tf):
    rows = x.shape[0]
    return pl.pallas_call(
        functools.partial(_mlp_kernel, tm=tm),
        grid=(rows // tm, D_FF // tf),
        in_specs=[pl.BlockSpec((tm, D_MODEL), lambda i, k: (i, 0)),
                  _layer_spec(l, (D_MODEL, tf), lambda i, k: (0, k)),
                  _layer_spec(l, (tf, D_MODEL), lambda i, k: (k, 0)),
                  pl.BlockSpec(memory_space=pl.ANY)],
        out_specs=pl.BlockSpec((tm, D_MODEL), lambda i, k: (i, 0)),
        out_shape=jax.ShapeDtypeStruct((rows, D_MODEL), F32),
        scratch_shapes=[pltpu.SemaphoreType.DMA(())],
        name="mlp",
        compiler_params=_params("parallel", "arbitrary"),
    )(h2, w_up, w_down, x)


def _swap_halves(a):
    return jnp.concatenate([a[..., HALF_ROPE:], a[..., :HALF_ROPE]], axis=-1)


def _twice(a):
    return jnp.concatenate([a, a], axis=-1)


def _widen_head(a):
    rope = a[..., QK_NOPE:]
    return jnp.concatenate([a[..., :QK_NOPE], rope, _swap_halves(rope)], axis=-1)


def _rope_gain_tables(gain, cos_t, sin_t):
    rope_gain = gain[QK_NOPE:]
    return cos_t * rope_gain[None], sin_t * _swap_halves(rope_gain)[None]


def _stacked_weights(w_in, w_uq, w_ukv, pool_w, w_branch_a, w_branch_b, w_branch_c, w_o, w_up,
                     w_down):
    depth = w_in.shape[0]
    wi = w_in.astype(BF)
    o_q = 3 * D_CONV
    o_kr = o_q + Q_LORA + KV_LORA
    o_pool = o_kr + QK_ROPE
    o_gate = o_pool + D_POOL
    w_kr = wi[:, :, o_kr:o_pool]
    w_ukv_h = w_ukv.astype(BF).reshape(depth, KV_LORA, N_HEADS, QK_NOPE + V_HEAD)
    return dict(
        w_in=wi,
        w_lat=jnp.concatenate([wi[:, :, o_q:o_kr], _twice(w_kr), _twice(_swap_halves(w_kr))],
                              axis=2),
        w_pool=wi[:, :, o_pool:o_gate],
        w_gate=wi[:, :, o_gate:],
        w_uq=_widen_head(w_uq.astype(BF).reshape(depth, Q_LORA, N_HEADS, QK_HEAD)).reshape(
            depth, Q_LORA, N_HEADS * HEAD_PAD),
        w_uk=w_ukv_h[..., :QK_NOPE].reshape(depth, KV_LORA, N_HEADS * QK_NOPE),
        w_uv=w_ukv_h[..., QK_NOPE:].reshape(depth, KV_LORA, N_HEADS * V_HEAD),
        pool_w=pool_w.astype(BF),
        wa=w_branch_a.astype(BF),
        wb=w_branch_b.astype(BF),
        wc=w_branch_c.astype(BF),
        w_o=w_o.astype(BF),
        w_up=w_up.astype(BF),
        w_down=w_down.astype(BF),
    )


def _layer_vectors(l, cos_t, sin_t, attn_norm, conv_w, q_lat_norm, kv_lat_norm, q_norm, k_norm,
                   pool_scale, mlp_norm):
    gq = q_norm[l] * (QK_HEAD ** -0.5 * LOG2_E)
    q_cos, q_sin = _rope_gain_tables(gq, cos_t, sin_t)
    k_cos, k_sin = _rope_gain_tables(k_norm[l], cos_t, sin_t)
    return dict(
        attn_norm=attn_norm[l][None],
        conv_w=conv_w[l],
        q_lat_norm=q_lat_norm[l][None],
        kv_lat_norm=kv_lat_norm[l][None],
        gq_nope=gq[None, :QK_NOPE],
        gk_nope=k_norm[l][None, :QK_NOPE],
        q_trig=jnp.concatenate([q_cos, q_sin], axis=1),
        k_rope=(_twice(k_cos), _twice(k_sin)),
        pool_scale=pool_scale[l][None],
        mlp_norm=mlp_norm[l][None],
    )


def _rope_tables(total):
    pos = jnp.arange(total, dtype=F32)
    inv = ROPE_THETA ** (-jnp.arange(0, QK_ROPE, 2, dtype=F32) / QK_ROPE)
    ang = pos[:, None] * inv[None, :]
    cos, sin = jnp.cos(ang), jnp.sin(ang)
    return _twice(cos), jnp.concatenate([-sin, sin], axis=1)


def _layer(x, w, l, p, positions, prefix, *, batch, seq, finish):
    is_meta = prefix is None
    q_trig = p['q_trig'][positions]
    k_cos, k_sin = (t[positions] for t in p['k_rope'])
    tm = min(512, seq)
    tps = seq // tm
    tm_big = min(1024, seq)
    if is_meta:
        conv_hist = jnp.zeros((HIST, D_CONV), F32)
        pool_hist = jnp.zeros((HIST, D_POOL), F32)
    else:
        conv_hist, pool_hist, kp, vp = prefix

    h, qn, kvn, kr = _latent(x, p['attn_norm'], w['w_lat'], l, p['q_lat_norm'], p['kv_lat_norm'],
                             tm)
    ya, *conv_tail = _conv_branch(h, w['w_in'], l, p['conv_w'], conv_hist, tm_big, 512,
                                  seq // tm_big, is_meta)
    yc, *pool_tail = _pool_branch(h, w['w_pool'], w['pool_w'], l, p['pool_scale'], pool_hist,
                                  tm_big, seq // tm_big, 0 if is_meta else N_META, is_meta)
    k, v = _kv_proj(kvn, w['w_uk'], w['w_uv'], l, kr, p['gk_nope'], k_cos, k_sin, tm, N_HEADS,
                    tps)
    out_prefix = (conv_tail[0], pool_tail[0], k, v) if is_meta else None
    if not finish:
        return None, out_prefix
    q = _q_proj(qn, w['w_uq'], l, p['gq_nope'], q_trig, tm, N_HEADS, tps)
    if is_meta:
        yb = _attention_meta(q, k, v)
    else:
        pad = ((0, 128 - N_META), (0, 0))
        per_batch = lambda a: a.reshape(batch, seq, a.shape[-1])
        yb = _attention(per_batch(q), per_batch(k), per_batch(v), jnp.pad(kp, pad),
                        jnp.pad(vp, pad), batch, seq, 512, 512, 256, 1).reshape(batch * seq, -1)
    merged = _merge(h, ya, yb, yc, w['w_gate'], w['wa'], w['wb'], w['wc'], l, tm_big, 512)
    x_mid, h2 = _oproj(merged, w['w_o'], l, x, p['mlp_norm'], min(256, seq))
    return _mlp(h2, w['w_up'], w['w_down'], l, x_mid, tm_big, 1024), out_prefix


def kernel(x, meta_tokens, attn_norm, w_in, conv_w, q_lat_norm, kv_lat_norm, w_uq, w_ukv, q_norm, k_norm, pool_w, pool_scale, w_branch_a, w_branch_b, w_branch_c, w_o, mlp_norm, w_up, w_down):
    batch, seq, d = x.shape
    depth = w_in.shape[0]
    cos_t, sin_t = _rope_tables(N_META + seq)
    xm = meta_tokens.astype(F32)
    xr = x.reshape(batch * seq, d)
    w = _stacked_weights(w_in, w_uq, w_ukv, pool_w, w_branch_a, w_branch_b, w_branch_c, w_o, w_up,
                         w_down)
    for l in range(depth):
        p = _layer_vectors(l, cos_t, sin_t, attn_norm, conv_w, q_lat_norm, kv_lat_norm, q_norm,
                           k_norm, pool_scale, mlp_norm)
        last = l == depth - 1
        xm, prefix = _layer(xm, w, l, p, slice(0, N_META), None, batch=1, seq=N_META,
                            finish=not last)
        xr, _ = _layer(xr, w, l, p, slice(N_META, None), prefix, batch=batch, seq=seq, finish=True)
    return xr.reshape(batch, seq, d)
```

```python
import functools

import jax
import jax.numpy as jnp
from jax import lax
from jax.experimental import pallas as pl
from jax.experimental.pallas import tpu as pltpu

D_MODEL = 2048
N_META = 16
EPS = 1e-6
D_CONV = 1024
N_HEADS = 16
QK_NOPE = 128
QK_ROPE = 64
QK_HEAD = QK_NOPE + QK_ROPE
HALF_ROPE = QK_ROPE // 2
V_HEAD = 128
Q_LORA = 512
KV_LORA = 512
ROPE_THETA = 10000.0
D_POOL = 1024
POOL_WINDOWS = (2, 4, 8, 16)
POOL_GROUP = D_POOL // len(POOL_WINDOWS)
D_FF = 4 * D_MODEL
HIST = 16
HEAD_PAD = 256
D_LAT = Q_LORA + KV_LORA + 256
VMEM_LIMIT = 56 * 2**20
NEG_BIG = -1e30
LOG2_E = 1.4426950408889634

BF = jnp.bfloat16
F32 = jnp.float32


def _dot(a, b):
    return jnp.dot(a, b, preferred_element_type=F32)


def _dot_nt(a, b):
    return lax.dot_general(a, b, (((1,), (1,)), ((), ())), preferred_element_type=F32)


def _layer_spec(l, shape, index_map):
    return pl.BlockSpec((None,) + shape, lambda *g: (l,) + index_map(*g))


def _params(*sem):
    return pltpu.CompilerParams(dimension_semantics=sem, vmem_limit_bytes=VMEM_LIMIT)


def _rms_norm(x, g):
    ms = jnp.mean(x * x, axis=-1, keepdims=True)
    return x * lax.rsqrt(ms + EPS) * g


def _conv_branch_kernel(h_ref, wu_ref, wb_ref, wc_ref, cw_ref, hist_ref, ya_ref, *rest,
                        tm, tiles_per_seq):
    carry_ref = rest[-1]
    i, j = pl.program_id(0), pl.program_id(1)
    h = h_ref[...]
    cu = _dot(h, wc_ref[...]) * _dot(h, wu_ref[...])

    @pl.when(i % tiles_per_seq == 0)
    def _():
        carry_ref[j] = hist_ref[...]

    ext = jnp.concatenate([carry_ref[j], cu], axis=0)
    tail = cu[tm - HIST:]
    carry_ref[j] = tail
    if len(rest) == 2:
        rest[0][...] = tail
    cw = cw_ref[...]
    y = (cw[2:3] * cu + cw[1:2] * ext[HIST - 1:HIST - 1 + tm] + cw[0:1] * ext[HIST - 2:HIST - 2 + tm])
    ya_ref[...] = (_dot(h, wb_ref[...]) * y).astype(ya_ref.dtype)


def _conv_branch(h, w_in, l, conv_w, hist, tm, tc, tiles_per_seq, emit_hist):
    rows = h.shape[0]
    ncb = D_CONV // tc
    assert not emit_hist or rows == tm
    kern = functools.partial(_conv_branch_kernel, tm=tm, tiles_per_seq=tiles_per_seq)
    n_out = 2 if emit_hist else 1
    return pl.pallas_call(
        kern,
        grid=(rows // tm, ncb),
        in_specs=[pl.BlockSpec((tm, D_MODEL), lambda i, j: (i, 0)),
                  _layer_spec(l, (D_MODEL, tc), lambda i, j: (0, j)),
                  _layer_spec(l, (D_MODEL, tc), lambda i, j: (0, j + ncb)),
                  _layer_spec(l, (D_MODEL, tc), lambda i, j: (0, j + 2 * ncb)),
                  pl.BlockSpec((3, tc), lambda i, j: (0, j)),
                  pl.BlockSpec((HIST, tc), lambda i, j: (0, j))],
        out_specs=[pl.BlockSpec((tm, tc), lambda i, j: (i, j)),
                   pl.BlockSpec((HIST, tc), lambda i, j: (0, j))][:n_out],
        out_shape=[jax.ShapeDtypeStruct((rows, D_CONV), BF),
                   jax.ShapeDtypeStruct((HIST, D_CONV), F32)][:n_out],
        scratch_shapes=[pltpu.VMEM((ncb, HIST, tc), F32)],
        name="conv_branch",
        compiler_params=_params("arbitrary", "arbitrary"),
    )(h, w_in, w_in, w_in, conv_w, hist)


def _latent_kernel(x_ref, g_ref, w_ref, gq_ref, gkv_ref, h_ref, qn_ref, kvn_ref, kr_ref):
    h = _rms_norm(x_ref[...], g_ref[...]).astype(h_ref.dtype)
    h_ref[...] = h
    acc = _dot(h, w_ref[...])
    qn_ref[...] = _rms_norm(acc[:, :Q_LORA], gq_ref[...]).astype(qn_ref.dtype)
    kvn_ref[...] = _rms_norm(acc[:, Q_LORA:Q_LORA + KV_LORA], gkv_ref[...]).astype(kvn_ref.dtype)
    kr_ref[...] = acc[:, Q_LORA + KV_LORA:]


def _latent(x, g, w_lat, l, gq, gkv, tm):
    rows = x.shape[0]
    row = lambda width: pl.BlockSpec((tm, width), lambda i: (i, 0))
    vec = lambda width: pl.BlockSpec((1, width), lambda i: (0, 0))
    return pl.pallas_call(
        _latent_kernel,
        grid=(rows // tm,),
        in_specs=[row(D_MODEL), vec(D_MODEL), _layer_spec(l, (D_MODEL, D_LAT), lambda i: (0, 0)),
                  vec(Q_LORA), vec(KV_LORA)],
        out_specs=[row(D_MODEL), row(Q_LORA), row(KV_LORA), row(256)],
        out_shape=[jax.ShapeDtypeStruct((rows, D_MODEL), BF),
                   jax.ShapeDtypeStruct((rows, Q_LORA), BF),
                   jax.ShapeDtypeStruct((rows, KV_LORA), BF),
                   jax.ShapeDtypeStruct((rows, 256), F32)],
        name="latent",
        compiler_params=_params("parallel"),
    )(x, g, w_lat, gq, gkv)


def _row_sumsq(a, n_real):
    keep = lax.broadcasted_iota(jnp.int32, (a.shape[1], 128), 0) < n_real
    return _dot((a * a).astype(BF), jnp.where(keep, 1.0, 0.0).astype(BF))


def _q_kernel(qn_ref, w_ref, g_ref, trig_ref, q_ref, acc_ref, ss_ref, *, heads, rb):
    acc_ref[...] = _dot(qn_ref[...], w_ref[...])
    for hd in range(heads):
        ss_ref[:, hd * 128:(hd + 1) * 128] = _row_sumsq(
            acc_ref[:, hd * HEAD_PAD:(hd + 1) * HEAD_PAD], QK_HEAD)
    g = g_ref[...]
    for r in range(0, acc_ref.shape[0], rb):
        rs = slice(r, r + rb)
        trig = trig_ref[rs, :]
        for hd in range(heads):
            c0 = hd * HEAD_PAD
            inv = lax.rsqrt(ss_ref[rs, hd * 128:(hd + 1) * 128] * (1.0 / QK_HEAD) + EPS)
            q_ref[rs, c0:c0 + QK_NOPE] = (
                acc_ref[rs, c0:c0 + QK_NOPE] * (inv * g)).astype(q_ref.dtype)
            q_ref[rs, c0 + QK_NOPE:c0 + HEAD_PAD] = (
                acc_ref[rs, c0 + QK_NOPE:c0 + HEAD_PAD] * (inv * trig)).astype(q_ref.dtype)


def _q_proj(qn, w_uq, l, g_nope, trig, tm, heads, tiles_per_seq):
    rows = qn.shape[0]
    kern = functools.partial(_q_kernel, heads=heads, rb=min(64, tm))
    return pl.pallas_call(
        kern,
        grid=(rows // tm, N_HEADS // heads),
        in_specs=[pl.BlockSpec((tm, Q_LORA), lambda i, j: (i, 0)),
                  _layer_spec(l, (Q_LORA, heads * HEAD_PAD), lambda i, j: (0, j)),
                  pl.BlockSpec((1, QK_NOPE), lambda i, j: (0, 0)),
                  pl.BlockSpec((tm, 128), lambda i, j: (i % tiles_per_seq, 0))],
        out_specs=pl.BlockSpec((tm, heads * HEAD_PAD), lambda i, j: (i, j)),
        out_shape=jax.ShapeDtypeStruct((rows, N_HEADS * HEAD_PAD), BF),
        scratch_shapes=[pltpu.VMEM((tm, heads * HEAD_PAD), F32),
                        pltpu.VMEM((tm, heads * 128), F32)],
        name="q_proj",
        compiler_params=_params("parallel", "parallel"),
    )(qn, w_uq, g_nope, trig)


def _kv_kernel(kvn_ref, wk_ref, wv_ref, kr_ref, gn_ref, cos_ref, sin_ref, k_ref, v_ref,
               acc_ref, ss_ref, *, heads, rb):
    kvn = kvn_ref[...]
    acc_ref[...] = _dot(kvn, wk_ref[...])
    v_ref[...] = _dot(kvn, wv_ref[...]).astype(v_ref.dtype)
    blk = (lax.broadcasted_iota(jnp.int32, (256, 256), 0) // 128
           == lax.broadcasted_iota(jnp.int32, (256, 256), 1) // 128)
    pair_ones = jnp.where(blk, 1.0, 0.0).astype(BF)
    for hd in range(0, heads, 2):
        a = acc_ref[:, hd * QK_NOPE:(hd + 2) * QK_NOPE]
        ss_ref[:, hd * 128:(hd + 2) * 128] = _dot((a * a).astype(BF), pair_ones)
    ss_ref[:, heads * 128:] = _row_sumsq(kr_ref[:, :128], QK_ROPE)
    gn = gn_ref[...]
    for r in range(0, acc_ref.shape[0], rb):
        rs = slice(r, r + rb)
        kr_rot = kr_ref[rs, :128] * cos_ref[rs, :] + kr_ref[rs, 128:] * sin_ref[rs, :]
        kr_ss = ss_ref[rs, heads * 128:]
        for hd in range(heads):
            ss = ss_ref[rs, hd * 128:(hd + 1) * 128] + kr_ss
            inv = lax.rsqrt(ss * (1.0 / QK_HEAD) + EPS)
            k_ref[rs, hd * HEAD_PAD:hd * HEAD_PAD + QK_NOPE] = (
                acc_ref[rs, hd * QK_NOPE:(hd + 1) * QK_NOPE] * (inv * gn)).astype(k_ref.dtype)
            k_ref[rs, hd * HEAD_PAD + QK_NOPE:(hd + 1) * HEAD_PAD] = (kr_rot * inv).astype(
                k_ref.dtype)


def _kv_proj(kvn, w_uk, w_uv, l, kr, gn, cos_t, sin_t, tm, heads, tiles_per_seq):
    rows = kvn.shape[0]
    kern = functools.partial(_kv_kernel, heads=heads, rb=min(64, tm))
    return pl.pallas_call(
        kern,
        grid=(rows // tm, N_HEADS // heads),
        in_specs=[pl.BlockSpec((tm, KV_LORA), lambda i, j: (i, 0)),
                  _layer_spec(l, (KV_LORA, heads * QK_NOPE), lambda i, j: (0, j)),
                  _layer_spec(l, (KV_LORA, heads * V_HEAD), lambda i, j: (0, j)),
                  pl.BlockSpec((tm, 256), lambda i, j: (i, 0)),
                  pl.BlockSpec((1, 128), lambda i, j: (0, 0)),
                  pl.BlockSpec((tm, 128), lambda i, j: (i % tiles_per_seq, 0)),
                  pl.BlockSpec((tm, 128), lambda i, j: (i % tiles_per_seq, 0))],
        out_specs=[pl.BlockSpec((tm, heads * HEAD_PAD), lambda i, j: (i, j)),
                   pl.BlockSpec((tm, heads * V_HEAD), lambda i, j: (i, j))],
        out_shape=[jax.ShapeDtypeStruct((rows, N_HEADS * HEAD_PAD), BF),
                   jax.ShapeDtypeStruct((rows, N_HEADS * V_HEAD), BF)],
        scratch_shapes=[pltpu.VMEM((tm, heads * QK_NOPE), F32),
                        pltpu.VMEM((tm, (heads + 1) * 128), F32)],
        name="kv_proj",
        compiler_params=_params("parallel", "parallel"),
    )(kvn, w_uk, w_uv, kr, gn, cos_t, sin_t)


def _attn_kernel(q_ref, k_ref, v_ref, kp_ref, vp_ref, o_ref, head_bufs, *, tq, tc, td, rb,
                 n_prefix):
    for hd, bufs in enumerate(head_bufs):
        for src, dst in ((v_ref, bufs[-2]), (vp_ref, bufs[-1])):
            dst[:, :V_HEAD] = src[:, hd * V_HEAD:(hd + 1) * V_HEAD]
            dst[:, V_HEAD:] = jnp.ones((dst.shape[0], dst.shape[1] - V_HEAD), dst.dtype)
    for q_tile in range(q_ref.shape[0] // tq):
        _attn_tile(q_ref, k_ref, v_ref, kp_ref, vp_ref, o_ref, head_bufs, q_tile=q_tile, tq=tq,
                   tc=tc, td=td, rb=rb, n_prefix=n_prefix)


def _attn_tile(q_ref, k_ref, v_ref, kp_ref, vp_ref, o_ref, head_bufs, *, q_tile, tq, tc, td, rb,
               n_prefix):
    d0 = q_tile * tq
    items = [None] + [c * tc for c in range(d0 // tc)]
    bands = range(tq // td)
    qcols = lambda hd: slice(hd * HEAD_PAD, (hd + 1) * HEAD_PAD)
    vcols = lambda hd: slice(hd * V_HEAD, (hd + 1) * V_HEAD)
    tile = slice(d0, d0 + tq)

    def scores(hd, k0, s_ref):
        if k0 is not None:
            s_ref[:, :tc] = _dot_nt(q_ref[tile, qcols(hd)], k_ref[k0:k0 + tc, qcols(hd)])
            return
        s_ref[:, tq:tq + 128] = _dot_nt(q_ref[tile, qcols(hd)], kp_ref[:, qcols(hd)])
        for g in bands:
            seen = (g + 1) * td
            s_ref[g * td:seen, :seen] = _dot_nt(q_ref[d0 + g * td:d0 + seen, qcols(hd)],
                                                k_ref[d0:d0 + seen, qcols(hd)])

    def softmax_rows(rs, pieces, p_ref, a_ref, m_ref):
        s = jnp.concatenate([x for _, x in pieces], axis=1)
        m_old = m_ref[rs, :]
        m_new = jnp.maximum(m_old, jnp.max(s, axis=-1, keepdims=True))
        alpha = jnp.exp2(m_old - m_new)
        p = jnp.exp2(s - jnp.concatenate([m_new] * (s.shape[1] // 128), axis=1))
        m_ref[rs, :] = m_new
        a_ref[rs, :] = alpha
        at = 0
        for cols, x in pieces:
            p_ref[rs, cols] = p[:, at:at + x.shape[1]].astype(p_ref.dtype)
            at += x.shape[1]

    def softmax(k0, s_ref, p_ref, a_ref, m_ref):
        for r in range(0, tq, rb):
            rs = slice(r, r + rb)
            if k0 is not None:
                pieces = [(slice(0, tc), s_ref[rs, :tc])]
            else:
                g = r // td
                open_cols, last = slice(0, g * td), slice(g * td, (g + 1) * td)
                row = lax.broadcasted_iota(jnp.int32, (rb, td), 0) + (r - g * td)
                col = lax.broadcasted_iota(jnp.int32, (rb, td), 1)
                colp = lax.broadcasted_iota(jnp.int32, (rb, 128), 1)
                pieces = [(open_cols, s_ref[rs, open_cols])] if g else []
                pieces += [(last, jnp.where(col <= row, s_ref[rs, last], NEG_BIG)),
                           (slice(tq, tq + 128),
                            jnp.where(colp < n_prefix, s_ref[rs, tq:tq + 128], NEG_BIG))]
            softmax_rows(rs, pieces, p_ref, a_ref, m_ref)

    def accumulate(k0, p_ref, a_ref, acc_ref, vx_ref, vpx_ref):
        def rescaled(rows):
            a = a_ref[rows, :]
            return jnp.concatenate([a, a], axis=1) * acc_ref[rows, :]

        if k0 is not None:
            acc_ref[...] = rescaled(slice(None)) + _dot(p_ref[:, :tc], vx_ref[k0:k0 + tc, :])
            return
        for g in bands:
            rows, seen = slice(g * td, (g + 1) * td), (g + 1) * td
            acc_ref[rows, :] = (rescaled(rows) + _dot(p_ref[rows, :seen], vx_ref[d0:d0 + seen, :])
                                + _dot(p_ref[rows, tq:tq + 128], vpx_ref[...]))

    for _, _, _, m_ref, acc_ref, _, _ in head_bufs:
        m_ref[...] = jnp.full(m_ref.shape, NEG_BIG, F32)
        acc_ref[...] = jnp.zeros(acc_ref.shape, F32)
    n = len(items)
    for t in range(n + 2):
        for hd, (s_refs, p_refs, a_refs, m_ref, acc_ref, vx_ref, vpx_ref) in enumerate(head_bufs):
            depth = len(s_refs)
            if t < n:
                scores(hd, items[t], s_refs[t % depth])
            if 1 <= t <= n:
                u = (t - 1) % depth
                softmax(items[t - 1], s_refs[u], p_refs[u], a_refs[u], m_ref)
            if t >= 2:
                u = (t - 2) % depth
                accumulate(items[t - 2], p_refs[u], a_refs[u], acc_ref, vx_ref, vpx_ref)
    for hd, bufs in enumerate(head_bufs):
        acc_ref = bufs[4]
        o_ref[tile, vcols(hd)] = (acc_ref[:, :V_HEAD] / acc_ref[:, V_HEAD:]).astype(o_ref.dtype)


def _attention(q, k, v, kp, vp, batch, seq, tq, tc, td, heads):
    depth = 3
    stat = pltpu.VMEM((tq, 128), F32)
    width = max(tc, tq + 128)
    head_bufs = [([pltpu.VMEM((tq, width), F32)] * depth, [pltpu.VMEM((tq, width), BF)] * depth,
                  [stat] * depth, stat, pltpu.VMEM((tq, 2 * V_HEAD), F32),
                  pltpu.VMEM((seq, 2 * V_HEAD), BF), pltpu.VMEM((128, 2 * V_HEAD), BF))] * heads
    kern = functools.partial(_attn_kernel, tq=tq, tc=tc, td=td, rb=64, n_prefix=N_META)
    return pl.pallas_call(
        kern,
        grid=(batch, N_HEADS // heads),
        in_specs=[pl.BlockSpec((None, seq, heads * HEAD_PAD), lambda b, h: (b, 0, h)),
                  pl.BlockSpec((None, seq, heads * HEAD_PAD), lambda b, h: (b, 0, h)),
                  pl.BlockSpec((None, seq, heads * V_HEAD), lambda b, h: (b, 0, h)),
                  pl.BlockSpec((128, heads * HEAD_PAD), lambda b, h: (0, h)),
                  pl.BlockSpec((128, heads * V_HEAD), lambda b, h: (0, h))],
        out_specs=pl.BlockSpec((None, seq, heads * V_HEAD), lambda b, h: (b, 0, h)),
        out_shape=jax.ShapeDtypeStruct((batch, seq, N_HEADS * V_HEAD), BF),
        scratch_shapes=[head_bufs],
        name="attention",
        compiler_params=_params("parallel", "parallel"),
    )(q, k, v, kp, vp)


def _attn_meta_kernel(q_ref, k_ref, v_ref, o_ref):
    s = _dot_nt(q_ref[...], k_ref[...])
    row = lax.broadcasted_iota(jnp.int32, s.shape, 0)
    col = lax.broadcasted_iota(jnp.int32, s.shape, 1)
    s = jnp.where(row >= col, s, NEG_BIG)
    p = jnp.exp2(s - jnp.max(s, axis=-1, keepdims=True))
    l = jnp.sum(p, axis=-1, keepdims=True)
    o_ref[...] = (_dot(p.astype(BF), v_ref[...]) / l).astype(o_ref.dtype)


def _attention_meta(q, k, v):
    return pl.pallas_call(
        _attn_meta_kernel,
        grid=(N_HEADS,),
        in_specs=[pl.BlockSpec((N_META, HEAD_PAD), lambda h: (0, h)),
                  pl.BlockSpec((N_META, HEAD_PAD), lambda h: (0, h)),
                  pl.BlockSpec((N_META, V_HEAD), lambda h: (0, h))],
        out_specs=pl.BlockSpec((N_META, V_HEAD), lambda h: (0, h)),
        out_shape=jax.ShapeDtypeStruct((N_META, N_HEADS * V_HEAD), BF),
        name="attention_meta",
        compiler_params=_params("parallel"),
    )(q, k, v)


def _pool_branch_kernel(h_ref, w_ref, pw_ref, ps_ref, hist_ref, yc_ref, *rest,
                        tm, tiles_per_seq, pos_offset):
    carry_ref = rest[-1]
    i = pl.program_id(0)
    pin = _dot(h_ref[...], w_ref[...])

    @pl.when(i % tiles_per_seq == 0)
    def _():
        carry_ref[...] = hist_ref[...]

    ext = jnp.concatenate([carry_ref[...], pin], axis=0)
    tail = pin[tm - HIST:]
    carry_ref[...] = tail
    if len(rest) == 2:
        rest[0][...] = tail
    n_ext = tm + HIST
    seen = (lax.broadcasted_iota(jnp.int32, (tm, 1), 0)
            + ((i % tiles_per_seq) * tm + pos_offset + 1)).astype(F32)
    for g, w in enumerate(POOL_WINDOWS):
        xg = ext[:, g * POOL_GROUP:(g + 1) * POOL_GROUP]
        s, span = xg, 1
        while span < w:
            s = s[span:] + s[:s.shape[0] - span]
            span *= 2
        s = s[n_ext - (w - 1) - tm:]
        pooled = s / jnp.minimum(seen, float(w)) - xg[HIST:]
        mixed = _dot(pooled.astype(BF), pw_ref[g])
        yc_ref[:, g * POOL_GROUP:(g + 1) * POOL_GROUP] = (
            mixed * ps_ref[:, g * POOL_GROUP:(g + 1) * POOL_GROUP]).astype(yc_ref.dtype)


def _pool_branch(h, w_pool, pool_w, l, pool_scale, hist, tm, tiles_per_seq, pos_offset,
                 emit_hist):
    rows = h.shape[0]
    assert not emit_hist or rows == tm
    kern = functools.partial(_pool_branch_kernel, tm=tm, tiles_per_seq=tiles_per_seq,
                             pos_offset=pos_offset)
    ng = len(POOL_WINDOWS)
    n_out = 2 if emit_hist else 1
    return pl.pallas_call(
        kern,
        grid=(rows // tm,),
        in_specs=[pl.BlockSpec((tm, D_MODEL), lambda i: (i, 0)),
                  _layer_spec(l, (D_MODEL, D_POOL), lambda i: (0, 0)),
                  _layer_spec(l, (ng, POOL_GROUP, POOL_GROUP), lambda i: (0, 0, 0)),
                  pl.BlockSpec((1, D_POOL), lambda i: (0, 0)),
                  pl.BlockSpec((HIST, D_POOL), lambda i: (0, 0))],
        out_specs=[pl.BlockSpec((tm, D_POOL), lambda i: (i, 0)),
                   pl.BlockSpec((HIST, D_POOL), lambda i: (0, 0))][:n_out],
        out_shape=[jax.ShapeDtypeStruct((rows, D_POOL), BF),
                   jax.ShapeDtypeStruct((HIST, D_POOL), F32)][:n_out],
        scratch_shapes=[pltpu.VMEM((HIST, D_POOL), F32)],
        name="pool_branch",
        compiler_params=_params("arbitrary"),
    )(h, w_pool, pool_w, pool_scale, hist)


def _merge_kernel(h_ref, ya_ref, yb_ref, yc_ref, wg0_ref, wg1_ref, wg2_ref, wa_ref, wb_ref, wc_ref,
                  o_ref):
    h = h_ref[...]
    m = jax.nn.sigmoid(_dot(h, wg0_ref[...])) * _dot(ya_ref[...], wa_ref[...])
    m += jax.nn.sigmoid(_dot(h, wg1_ref[...])) * _dot(yb_ref[...], wb_ref[...])
    m += jax.nn.sigmoid(_dot(h, wg2_ref[...])) * _dot(yc_ref[...], wc_ref[...])
    o_ref[...] = m.astype(o_ref.dtype)


def _merge(h, ya, yb, yc, w_gate, wa, wb, wc, l, tm, tn):
    rows = h.shape[0]
    ncb = D_MODEL // tn
    row = lambda width: pl.BlockSpec((tm, width), lambda i, j: (i, 0))
    col = lambda depth, off: _layer_spec(l, (depth, tn), lambda i, j: (0, j + off))
    return pl.pallas_call(
        _merge_kernel,
        grid=(rows // tm, ncb),
        in_specs=[row(D_MODEL), row(D_CONV), row(N_HEADS * V_HEAD), row(D_POOL),
                  col(D_MODEL, 0), col(D_MODEL, ncb), col(D_MODEL, 2 * ncb),
                  col(D_CONV, 0), col(N_HEADS * V_HEAD, 0), col(D_POOL, 0)],
        out_specs=pl.BlockSpec((tm, tn), lambda i, j: (i, j)),
        out_shape=jax.ShapeDtypeStruct((rows, D_MODEL), BF),
        name="merge",
        compiler_params=_params("parallel", "arbitrary"),
    )(h, ya, yb, yc, w_gate, w_gate, w_gate, wa, wb, wc)


def _oproj_kernel(m_ref, w_ref, x_ref, g_ref, xo_ref, h2_ref):
    x = x_ref[...] + _dot(m_ref[...], w_ref[...])
    xo_ref[...] = x
    h2_ref[...] = _rms_norm(x, g_ref[...]).astype(h2_ref.dtype)


def _oproj(merged, w_o, l, x, g, tm):
    rows = x.shape[0]
    return pl.pallas_call(
        _oproj_kernel,
        grid=(rows // tm,),
        in_specs=[pl.BlockSpec((tm, D_MODEL), lambda i: (i, 0)),
                  _layer_spec(l, (D_MODEL, D_MODEL), lambda i: (0, 0)),
                  pl.BlockSpec((tm, D_MODEL), lambda i: (i, 0)),
                  pl.BlockSpec((1, D_MODEL), lambda i: (0, 0))],
        out_specs=[pl.BlockSpec((tm, D_MODEL), lambda i: (i, 0)),
                   pl.BlockSpec((tm, D_MODEL), lambda i: (i, 0))],
        out_shape=[jax.ShapeDtypeStruct((rows, D_MODEL), F32),
                   jax.ShapeDtypeStruct((rows, D_MODEL), BF)],
        name="oproj",
        compiler_params=_params("parallel"),
    )(merged, w_o, x, g)


def _mlp_kernel(h2_ref, wu_ref, wd_ref, x_ref, o_ref):
    @pl.when(pl.program_id(1) == 0)
    def _():
        o_ref[...] = x_ref[...]

    a = jnp.maximum(_dot(h2_ref[...], wu_ref[...]), 0.0)
    o_ref[...] += _dot((a * a).astype(BF), wd_ref[...])


def _mlp(h2, w_up, w_down, l, x, tm, tf):
    rows = x.shape[0]
    return pl.pallas_call(
        _mlp_kernel,
        grid=(rows // tm, D_FF // tf),
        in_specs=[pl.BlockSpec((tm, D_MODEL), lambda i, k: (i, 0)),
                  _layer_spec(l, (D_MODEL, tf), lambda i, k: (0, k)),
                  _layer_spec(l, (tf, D_MODEL), lambda i, k: (k, 0)),
                  pl.BlockSpec((tm, D_MODEL), lambda i, k: (i, 0))],
        out_specs=pl.BlockSpec((tm, D_MODEL), lambda i, k: (i, 0)),
        out_shape=jax.ShapeDtypeStruct((rows, D_MODEL), F32),
        name="mlp",
        compiler_params=_params("parallel", "arbitrary"),
    )(h2, w_up, w_down, x)


def _swap_halves(a):
    return jnp.concatenate([a[..., HALF_ROPE:], a[..., :HALF_ROPE]], axis=-1)


def _twice(a):
    return jnp.concatenate([a, a], axis=-1)


def _widen_head(a):
    rope = a[..., QK_NOPE:]
    return jnp.concatenate([a[..., :QK_NOPE], rope, _swap_halves(rope)], axis=-1)


def _rope_gain_tables(gain, cos_t, sin_t):
    rope_gain = gain[QK_NOPE:]
    return cos_t * rope_gain[None], sin_t * _swap_halves(rope_gain)[None]


def _stacked_weights(w_in, w_uq, w_ukv, pool_w, w_branch_a, w_branch_b, w_branch_c, w_o, w_up,
                     w_down):
    depth = w_in.shape[0]
    wi = w_in.astype(BF)
    o_q = 3 * D_CONV
    o_kr = o_q + Q_LORA + KV_LORA
    o_pool = o_kr + QK_ROPE
    o_gate = o_pool + D_POOL
    w_kr = wi[:, :, o_kr:o_pool]
    w_ukv_h = w_ukv.astype(BF).reshape(depth, KV_LORA, N_HEADS, QK_NOPE + V_HEAD)
    return dict(
        w_in=wi,
        w_lat=jnp.concatenate([wi[:, :, o_q:o_kr], _twice(w_kr), _twice(_swap_halves(w_kr))],
                              axis=2),
        w_pool=wi[:, :, o_pool:o_gate],
        w_gate=wi[:, :, o_gate:],
        w_uq=_widen_head(w_uq.astype(BF).reshape(depth, Q_LORA, N_HEADS, QK_HEAD)).reshape(
            depth, Q_LORA, N_HEADS * HEAD_PAD),
        w_uk=w_ukv_h[..., :QK_NOPE].reshape(depth, KV_LORA, N_HEADS * QK_NOPE),
        w_uv=w_ukv_h[..., QK_NOPE:].reshape(depth, KV_LORA, N_HEADS * V_HEAD),
        pool_w=pool_w.astype(BF),
        wa=w_branch_a.astype(BF),
        wb=w_branch_b.astype(BF),
        wc=w_branch_c.astype(BF),
        w_o=w_o.astype(BF),
        w_up=w_up.astype(BF),
        w_down=w_down.astype(BF),
    )


def _layer_vectors(l, cos_t, sin_t, attn_norm, conv_w, q_lat_norm, kv_lat_norm, q_norm, k_norm,
                   pool_scale, mlp_norm):
    gq = q_norm[l] * (QK_HEAD ** -0.5 * LOG2_E)
    q_cos, q_sin = _rope_gain_tables(gq, cos_t, sin_t)
    k_cos, k_sin = _rope_gain_tables(k_norm[l], cos_t, sin_t)
    return dict(
        attn_norm=attn_norm[l][None],
        conv_w=conv_w[l],
        q_lat_norm=q_lat_norm[l][None],
        kv_lat_norm=kv_lat_norm[l][None],
        gq_nope=gq[None, :QK_NOPE],
        gk_nope=k_norm[l][None, :QK_NOPE],
        q_trig=jnp.concatenate([q_cos, q_sin], axis=1),
        k_rope=(_twice(k_cos), _twice(k_sin)),
        pool_scale=pool_scale[l][None],
        mlp_norm=mlp_norm[l][None],
    )


def _rope_tables(total):
    pos = jnp.arange(total, dtype=F32)
    inv = ROPE_THETA ** (-jnp.arange(0, QK_ROPE, 2, dtype=F32) / QK_ROPE)
    ang = pos[:, None] * inv[None, :]
    cos, sin = jnp.cos(ang), jnp.sin(ang)
    return _twice(cos), jnp.concatenate([-sin, sin], axis=1)


def _layer(x, w, l, p, positions, prefix, *, batch, seq, finish):
    is_meta = prefix is None
    q_trig = p['q_trig'][positions]
    k_cos, k_sin = (t[positions] for t in p['k_rope'])
    tm = min(512, seq)
    tps = seq // tm
    tm_big = min(1024, seq)
    if is_meta:
        conv_hist = jnp.zeros((HIST, D_CONV), F32)
        pool_hist = jnp.zeros((HIST, D_POOL), F32)
    else:
        conv_hist, pool_hist, kp, vp = prefix

    h, qn, kvn, kr = _latent(x, p['attn_norm'], w['w_lat'], l, p['q_lat_norm'], p['kv_lat_norm'],
                             tm)
    ya, *conv_tail = _conv_branch(h, w['w_in'], l, p['conv_w'], conv_hist, tm_big, 512,
                                  seq // tm_big, is_meta)
    yc, *pool_tail = _pool_branch(h, w['w_pool'], w['pool_w'], l, p['pool_scale'], pool_hist,
                                  tm_big, seq // tm_big, 0 if is_meta else N_META, is_meta)
    k, v = _kv_proj(kvn, w['w_uk'], w['w_uv'], l, kr, p['gk_nope'], k_cos, k_sin, tm, N_HEADS,
                    tps)
    out_prefix = (conv_tail[0], pool_tail[0], k, v) if is_meta else None
    if not finish:
        return None, out_prefix
    q = _q_proj(qn, w['w_uq'], l, p['gq_nope'], q_trig, tm, N_HEADS, tps)
    if is_meta:
        yb = _attention_meta(q, k, v)
    else:
        pad = ((0, 128 - N_META), (0, 0))
        per_batch = lambda a: a.reshape(batch, seq, a.shape[-1])
        yb = _attention(per_batch(q), per_batch(k), per_batch(v), jnp.pad(kp, pad),
                        jnp.pad(vp, pad), batch, seq, 512, 512, 256, 1).reshape(batch * seq, -1)
    merged = _merge(h, ya, yb, yc, w['w_gate'], w['wa'], w['wb'], w['wc'], l, tm_big, 512)
    x_mid, h2 = _oproj(merged, w['w_o'], l, x, p['mlp_norm'], min(256, seq))
    return _mlp(h2, w['w_up'], w['w_down'], l, x_mid, tm_big, 512), out_prefix


def kernel(x, meta_tokens, attn_norm, w_in, conv_w, q_lat_norm, kv_lat_norm, w_uq, w_ukv, q_norm, k_norm, pool_w, pool_scale, w_branch_a, w_branch_b, w_branch_c, w_o, mlp_norm, w_up, w_down):
    batch, seq, d = x.shape
    depth = w_in.shape[0]
    cos_t, sin_t = _rope_tables(N_META + seq)
    xm = meta_tokens.astype(F32)
    xr = x.reshape(batch * seq, d)
    w = _stacked_weights(w_in, w_uq, w_ukv, pool_w, w_branch_a, w_branch_b, w_branch_c, w_o, w_up,
                         w_down)
    for l in range(depth):
        p = _layer_vectors(l, cos_t, sin_t, attn_norm, conv_w, q_lat_norm, kv_lat_norm, q_norm,
                           k_norm, pool_scale, mlp_norm)
        last = l == depth - 1
        xm, prefix = _layer(xm, w, l, p, slice(0, N_META), None, batch=1, seq=N_META,
                            finish=not last)
        xr, _ = _layer(xr, w, l, p, slice(N_META, None), prefix, batch=batch, seq=seq, finish=True)
    return xr.reshape(batch, seq, d)
```

```python
import functools

import jax
import jax.numpy as jnp
from jax import lax
from jax.experimental import pallas as pl
from jax.experimental.pallas import tpu as pltpu

D_MODEL = 2048
N_META = 16
EPS = 1e-6
D_CONV = 1024
N_HEADS = 16
QK_NOPE = 128
QK_ROPE = 64
QK_HEAD = QK_NOPE + QK_ROPE
HALF_ROPE = QK_ROPE // 2
V_HEAD = 128
Q_LORA = 512
KV_LORA = 512
ROPE_THETA = 10000.0
D_POOL = 1024
POOL_WINDOWS = (2, 4, 8, 16)
POOL_GROUP = D_POOL // len(POOL_WINDOWS)
D_FF = 4 * D_MODEL
HIST = 16
HEAD_PAD = 256
D_LAT = Q_LORA + KV_LORA + 256
VMEM_LIMIT = 56 * 2**20
ROW_BLOCK_ELEMS = 32 * 1024
NEG_BIG = -1e30
LOG2_E = 1.4426950408889634

BF = jnp.bfloat16
F32 = jnp.float32


def _dot(a, b):
    return jnp.dot(a, b, preferred_element_type=F32)


def _dot_nt(a, b):
    return lax.dot_general(a, b, (((1,), (1,)), ((), ())), preferred_element_type=F32)


def _layer_spec(l, shape, index_map):
    return pl.BlockSpec((None,) + shape, lambda *g: (l,) + index_map(*g))


def _params(*sem):
    return pltpu.CompilerParams(dimension_semantics=sem, vmem_limit_bytes=VMEM_LIMIT)


def _rms_norm(x, g):
    ms = jnp.mean(x * x, axis=-1, keepdims=True)
    return x * lax.rsqrt(ms + EPS) * g


def _conv_branch_kernel(h_ref, wu_ref, wb_ref, wc_ref, cw_ref, hist_ref, ya_ref, *rest,
                        tm, tiles_per_seq):
    carry_ref = rest[-1]
    i, j = pl.program_id(0), pl.program_id(1)
    h = h_ref[...]
    cu = _dot(h, wc_ref[...]) * _dot(h, wu_ref[...])

    @pl.when(i % tiles_per_seq == 0)
    def _():
        carry_ref[j] = hist_ref[...]

    ext = jnp.concatenate([carry_ref[j], cu], axis=0)
    tail = cu[tm - HIST:]
    carry_ref[j] = tail
    if len(rest) == 2:
        rest[0][...] = tail
    cw = cw_ref[...]
    y = (cw[2:3] * cu + cw[1:2] * ext[HIST - 1:HIST - 1 + tm] + cw[0:1] * ext[HIST - 2:HIST - 2 + tm])
    ya_ref[...] = (_dot(h, wb_ref[...]) * y).astype(ya_ref.dtype)


def _conv_branch(h, w_in, l, conv_w, hist, tm, tc, tiles_per_seq, emit_hist):
    rows = h.shape[0]
    ncb = D_CONV // tc
    assert not emit_hist or rows == tm
    kern = functools.partial(_conv_branch_kernel, tm=tm, tiles_per_seq=tiles_per_seq)
    n_out = 2 if emit_hist else 1
    return pl.pallas_call(
        kern,
        grid=(rows // tm, ncb),
        in_specs=[pl.BlockSpec((tm, D_MODEL), lambda i, j: (i, 0)),
                  _layer_spec(l, (D_MODEL, tc), lambda i, j: (0, j)),
                  _layer_spec(l, (D_MODEL, tc), lambda i, j: (0, j + ncb)),
                  _layer_spec(l, (D_MODEL, tc), lambda i, j: (0, j + 2 * ncb)),
                  pl.BlockSpec((3, tc), lambda i, j: (0, j)),
                  pl.BlockSpec((HIST, tc), lambda i, j: (0, j))],
        out_specs=[pl.BlockSpec((tm, tc), lambda i, j: (i, j)),
                   pl.BlockSpec((HIST, tc), lambda i, j: (0, j))][:n_out],
        out_shape=[jax.ShapeDtypeStruct((rows, D_CONV), BF),
                   jax.ShapeDtypeStruct((HIST, D_CONV), F32)][:n_out],
        scratch_shapes=[pltpu.VMEM((ncb, HIST, tc), F32)],
        name="conv_branch",
        compiler_params=_params("arbitrary", "arbitrary"),
    )(h, w_in, w_in, w_in, conv_w, hist)


def _latent_kernel(x_ref, g_ref, w_ref, gq_ref, gkv_ref, h_ref, qn_ref, kvn_ref, kr_ref):
    h = _rms_norm(x_ref[...], g_ref[...]).astype(h_ref.dtype)
    h_ref[...] = h
    acc = _dot(h, w_ref[...])
    qn_ref[...] = _rms_norm(acc[:, :Q_LORA], gq_ref[...]).astype(qn_ref.dtype)
    kvn_ref[...] = _rms_norm(acc[:, Q_LORA:Q_LORA + KV_LORA], gkv_ref[...]).astype(kvn_ref.dtype)
    kr_ref[...] = acc[:, Q_LORA + KV_LORA:]


def _latent(x, g, w_lat, l, gq, gkv, tm):
    rows = x.shape[0]
    row = lambda width: pl.BlockSpec((tm, width), lambda i: (i, 0))
    vec = lambda width: pl.BlockSpec((1, width), lambda i: (0, 0))
    return pl.pallas_call(
        _latent_kernel,
        grid=(rows // tm,),
        in_specs=[row(D_MODEL), vec(D_MODEL), _layer_spec(l, (D_MODEL, D_LAT), lambda i: (0, 0)),
                  vec(Q_LORA), vec(KV_LORA)],
        out_specs=[row(D_MODEL), row(Q_LORA), row(KV_LORA), row(256)],
        out_shape=[jax.ShapeDtypeStruct((rows, D_MODEL), BF),
                   jax.ShapeDtypeStruct((rows, Q_LORA), BF),
                   jax.ShapeDtypeStruct((rows, KV_LORA), BF),
                   jax.ShapeDtypeStruct((rows, 256), F32)],
        name="latent",
        compiler_params=_params("parallel"),
    )(x, g, w_lat, gq, gkv)


def _row_sumsq(a, n_real):
    keep = lax.broadcasted_iota(jnp.int32, (a.shape[1], 128), 0) < n_real
    return _dot((a * a).astype(BF), jnp.where(keep, 1.0, 0.0).astype(BF))


def _q_kernel(qn_ref, w_ref, g_ref, trig_ref, q_ref, acc_ref, ss_ref, *, heads, rb):
    acc_ref[...] = _dot(qn_ref[...], w_ref[...])
    for hd in range(heads):
        ss_ref[:, hd * 128:(hd + 1) * 128] = _row_sumsq(
            acc_ref[:, hd * HEAD_PAD:(hd + 1) * HEAD_PAD], QK_HEAD)
    g = g_ref[...]
    for r in range(0, acc_ref.shape[0], rb):
        rs = slice(r, r + rb)
        trig = trig_ref[rs, :]
        for hd in range(heads):
            c0 = hd * HEAD_PAD
            inv = lax.rsqrt(ss_ref[rs, hd * 128:(hd + 1) * 128] * (1.0 / QK_HEAD) + EPS)
            q_ref[rs, c0:c0 + QK_NOPE] = (
                acc_ref[rs, c0:c0 + QK_NOPE] * (inv * g)).astype(q_ref.dtype)
            q_ref[rs, c0 + QK_NOPE:c0 + HEAD_PAD] = (
                acc_ref[rs, c0 + QK_NOPE:c0 + HEAD_PAD] * (inv * trig)).astype(q_ref.dtype)


def _q_proj(qn, w_uq, l, g_nope, trig, tm, heads, tiles_per_seq):
    rows = qn.shape[0]
    kern = functools.partial(_q_kernel, heads=heads, rb=min(64, tm))
    return pl.pallas_call(
        kern,
        grid=(rows // tm, N_HEADS // heads),
        in_specs=[pl.BlockSpec((tm, Q_LORA), lambda i, j: (i, 0)),
                  _layer_spec(l, (Q_LORA, heads * HEAD_PAD), lambda i, j: (0, j)),
                  pl.BlockSpec((1, QK_NOPE), lambda i, j: (0, 0)),
                  pl.BlockSpec((tm, 128), lambda i, j: (i % tiles_per_seq, 0))],
        out_specs=pl.BlockSpec((tm, heads * HEAD_PAD), lambda i, j: (i, j)),
        out_shape=jax.ShapeDtypeStruct((rows, N_HEADS * HEAD_PAD), BF),
        scratch_shapes=[pltpu.VMEM((tm, heads * HEAD_PAD), F32),
                        pltpu.VMEM((tm, heads * 128), F32)],
        name="q_proj",
        compiler_params=_params("parallel", "parallel"),
    )(qn, w_uq, g_nope, trig)


def _kv_kernel(kvn_ref, wk_ref, wv_ref, kr_ref, gn_ref, cos_ref, sin_ref, k_ref, v_ref,
               acc_ref, ss_ref, *, heads, rb):
    kvn = kvn_ref[...]
    acc_ref[...] = _dot(kvn, wk_ref[...])
    v_ref[...] = _dot(kvn, wv_ref[...]).astype(v_ref.dtype)
    blk = (lax.broadcasted_iota(jnp.int32, (256, 256), 0) // 128
           == lax.broadcasted_iota(jnp.int32, (256, 256), 1) // 128)
    pair_ones = jnp.where(blk, 1.0, 0.0).astype(BF)
    for hd in range(0, heads, 2):
        a = acc_ref[:, hd * QK_NOPE:(hd + 2) * QK_NOPE]
        ss_ref[:, hd * 128:(hd + 2) * 128] = _dot((a * a).astype(BF), pair_ones)
    ss_ref[:, heads * 128:] = _row_sumsq(kr_ref[:, :128], QK_ROPE)
    gn = gn_ref[...]
    for r in range(0, acc_ref.shape[0], rb):
        rs = slice(r, r + rb)
        kr_rot = kr_ref[rs, :128] * cos_ref[rs, :] + kr_ref[rs, 128:] * sin_ref[rs, :]
        kr_ss = ss_ref[rs, heads * 128:]
        for hd in range(heads):
            ss = ss_ref[rs, hd * 128:(hd + 1) * 128] + kr_ss
            inv = lax.rsqrt(ss * (1.0 / QK_HEAD) + EPS)
            k_ref[rs, hd * HEAD_PAD:hd * HEAD_PAD + QK_NOPE] = (
                acc_ref[rs, hd * QK_NOPE:(hd + 1) * QK_NOPE] * (inv * gn)).astype(k_ref.dtype)
            k_ref[rs, hd * HEAD_PAD + QK_NOPE:(hd + 1) * HEAD_PAD] = (kr_rot * inv).astype(
                k_ref.dtype)


def _kv_proj(kvn, w_uk, w_uv, l, kr, gn, cos_t, sin_t, tm, heads, tiles_per_seq):
    rows = kvn.shape[0]
    kern = functools.partial(_kv_kernel, heads=heads, rb=min(64, tm))
    return pl.pallas_call(
        kern,
        grid=(rows // tm, N_HEADS // heads),
        in_specs=[pl.BlockSpec((tm, KV_LORA), lambda i, j: (i, 0)),
                  _layer_spec(l, (KV_LORA, heads * QK_NOPE), lambda i, j: (0, j)),
                  _layer_spec(l, (KV_LORA, heads * V_HEAD), lambda i, j: (0, j)),
                  pl.BlockSpec((tm, 256), lambda i, j: (i, 0)),
                  pl.BlockSpec((1, 128), lambda i, j: (0, 0)),
                  pl.BlockSpec((tm, 128), lambda i, j: (i % tiles_per_seq, 0)),
                  pl.BlockSpec((tm, 128), lambda i, j: (i % tiles_per_seq, 0))],
        out_specs=[pl.BlockSpec((tm, heads * HEAD_PAD), lambda i, j: (i, j)),
                   pl.BlockSpec((tm, heads * V_HEAD), lambda i, j: (i, j))],
        out_shape=[jax.ShapeDtypeStruct((rows, N_HEADS * HEAD_PAD), BF),
                   jax.ShapeDtypeStruct((rows, N_HEADS * V_HEAD), BF)],
        scratch_shapes=[pltpu.VMEM((tm, heads * QK_NOPE), F32),
                        pltpu.VMEM((tm, (heads + 1) * 128), F32)],
        name="kv_proj",
        compiler_params=_params("parallel", "parallel"),
    )(kvn, w_uk, w_uv, kr, gn, cos_t, sin_t)


def _attn_kernel(q_ref, k_ref, v_ref, kp_ref, vp_ref, o_ref, head_bufs, *, tq, tc, td, rb,
                 n_prefix):
    for hd, bufs in enumerate(head_bufs):
        for src, dst in ((v_ref, bufs[-2]), (vp_ref, bufs[-1])):
            dst[:, :V_HEAD] = src[:, hd * V_HEAD:(hd + 1) * V_HEAD]
            dst[:, V_HEAD:] = jnp.ones((dst.shape[0], dst.shape[1] - V_HEAD), dst.dtype)
    for q_tile in range(q_ref.shape[0] // tq):
        _attn_tile(q_ref, k_ref, v_ref, kp_ref, vp_ref, o_ref, head_bufs, q_tile=q_tile, tq=tq,
                   tc=tc, td=td, rb=rb, n_prefix=n_prefix)


def _attn_tile(q_ref, k_ref, v_ref, kp_ref, vp_ref, o_ref, head_bufs, *, q_tile, tq, tc, td, rb,
               n_prefix):
    d0 = q_tile * tq
    items = [None] + [(k0, min(tc, d0 - k0)) for k0 in range(0, d0, tc)]
    bands = range(tq // td)
    qcols = lambda hd: slice(hd * HEAD_PAD, (hd + 1) * HEAD_PAD)
    vcols = lambda hd: slice(hd * V_HEAD, (hd + 1) * V_HEAD)
    tile = slice(d0, d0 + tq)

    def scores(hd, item, s_ref):
        if item is not None:
            k0, width = item
            s_ref[:, :width] = _dot_nt(q_ref[tile, qcols(hd)], k_ref[k0:k0 + width, qcols(hd)])
            return
        s_ref[:, tq:tq + 128] = _dot_nt(q_ref[tile, qcols(hd)], kp_ref[:, qcols(hd)])
        for g in bands:
            seen = (g + 1) * td
            s_ref[g * td:seen, :seen] = _dot_nt(q_ref[d0 + g * td:d0 + seen, qcols(hd)],
                                                k_ref[d0:d0 + seen, qcols(hd)])

    def softmax_rows(rs, pieces, p_ref, a_ref, m_ref):
        s = jnp.concatenate([x for _, x in pieces], axis=1)
        m_old = m_ref[rs, :]
        m_new = jnp.maximum(m_old, jnp.max(s, axis=-1, keepdims=True))
        alpha = jnp.exp2(m_old - m_new)
        p = jnp.exp2(s - jnp.concatenate([m_new] * (s.shape[1] // 128), axis=1))
        m_ref[rs, :] = m_new
        a_ref[rs, :] = alpha
        at = 0
        for cols, x in pieces:
            p_ref[rs, cols] = p[:, at:at + x.shape[1]].astype(p_ref.dtype)
            at += x.shape[1]

    def softmax(item, s_ref, p_ref, a_ref, m_ref):
        if item is not None:
            width = item[1]
            rows_per_block = min(rb, ROW_BLOCK_ELEMS // width)
            for r in range(0, tq, rows_per_block):
                rs = slice(r, r + rows_per_block)
                softmax_rows(rs, [(slice(0, width), s_ref[rs, :width])], p_ref, a_ref, m_ref)
            return
        for r in range(0, tq, rb):
            rs = slice(r, r + rb)
            g = r // td
            open_cols, last = slice(0, g * td), slice(g * td, (g + 1) * td)
            row = lax.broadcasted_iota(jnp.int32, (rb, td), 0) + (r - g * td)
            col = lax.broadcasted_iota(jnp.int32, (rb, td), 1)
            colp = lax.broadcasted_iota(jnp.int32, (rb, 128), 1)
            pieces = [(open_cols, s_ref[rs, open_cols])] if g else []
            pieces += [(last, jnp.where(col <= row, s_ref[rs, last], NEG_BIG)),
                       (slice(tq, tq + 128),
                        jnp.where(colp < n_prefix, s_ref[rs, tq:tq + 128], NEG_BIG))]
            softmax_rows(rs, pieces, p_ref, a_ref, m_ref)

    def accumulate(item, p_ref, a_ref, acc_ref, vx_ref, vpx_ref):
        def rescaled(rows):
            a = a_ref[rows, :]
            return jnp.concatenate([a, a], axis=1) * acc_ref[rows, :]

        if item is not None:
            k0, width = item
            acc_ref[...] = rescaled(slice(None)) + _dot(p_ref[:, :width],
                                                        vx_ref[k0:k0 + width, :])
            return
        for g in bands:
            rows, seen = slice(g * td, (g + 1) * td), (g + 1) * td
            acc_ref[rows, :] = (rescaled(rows) + _dot(p_ref[rows, :seen], vx_ref[d0:d0 + seen, :])
                                + _dot(p_ref[rows, tq:tq + 128], vpx_ref[...]))

    for _, _, _, m_ref, acc_ref, _, _ in head_bufs:
        m_ref[...] = jnp.full(m_ref.shape, NEG_BIG, F32)
        acc_ref[...] = jnp.zeros(acc_ref.shape, F32)
    n = len(items)
    for t in range(n + 2):
        for hd, (s_refs, p_refs, a_refs, m_ref, acc_ref, vx_ref, vpx_ref) in enumerate(head_bufs):
            depth = len(s_refs)
            if t < n:
                scores(hd, items[t], s_refs[t % depth])
            if 1 <= t <= n:
                u = (t - 1) % depth
                softmax(items[t - 1], s_refs[u], p_refs[u], a_refs[u], m_ref)
            if t >= 2:
                u = (t - 2) % depth
                accumulate(items[t - 2], p_refs[u], a_refs[u], acc_ref, vx_ref, vpx_ref)
    for hd, bufs in enumerate(head_bufs):
        acc_ref = bufs[4]
        o_ref[tile, vcols(hd)] = (acc_ref[:, :V_HEAD] / acc_ref[:, V_HEAD:]).astype(o_ref.dtype)


def _attention(q, k, v, kp, vp, batch, seq, tq, tc, td, heads):
    depth = 3
    stat = pltpu.VMEM((tq, 128), F32)
    width = max(tc, tq + 128)
    head_bufs = [([pltpu.VMEM((tq, width), F32)] * depth, [pltpu.VMEM((tq, width), BF)] * depth,
                  [stat] * depth, stat, pltpu.VMEM((tq, 2 * V_HEAD), F32),
                  pltpu.VMEM((seq, 2 * V_HEAD), BF), pltpu.VMEM((128, 2 * V_HEAD), BF))] * heads
    kern = functools.partial(_attn_kernel, tq=tq, tc=tc, td=td, rb=64, n_prefix=N_META)
    return pl.pallas_call(
        kern,
        grid=(batch, N_HEADS // heads),
        in_specs=[pl.BlockSpec((None, seq, heads * HEAD_PAD), lambda b, h: (b, 0, h)),
                  pl.BlockSpec((None, seq, heads * HEAD_PAD), lambda b, h: (b, 0, h)),
                  pl.BlockSpec((None, seq, heads * V_HEAD), lambda b, h: (b, 0, h)),
                  pl.BlockSpec((128, heads * HEAD_PAD), lambda b, h: (0, h)),
                  pl.BlockSpec((128, heads * V_HEAD), lambda b, h: (0, h))],
        out_specs=pl.BlockSpec((None, seq, heads * V_HEAD), lambda b, h: (b, 0, h)),
        out_shape=jax.ShapeDtypeStruct((batch, seq, N_HEADS * V_HEAD), BF),
        scratch_shapes=[head_bufs],
        name="attention",
        compiler_params=_params("parallel", "parallel"),
    )(q, k, v, kp, vp)


def _attn_meta_kernel(q_ref, k_ref, v_ref, o_ref):
    s = _dot_nt(q_ref[...], k_ref[...])
    row = lax.broadcasted_iota(jnp.int32, s.shape, 0)
    col = lax.broadcasted_iota(jnp.int32, s.shape, 1)
    s = jnp.where(row >= col, s, NEG_BIG)
    p = jnp.exp2(s - jnp.max(s, axis=-1, keepdims=True))
    l = jnp.sum(p, axis=-1, keepdims=True)
    o_ref[...] = (_dot(p.astype(BF), v_ref[...]) / l).astype(o_ref.dtype)


def _attention_meta(q, k, v):
    return pl.pallas_call(
        _attn_meta_kernel,
        grid=(N_HEADS,),
        in_specs=[pl.BlockSpec((N_META, HEAD_PAD), lambda h: (0, h)),
                  pl.BlockSpec((N_META, HEAD_PAD), lambda h: (0, h)),
                  pl.BlockSpec((N_META, V_HEAD), lambda h: (0, h))],
        out_specs=pl.BlockSpec((N_META, V_HEAD), lambda h: (0, h)),
        out_shape=jax.ShapeDtypeStruct((N_META, N_HEADS * V_HEAD), BF),
        name="attention_meta",
        compiler_params=_params("parallel"),
    )(q, k, v)


def _pool_branch_kernel(h_ref, w_ref, pw_ref, ps_ref, hist_ref, yc_ref, *rest,
                        tm, tiles_per_seq, pos_offset):
    carry_ref = rest[-1]
    i = pl.program_id(0)
    pin = _dot(h_ref[...], w_ref[...])

    @pl.when(i % tiles_per_seq == 0)
    def _():
        carry_ref[...] = hist_ref[...]

    ext = jnp.concatenate([carry_ref[...], pin], axis=0)
    tail = pin[tm - HIST:]
    carry_ref[...] = tail
    if len(rest) == 2:
        rest[0][...] = tail
    n_ext = tm + HIST
    seen = (lax.broadcasted_iota(jnp.int32, (tm, 1), 0)
            + ((i % tiles_per_seq) * tm + pos_offset + 1)).astype(F32)
    for g, w in enumerate(POOL_WINDOWS):
        xg = ext[:, g * POOL_GROUP:(g + 1) * POOL_GROUP]
        s, span = xg, 1
        while span < w:
            s = s[span:] + s[:s.shape[0] - span]
            span *= 2
        s = s[n_ext - (w - 1) - tm:]
        pooled = s / jnp.minimum(seen, float(w)) - xg[HIST:]
        mixed = _dot(pooled.astype(BF), pw_ref[g])
        yc_ref[:, g * POOL_GROUP:(g + 1) * POOL_GROUP] = (
            mixed * ps_ref[:, g * POOL_GROUP:(g + 1) * POOL_GROUP]).astype(yc_ref.dtype)


def _pool_branch(h, w_pool, pool_w, l, pool_scale, hist, tm, tiles_per_seq, pos_offset,
                 emit_hist):
    rows = h.shape[0]
    assert not emit_hist or rows == tm
    kern = functools.partial(_pool_branch_kernel, tm=tm, tiles_per_seq=tiles_per_seq,
                             pos_offset=pos_offset)
    ng = len(POOL_WINDOWS)
    n_out = 2 if emit_hist else 1
    return pl.pallas_call(
        kern,
        grid=(rows // tm,),
        in_specs=[pl.BlockSpec((tm, D_MODEL), lambda i: (i, 0)),
                  _layer_spec(l, (D_MODEL, D_POOL), lambda i: (0, 0)),
                  _layer_spec(l, (ng, POOL_GROUP, POOL_GROUP), lambda i: (0, 0, 0)),
                  pl.BlockSpec((1, D_POOL), lambda i: (0, 0)),
                  pl.BlockSpec((HIST, D_POOL), lambda i: (0, 0))],
        out_specs=[pl.BlockSpec((tm, D_POOL), lambda i: (i, 0)),
                   pl.BlockSpec((HIST, D_POOL), lambda i: (0, 0))][:n_out],
        out_shape=[jax.ShapeDtypeStruct((rows, D_POOL), BF),
                   jax.ShapeDtypeStruct((HIST, D_POOL), F32)][:n_out],
        scratch_shapes=[pltpu.VMEM((HIST, D_POOL), F32)],
        name="pool_branch",
        compiler_params=_params("arbitrary"),
    )(h, w_pool, pool_w, pool_scale, hist)


def _merge_kernel(h_ref, ya_ref, yb_ref, yc_ref, wg0_ref, wg1_ref, wg2_ref, wa_ref, wb_ref, wc_ref,
                  o_ref):
    h = h_ref[...]
    m = jax.nn.sigmoid(_dot(h, wg0_ref[...])) * _dot(ya_ref[...], wa_ref[...])
    m += jax.nn.sigmoid(_dot(h, wg1_ref[...])) * _dot(yb_ref[...], wb_ref[...])
    m += jax.nn.sigmoid(_dot(h, wg2_ref[...])) * _dot(yc_ref[...], wc_ref[...])
    o_ref[...] = m.astype(o_ref.dtype)


def _merge(h, ya, yb, yc, w_gate, wa, wb, wc, l, tm, tn):
    rows = h.shape[0]
    ncb = D_MODEL // tn
    row = lambda width: pl.BlockSpec((tm, width), lambda i, j: (i, 0))
    col = lambda depth, off: _layer_spec(l, (depth, tn), lambda i, j: (0, j + off))
    return pl.pallas_call(
        _merge_kernel,
        grid=(rows // tm, ncb),
        in_specs=[row(D_MODEL), row(D_CONV), row(N_HEADS * V_HEAD), row(D_POOL),
                  col(D_MODEL, 0), col(D_MODEL, ncb), col(D_MODEL, 2 * ncb),
                  col(D_CONV, 0), col(N_HEADS * V_HEAD, 0), col(D_POOL, 0)],
        out_specs=pl.BlockSpec((tm, tn), lambda i, j: (i, j)),
        out_shape=jax.ShapeDtypeStruct((rows, D_MODEL), BF),
        name="merge",
        compiler_params=_params("parallel", "arbitrary"),
    )(h, ya, yb, yc, w_gate, w_gate, w_gate, wa, wb, wc)


def _oproj_kernel(m_ref, w_ref, x_ref, g_ref, xo_ref, h2_ref):
    x = x_ref[...] + _dot(m_ref[...], w_ref[...])
    xo_ref[...] = x
    h2_ref[...] = _rms_norm(x, g_ref[...]).astype(h2_ref.dtype)


def _oproj(merged, w_o, l, x, g, tm):
    rows = x.shape[0]
    return pl.pallas_call(
        _oproj_kernel,
        grid=(rows // tm,),
        in_specs=[pl.BlockSpec((tm, D_MODEL), lambda i: (i, 0)),
                  _layer_spec(l, (D_MODEL, D_MODEL), lambda i: (0, 0)),
                  pl.BlockSpec((tm, D_MODEL), lambda i: (i, 0)),
                  pl.BlockSpec((1, D_MODEL), lambda i: (0, 0))],
        out_specs=[pl.BlockSpec((tm, D_MODEL), lambda i: (i, 0)),
                   pl.BlockSpec((tm, D_MODEL), lambda i: (i, 0))],
        out_shape=[jax.ShapeDtypeStruct((rows, D_MODEL), F32),
                   jax.ShapeDtypeStruct((rows, D_MODEL), BF)],
        name="oproj",
        compiler_params=_params("parallel"),
    )(merged, w_o, x, g)


def _mlp_kernel(h2_ref, wu_ref, wd_ref, x_ref, o_ref):
    @pl.when(pl.program_id(1) == 0)
    def _():
        o_ref[...] = x_ref[...]

    a = jnp.maximum(_dot(h2_ref[...], wu_ref[...]), 0.0)
    o_ref[...] += _dot((a * a).astype(BF), wd_ref[...])


def _mlp(h2, w_up, w_down, l, x, tm, tf):
    rows = x.shape[0]
    return pl.pallas_call(
        _mlp_kernel,
        grid=(rows // tm, D_FF // tf),
        in_specs=[pl.BlockSpec((tm, D_MODEL), lambda i, k: (i, 0)),
                  _layer_spec(l, (D_MODEL, tf), lambda i, k: (0, k)),
                  _layer_spec(l, (tf, D_MODEL), lambda i, k: (k, 0)),
                  pl.BlockSpec((tm, D_MODEL), lambda i, k: (i, 0))],
        out_specs=pl.BlockSpec((tm, D_MODEL), lambda i, k: (i, 0)),
        out_shape=jax.ShapeDtypeStruct((rows, D_MODEL), F32),
        name="mlp",
        compiler_params=_params("parallel", "arbitrary"),
    )(h2, w_up, w_down, x)


def _swap_halves(a):
    return jnp.concatenate([a[..., HALF_ROPE:], a[..., :HALF_ROPE]], axis=-1)


def _twice(a):
    return jnp.concatenate([a, a], axis=-1)


def _widen_head(a):
    rope = a[..., QK_NOPE:]
    return jnp.concatenate([a[..., :QK_NOPE], rope, _swap_halves(rope)], axis=-1)


def _rope_gain_tables(gain, cos_t, sin_t):
    rope_gain = gain[QK_NOPE:]
    return cos_t * rope_gain[None], sin_t * _swap_halves(rope_gain)[None]


def _stacked_weights(w_in, w_uq, w_ukv, pool_w, w_branch_a, w_branch_b, w_branch_c, w_o, w_up,
                     w_down):
    depth = w_in.shape[0]
    wi = w_in.astype(BF)
    o_q = 3 * D_CONV
    o_kr = o_q + Q_LORA + KV_LORA
    o_pool = o_kr + QK_ROPE
    o_gate = o_pool + D_POOL
    w_kr = wi[:, :, o_kr:o_pool]
    w_ukv_h = w_ukv.astype(BF).reshape(depth, KV_LORA, N_HEADS, QK_NOPE + V_HEAD)
    return dict(
        w_in=wi,
        w_lat=jnp.concatenate([wi[:, :, o_q:o_kr], _twice(w_kr), _twice(_swap_halves(w_kr))],
                              axis=2),
        w_pool=wi[:, :, o_pool:o_gate],
        w_gate=wi[:, :, o_gate:],
        w_uq=_widen_head(w_uq.astype(BF).reshape(depth, Q_LORA, N_HEADS, QK_HEAD)).reshape(
            depth, Q_LORA, N_HEADS * HEAD_PAD),
        w_uk=w_ukv_h[..., :QK_NOPE].reshape(depth, KV_LORA, N_HEADS * QK_NOPE),
        w_uv=w_ukv_h[..., QK_NOPE:].reshape(depth, KV_LORA, N_HEADS * V_HEAD),
        pool_w=pool_w.astype(BF),
        wa=w_branch_a.astype(BF),
        wb=w_branch_b.astype(BF),
        wc=w_branch_c.astype(BF),
        w_o=w_o.astype(BF),
        w_up=w_up.astype(BF),
        w_down=w_down.astype(BF),
    )


def _layer_vectors(l, cos_t, sin_t, attn_norm, conv_w, q_lat_norm, kv_lat_norm, q_norm, k_norm,
                   pool_scale, mlp_norm):
    gq = q_norm[l] * (QK_HEAD ** -0.5 * LOG2_E)
    q_cos, q_sin = _rope_gain_tables(gq, cos_t, sin_t)
    k_cos, k_sin = _rope_gain_tables(k_norm[l], cos_t, sin_t)
    return dict(
        attn_norm=attn_norm[l][None],
        conv_w=conv_w[l],
        q_lat_norm=q_lat_norm[l][None],
        kv_lat_norm=kv_lat_norm[l][None],
        gq_nope=gq[None, :QK_NOPE],
        gk_nope=k_norm[l][None, :QK_NOPE],
        q_trig=jnp.concatenate([q_cos, q_sin], axis=1),
        k_rope=(_twice(k_cos), _twice(k_sin)),
        pool_scale=pool_scale[l][None],
        mlp_norm=mlp_norm[l][None],
    )


def _rope_tables(total):
    pos = jnp.arange(total, dtype=F32)
    inv = ROPE_THETA ** (-jnp.arange(0, QK_ROPE, 2, dtype=F32) / QK_ROPE)
    ang = pos[:, None] * inv[None, :]
    cos, sin = jnp.cos(ang), jnp.sin(ang)
    return _twice(cos), jnp.concatenate([-sin, sin], axis=1)


def _layer(x, w, l, p, positions, prefix, *, batch, seq, finish):
    is_meta = prefix is None
    q_trig = p['q_trig'][positions]
    k_cos, k_sin = (t[positions] for t in p['k_rope'])
    tm = min(512, seq)
    tps = seq // tm
    tm_big = min(1024, seq)
    if is_meta:
        conv_hist = jnp.zeros((HIST, D_CONV), F32)
        pool_hist = jnp.zeros((HIST, D_POOL), F32)
    else:
        conv_hist, pool_hist, kp, vp = prefix

    h, qn, kvn, kr = _latent(x, p['attn_norm'], w['w_lat'], l, p['q_lat_norm'], p['kv_lat_norm'],
                             tm)
    ya, *conv_tail = _conv_branch(h, w['w_in'], l, p['conv_w'], conv_hist, tm_big, 512,
                                  seq // tm_big, is_meta)
    yc, *pool_tail = _pool_branch(h, w['w_pool'], w['pool_w'], l, p['pool_scale'], pool_hist,
                                  tm_big, seq // tm_big, 0 if is_meta else N_META, is_meta)
    k, v = _kv_proj(kvn, w['w_uk'], w['w_uv'], l, kr, p['gk_nope'], k_cos, k_sin, tm, N_HEADS,
                    tps)
    out_prefix = (conv_tail[0], pool_tail[0], k, v) if is_meta else None
    if not finish:
        return None, out_prefix
    q = _q_proj(qn, w['w_uq'], l, p['gq_nope'], q_trig, tm, N_HEADS, tps)
    if is_meta:
        yb = _attention_meta(q, k, v)
    else:
        pad = ((0, 128 - N_META), (0, 0))
        per_batch = lambda a: a.reshape(batch, seq, a.shape[-1])
        yb = _attention(per_batch(q), per_batch(k), per_batch(v), jnp.pad(kp, pad),
                        jnp.pad(vp, pad), batch, seq, 512, 1024, 256, 1).reshape(batch * seq, -1)
    merged = _merge(h, ya, yb, yc, w['w_gate'], w['wa'], w['wb'], w['wc'], l, tm_big, 512)
    x_mid, h2 = _oproj(merged, w['w_o'], l, x, p['mlp_norm'], min(256, seq))
    return _mlp(h2, w['w_up'], w['w_down'], l, x_mid, tm_big, 512), out_prefix


def kernel(x, meta_tokens, attn_norm, w_in, conv_w, q_lat_norm, kv_lat_norm, w_uq, w_ukv, q_norm, k_norm, pool_w, pool_scale, w_branch_a, w_branch_b, w_branch_c, w_o, mlp_norm, w_up, w_down):
    batch, seq, d = x.shape
    depth = w_in.shape[0]
    cos_t, sin_t = _rope_tables(N_META + seq)
    xm = meta_tokens.astype(F32)
    xr = x.reshape(batch * seq, d)
    w = _stacked_weights(w_in, w_uq, w_ukv, pool_w, w_branch_a, w_branch_b, w_branch_c, w_o, w_up,
                         w_down)
    for l in range(depth):
        p = _layer_vectors(l, cos_t, sin_t, attn_norm, conv_w, q_lat_norm, kv_lat_norm, q_norm,
                           k_norm, pool_scale, mlp_norm)
        last = l == depth - 1
        xm, prefix = _layer(xm, w, l, p, slice(0, N_META), None, batch=1, seq=N_META,
                            finish=not last)
        xr, _ = _layer(xr, w, l, p, slice(N_META, None), prefix, batch=batch, seq=seq, finish=True)
    return xr.reshape(batch, seq, d)
```

```python
import functools

import jax
import jax.numpy as jnp
from jax import lax
from jax.experimental import pallas as pl
from jax.experimental.pallas import tpu as pltpu

D_MODEL = 2048
N_META = 16
EPS = 1e-6
D_CONV = 1024
N_HEADS = 16
QK_NOPE = 128
QK_ROPE = 64
QK_HEAD = QK_NOPE + QK_ROPE
HALF_ROPE = QK_ROPE // 2
V_HEAD = 128
Q_LORA = 512
KV_LORA = 512
ROPE_THETA = 10000.0
D_POOL = 1024
POOL_WINDOWS = (2, 4, 8, 16)
POOL_GROUP = D_POOL // len(POOL_WINDOWS)
D_FF = 4 * D_MODEL
HIST = 16
HEAD_PAD = 256
D_LAT = Q_LORA + KV_LORA + 256
VMEM_LIMIT = 56 * 2**20
ROW_BLOCK_ELEMS = 32 * 1024
NEG_BIG = -1e30
LOG2_E = 1.4426950408889634

BF = jnp.bfloat16
F32 = jnp.float32


def _dot(a, b):
    return jnp.dot(a, b, preferred_element_type=F32)


def _dot_nt(a, b):
    return lax.dot_general(a, b, (((1,), (1,)), ((), ())), preferred_element_type=F32)


def _layer_spec(l, shape, index_map):
    return pl.BlockSpec((None,) + shape, lambda *g: (l,) + index_map(*g))


def _params(*sem):
    return pltpu.CompilerParams(dimension_semantics=sem, vmem_limit_bytes=VMEM_LIMIT)


def _rms_norm(x, g):
    ms = jnp.mean(x * x, axis=-1, keepdims=True)
    return x * lax.rsqrt(ms + EPS) * g


def _conv_branch_kernel(h_ref, wu_ref, wb_ref, wc_ref, cw_ref, hist_ref, ya_ref, *rest,
                        tm, tiles_per_seq):
    carry_ref = rest[-1]
    i, j = pl.program_id(0), pl.program_id(1)
    h = h_ref[...]
    cu = _dot(h, wc_ref[...]) * _dot(h, wu_ref[...])

    @pl.when(i % tiles_per_seq == 0)
    def _():
        carry_ref[j] = hist_ref[...]

    ext = jnp.concatenate([carry_ref[j], cu], axis=0)
    tail = cu[tm - HIST:]
    carry_ref[j] = tail
    if len(rest) == 2:
        rest[0][...] = tail
    cw = cw_ref[...]
    y = (cw[2:3] * cu + cw[1:2] * ext[HIST - 1:HIST - 1 + tm] + cw[0:1] * ext[HIST - 2:HIST - 2 + tm])
    ya_ref[...] = (_dot(h, wb_ref[...]) * y).astype(ya_ref.dtype)


def _conv_branch(h, w_in, l, conv_w, hist, tm, tc, tiles_per_seq, emit_hist):
    rows = h.shape[0]
    ncb = D_CONV // tc
    assert not emit_hist or rows == tm
    kern = functools.partial(_conv_branch_kernel, tm=tm, tiles_per_seq=tiles_per_seq)
    n_out = 2 if emit_hist else 1
    return pl.pallas_call(
        kern,
        grid=(rows // tm, ncb),
        in_specs=[pl.BlockSpec((tm, D_MODEL), lambda i, j: (i, 0)),
                  _layer_spec(l, (D_MODEL, tc), lambda i, j: (0, j)),
                  _layer_spec(l, (D_MODEL, tc), lambda i, j: (0, j + ncb)),
                  _layer_spec(l, (D_MODEL, tc), lambda i, j: (0, j + 2 * ncb)),
                  pl.BlockSpec((3, tc), lambda i, j: (0, j)),
                  pl.BlockSpec((HIST, tc), lambda i, j: (0, j))],
        out_specs=[pl.BlockSpec((tm, tc), lambda i, j: (i, j)),
                   pl.BlockSpec((HIST, tc), lambda i, j: (0, j))][:n_out],
        out_shape=[jax.ShapeDtypeStruct((rows, D_CONV), BF),
                   jax.ShapeDtypeStruct((HIST, D_CONV), F32)][:n_out],
        scratch_shapes=[pltpu.VMEM((ncb, HIST, tc), F32)],
        name="conv_branch",
        compiler_params=_params("arbitrary", "arbitrary"),
    )(h, w_in, w_in, w_in, conv_w, hist)


def _latent_kernel(x_ref, g_ref, w_ref, gq_ref, gkv_ref, h_ref, qn_ref, kvn_ref, kr_ref):
    h = _rms_norm(x_ref[...], g_ref[...]).astype(h_ref.dtype)
    h_ref[...] = h
    acc = _dot(h, w_ref[...])
    qn_ref[...] = _rms_norm(acc[:, :Q_LORA], gq_ref[...]).astype(qn_ref.dtype)
    kvn_ref[...] = _rms_norm(acc[:, Q_LORA:Q_LORA + KV_LORA], gkv_ref[...]).astype(kvn_ref.dtype)
    kr_ref[...] = acc[:, Q_LORA + KV_LORA:]


def _latent(x, g, w_lat, l, gq, gkv, tm):
    rows = x.shape[0]
    row = lambda width: pl.BlockSpec((tm, width), lambda i: (i, 0))
    vec = lambda width: pl.BlockSpec((1, width), lambda i: (0, 0))
    return pl.pallas_call(
        _latent_kernel,
        grid=(rows // tm,),
        in_specs=[row(D_MODEL), vec(D_MODEL), _layer_spec(l, (D_MODEL, D_LAT), lambda i: (0, 0)),
                  vec(Q_LORA), vec(KV_LORA)],
        out_specs=[row(D_MODEL), row(Q_LORA), row(KV_LORA), row(256)],
        out_shape=[jax.ShapeDtypeStruct((rows, D_MODEL), BF),
                   jax.ShapeDtypeStruct((rows, Q_LORA), BF),
                   jax.ShapeDtypeStruct((rows, KV_LORA), BF),
                   jax.ShapeDtypeStruct((rows, 256), F32)],
        name="latent",
        compiler_params=_params("parallel"),
    )(x, g, w_lat, gq, gkv)


def _row_sumsq(a, n_real):
    keep = lax.broadcasted_iota(jnp.int32, (a.shape[1], 128), 0) < n_real
    return _dot((a * a).astype(BF), jnp.where(keep, 1.0, 0.0).astype(BF))


def _q_kernel(qn_ref, w_ref, g_ref, trig_ref, q_ref, acc_ref, ss_ref, *, heads, rb):
    acc_ref[...] = _dot(qn_ref[...], w_ref[...])
    for hd in range(heads):
        ss_ref[:, hd * 128:(hd + 1) * 128] = _row_sumsq(
            acc_ref[:, hd * HEAD_PAD:(hd + 1) * HEAD_PAD], QK_HEAD)
    g = g_ref[...]
    for r in range(0, acc_ref.shape[0], rb):
        rs = slice(r, r + rb)
        trig = trig_ref[rs, :]
        for hd in range(heads):
            c0 = hd * HEAD_PAD
            inv = lax.rsqrt(ss_ref[rs, hd * 128:(hd + 1) * 128] * (1.0 / QK_HEAD) + EPS)
            q_ref[rs, c0:c0 + QK_NOPE] = (
                acc_ref[rs, c0:c0 + QK_NOPE] * (inv * g)).astype(q_ref.dtype)
            q_ref[rs, c0 + QK_NOPE:c0 + HEAD_PAD] = (
                acc_ref[rs, c0 + QK_NOPE:c0 + HEAD_PAD] * (inv * trig)).astype(q_ref.dtype)


def _q_proj(qn, w_uq, l, g_nope, trig, tm, heads, tiles_per_seq):
    rows = qn.shape[0]
    kern = functools.partial(_q_kernel, heads=heads, rb=min(64, tm))
    return pl.pallas_call(
        kern,
        grid=(rows // tm, N_HEADS // heads),
        in_specs=[pl.BlockSpec((tm, Q_LORA), lambda i, j: (i, 0)),
                  _layer_spec(l, (Q_LORA, heads * HEAD_PAD), lambda i, j: (0, j)),
                  pl.BlockSpec((1, QK_NOPE), lambda i, j: (0, 0)),
                  pl.BlockSpec((tm, 128), lambda i, j: (i % tiles_per_seq, 0))],
        out_specs=pl.BlockSpec((tm, heads * HEAD_PAD), lambda i, j: (i, j)),
        out_shape=jax.ShapeDtypeStruct((rows, N_HEADS * HEAD_PAD), BF),
        scratch_shapes=[pltpu.VMEM((tm, heads * HEAD_PAD), F32),
                        pltpu.VMEM((tm, heads * 128), F32)],
        name="q_proj",
        compiler_params=_params("parallel", "parallel"),
    )(qn, w_uq, g_nope, trig)


def _kv_kernel(kvn_ref, wk_ref, wv_ref, kr_ref, gn_ref, cos_ref, sin_ref, k_ref, v_ref,
               acc_ref, ss_ref, *, heads, rb):
    kvn = kvn_ref[...]
    acc_ref[...] = _dot(kvn, wk_ref[...])
    v_ref[...] = _dot(kvn, wv_ref[...]).astype(v_ref.dtype)
    blk = (lax.broadcasted_iota(jnp.int32, (256, 256), 0) // 128
           == lax.broadcasted_iota(jnp.int32, (256, 256), 1) // 128)
    pair_ones = jnp.where(blk, 1.0, 0.0).astype(BF)
    for hd in range(0, heads, 2):
        a = acc_ref[:, hd * QK_NOPE:(hd + 2) * QK_NOPE]
        ss_ref[:, hd * 128:(hd + 2) * 128] = _dot((a * a).astype(BF), pair_ones)
    ss_ref[:, heads * 128:] = _row_sumsq(kr_ref[:, :128], QK_ROPE)
    gn = gn_ref[...]
    for r in range(0, acc_ref.shape[0], rb):
        rs = slice(r, r + rb)
        kr_rot = kr_ref[rs, :128] * cos_ref[rs, :] + kr_ref[rs, 128:] * sin_ref[rs, :]
        kr_ss = ss_ref[rs, heads * 128:]
        for hd in range(heads):
            ss = ss_ref[rs, hd * 128:(hd + 1) * 128] + kr_ss
            inv = lax.rsqrt(ss * (1.0 / QK_HEAD) + EPS)
            k_ref[rs, hd * HEAD_PAD:hd * HEAD_PAD + QK_NOPE] = (
                acc_ref[rs, hd * QK_NOPE:(hd + 1) * QK_NOPE] * (inv * gn)).astype(k_ref.dtype)
            k_ref[rs, hd * HEAD_PAD + QK_NOPE:(hd + 1) * HEAD_PAD] = (kr_rot * inv).astype(
                k_ref.dtype)


def _kv_proj(kvn, w_uk, w_uv, l, kr, gn, cos_t, sin_t, tm, heads, tiles_per_seq):
    rows = kvn.shape[0]
    kern = functools.partial(_kv_kernel, heads=heads, rb=min(64, tm))
    return pl.pallas_call(
        kern,
        grid=(rows // tm, N_HEADS // heads),
        in_specs=[pl.BlockSpec((tm, KV_LORA), lambda i, j: (i, 0)),
                  _layer_spec(l, (KV_LORA, heads * QK_NOPE), lambda i, j: (0, j)),
                  _layer_spec(l, (KV_LORA, heads * V_HEAD), lambda i, j: (0, j)),
                  pl.BlockSpec((tm, 256), lambda i, j: (i, 0)),
                  pl.BlockSpec((1, 128), lambda i, j: (0, 0)),
                  pl.BlockSpec((tm, 128), lambda i, j: (i % tiles_per_seq, 0)),
                  pl.BlockSpec((tm, 128), lambda i, j: (i % tiles_per_seq, 0))],
        out_specs=[pl.BlockSpec((tm, heads * HEAD_PAD), lambda i, j: (i, j)),
                   pl.BlockSpec((tm, heads * V_HEAD), lambda i, j: (i, j))],
        out_shape=[jax.ShapeDtypeStruct((rows, N_HEADS * HEAD_PAD), BF),
                   jax.ShapeDtypeStruct((rows, N_HEADS * V_HEAD), BF)],
        scratch_shapes=[pltpu.VMEM((tm, heads * QK_NOPE), F32),
                        pltpu.VMEM((tm, (heads + 1) * 128), F32)],
        name="kv_proj",
        compiler_params=_params("parallel", "parallel"),
    )(kvn, w_uk, w_uv, kr, gn, cos_t, sin_t)


def _attn_kernel(q_ref, k_ref, v_ref, kp_ref, vp_ref, o_ref, head_bufs, *, tq, tc, td, rb,
                 n_prefix):
    for hd, bufs in enumerate(head_bufs):
        for src, dst in ((v_ref, bufs[-2]), (vp_ref, bufs[-1])):
            dst[:, :V_HEAD] = src[:, hd * V_HEAD:(hd + 1) * V_HEAD]
            dst[:, V_HEAD:] = jnp.ones((dst.shape[0], dst.shape[1] - V_HEAD), dst.dtype)
    for q_tile in range(q_ref.shape[0] // tq):
        _attn_tile(q_ref, k_ref, v_ref, kp_ref, vp_ref, o_ref, head_bufs, q_tile=q_tile, tq=tq,
                   tc=tc, td=td, rb=rb, n_prefix=n_prefix)


def _attn_tile(q_ref, k_ref, v_ref, kp_ref, vp_ref, o_ref, head_bufs, *, q_tile, tq, tc, td, rb,
               n_prefix):
    d0 = q_tile * tq
    items = [None] + [(k0, min(tc, d0 - k0)) for k0 in range(0, d0, tc)]
    bands = range(tq // td)
    qcols = lambda hd: slice(hd * HEAD_PAD, (hd + 1) * HEAD_PAD)
    vcols = lambda hd: slice(hd * V_HEAD, (hd + 1) * V_HEAD)
    tile = slice(d0, d0 + tq)

    def scores(hd, item, s_ref):
        if item is not None:
            k0, width = item
            s_ref[:, :width] = _dot_nt(q_ref[tile, qcols(hd)], k_ref[k0:k0 + width, qcols(hd)])
            return
        s_ref[:, tq:tq + 128] = _dot_nt(q_ref[tile, qcols(hd)], kp_ref[:, qcols(hd)])
        for g in bands:
            seen = (g + 1) * td
            s_ref[g * td:seen, :seen] = _dot_nt(q_ref[d0 + g * td:d0 + seen, qcols(hd)],
                                                k_ref[d0:d0 + seen, qcols(hd)])

    def softmax_rows(rs, pieces, p_ref, a_ref, m_ref):
        s = jnp.concatenate([x for _, x in pieces], axis=1)
        m_old = m_ref[rs, :]
        m_new = jnp.maximum(m_old, jnp.max(s, axis=-1, keepdims=True))
        alpha = jnp.exp2(m_old - m_new)
        p = jnp.exp2(s - jnp.concatenate([m_new] * (s.shape[1] // 128), axis=1))
        m_ref[rs, :] = m_new
        a_ref[rs, :] = alpha
        at = 0
        for cols, x in pieces:
            p_ref[rs, cols] = p[:, at:at + x.shape[1]].astype(p_ref.dtype)
            at += x.shape[1]

    def softmax(item, s_ref, p_ref, a_ref, m_ref):
        if item is not None:
            width = item[1]
            rows_per_block = min(rb, ROW_BLOCK_ELEMS // width)
            for r in range(0, tq, rows_per_block):
                rs = slice(r, r + rows_per_block)
                softmax_rows(rs, [(slice(0, width), s_ref[rs, :width])], p_ref, a_ref, m_ref)
            return
        for r in range(0, tq, rb):
            rs = slice(r, r + rb)
            g = r // td
            open_cols, last = slice(0, g * td), slice(g * td, (g + 1) * td)
            row = lax.broadcasted_iota(jnp.int32, (rb, td), 0) + (r - g * td)
            col = lax.broadcasted_iota(jnp.int32, (rb, td), 1)
            colp = lax.broadcasted_iota(jnp.int32, (rb, 128), 1)
            pieces = [(open_cols, s_ref[rs, open_cols])] if g else []
            pieces += [(last, jnp.where(col <= row, s_ref[rs, last], NEG_BIG)),
                       (slice(tq, tq + 128),
                        jnp.where(colp < n_prefix, s_ref[rs, tq:tq + 128], NEG_BIG))]
            softmax_rows(rs, pieces, p_ref, a_ref, m_ref)

    def accumulate(item, p_ref, a_ref, acc_ref, vx_ref, vpx_ref):
        def rescaled(rows):
            a = a_ref[rows, :]
            return jnp.concatenate([a, a], axis=1) * acc_ref[rows, :]

        if item is not None:
            k0, width = item
            acc_ref[...] = rescaled(slice(None)) + _dot(p_ref[:, :width],
                                                        vx_ref[k0:k0 + width, :])
            return
        for g in bands:
            rows, seen = slice(g * td, (g + 1) * td), (g + 1) * td
            acc_ref[rows, :] = (rescaled(rows) + _dot(p_ref[rows, :seen], vx_ref[d0:d0 + seen, :])
                                + _dot(p_ref[rows, tq:tq + 128], vpx_ref[...]))

    for _, _, _, m_ref, acc_ref, _, _ in head_bufs:
        m_ref[...] = jnp.full(m_ref.shape, NEG_BIG, F32)
        acc_ref[...] = jnp.zeros(acc_ref.shape, F32)
    n = len(items)
    for t in range(n + 2):
        for hd, (s_refs, p_refs, a_refs, m_ref, acc_ref, vx_ref, vpx_ref) in enumerate(head_bufs):
            depth = len(s_refs)
            if t < n:
                scores(hd, items[t], s_refs[t % depth])
            if 1 <= t <= n:
                u = (t - 1) % depth
                softmax(items[t - 1], s_refs[u], p_refs[u], a_refs[u], m_ref)
            if t >= 2:
                u = (t - 2) % depth
                accumulate(items[t - 2], p_refs[u], a_refs[u], acc_ref, vx_ref, vpx_ref)
    for hd, bufs in enumerate(head_bufs):
        acc_ref = bufs[4]
        o_ref[tile, vcols(hd)] = (acc_ref[:, :V_HEAD] / acc_ref[:, V_HEAD:]).astype(o_ref.dtype)


def _attention(q, k, v, kp, vp, batch, seq, tq, tc, td, heads):
    depth = 3
    stat = pltpu.VMEM((tq, 128), F32)
    width = max(tc, tq + 128)
    head_bufs = [([pltpu.VMEM((tq, width), F32)] * depth, [pltpu.VMEM((tq, width), BF)] * depth,
                  [stat] * depth, stat, pltpu.VMEM((tq, 2 * V_HEAD), F32),
                  pltpu.VMEM((seq, 2 * V_HEAD), BF), pltpu.VMEM((128, 2 * V_HEAD), BF))] * heads
    kern = functools.partial(_attn_kernel, tq=tq, tc=tc, td=td, rb=64, n_prefix=N_META)
    return pl.pallas_call(
        kern,
        grid=(batch, N_HEADS // heads),
        in_specs=[pl.BlockSpec((None, seq, heads * HEAD_PAD), lambda b, h: (b, 0, h)),
                  pl.BlockSpec((None, seq, heads * HEAD_PAD), lambda b, h: (b, 0, h)),
                  pl.BlockSpec((None, seq, heads * V_HEAD), lambda b, h: (b, 0, h)),
                  pl.BlockSpec((128, heads * HEAD_PAD), lambda b, h: (0, h)),
                  pl.BlockSpec((128, heads * V_HEAD), lambda b, h: (0, h))],
        out_specs=pl.BlockSpec((None, seq, heads * V_HEAD), lambda b, h: (b, 0, h)),
        out_shape=jax.ShapeDtypeStruct((batch, seq, N_HEADS * V_HEAD), BF),
        scratch_shapes=[head_bufs],
        name="attention",
        compiler_params=_params("parallel", "parallel"),
    )(q, k, v, kp, vp)


def _attn_meta_kernel(q_ref, k_ref, v_ref, o_ref):
    s = _dot_nt(q_ref[...], k_ref[...])
    row = lax.broadcasted_iota(jnp.int32, s.shape, 0)
    col = lax.broadcasted_iota(jnp.int32, s.shape, 1)
    s = jnp.where(row >= col, s, NEG_BIG)
    p = jnp.exp2(s - jnp.max(s, axis=-1, keepdims=True))
    l = jnp.sum(p, axis=-1, keepdims=True)
    o_ref[...] = (_dot(p.astype(BF), v_ref[...]) / l).astype(o_ref.dtype)


def _attention_meta(q, k, v):
    return pl.pallas_call(
        _attn_meta_kernel,
        grid=(N_HEADS,),
        in_specs=[pl.BlockSpec((N_META, HEAD_PAD), lambda h: (0, h)),
                  pl.BlockSpec((N_META, HEAD_PAD), lambda h: (0, h)),
                  pl.BlockSpec((N_META, V_HEAD), lambda h: (0, h))],
        out_specs=pl.BlockSpec((N_META, V_HEAD), lambda h: (0, h)),
        out_shape=jax.ShapeDtypeStruct((N_META, N_HEADS * V_HEAD), BF),
        name="attention_meta",
        compiler_params=_params("parallel"),
    )(q, k, v)


def _pool_branch_kernel(h_ref, w_ref, pw_ref, ps_ref, hist_ref, yc_ref, *rest,
                        tm, tiles_per_seq, pos_offset):
    carry_ref = rest[-1]
    i = pl.program_id(0)
    pin = _dot(h_ref[...], w_ref[...])

    @pl.when(i % tiles_per_seq == 0)
    def _():
        carry_ref[...] = hist_ref[...]

    ext = jnp.concatenate([carry_ref[...], pin], axis=0)
    tail = pin[tm - HIST:]
    carry_ref[...] = tail
    if len(rest) == 2:
        rest[0][...] = tail
    n_ext = tm + HIST
    seen = (lax.broadcasted_iota(jnp.int32, (tm, 1), 0)
            + ((i % tiles_per_seq) * tm + pos_offset + 1)).astype(F32)
    for g, w in enumerate(POOL_WINDOWS):
        xg = ext[:, g * POOL_GROUP:(g + 1) * POOL_GROUP]
        s, span = xg, 1
        while span < w:
            s = s[span:] + s[:s.shape[0] - span]
            span *= 2
        s = s[n_ext - (w - 1) - tm:]
        pooled = s / jnp.minimum(seen, float(w)) - xg[HIST:]
        mixed = _dot(pooled.astype(BF), pw_ref[g])
        yc_ref[:, g * POOL_GROUP:(g + 1) * POOL_GROUP] = (
            mixed * ps_ref[:, g * POOL_GROUP:(g + 1) * POOL_GROUP]).astype(yc_ref.dtype)


def _pool_branch(h, w_pool, pool_w, l, pool_scale, hist, tm, tiles_per_seq, pos_offset,
                 emit_hist):
    rows = h.shape[0]
    assert not emit_hist or rows == tm
    kern = functools.partial(_pool_branch_kernel, tm=tm, tiles_per_seq=tiles_per_seq,
                             pos_offset=pos_offset)
    ng = len(POOL_WINDOWS)
    n_out = 2 if emit_hist else 1
    return pl.pallas_call(
        kern,
        grid=(rows // tm,),
        in_specs=[pl.BlockSpec((tm, D_MODEL), lambda i: (i, 0)),
                  _layer_spec(l, (D_MODEL, D_POOL), lambda i: (0, 0)),
                  _layer_spec(l, (ng, POOL_GROUP, POOL_GROUP), lambda i: (0, 0, 0)),
                  pl.BlockSpec((1, D_POOL), lambda i: (0, 0)),
                  pl.BlockSpec((HIST, D_POOL), lambda i: (0, 0))],
        out_specs=[pl.BlockSpec((tm, D_POOL), lambda i: (i, 0)),
                   pl.BlockSpec((HIST, D_POOL), lambda i: (0, 0))][:n_out],
        out_shape=[jax.ShapeDtypeStruct((rows, D_POOL), BF),
                   jax.ShapeDtypeStruct((HIST, D_POOL), F32)][:n_out],
        scratch_shapes=[pltpu.VMEM((HIST, D_POOL), F32)],
        name="pool_branch",
        compiler_params=_params("arbitrary"),
    )(h, w_pool, pool_w, pool_scale, hist)


def _merge_kernel(h_ref, ya_ref, yb_ref, yc_ref, wg0_ref, wg1_ref, wg2_ref, wa_ref, wb_ref, wc_ref,
                  o_ref):
    h = h_ref[...]
    m = jax.nn.sigmoid(_dot(h, wg0_ref[...])) * _dot(ya_ref[...], wa_ref[...])
    m += jax.nn.sigmoid(_dot(h, wg1_ref[...])) * _dot(yb_ref[...], wb_ref[...])
    m += jax.nn.sigmoid(_dot(h, wg2_ref[...])) * _dot(yc_ref[...], wc_ref[...])
    o_ref[...] = m.astype(o_ref.dtype)


def _merge(h, ya, yb, yc, w_gate, wa, wb, wc, l, tm, tn):
    rows = h.shape[0]
    ncb = D_MODEL // tn
    row = lambda width: pl.BlockSpec((tm, width), lambda i, j: (i, 0))
    col = lambda depth, off: _layer_spec(l, (depth, tn), lambda i, j: (0, j + off))
    return pl.pallas_call(
        _merge_kernel,
        grid=(rows // tm, ncb),
        in_specs=[row(D_MODEL), row(D_CONV), row(N_HEADS * V_HEAD), row(D_POOL),
                  col(D_MODEL, 0), col(D_MODEL, ncb), col(D_MODEL, 2 * ncb),
                  col(D_CONV, 0), col(N_HEADS * V_HEAD, 0), col(D_POOL, 0)],
        out_specs=pl.BlockSpec((tm, tn), lambda i, j: (i, j)),
        out_shape=jax.ShapeDtypeStruct((rows, D_MODEL), BF),
        name="merge",
        compiler_params=_params("parallel", "arbitrary"),
    )(h, ya, yb, yc, w_gate, w_gate, w_gate, wa, wb, wc)


def _oproj_kernel(m_ref, w_ref, x_ref, g_ref, xo_ref, h2_ref):
    x = x_ref[...] + _dot(m_ref[...], w_ref[...])
    xo_ref[...] = x
    h2_ref[...] = _rms_norm(x, g_ref[...]).astype(h2_ref.dtype)


def _oproj(merged, w_o, l, x, g, tm):
    rows = x.shape[0]
    return pl.pallas_call(
        _oproj_kernel,
        grid=(rows // tm,),
        in_specs=[pl.BlockSpec((tm, D_MODEL), lambda i: (i, 0)),
                  _layer_spec(l, (D_MODEL, D_MODEL), lambda i: (0, 0)),
                  pl.BlockSpec((tm, D_MODEL), lambda i: (i, 0)),
                  pl.BlockSpec((1, D_MODEL), lambda i: (0, 0))],
        out_specs=[pl.BlockSpec((tm, D_MODEL), lambda i: (i, 0)),
                   pl.BlockSpec((tm, D_MODEL), lambda i: (i, 0))],
        out_shape=[jax.ShapeDtypeStruct((rows, D_MODEL), F32),
                   jax.ShapeDtypeStruct((rows, D_MODEL), BF)],
        name="oproj",
        compiler_params=_params("parallel"),
    )(merged, w_o, x, g)


def _mlp_kernel(h2_ref, wu_ref, wd_ref, x_ref, o_ref):
    @pl.when(pl.program_id(1) == 0)
    def _():
        o_ref[...] = x_ref[...]

    a = jnp.maximum(_dot(h2_ref[...], wu_ref[...]), 0.0)
    o_ref[...] += _dot((a * a).astype(BF), wd_ref[...])


def _mlp(h2, w_up, w_down, l, x, tm, tf):
    rows = x.shape[0]
    return pl.pallas_call(
        _mlp_kernel,
        grid=(rows // tm, D_FF // tf),
        in_specs=[pl.BlockSpec((tm, D_MODEL), lambda i, k: (i, 0)),
                  _layer_spec(l, (D_MODEL, tf), lambda i, k: (0, k)),
                  _layer_spec(l, (tf, D_MODEL), lambda i, k: (k, 0)),
                  pl.BlockSpec((tm, D_MODEL), lambda i, k: (i, 0))],
        out_specs=pl.BlockSpec((tm, D_MODEL), lambda i, k: (i, 0)),
        out_shape=jax.ShapeDtypeStruct((rows, D_MODEL), F32),
        name="mlp",
        compiler_params=_params("parallel", "arbitrary"),
    )(h2, w_up, w_down, x)


def _swap_halves(a):
    return jnp.concatenate([a[..., HALF_ROPE:], a[..., :HALF_ROPE]], axis=-1)


def _twice(a):
    return jnp.concatenate([a, a], axis=-1)


def _widen_head(a):
    rope = a[..., QK_NOPE:]
    return jnp.concatenate([a[..., :QK_NOPE], rope, _swap_halves(rope)], axis=-1)


def _rope_gain_tables(gain, cos_t, sin_t):
    rope_gain = gain[QK_NOPE:]
    return cos_t * rope_gain[None], sin_t * _swap_halves(rope_gain)[None]


def _stacked_weights(w_in, w_uq, w_ukv, pool_w, w_branch_a, w_branch_b, w_branch_c, w_o, w_up,
                     w_down):
    depth = w_in.shape[0]
    wi = w_in.astype(BF)
    o_q = 3 * D_CONV
    o_kr = o_q + Q_LORA + KV_LORA
    o_pool = o_kr + QK_ROPE
    o_gate = o_pool + D_POOL
    w_kr = wi[:, :, o_kr:o_pool]
    w_ukv_h = w_ukv.astype(BF).reshape(depth, KV_LORA, N_HEADS, QK_NOPE + V_HEAD)
    return dict(
        w_in=wi,
        w_lat=jnp.concatenate([wi[:, :, o_q:o_kr], _twice(w_kr), _twice(_swap_halves(w_kr))],
                              axis=2),
        w_pool=wi[:, :, o_pool:o_gate],
        w_gate=wi[:, :, o_gate:],
        w_uq=_widen_head(w_uq.astype(BF).reshape(depth, Q_LORA, N_HEADS, QK_HEAD)).reshape(
            depth, Q_LORA, N_HEADS * HEAD_PAD),
        w_uk=w_ukv_h[..., :QK_NOPE].reshape(depth, KV_LORA, N_HEADS * QK_NOPE),
        w_uv=w_ukv_h[..., QK_NOPE:].reshape(depth, KV_LORA, N_HEADS * V_HEAD),
        pool_w=pool_w.astype(BF),
        wa=w_branch_a.astype(BF),
        wb=w_branch_b.astype(BF),
        wc=w_branch_c.astype(BF),
        w_o=w_o.astype(BF),
        w_up=w_up.astype(BF),
        w_down=w_down.astype(BF),
    )


def _layer_vectors(l, cos_t, sin_t, attn_norm, conv_w, q_lat_norm, kv_lat_norm, q_norm, k_norm,
                   pool_scale, mlp_norm):
    gq = q_norm[l] * (QK_HEAD ** -0.5 * LOG2_E)
    q_cos, q_sin = _rope_gain_tables(gq, cos_t, sin_t)
    k_cos, k_sin = _rope_gain_tables(k_norm[l], cos_t, sin_t)
    return dict(
        attn_norm=attn_norm[l][None],
        conv_w=conv_w[l],
        q_lat_norm=q_lat_norm[l][None],
        kv_lat_norm=kv_lat_norm[l][None],
        gq_nope=gq[None, :QK_NOPE],
        gk_nope=k_norm[l][None, :QK_NOPE],
        q_trig=jnp.concatenate([q_cos, q_sin], axis=1),
        k_rope=(_twice(k_cos), _twice(k_sin)),
        pool_scale=pool_scale[l][None],
        mlp_norm=mlp_norm[l][None],
    )


def _rope_tables(total):
    pos = jnp.arange(total, dtype=F32)
    inv = ROPE_THETA ** (-jnp.arange(0, QK_ROPE, 2, dtype=F32) / QK_ROPE)
    ang = pos[:, None] * inv[None, :]
    cos, sin = jnp.cos(ang), jnp.sin(ang)
    return _twice(cos), jnp.concatenate([-sin, sin], axis=1)


def _layer(x, w, l, p, positions, prefix, *, batch, seq, finish):
    is_meta = prefix is None
    q_trig = p['q_trig'][positions]
    k_cos, k_sin = (t[positions] for t in p['k_rope'])
    tm = min(512, seq)
    tps = seq // tm
    tm_big = min(1024, seq)
    if is_meta:
        conv_hist = jnp.zeros((HIST, D_CONV), F32)
        pool_hist = jnp.zeros((HIST, D_POOL), F32)
    else:
        conv_hist, pool_hist, kp, vp = prefix

    h, qn, kvn, kr = _latent(x, p['attn_norm'], w['w_lat'], l, p['q_lat_norm'], p['kv_lat_norm'],
                             tm_big)
    ya, *conv_tail = _conv_branch(h, w['w_in'], l, p['conv_w'], conv_hist, tm_big, 1024,
                                  seq // tm_big, is_meta)
    yc, *pool_tail = _pool_branch(h, w['w_pool'], w['pool_w'], l, p['pool_scale'], pool_hist,
                                  tm_big, seq // tm_big, 0 if is_meta else N_META, is_meta)
    k, v = _kv_proj(kvn, w['w_uk'], w['w_uv'], l, kr, p['gk_nope'], k_cos, k_sin, tm, N_HEADS,
                    tps)
    out_prefix = (conv_tail[0], pool_tail[0], k, v) if is_meta else None
    if not finish:
        return None, out_prefix
    q = _q_proj(qn, w['w_uq'], l, p['gq_nope'], q_trig, tm, N_HEADS, tps)
    if is_meta:
        yb = _attention_meta(q, k, v)
    else:
        pad = ((0, 128 - N_META), (0, 0))
        per_batch = lambda a: a.reshape(batch, seq, a.shape[-1])
        yb = _attention(per_batch(q), per_batch(k), per_batch(v), jnp.pad(kp, pad),
                        jnp.pad(vp, pad), batch, seq, 512, 1024, 256, 1).reshape(batch * seq, -1)
    merged = _merge(h, ya, yb, yc, w['w_gate'], w['wa'], w['wb'], w['wc'], l, tm_big, 512)
    x_mid, h2 = _oproj(merged, w['w_o'], l, x, p['mlp_norm'], tm)
    return _mlp(h2, w['w_up'], w['w_down'], l, x_mid, tm_big, 512), out_prefix


def kernel(x, meta_tokens, attn_norm, w_in, conv_w, q_lat_norm, kv_lat_norm, w_uq, w_ukv, q_norm, k_norm, pool_w, pool_scale, w_branch_a, w_branch_b, w_branch_c, w_o, mlp_norm, w_up, w_down):
    batch, seq, d = x.shape
    depth = w_in.shape[0]
    cos_t, sin_t = _rope_tables(N_META + seq)
    xm = meta_tokens.astype(F32)
    xr = x.reshape(batch * seq, d)
    w = _stacked_weights(w_in, w_uq, w_ukv, pool_w, w_branch_a, w_branch_b, w_branch_c, w_o, w_up,
                         w_down)
    for l in range(depth):
        p = _layer_vectors(l, cos_t, sin_t, attn_norm, conv_w, q_lat_norm, kv_lat_norm, q_norm,
                           k_norm, pool_scale, mlp_norm)
        last = l == depth - 1
        xm, prefix = _layer(xm, w, l, p, slice(0, N_META), None, batch=1, seq=N_META,
                            finish=not last)
        xr, _ = _layer(xr, w, l, p, slice(N_META, None), prefix, batch=batch, seq=seq, finish=True)
    return xr.reshape(batch, seq, d)
```

```python
import functools
from typing import NamedTuple

import jax
import jax.numpy as jnp
from jax import lax
from jax.experimental import pallas as pl
from jax.experimental.pallas import tpu as pltpu

D_MODEL = 2048
N_META = 16
EPS = 1e-6
D_CONV = 1024
N_HEADS = 16
QK_NOPE = 128
QK_ROPE = 64
QK_HEAD = QK_NOPE + QK_ROPE
HALF_ROPE = QK_ROPE // 2
V_HEAD = 128
Q_LORA = 512
KV_LORA = 512
ROPE_THETA = 10000.0
D_POOL = 1024
POOL_WINDOWS = (2, 4, 8, 16)
POOL_GROUP = D_POOL // len(POOL_WINDOWS)
D_FF = 4 * D_MODEL
HIST = 16
HEAD_PAD = 256
D_LAT = Q_LORA + KV_LORA + 256
V7X_VMEM_BYTES = 64 * 2**20
VMEM_LIMIT = V7X_VMEM_BYTES - 8 * 2**20
V7X_VREG_F32 = 8 * 128
ROW_BLOCK_ELEMS = 32 * V7X_VREG_F32
NEG_BIG = -1e30
LOG2_E = 1.4426950408889634

BF = jnp.bfloat16
F32 = jnp.float32


def _dot(a, b):
    return jnp.dot(a, b, preferred_element_type=F32)


def _dot_nt(a, b):
    return lax.dot_general(a, b, (((1,), (1,)), ((), ())), preferred_element_type=F32)


def _layer_spec(l, shape, index_map):
    return pl.BlockSpec((None,) + shape, lambda *g: (l,) + index_map(*g))


def _params(*sem):
    return pltpu.CompilerParams(dimension_semantics=sem, vmem_limit_bytes=VMEM_LIMIT)


def _rms_norm(x, g):
    ms = jnp.mean(x * x, axis=-1, keepdims=True)
    return x * lax.rsqrt(ms + EPS) * g


def _conv_branch_kernel(h_ref, wu_ref, wb_ref, wc_ref, cw_ref, hist_ref, ya_ref, *rest,
                        tm, tiles_per_seq):
    carry_ref = rest[-1]
    i, j = pl.program_id(0), pl.program_id(1)
    h = h_ref[...]
    cu = _dot(h, wc_ref[...]) * _dot(h, wu_ref[...])

    @pl.when(i % tiles_per_seq == 0)
    def _():
        carry_ref[j] = hist_ref[...]

    ext = jnp.concatenate([carry_ref[j], cu], axis=0)
    tail = cu[tm - HIST:]
    carry_ref[j] = tail
    if len(rest) == 2:
        rest[0][...] = tail
    cw = cw_ref[...]
    y = (cw[2:3] * cu + cw[1:2] * ext[HIST - 1:HIST - 1 + tm] + cw[0:1] * ext[HIST - 2:HIST - 2 + tm])
    ya_ref[...] = (_dot(h, wb_ref[...]) * y).astype(ya_ref.dtype)


def _conv_branch(h, w_in, l, conv_w, hist, tm, tc, tiles_per_seq, emit_hist):
    rows = h.shape[0]
    ncb = D_CONV // tc
    assert not emit_hist or rows == tm
    kern = functools.partial(_conv_branch_kernel, tm=tm, tiles_per_seq=tiles_per_seq)
    n_out = 2 if emit_hist else 1
    return pl.pallas_call(
        kern,
        grid=(rows // tm, ncb),
        in_specs=[pl.BlockSpec((tm, D_MODEL), lambda i, j: (i, 0)),
                  _layer_spec(l, (D_MODEL, tc), lambda i, j: (0, j)),
                  _layer_spec(l, (D_MODEL, tc), lambda i, j: (0, j + ncb)),
                  _layer_spec(l, (D_MODEL, tc), lambda i, j: (0, j + 2 * ncb)),
                  pl.BlockSpec((3, tc), lambda i, j: (0, j)),
                  pl.BlockSpec((HIST, tc), lambda i, j: (0, j))],
        out_specs=[pl.BlockSpec((tm, tc), lambda i, j: (i, j)),
                   pl.BlockSpec((HIST, tc), lambda i, j: (0, j))][:n_out],
        out_shape=[jax.ShapeDtypeStruct((rows, D_CONV), BF),
                   jax.ShapeDtypeStruct((HIST, D_CONV), F32)][:n_out],
        scratch_shapes=[pltpu.VMEM((ncb, HIST, tc), F32)],
        name="conv_branch",
        compiler_params=_params("arbitrary", "arbitrary"),
    )(h, w_in, w_in, w_in, conv_w, hist)


def _latent_kernel(x_ref, g_ref, w_ref, gq_ref, gkv_ref, h_ref, qn_ref, kvn_ref, kr_ref):
    h = _rms_norm(x_ref[...], g_ref[...]).astype(h_ref.dtype)
    h_ref[...] = h
    acc = _dot(h, w_ref[...])
    qn_ref[...] = _rms_norm(acc[:, :Q_LORA], gq_ref[...]).astype(qn_ref.dtype)
    kvn_ref[...] = _rms_norm(acc[:, Q_LORA:Q_LORA + KV_LORA], gkv_ref[...]).astype(kvn_ref.dtype)
    kr_ref[...] = acc[:, Q_LORA + KV_LORA:]


def _latent(x, g, w_lat, l, gq, gkv, tm):
    rows = x.shape[0]
    row = lambda width: pl.BlockSpec((tm, width), lambda i: (i, 0))
    vec = lambda width: pl.BlockSpec((1, width), lambda i: (0, 0))
    return pl.pallas_call(
        _latent_kernel,
        grid=(rows // tm,),
        in_specs=[row(D_MODEL), vec(D_MODEL), _layer_spec(l, (D_MODEL, D_LAT), lambda i: (0, 0)),
                  vec(Q_LORA), vec(KV_LORA)],
        out_specs=[row(D_MODEL), row(Q_LORA), row(KV_LORA), row(256)],
        out_shape=[jax.ShapeDtypeStruct((rows, D_MODEL), BF),
                   jax.ShapeDtypeStruct((rows, Q_LORA), BF),
                   jax.ShapeDtypeStruct((rows, KV_LORA), BF),
                   jax.ShapeDtypeStruct((rows, 256), F32)],
        name="latent",
        compiler_params=_params("parallel"),
    )(x, g, w_lat, gq, gkv)


def _row_sumsq(a, n_real):
    keep = lax.broadcasted_iota(jnp.int32, (a.shape[1], 128), 0) < n_real
    return _dot((a * a).astype(BF), jnp.where(keep, 1.0, 0.0).astype(BF))


def _q_kernel(qn_ref, w_ref, g_ref, trig_ref, q_ref, acc_ref, ss_ref, *, heads, rb):
    acc_ref[...] = _dot(qn_ref[...], w_ref[...])
    for hd in range(heads):
        ss_ref[:, hd * 128:(hd + 1) * 128] = _row_sumsq(
            acc_ref[:, hd * HEAD_PAD:(hd + 1) * HEAD_PAD], QK_HEAD)
    g = g_ref[...]
    for r in range(0, acc_ref.shape[0], rb):
        rs = slice(r, r + rb)
        trig = trig_ref[rs, :]
        for hd in range(heads):
            c0 = hd * HEAD_PAD
            inv = lax.rsqrt(ss_ref[rs, hd * 128:(hd + 1) * 128] * (1.0 / QK_HEAD) + EPS)
            q_ref[rs, c0:c0 + QK_NOPE] = (
                acc_ref[rs, c0:c0 + QK_NOPE] * (inv * g)).astype(q_ref.dtype)
            q_ref[rs, c0 + QK_NOPE:c0 + HEAD_PAD] = (
                acc_ref[rs, c0 + QK_NOPE:c0 + HEAD_PAD] * (inv * trig)).astype(q_ref.dtype)


def _q_proj(qn, w_uq, l, g_nope, trig, tm, heads, tiles_per_seq):
    rows = qn.shape[0]
    kern = functools.partial(_q_kernel, heads=heads, rb=min(64, tm))
    return pl.pallas_call(
        kern,
        grid=(rows // tm, N_HEADS // heads),
        in_specs=[pl.BlockSpec((tm, Q_LORA), lambda i, j: (i, 0)),
                  _layer_spec(l, (Q_LORA, heads * HEAD_PAD), lambda i, j: (0, j)),
                  pl.BlockSpec((1, QK_NOPE), lambda i, j: (0, 0)),
                  pl.BlockSpec((tm, 128), lambda i, j: (i % tiles_per_seq, 0))],
        out_specs=pl.BlockSpec((tm, heads * HEAD_PAD), lambda i, j: (i, j)),
        out_shape=jax.ShapeDtypeStruct((rows, N_HEADS * HEAD_PAD), BF),
        scratch_shapes=[pltpu.VMEM((tm, heads * HEAD_PAD), F32),
                        pltpu.VMEM((tm, heads * 128), F32)],
        name="q_proj",
        compiler_params=_params("parallel", "parallel"),
    )(qn, w_uq, g_nope, trig)


def _kv_kernel(kvn_ref, wk_ref, wv_ref, kr_ref, gn_ref, cos_ref, sin_ref, k_ref, v_ref,
               acc_ref, ss_ref, *, heads, rb):
    kvn = kvn_ref[...]
    acc_ref[...] = _dot(kvn, wk_ref[...])
    v_ref[...] = _dot(kvn, wv_ref[...]).astype(v_ref.dtype)
    blk = (lax.broadcasted_iota(jnp.int32, (256, 256), 0) // 128
           == lax.broadcasted_iota(jnp.int32, (256, 256), 1) // 128)
    pair_ones = jnp.where(blk, 1.0, 0.0).astype(BF)
    for hd in range(0, heads, 2):
        a = acc_ref[:, hd * QK_NOPE:(hd + 2) * QK_NOPE]
        ss_ref[:, hd * 128:(hd + 2) * 128] = _dot((a * a).astype(BF), pair_ones)
    ss_ref[:, heads * 128:] = _row_sumsq(kr_ref[:, :128], QK_ROPE)
    gn = gn_ref[...]
    for r in range(0, acc_ref.shape[0], rb):
        rs = slice(r, r + rb)
        kr_rot = kr_ref[rs, :128] * cos_ref[rs, :] + kr_ref[rs, 128:] * sin_ref[rs, :]
        kr_ss = ss_ref[rs, heads * 128:]
        for hd in range(heads):
            ss = ss_ref[rs, hd * 128:(hd + 1) * 128] + kr_ss
            inv = lax.rsqrt(ss * (1.0 / QK_HEAD) + EPS)
            k_ref[rs, hd * HEAD_PAD:hd * HEAD_PAD + QK_NOPE] = (
                acc_ref[rs, hd * QK_NOPE:(hd + 1) * QK_NOPE] * (inv * gn)).astype(k_ref.dtype)
            k_ref[rs, hd * HEAD_PAD + QK_NOPE:(hd + 1) * HEAD_PAD] = (kr_rot * inv).astype(
                k_ref.dtype)


def _kv_proj(kvn, w_uk, w_uv, l, kr, gn, cos_t, sin_t, tm, heads, tiles_per_seq):
    rows = kvn.shape[0]
    kern = functools.partial(_kv_kernel, heads=heads, rb=min(64, tm))
    return pl.pallas_call(
        kern,
        grid=(rows // tm, N_HEADS // heads),
        in_specs=[pl.BlockSpec((tm, KV_LORA), lambda i, j: (i, 0)),
                  _layer_spec(l, (KV_LORA, heads * QK_NOPE), lambda i, j: (0, j)),
                  _layer_spec(l, (KV_LORA, heads * V_HEAD), lambda i, j: (0, j)),
                  pl.BlockSpec((tm, 256), lambda i, j: (i, 0)),
                  pl.BlockSpec((1, 128), lambda i, j: (0, 0)),
                  pl.BlockSpec((tm, 128), lambda i, j: (i % tiles_per_seq, 0)),
                  pl.BlockSpec((tm, 128), lambda i, j: (i % tiles_per_seq, 0))],
        out_specs=[pl.BlockSpec((tm, heads * HEAD_PAD), lambda i, j: (i, j)),
                   pl.BlockSpec((tm, heads * V_HEAD), lambda i, j: (i, j))],
        out_shape=[jax.ShapeDtypeStruct((rows, N_HEADS * HEAD_PAD), BF),
                   jax.ShapeDtypeStruct((rows, N_HEADS * V_HEAD), BF)],
        scratch_shapes=[pltpu.VMEM((tm, heads * QK_NOPE), F32),
                        pltpu.VMEM((tm, (heads + 1) * 128), F32)],
        name="kv_proj",
        compiler_params=_params("parallel", "parallel"),
    )(kvn, w_uk, w_uv, kr, gn, cos_t, sin_t)


def _attn_kernel(q_ref, k_ref, v_ref, kp_ref, vp_ref, o_ref, head_bufs, *, tq, tc, td, rb,
                 n_prefix):
    for hd, bufs in enumerate(head_bufs):
        for src, dst in ((v_ref, bufs[-2]), (vp_ref, bufs[-1])):
            dst[:, :V_HEAD] = src[:, hd * V_HEAD:(hd + 1) * V_HEAD]
            dst[:, V_HEAD:] = jnp.ones((dst.shape[0], dst.shape[1] - V_HEAD), dst.dtype)
    for q_tile in range(q_ref.shape[0] // tq):
        _attn_tile(q_ref, k_ref, kp_ref, o_ref, head_bufs, q_tile=q_tile, tq=tq, tc=tc, td=td,
                   rb=rb, n_prefix=n_prefix)


def _attn_tile(q_ref, k_ref, kp_ref, o_ref, head_bufs, *, q_tile, tq, tc, td, rb, n_prefix):
    d0 = q_tile * tq
    items = [None] + [(k0, min(tc, d0 - k0)) for k0 in range(0, d0, tc)]
    bands = range(tq // td)
    qcols = lambda hd: slice(hd * HEAD_PAD, (hd + 1) * HEAD_PAD)
    vcols = lambda hd: slice(hd * V_HEAD, (hd + 1) * V_HEAD)
    tile = slice(d0, d0 + tq)

    def scores(hd, item, s_ref):
        if item is not None:
            k0, width = item
            s_ref[:, :width] = _dot_nt(q_ref[tile, qcols(hd)], k_ref[k0:k0 + width, qcols(hd)])
            return
        s_ref[:, tq:tq + 128] = _dot_nt(q_ref[tile, qcols(hd)], kp_ref[:, qcols(hd)])
        for g in bands:
            seen = (g + 1) * td
            s_ref[g * td:seen, :seen] = _dot_nt(q_ref[d0 + g * td:d0 + seen, qcols(hd)],
                                                k_ref[d0:d0 + seen, qcols(hd)])

    def softmax_rows(rs, pieces, p_ref, a_ref, m_ref):
        s = jnp.concatenate([x for _, x in pieces], axis=1)
        m_old = m_ref[rs, :]
        m_new = jnp.maximum(m_old, jnp.max(s, axis=-1, keepdims=True))
        alpha = jnp.exp2(m_old - m_new)
        p = jnp.exp2(s - jnp.concatenate([m_new] * (s.shape[1] // 128), axis=1))
        m_ref[rs, :] = m_new
        a_ref[rs, :] = alpha
        at = 0
        for cols, x in pieces:
            p_ref[rs, cols] = p[:, at:at + x.shape[1]].astype(p_ref.dtype)
            at += x.shape[1]

    def softmax(item, s_ref, p_ref, a_ref, m_ref):
        if item is not None:
            width = item[1]
            rows_per_block = min(rb, ROW_BLOCK_ELEMS // width)
            for r in range(0, tq, rows_per_block):
                rs = slice(r, r + rows_per_block)
                softmax_rows(rs, [(slice(0, width), s_ref[rs, :width])], p_ref, a_ref, m_ref)
            return
        for r in range(0, tq, rb):
            rs = slice(r, r + rb)
            g = r // td
            open_cols, last = slice(0, g * td), slice(g * td, (g + 1) * td)
            row = lax.broadcasted_iota(jnp.int32, (rb, td), 0) + (r - g * td)
            col = lax.broadcasted_iota(jnp.int32, (rb, td), 1)
            colp = lax.broadcasted_iota(jnp.int32, (rb, 128), 1)
            pieces = [(open_cols, s_ref[rs, open_cols])] if g else []
            pieces += [(last, jnp.where(col <= row, s_ref[rs, last], NEG_BIG)),
                       (slice(tq, tq + 128),
                        jnp.where(colp < n_prefix, s_ref[rs, tq:tq + 128], NEG_BIG))]
            softmax_rows(rs, pieces, p_ref, a_ref, m_ref)

    def accumulate(item, p_ref, a_ref, acc_ref, vx_ref, vpx_ref):
        def rescaled(rows):
            a = a_ref[rows, :]
            return jnp.concatenate([a, a], axis=1) * acc_ref[rows, :]

        if item is not None:
            k0, width = item
            acc_ref[...] = rescaled(slice(None)) + _dot(p_ref[:, :width],
                                                        vx_ref[k0:k0 + width, :])
            return
        for g in bands:
            rows, seen = slice(g * td, (g + 1) * td), (g + 1) * td
            acc_ref[rows, :] = (rescaled(rows) + _dot(p_ref[rows, :seen], vx_ref[d0:d0 + seen, :])
                                + _dot(p_ref[rows, tq:tq + 128], vpx_ref[...]))

    for _, _, _, m_ref, acc_ref, _, _ in head_bufs:
        m_ref[...] = jnp.full(m_ref.shape, NEG_BIG, F32)
        acc_ref[...] = jnp.zeros(acc_ref.shape, F32)
    n = len(items)
    for t in range(n + 2):
        for hd, (s_refs, p_refs, a_refs, m_ref, acc_ref, vx_ref, vpx_ref) in enumerate(head_bufs):
            depth = len(s_refs)
            if t < n:
                scores(hd, items[t], s_refs[t % depth])
            if 1 <= t <= n:
                u = (t - 1) % depth
                softmax(items[t - 1], s_refs[u], p_refs[u], a_refs[u], m_ref)
            if t >= 2:
                u = (t - 2) % depth
                accumulate(items[t - 2], p_refs[u], a_refs[u], acc_ref, vx_ref, vpx_ref)
    for hd, bufs in enumerate(head_bufs):
        acc_ref = bufs[4]
        o_ref[tile, vcols(hd)] = (acc_ref[:, :V_HEAD] / acc_ref[:, V_HEAD:]).astype(o_ref.dtype)


def _attention(q, k, v, kp, vp, batch, seq, tq, tc, td, heads):
    depth = 3
    stat = pltpu.VMEM((tq, 128), F32)
    width = max(tc, tq + 128)
    head_bufs = [([pltpu.VMEM((tq, width), F32)] * depth, [pltpu.VMEM((tq, width), BF)] * depth,
                  [stat] * depth, stat, pltpu.VMEM((tq, 2 * V_HEAD), F32),
                  pltpu.VMEM((seq, 2 * V_HEAD), BF), pltpu.VMEM((128, 2 * V_HEAD), BF))] * heads
    kern = functools.partial(_attn_kernel, tq=tq, tc=tc, td=td, rb=64, n_prefix=N_META)
    return pl.pallas_call(
        kern,
        grid=(batch, N_HEADS // heads),
        in_specs=[pl.BlockSpec((None, seq, heads * HEAD_PAD), lambda b, h: (b, 0, h)),
                  pl.BlockSpec((None, seq, heads * HEAD_PAD), lambda b, h: (b, 0, h)),
                  pl.BlockSpec((None, seq, heads * V_HEAD), lambda b, h: (b, 0, h)),
                  pl.BlockSpec((128, heads * HEAD_PAD), lambda b, h: (0, h)),
                  pl.BlockSpec((128, heads * V_HEAD), lambda b, h: (0, h))],
        out_specs=pl.BlockSpec((None, seq, heads * V_HEAD), lambda b, h: (b, 0, h)),
        out_shape=jax.ShapeDtypeStruct((batch, seq, N_HEADS * V_HEAD), BF),
        scratch_shapes=[head_bufs],
        name="attention",
        compiler_params=_params("parallel", "parallel"),
    )(q, k, v, kp, vp)


def _attn_meta_kernel(q_ref, k_ref, v_ref, o_ref):
    s = _dot_nt(q_ref[...], k_ref[...])
    row = lax.broadcasted_iota(jnp.int32, s.shape, 0)
    col = lax.broadcasted_iota(jnp.int32, s.shape, 1)
    s = jnp.where(row >= col, s, NEG_BIG)
    p = jnp.exp2(s - jnp.max(s, axis=-1, keepdims=True))
    l = jnp.sum(p, axis=-1, keepdims=True)
    o_ref[...] = (_dot(p.astype(BF), v_ref[...]) / l).astype(o_ref.dtype)


def _attention_meta(q, k, v):
    return pl.pallas_call(
        _attn_meta_kernel,
        grid=(N_HEADS,),
        in_specs=[pl.BlockSpec((N_META, HEAD_PAD), lambda h: (0, h)),
                  pl.BlockSpec((N_META, HEAD_PAD), lambda h: (0, h)),
                  pl.BlockSpec((N_META, V_HEAD), lambda h: (0, h))],
        out_specs=pl.BlockSpec((N_META, V_HEAD), lambda h: (0, h)),
        out_shape=jax.ShapeDtypeStruct((N_META, N_HEADS * V_HEAD), BF),
        name="attention_meta",
        compiler_params=_params("parallel"),
    )(q, k, v)


def _pool_branch_kernel(h_ref, w_ref, pw_ref, ps_ref, hist_ref, yc_ref, *rest,
                        tm, tiles_per_seq, pos_offset):
    carry_ref = rest[-1]
    i = pl.program_id(0)
    pin = _dot(h_ref[...], w_ref[...])

    @pl.when(i % tiles_per_seq == 0)
    def _():
        carry_ref[...] = hist_ref[...]

    ext = jnp.concatenate([carry_ref[...], pin], axis=0)
    tail = pin[tm - HIST:]
    carry_ref[...] = tail
    if len(rest) == 2:
        rest[0][...] = tail
    n_ext = tm + HIST
    seen = (lax.broadcasted_iota(jnp.int32, (tm, 1), 0)
            + ((i % tiles_per_seq) * tm + pos_offset + 1)).astype(F32)
    for g, w in enumerate(POOL_WINDOWS):
        xg = ext[:, g * POOL_GROUP:(g + 1) * POOL_GROUP]
        s, span = xg, 1
        while span < w:
            s = s[span:] + s[:s.shape[0] - span]
            span *= 2
        s = s[n_ext - (w - 1) - tm:]
        pooled = s / jnp.minimum(seen, float(w)) - xg[HIST:]
        mixed = _dot(pooled.astype(BF), pw_ref[g])
        yc_ref[:, g * POOL_GROUP:(g + 1) * POOL_GROUP] = (
            mixed * ps_ref[:, g * POOL_GROUP:(g + 1) * POOL_GROUP]).astype(yc_ref.dtype)


def _pool_branch(h, w_pool, pool_w, l, pool_scale, hist, tm, tiles_per_seq, pos_offset,
                 emit_hist):
    rows = h.shape[0]
    assert not emit_hist or rows == tm
    kern = functools.partial(_pool_branch_kernel, tm=tm, tiles_per_seq=tiles_per_seq,
                             pos_offset=pos_offset)
    ng = len(POOL_WINDOWS)
    n_out = 2 if emit_hist else 1
    return pl.pallas_call(
        kern,
        grid=(rows // tm,),
        in_specs=[pl.BlockSpec((tm, D_MODEL), lambda i: (i, 0)),
                  _layer_spec(l, (D_MODEL, D_POOL), lambda i: (0, 0)),
                  _layer_spec(l, (ng, POOL_GROUP, POOL_GROUP), lambda i: (0, 0, 0)),
                  pl.BlockSpec((1, D_POOL), lambda i: (0, 0)),
                  pl.BlockSpec((HIST, D_POOL), lambda i: (0, 0))],
        out_specs=[pl.BlockSpec((tm, D_POOL), lambda i: (i, 0)),
                   pl.BlockSpec((HIST, D_POOL), lambda i: (0, 0))][:n_out],
        out_shape=[jax.ShapeDtypeStruct((rows, D_POOL), BF),
                   jax.ShapeDtypeStruct((HIST, D_POOL), F32)][:n_out],
        scratch_shapes=[pltpu.VMEM((HIST, D_POOL), F32)],
        name="pool_branch",
        compiler_params=_params("arbitrary"),
    )(h, w_pool, pool_w, pool_scale, hist)


def _merge_kernel(h_ref, ya_ref, yb_ref, yc_ref, wg0_ref, wg1_ref, wg2_ref, wa_ref, wb_ref, wc_ref,
                  o_ref):
    h = h_ref[...]
    m = jax.nn.sigmoid(_dot(h, wg0_ref[...])) * _dot(ya_ref[...], wa_ref[...])
    m += jax.nn.sigmoid(_dot(h, wg1_ref[...])) * _dot(yb_ref[...], wb_ref[...])
    m += jax.nn.sigmoid(_dot(h, wg2_ref[...])) * _dot(yc_ref[...], wc_ref[...])
    o_ref[...] = m.astype(o_ref.dtype)


def _merge(h, ya, yb, yc, w_gate, wa, wb, wc, l, tm, tn):
    rows = h.shape[0]
    ncb = D_MODEL // tn
    row = lambda width: pl.BlockSpec((tm, width), lambda i, j: (i, 0))
    col = lambda depth, off: _layer_spec(l, (depth, tn), lambda i, j: (0, j + off))
    return pl.pallas_call(
        _merge_kernel,
        grid=(rows // tm, ncb),
        in_specs=[row(D_MODEL), row(D_CONV), row(N_HEADS * V_HEAD), row(D_POOL),
                  col(D_MODEL, 0), col(D_MODEL, ncb), col(D_MODEL, 2 * ncb),
                  col(D_CONV, 0), col(N_HEADS * V_HEAD, 0), col(D_POOL, 0)],
        out_specs=pl.BlockSpec((tm, tn), lambda i, j: (i, j)),
        out_shape=jax.ShapeDtypeStruct((rows, D_MODEL), BF),
        name="merge",
        compiler_params=_params("parallel", "arbitrary"),
    )(h, ya, yb, yc, w_gate, w_gate, w_gate, wa, wb, wc)


def _oproj_kernel(m_ref, w_ref, x_ref, g_ref, xo_ref, h2_ref):
    x = x_ref[...] + _dot(m_ref[...], w_ref[...])
    xo_ref[...] = x
    h2_ref[...] = _rms_norm(x, g_ref[...]).astype(h2_ref.dtype)


def _oproj(merged, w_o, l, x, g, tm):
    rows = x.shape[0]
    return pl.pallas_call(
        _oproj_kernel,
        grid=(rows // tm,),
        in_specs=[pl.BlockSpec((tm, D_MODEL), lambda i: (i, 0)),
                  _layer_spec(l, (D_MODEL, D_MODEL), lambda i: (0, 0)),
                  pl.BlockSpec((tm, D_MODEL), lambda i: (i, 0)),
                  pl.BlockSpec((1, D_MODEL), lambda i: (0, 0))],
        out_specs=[pl.BlockSpec((tm, D_MODEL), lambda i: (i, 0)),
                   pl.BlockSpec((tm, D_MODEL), lambda i: (i, 0))],
        out_shape=[jax.ShapeDtypeStruct((rows, D_MODEL), F32),
                   jax.ShapeDtypeStruct((rows, D_MODEL), BF)],
        name="oproj",
        compiler_params=_params("parallel"),
    )(merged, w_o, x, g)


def _mlp_kernel(h2_ref, wu_ref, wd_ref, x_ref, o_ref):
    @pl.when(pl.program_id(1) == 0)
    def _():
        o_ref[...] = x_ref[...]

    a = jnp.maximum(_dot(h2_ref[...], wu_ref[...]), 0.0)
    o_ref[...] += _dot((a * a).astype(BF), wd_ref[...])


def _mlp(h2, w_up, w_down, l, x, tm, tf):
    rows = x.shape[0]
    return pl.pallas_call(
        _mlp_kernel,
        grid=(rows // tm, D_FF // tf),
        in_specs=[pl.BlockSpec((tm, D_MODEL), lambda i, k: (i, 0)),
                  _layer_spec(l, (D_MODEL, tf), lambda i, k: (0, k)),
                  _layer_spec(l, (tf, D_MODEL), lambda i, k: (k, 0)),
                  pl.BlockSpec((tm, D_MODEL), lambda i, k: (i, 0))],
        out_specs=pl.BlockSpec((tm, D_MODEL), lambda i, k: (i, 0)),
        out_shape=jax.ShapeDtypeStruct((rows, D_MODEL), F32),
        name="mlp",
        compiler_params=_params("parallel", "arbitrary"),
    )(h2, w_up, w_down, x)


def _swap_halves(a):
    return jnp.concatenate([a[..., HALF_ROPE:], a[..., :HALF_ROPE]], axis=-1)


def _twice(a):
    return jnp.concatenate([a, a], axis=-1)


def _widen_head(a):
    rope = a[..., QK_NOPE:]
    return jnp.concatenate([a[..., :QK_NOPE], rope, _swap_halves(rope)], axis=-1)


def _rope_gain_tables(gain, cos_t, sin_t):
    rope_gain = gain[QK_NOPE:]
    return cos_t * rope_gain[None], sin_t * _swap_halves(rope_gain)[None]


def _stacked_weights(w_in, w_uq, w_ukv, pool_w, w_branch_a, w_branch_b, w_branch_c, w_o, w_up,
                     w_down):
    depth = w_in.shape[0]
    wi = w_in.astype(BF)
    o_q = 3 * D_CONV
    o_kr = o_q + Q_LORA + KV_LORA
    o_pool = o_kr + QK_ROPE
    o_gate = o_pool + D_POOL
    w_kr = wi[:, :, o_kr:o_pool]
    w_ukv_h = w_ukv.astype(BF).reshape(depth, KV_LORA, N_HEADS, QK_NOPE + V_HEAD)
    return dict(
        w_in=wi,
        w_lat=jnp.concatenate([wi[:, :, o_q:o_kr], _twice(w_kr), _twice(_swap_halves(w_kr))],
                              axis=2),
        w_pool=wi[:, :, o_pool:o_gate],
        w_gate=wi[:, :, o_gate:],
        w_uq=_widen_head(w_uq.astype(BF).reshape(depth, Q_LORA, N_HEADS, QK_HEAD)).reshape(
            depth, Q_LORA, N_HEADS * HEAD_PAD),
        w_uk=w_ukv_h[..., :QK_NOPE].reshape(depth, KV_LORA, N_HEADS * QK_NOPE),
        w_uv=w_ukv_h[..., QK_NOPE:].reshape(depth, KV_LORA, N_HEADS * V_HEAD),
        pool_w=pool_w.astype(BF),
        wa=w_branch_a.astype(BF),
        wb=w_branch_b.astype(BF),
        wc=w_branch_c.astype(BF),
        w_o=w_o.astype(BF),
        w_up=w_up.astype(BF),
        w_down=w_down.astype(BF),
    )


def _layer_vectors(l, cos_t, sin_t, attn_norm, conv_w, q_lat_norm, kv_lat_norm, q_norm, k_norm,
                   pool_scale, mlp_norm):
    gq = q_norm[l] * (QK_HEAD ** -0.5 * LOG2_E)
    q_cos, q_sin = _rope_gain_tables(gq, cos_t, sin_t)
    k_cos, k_sin = _rope_gain_tables(k_norm[l], cos_t, sin_t)
    return dict(
        attn_norm=attn_norm[l][None],
        conv_w=conv_w[l],
        q_lat_norm=q_lat_norm[l][None],
        kv_lat_norm=kv_lat_norm[l][None],
        gq_nope=gq[None, :QK_NOPE],
        gk_nope=k_norm[l][None, :QK_NOPE],
        q_trig=jnp.concatenate([q_cos, q_sin], axis=1),
        k_rope=(_twice(k_cos), _twice(k_sin)),
        pool_scale=pool_scale[l][None],
        mlp_norm=mlp_norm[l][None],
    )


def _rope_tables(total):
    pos = jnp.arange(total, dtype=F32)
    inv = ROPE_THETA ** (-jnp.arange(0, QK_ROPE, 2, dtype=F32) / QK_ROPE)
    ang = pos[:, None] * inv[None, :]
    cos, sin = jnp.cos(ang), jnp.sin(ang)
    return _twice(cos), jnp.concatenate([-sin, sin], axis=1)


class _Tiles(NamedTuple):
    rows: int
    rows_big: int
    conv_cols: int
    merge_cols: int
    mlp_hidden: int
    attn_q: int
    attn_chunk: int
    attn_band: int
    attn_heads: int


def _tiles(seq):
    return _Tiles(rows=min(512, seq), rows_big=min(1024, seq), conv_cols=1024, merge_cols=512,
                  mlp_hidden=512, attn_q=512, attn_chunk=1024, attn_band=256, attn_heads=1)


def _layer(x, w, l, p, positions, prefix, *, batch, seq, finish):
    is_meta = prefix is None
    q_trig = p['q_trig'][positions]
    k_cos, k_sin = (t[positions] for t in p['k_rope'])
    t = _tiles(seq)
    tm, tm_big = t.rows, t.rows_big
    tps = seq // tm
    if is_meta:
        conv_hist = jnp.zeros((HIST, D_CONV), F32)
        pool_hist = jnp.zeros((HIST, D_POOL), F32)
    else:
        conv_hist, pool_hist, kp, vp = prefix

    h, qn, kvn, kr = _latent(x, p['attn_norm'], w['w_lat'], l, p['q_lat_norm'], p['kv_lat_norm'],
                             tm_big)
    ya, *conv_tail = _conv_branch(h, w['w_in'], l, p['conv_w'], conv_hist, tm_big, t.conv_cols,
                                  seq // tm_big, is_meta)
    yc, *pool_tail = _pool_branch(h, w['w_pool'], w['pool_w'], l, p['pool_scale'], pool_hist,
                                  tm_big, seq // tm_big, 0 if is_meta else N_META, is_meta)
    k, v = _kv_proj(kvn, w['w_uk'], w['w_uv'], l, kr, p['gk_nope'], k_cos, k_sin, tm, N_HEADS,
                    tps)
    out_prefix = (conv_tail[0], pool_tail[0], k, v) if is_meta else None
    if not finish:
        return None, out_prefix
    q = _q_proj(qn, w['w_uq'], l, p['gq_nope'], q_trig, tm, N_HEADS, tps)
    if is_meta:
        yb = _attention_meta(q, k, v)
    else:
        pad = ((0, 128 - N_META), (0, 0))
        per_batch = lambda a: a.reshape(batch, seq, a.shape[-1])
        yb = _attention(per_batch(q), per_batch(k), per_batch(v), jnp.pad(kp, pad),
                        jnp.pad(vp, pad), batch, seq, t.attn_q, t.attn_chunk, t.attn_band,
                        t.attn_heads).reshape(batch * seq, -1)
    merged = _merge(h, ya, yb, yc, w['w_gate'], w['wa'], w['wb'], w['wc'], l, tm_big, t.merge_cols)
    x_mid, h2 = _oproj(merged, w['w_o'], l, x, p['mlp_norm'], tm)
    return _mlp(h2, w['w_up'], w['w_down'], l, x_mid, tm_big, t.mlp_hidden), out_prefix


def kernel(x, meta_tokens, attn_norm, w_in, conv_w, q_lat_norm, kv_lat_norm, w_uq, w_ukv, q_norm, k_norm, pool_w, pool_scale, w_branch_a, w_branch_b, w_branch_c, w_o, mlp_norm, w_up, w_down):
    batch, seq, d = x.shape
    depth = w_in.shape[0]
    cos_t, sin_t = _rope_tables(N_META + seq)
    xm = meta_tokens.astype(F32)
    xr = x.reshape(batch * seq, d)
    w = _stacked_weights(w_in, w_uq, w_ukv, pool_w, w_branch_a, w_branch_b, w_branch_c, w_o, w_up,
                         w_down)
    for l in range(depth):
        p = _layer_vectors(l, cos_t, sin_t, attn_norm, conv_w, q_lat_norm, kv_lat_norm, q_norm,
                           k_norm, pool_scale, mlp_norm)
        last = l == depth - 1
        xm, prefix = _layer(xm, w, l, p, slice(0, N_META), None, batch=1, seq=N_META,
                            finish=not last)
        xr, _ = _layer(xr, w, l, p, slice(N_META, None), prefix, batch=batch, seq=seq, finish=True)
    return xr.reshape(batch, seq, d)
```

```python
import functools
from typing import NamedTuple

import jax
import jax.numpy as jnp
from jax import lax
from jax.experimental import pallas as pl
from jax.experimental.pallas import tpu as pltpu

D_MODEL = 2048
N_META = 16
EPS = 1e-6
D_CONV = 1024
N_HEADS = 16
QK_NOPE = 128
QK_ROPE = 64
QK_HEAD = QK_NOPE + QK_ROPE
HALF_ROPE = QK_ROPE // 2
V_HEAD = 128
Q_LORA = 512
KV_LORA = 512
ROPE_THETA = 10000.0
D_POOL = 1024
POOL_WINDOWS = (2, 4, 8, 16)
POOL_GROUP = D_POOL // len(POOL_WINDOWS)
D_FF = 4 * D_MODEL
HIST = 16
HEAD_PAD = 256
HEAD_GROUP = 4
D_LAT = Q_LORA + KV_LORA + 256
V7X_VMEM_BYTES = 64 * 2**20
VMEM_LIMIT = V7X_VMEM_BYTES - 8 * 2**20
V7X_VREG_F32 = 8 * 128
ROW_BLOCK_ELEMS = 32 * V7X_VREG_F32
NEG_BIG = -1e30
LOG2_E = 1.4426950408889634

BF = jnp.bfloat16
F32 = jnp.float32


def _dot(a, b):
    return jnp.dot(a, b, preferred_element_type=F32)


def _dot_nt(a, b):
    return lax.dot_general(a, b, (((1,), (1,)), ((), ())), preferred_element_type=F32)


def _layer_spec(l, shape, index_map):
    return pl.BlockSpec((None,) + shape, lambda *g: (l,) + index_map(*g))


def _params(*sem):
    return pltpu.CompilerParams(dimension_semantics=sem, vmem_limit_bytes=VMEM_LIMIT)


def _rms_norm(x, g):
    ms = jnp.mean(x * x, axis=-1, keepdims=True)
    return x * lax.rsqrt(ms + EPS) * g


def _conv_branch_kernel(h_ref, wu_ref, wb_ref, wc_ref, cw_ref, hist_ref, ya_ref, *rest,
                        tm, tiles_per_seq):
    carry_ref = rest[-1]
    i, j = pl.program_id(0), pl.program_id(1)
    h = h_ref[...]
    cu = _dot(h, wc_ref[...]) * _dot(h, wu_ref[...])

    @pl.when(i % tiles_per_seq == 0)
    def _():
        carry_ref[j] = hist_ref[...]

    ext = jnp.concatenate([carry_ref[j], cu], axis=0)
    tail = cu[tm - HIST:]
    carry_ref[j] = tail
    if len(rest) == 2:
        rest[0][...] = tail
    cw = cw_ref[...]
    y = (cw[2:3] * cu + cw[1:2] * ext[HIST - 1:HIST - 1 + tm] + cw[0:1] * ext[HIST - 2:HIST - 2 + tm])
    ya_ref[...] = (_dot(h, wb_ref[...]) * y).astype(ya_ref.dtype)


def _conv_branch(h, w_in, l, conv_w, hist, tm, tc, tiles_per_seq, emit_hist):
    rows = h.shape[0]
    ncb = D_CONV // tc
    assert not emit_hist or rows == tm
    kern = functools.partial(_conv_branch_kernel, tm=tm, tiles_per_seq=tiles_per_seq)
    n_out = 2 if emit_hist else 1
    return pl.pallas_call(
        kern,
        grid=(rows // tm, ncb),
        in_specs=[pl.BlockSpec((tm, D_MODEL), lambda i, j: (i, 0)),
                  _layer_spec(l, (D_MODEL, tc), lambda i, j: (0, j)),
                  _layer_spec(l, (D_MODEL, tc), lambda i, j: (0, j + ncb)),
                  _layer_spec(l, (D_MODEL, tc), lambda i, j: (0, j + 2 * ncb)),
                  pl.BlockSpec((3, tc), lambda i, j: (0, j)),
                  pl.BlockSpec((HIST, tc), lambda i, j: (0, j))],
        out_specs=[pl.BlockSpec((tm, tc), lambda i, j: (i, j)),
                   pl.BlockSpec((HIST, tc), lambda i, j: (0, j))][:n_out],
        out_shape=[jax.ShapeDtypeStruct((rows, D_CONV), BF),
                   jax.ShapeDtypeStruct((HIST, D_CONV), F32)][:n_out],
        scratch_shapes=[pltpu.VMEM((ncb, HIST, tc), F32)],
        name="conv_branch",
        compiler_params=_params("arbitrary", "arbitrary"),
    )(h, w_in, w_in, w_in, conv_w, hist)


def _latent_kernel(x_ref, g_ref, w_ref, gq_ref, gkv_ref, h_ref, qn_ref, kvn_ref, kr_ref):
    h = _rms_norm(x_ref[...], g_ref[...]).astype(h_ref.dtype)
    h_ref[...] = h
    acc = _dot(h, w_ref[...])
    qn_ref[...] = _rms_norm(acc[:, :Q_LORA], gq_ref[...]).astype(qn_ref.dtype)
    kvn_ref[...] = _rms_norm(acc[:, Q_LORA:Q_LORA + KV_LORA], gkv_ref[...]).astype(kvn_ref.dtype)
    kr_ref[...] = acc[:, Q_LORA + KV_LORA:]


def _latent(x, g, w_lat, l, gq, gkv, tm):
    rows = x.shape[0]
    row = lambda width: pl.BlockSpec((tm, width), lambda i: (i, 0))
    vec = lambda width: pl.BlockSpec((1, width), lambda i: (0, 0))
    return pl.pallas_call(
        _latent_kernel,
        grid=(rows // tm,),
        in_specs=[row(D_MODEL), vec(D_MODEL), _layer_spec(l, (D_MODEL, D_LAT), lambda i: (0, 0)),
                  vec(Q_LORA), vec(KV_LORA)],
        out_specs=[row(D_MODEL), row(Q_LORA), row(KV_LORA), row(256)],
        out_shape=[jax.ShapeDtypeStruct((rows, D_MODEL), BF),
                   jax.ShapeDtypeStruct((rows, Q_LORA), BF),
                   jax.ShapeDtypeStruct((rows, KV_LORA), BF),
                   jax.ShapeDtypeStruct((rows, 256), F32)],
        name="latent",
        compiler_params=_params("parallel"),
    )(x, g, w_lat, gq, gkv)


def _row_sumsq(a, n_real):
    keep = lax.broadcasted_iota(jnp.int32, (a.shape[1], 128), 0) < n_real
    return _dot((a * a).astype(BF), jnp.where(keep, 1.0, 0.0).astype(BF))


def _q_kernel(qn_ref, w_ref, g_ref, trig_ref, q_ref, acc_ref, ss_ref, *, heads, rb):
    qn = qn_ref[...]
    g = g_ref[...]
    for g0 in range(0, heads, HEAD_GROUP):
        group = range(g0, g0 + HEAD_GROUP)
        cols = slice(g0 * HEAD_PAD, (g0 + HEAD_GROUP) * HEAD_PAD)
        acc_ref[:, cols] = _dot(qn, w_ref[:, cols])
        for hd in group:
            ss_ref[:, hd * 128:(hd + 1) * 128] = _row_sumsq(
                acc_ref[:, hd * HEAD_PAD:(hd + 1) * HEAD_PAD], QK_HEAD)
        for r in range(0, acc_ref.shape[0], rb):
            rs = slice(r, r + rb)
            trig = trig_ref[rs, :]
            for hd in group:
                c0 = hd * HEAD_PAD
                inv = lax.rsqrt(ss_ref[rs, hd * 128:(hd + 1) * 128] * (1.0 / QK_HEAD) + EPS)
                q_ref[rs, c0:c0 + QK_NOPE] = (
                    acc_ref[rs, c0:c0 + QK_NOPE] * (inv * g)).astype(q_ref.dtype)
                q_ref[rs, c0 + QK_NOPE:c0 + HEAD_PAD] = (
                    acc_ref[rs, c0 + QK_NOPE:c0 + HEAD_PAD] * (inv * trig)).astype(q_ref.dtype)


def _q_proj(qn, w_uq, l, g_nope, trig, tm, heads, tiles_per_seq):
    rows = qn.shape[0]
    kern = functools.partial(_q_kernel, heads=heads, rb=min(64, tm))
    return pl.pallas_call(
        kern,
        grid=(rows // tm, N_HEADS // heads),
        in_specs=[pl.BlockSpec((tm, Q_LORA), lambda i, j: (i, 0)),
                  _layer_spec(l, (Q_LORA, heads * HEAD_PAD), lambda i, j: (0, j)),
                  pl.BlockSpec((1, QK_NOPE), lambda i, j: (0, 0)),
                  pl.BlockSpec((tm, 128), lambda i, j: (i % tiles_per_seq, 0))],
        out_specs=pl.BlockSpec((tm, heads * HEAD_PAD), lambda i, j: (i, j)),
        out_shape=jax.ShapeDtypeStruct((rows, N_HEADS * HEAD_PAD), BF),
        scratch_shapes=[pltpu.VMEM((tm, heads * HEAD_PAD), F32),
                        pltpu.VMEM((tm, heads * 128), F32)],
        name="q_proj",
        compiler_params=_params("parallel", "parallel"),
    )(qn, w_uq, g_nope, trig)


def _kv_kernel(kvn_ref, wk_ref, wv_ref, kr_ref, gn_ref, cos_ref, sin_ref, k_ref, v_ref,
               acc_ref, ss_ref, *, heads, rb):
    kvn = kvn_ref[...]
    blk = (lax.broadcasted_iota(jnp.int32, (256, 256), 0) // 128
           == lax.broadcasted_iota(jnp.int32, (256, 256), 1) // 128)
    pair_ones = jnp.where(blk, 1.0, 0.0).astype(BF)
    ss_ref[:, heads * 128:] = _row_sumsq(kr_ref[:, :128], QK_ROPE)
    gn = gn_ref[...]
    for g0 in range(0, heads, HEAD_GROUP):
        cols = slice(g0 * QK_NOPE, (g0 + HEAD_GROUP) * QK_NOPE)
        acc_ref[:, cols] = _dot(kvn, wk_ref[:, cols])
        v_ref[:, cols] = _dot(kvn, wv_ref[:, cols]).astype(v_ref.dtype)
        for hd in range(g0, g0 + HEAD_GROUP, 2):
            a = acc_ref[:, hd * QK_NOPE:(hd + 2) * QK_NOPE]
            ss_ref[:, hd * 128:(hd + 2) * 128] = _dot((a * a).astype(BF), pair_ones)
        for r in range(0, acc_ref.shape[0], rb):
            rs = slice(r, r + rb)
            kr_rot = kr_ref[rs, :128] * cos_ref[rs, :] + kr_ref[rs, 128:] * sin_ref[rs, :]
            kr_ss = ss_ref[rs, heads * 128:]
            for hd in range(g0, g0 + HEAD_GROUP):
                ss = ss_ref[rs, hd * 128:(hd + 1) * 128] + kr_ss
                inv = lax.rsqrt(ss * (1.0 / QK_HEAD) + EPS)
                k_ref[rs, hd * HEAD_PAD:hd * HEAD_PAD + QK_NOPE] = (
                    acc_ref[rs, hd * QK_NOPE:(hd + 1) * QK_NOPE] * (inv * gn)).astype(k_ref.dtype)
                k_ref[rs, hd * HEAD_PAD + QK_NOPE:(hd + 1) * HEAD_PAD] = (kr_rot * inv).astype(
                    k_ref.dtype)


def _kv_proj(kvn, w_uk, w_uv, l, kr, gn, cos_t, sin_t, tm, heads, tiles_per_seq):
    rows = kvn.shape[0]
    kern = functools.partial(_kv_kernel, heads=heads, rb=min(64, tm))
    return pl.pallas_call(
        kern,
        grid=(rows // tm, N_HEADS // heads),
        in_specs=[pl.BlockSpec((tm, KV_LORA), lambda i, j: (i, 0)),
                  _layer_spec(l, (KV_LORA, heads * QK_NOPE), lambda i, j: (0, j)),
                  _layer_spec(l, (KV_LORA, heads * V_HEAD), lambda i, j: (0, j)),
                  pl.BlockSpec((tm, 256), lambda i, j: (i, 0)),
                  pl.BlockSpec((1, 128), lambda i, j: (0, 0)),
                  pl.BlockSpec((tm, 128), lambda i, j: (i % tiles_per_seq, 0)),
                  pl.BlockSpec((tm, 128), lambda i, j: (i % tiles_per_seq, 0))],
        out_specs=[pl.BlockSpec((tm, heads * HEAD_PAD), lambda i, j: (i, j)),
                   pl.BlockSpec((tm, heads * V_HEAD), lambda i, j: (i, j))],
        out_shape=[jax.ShapeDtypeStruct((rows, N_HEADS * HEAD_PAD), BF),
                   jax.ShapeDtypeStruct((rows, N_HEADS * V_HEAD), BF)],
        scratch_shapes=[pltpu.VMEM((tm, heads * QK_NOPE), F32),
                        pltpu.VMEM((tm, (heads + 1) * 128), F32)],
        name="kv_proj",
        compiler_params=_params("parallel", "parallel"),
    )(kvn, w_uk, w_uv, kr, gn, cos_t, sin_t)


def _attn_kernel(q_ref, k_ref, v_ref, kp_ref, vp_ref, o_ref, head_bufs, *, tq, tc, td, rb,
                 n_prefix):
    for hd, bufs in enumerate(head_bufs):
        for src, dst in ((v_ref, bufs[-2]), (vp_ref, bufs[-1])):
            dst[:, :V_HEAD] = src[:, hd * V_HEAD:(hd + 1) * V_HEAD]
            dst[:, V_HEAD:] = jnp.ones((dst.shape[0], dst.shape[1] - V_HEAD), dst.dtype)
    for q_tile in range(q_ref.shape[0] // tq):
        _attn_tile(q_ref, k_ref, kp_ref, o_ref, head_bufs, q_tile=q_tile, tq=tq, tc=tc, td=td,
                   rb=rb, n_prefix=n_prefix)


def _attn_tile(q_ref, k_ref, kp_ref, o_ref, head_bufs, *, q_tile, tq, tc, td, rb, n_prefix):
    d0 = q_tile * tq
    items = [None] + [(k0, min(tc, d0 - k0)) for k0 in range(0, d0, tc)]
    bands = range(tq // td)
    qcols = lambda hd: slice(hd * HEAD_PAD, (hd + 1) * HEAD_PAD)
    vcols = lambda hd: slice(hd * V_HEAD, (hd + 1) * V_HEAD)
    tile = slice(d0, d0 + tq)

    def scores(hd, item, s_ref):
        if item is not None:
            k0, width = item
            s_ref[:, :width] = _dot_nt(q_ref[tile, qcols(hd)], k_ref[k0:k0 + width, qcols(hd)])
            return
        s_ref[:, tq:tq + 128] = _dot_nt(q_ref[tile, qcols(hd)], kp_ref[:, qcols(hd)])
        for g in bands:
            seen = (g + 1) * td
            s_ref[g * td:seen, :seen] = _dot_nt(q_ref[d0 + g * td:d0 + seen, qcols(hd)],
                                                k_ref[d0:d0 + seen, qcols(hd)])

    def softmax_rows(rs, pieces, p_ref, a_ref, m_ref):
        s = jnp.concatenate([x for _, x in pieces], axis=1)
        m_old = m_ref[rs, :]
        m_new = jnp.maximum(m_old, jnp.max(s, axis=-1, keepdims=True))
        alpha = jnp.exp2(m_old - m_new)
        p = jnp.exp2(s - jnp.concatenate([m_new] * (s.shape[1] // 128), axis=1))
        m_ref[rs, :] = m_new
        a_ref[rs, :] = alpha
        at = 0
        for cols, x in pieces:
            p_ref[rs, cols] = p[:, at:at + x.shape[1]].astype(p_ref.dtype)
            at += x.shape[1]

    def softmax(item, s_ref, p_ref, a_ref, m_ref):
        if item is not None:
            width = item[1]
            rows_per_block = min(rb, ROW_BLOCK_ELEMS // width)
            for r in range(0, tq, rows_per_block):
                rs = slice(r, r + rows_per_block)
                softmax_rows(rs, [(slice(0, width), s_ref[rs, :width])], p_ref, a_ref, m_ref)
            return
        for r in range(0, tq, rb):
            rs = slice(r, r + rb)
            g = r // td
            open_cols, last = slice(0, g * td), slice(g * td, (g + 1) * td)
            row = lax.broadcasted_iota(jnp.int32, (rb, td), 0) + (r - g * td)
            col = lax.broadcasted_iota(jnp.int32, (rb, td), 1)
            colp = lax.broadcasted_iota(jnp.int32, (rb, 128), 1)
            pieces = [(open_cols, s_ref[rs, open_cols])] if g else []
            pieces += [(last, jnp.where(col <= row, s_ref[rs, last], NEG_BIG)),
                       (slice(tq, tq + 128),
                        jnp.where(colp < n_prefix, s_ref[rs, tq:tq + 128], NEG_BIG))]
            softmax_rows(rs, pieces, p_ref, a_ref, m_ref)

    def accumulate(item, p_ref, a_ref, acc_ref, vx_ref, vpx_ref):
        def rescaled(rows):
            a = a_ref[rows, :]
            return jnp.concatenate([a, a], axis=1) * acc_ref[rows, :]

        if item is not None:
            k0, width = item
            acc_ref[...] = rescaled(slice(None)) + _dot(p_ref[:, :width],
                                                        vx_ref[k0:k0 + width, :])
            return
        for g in bands:
            rows, seen = slice(g * td, (g + 1) * td), (g + 1) * td
            acc_ref[rows, :] = (rescaled(rows) + _dot(p_ref[rows, :seen], vx_ref[d0:d0 + seen, :])
                                + _dot(p_ref[rows, tq:tq + 128], vpx_ref[...]))

    for _, _, _, m_ref, acc_ref, _, _ in head_bufs:
        m_ref[...] = jnp.full(m_ref.shape, NEG_BIG, F32)
        acc_ref[...] = jnp.zeros(acc_ref.shape, F32)
    n = len(items)
    for t in range(n + 2):
        for hd, (s_refs, p_refs, a_refs, m_ref, acc_ref, vx_ref, vpx_ref) in enumerate(head_bufs):
            depth = len(s_refs)
            if t < n:
                scores(hd, items[t], s_refs[t % depth])
            if 1 <= t <= n:
                u = (t - 1) % depth
                softmax(items[t - 1], s_refs[u], p_refs[u], a_refs[u], m_ref)
            if t >= 2:
                u = (t - 2) % depth
                accumulate(items[t - 2], p_refs[u], a_refs[u], acc_ref, vx_ref, vpx_ref)
    for hd, bufs in enumerate(head_bufs):
        acc_ref = bufs[4]
        o_ref[tile, vcols(hd)] = (acc_ref[:, :V_HEAD] / acc_ref[:, V_HEAD:]).astype(o_ref.dtype)


def _attention(q, k, v, kp, vp, batch, seq, tq, tc, td, heads):
    depth = 3
    stat = pltpu.VMEM((tq, 128), F32)
    width = max(tc, tq + 128)
    head_bufs = [([pltpu.VMEM((tq, width), F32)] * depth, [pltpu.VMEM((tq, width), BF)] * depth,
                  [stat] * depth, stat, pltpu.VMEM((tq, 2 * V_HEAD), F32),
                  pltpu.VMEM((seq, 2 * V_HEAD), BF), pltpu.VMEM((128, 2 * V_HEAD), BF))] * heads
    kern = functools.partial(_attn_kernel, tq=tq, tc=tc, td=td, rb=64, n_prefix=N_META)
    return pl.pallas_call(
        kern,
        grid=(batch, N_HEADS // heads),
        in_specs=[pl.BlockSpec((None, seq, heads * HEAD_PAD), lambda b, h: (b, 0, h)),
                  pl.BlockSpec((None, seq, heads * HEAD_PAD), lambda b, h: (b, 0, h)),
                  pl.BlockSpec((None, seq, heads * V_HEAD), lambda b, h: (b, 0, h)),
                  pl.BlockSpec((128, heads * HEAD_PAD), lambda b, h: (0, h)),
                  pl.BlockSpec((128, heads * V_HEAD), lambda b, h: (0, h))],
        out_specs=pl.BlockSpec((None, seq, heads * V_HEAD), lambda b, h: (b, 0, h)),
        out_shape=jax.ShapeDtypeStruct((batch, seq, N_HEADS * V_HEAD), BF),
        scratch_shapes=[head_bufs],
        name="attention",
        compiler_params=_params("parallel", "parallel"),
    )(q, k, v, kp, vp)


def _attn_meta_kernel(q_ref, k_ref, v_ref, o_ref):
    s = _dot_nt(q_ref[...], k_ref[...])
    row = lax.broadcasted_iota(jnp.int32, s.shape, 0)
    col = lax.broadcasted_iota(jnp.int32, s.shape, 1)
    s = jnp.where(row >= col, s, NEG_BIG)
    p = jnp.exp2(s - jnp.max(s, axis=-1, keepdims=True))
    l = jnp.sum(p, axis=-1, keepdims=True)
    o_ref[...] = (_dot(p.astype(BF), v_ref[...]) / l).astype(o_ref.dtype)


def _attention_meta(q, k, v):
    return pl.pallas_call(
        _attn_meta_kernel,
        grid=(N_HEADS,),
        in_specs=[pl.BlockSpec((N_META, HEAD_PAD), lambda h: (0, h)),
                  pl.BlockSpec((N_META, HEAD_PAD), lambda h: (0, h)),
                  pl.BlockSpec((N_META, V_HEAD), lambda h: (0, h))],
        out_specs=pl.BlockSpec((N_META, V_HEAD), lambda h: (0, h)),
        out_shape=jax.ShapeDtypeStruct((N_META, N_HEADS * V_HEAD), BF),
        name="attention_meta",
        compiler_params=_params("parallel"),
    )(q, k, v)


def _pool_branch_kernel(h_ref, w_ref, pw_ref, ps_ref, hist_ref, yc_ref, *rest,
                        tm, tiles_per_seq, pos_offset):
    carry_ref = rest[-1]
    i = pl.program_id(0)
    pin = _dot(h_ref[...], w_ref[...])

    @pl.when(i % tiles_per_seq == 0)
    def _():
        carry_ref[...] = hist_ref[...]

    ext = jnp.concatenate([carry_ref[...], pin], axis=0)
    tail = pin[tm - HIST:]
    carry_ref[...] = tail
    if len(rest) == 2:
        rest[0][...] = tail
    n_ext = tm + HIST
    seen = (lax.broadcasted_iota(jnp.int32, (tm, 1), 0)
            + ((i % tiles_per_seq) * tm + pos_offset + 1)).astype(F32)
    for g, w in enumerate(POOL_WINDOWS):
        xg = ext[:, g * POOL_GROUP:(g + 1) * POOL_GROUP]
        s, span = xg, 1
        while span < w:
            s = s[span:] + s[:s.shape[0] - span]
            span *= 2
        s = s[n_ext - (w - 1) - tm:]
        pooled = s / jnp.minimum(seen, float(w)) - xg[HIST:]
        mixed = _dot(pooled.astype(BF), pw_ref[g])
        yc_ref[:, g * POOL_GROUP:(g + 1) * POOL_GROUP] = (
            mixed * ps_ref[:, g * POOL_GROUP:(g + 1) * POOL_GROUP]).astype(yc_ref.dtype)


def _pool_branch(h, w_pool, pool_w, l, pool_scale, hist, tm, tiles_per_seq, pos_offset,
                 emit_hist):
    rows = h.shape[0]
    assert not emit_hist or rows == tm
    kern = functools.partial(_pool_branch_kernel, tm=tm, tiles_per_seq=tiles_per_seq,
                             pos_offset=pos_offset)
    ng = len(POOL_WINDOWS)
    n_out = 2 if emit_hist else 1
    return pl.pallas_call(
        kern,
        grid=(rows // tm,),
        in_specs=[pl.BlockSpec((tm, D_MODEL), lambda i: (i, 0)),
                  _layer_spec(l, (D_MODEL, D_POOL), lambda i: (0, 0)),
                  _layer_spec(l, (ng, POOL_GROUP, POOL_GROUP), lambda i: (0, 0, 0)),
                  pl.BlockSpec((1, D_POOL), lambda i: (0, 0)),
                  pl.BlockSpec((HIST, D_POOL), lambda i: (0, 0))],
        out_specs=[pl.BlockSpec((tm, D_POOL), lambda i: (i, 0)),
                   pl.BlockSpec((HIST, D_POOL), lambda i: (0, 0))][:n_out],
        out_shape=[jax.ShapeDtypeStruct((rows, D_POOL), BF),
                   jax.ShapeDtypeStruct((HIST, D_POOL), F32)][:n_out],
        scratch_shapes=[pltpu.VMEM((HIST, D_POOL), F32)],
        name="pool_branch",
        compiler_params=_params("arbitrary"),
    )(h, w_pool, pool_w, pool_scale, hist)


def _merge_kernel(h_ref, ya_ref, yb_ref, yc_ref, wg0_ref, wg1_ref, wg2_ref, wa_ref, wb_ref, wc_ref,
                  o_ref):
    h = h_ref[...]
    m = jax.nn.sigmoid(_dot(h, wg0_ref[...])) * _dot(ya_ref[...], wa_ref[...])
    m += jax.nn.sigmoid(_dot(h, wg1_ref[...])) * _dot(yb_ref[...], wb_ref[...])
    m += jax.nn.sigmoid(_dot(h, wg2_ref[...])) * _dot(yc_ref[...], wc_ref[...])
    o_ref[...] = m.astype(o_ref.dtype)


def _merge(h, ya, yb, yc, w_gate, wa, wb, wc, l, tm, tn):
    rows = h.shape[0]
    ncb = D_MODEL // tn
    row = lambda width: pl.BlockSpec((tm, width), lambda i, j: (i, 0))
    col = lambda depth, off: _layer_spec(l, (depth, tn), lambda i, j: (0, j + off))
    return pl.pallas_call(
        _merge_kernel,
        grid=(rows // tm, ncb),
        in_specs=[row(D_MODEL), row(D_CONV), row(N_HEADS * V_HEAD), row(D_POOL),
                  col(D_MODEL, 0), col(D_MODEL, ncb), col(D_MODEL, 2 * ncb),
                  col(D_CONV, 0), col(N_HEADS * V_HEAD, 0), col(D_POOL, 0)],
        out_specs=pl.BlockSpec((tm, tn), lambda i, j: (i, j)),
        out_shape=jax.ShapeDtypeStruct((rows, D_MODEL), BF),
        name="merge",
        compiler_params=_params("parallel", "arbitrary"),
    )(h, ya, yb, yc, w_gate, w_gate, w_gate, wa, wb, wc)


def _oproj_kernel(m_ref, w_ref, x_ref, g_ref, xo_ref, h2_ref):
    x = x_ref[...] + _dot(m_ref[...], w_ref[...])
    xo_ref[...] = x
    h2_ref[...] = _rms_norm(x, g_ref[...]).astype(h2_ref.dtype)


def _oproj(merged, w_o, l, x, g, tm):
    rows = x.shape[0]
    return pl.pallas_call(
        _oproj_kernel,
        grid=(rows // tm,),
        in_specs=[pl.BlockSpec((tm, D_MODEL), lambda i: (i, 0)),
                  _layer_spec(l, (D_MODEL, D_MODEL), lambda i: (0, 0)),
                  pl.BlockSpec((tm, D_MODEL), lambda i: (i, 0)),
                  pl.BlockSpec((1, D_MODEL), lambda i: (0, 0))],
        out_specs=[pl.BlockSpec((tm, D_MODEL), lambda i: (i, 0)),
                   pl.BlockSpec((tm, D_MODEL), lambda i: (i, 0))],
        out_shape=[jax.ShapeDtypeStruct((rows, D_MODEL), F32),
                   jax.ShapeDtypeStruct((rows, D_MODEL), BF)],
        name="oproj",
        compiler_params=_params("parallel"),
    )(merged, w_o, x, g)


def _mlp_kernel(h2_ref, wu_ref, wd_ref, x_ref, o_ref):
    @pl.when(pl.program_id(1) == 0)
    def _():
        o_ref[...] = x_ref[...]

    a = jnp.maximum(_dot(h2_ref[...], wu_ref[...]), 0.0)
    o_ref[...] += _dot((a * a).astype(BF), wd_ref[...])


def _mlp(h2, w_up, w_down, l, x, tm, tf):
    rows = x.shape[0]
    return pl.pallas_call(
        _mlp_kernel,
        grid=(rows // tm, D_FF // tf),
        in_specs=[pl.BlockSpec((tm, D_MODEL), lambda i, k: (i, 0)),
                  _layer_spec(l, (D_MODEL, tf), lambda i, k: (0, k)),
                  _layer_spec(l, (tf, D_MODEL), lambda i, k: (k, 0)),
                  pl.BlockSpec((tm, D_MODEL), lambda i, k: (i, 0))],
        out_specs=pl.BlockSpec((tm, D_MODEL), lambda i, k: (i, 0)),
        out_shape=jax.ShapeDtypeStruct((rows, D_MODEL), F32),
        name="mlp",
        compiler_params=_params("parallel", "arbitrary"),
    )(h2, w_up, w_down, x)


def _swap_halves(a):
    return jnp.concatenate([a[..., HALF_ROPE:], a[..., :HALF_ROPE]], axis=-1)


def _twice(a):
    return jnp.concatenate([a, a], axis=-1)


def _widen_head(a):
    rope = a[..., QK_NOPE:]
    return jnp.concatenate([a[..., :QK_NOPE], rope, _swap_halves(rope)], axis=-1)


def _rope_gain_tables(gain, cos_t, sin_t):
    rope_gain = gain[QK_NOPE:]
    return cos_t * rope_gain[None], sin_t * _swap_halves(rope_gain)[None]


def _stacked_weights(w_in, w_uq, w_ukv, pool_w, w_branch_a, w_branch_b, w_branch_c, w_o, w_up,
                     w_down):
    depth = w_in.shape[0]
    wi = w_in.astype(BF)
    o_q = 3 * D_CONV
    o_kr = o_q + Q_LORA + KV_LORA
    o_pool = o_kr + QK_ROPE
    o_gate = o_pool + D_POOL
    w_kr = wi[:, :, o_kr:o_pool]
    w_ukv_h = w_ukv.astype(BF).reshape(depth, KV_LORA, N_HEADS, QK_NOPE + V_HEAD)
    return dict(
        w_in=wi,
        w_lat=jnp.concatenate([wi[:, :, o_q:o_kr], _twice(w_kr), _twice(_swap_halves(w_kr))],
                              axis=2),
        w_pool=wi[:, :, o_pool:o_gate],
        w_gate=wi[:, :, o_gate:],
        w_uq=_widen_head(w_uq.astype(BF).reshape(depth, Q_LORA, N_HEADS, QK_HEAD)).reshape(
            depth, Q_LORA, N_HEADS * HEAD_PAD),
        w_uk=w_ukv_h[..., :QK_NOPE].reshape(depth, KV_LORA, N_HEADS * QK_NOPE),
        w_uv=w_ukv_h[..., QK_NOPE:].reshape(depth, KV_LORA, N_HEADS * V_HEAD),
        pool_w=pool_w.astype(BF),
        wa=w_branch_a.astype(BF),
        wb=w_branch_b.astype(BF),
        wc=w_branch_c.astype(BF),
        w_o=w_o.astype(BF),
        w_up=w_up.astype(BF),
        w_down=w_down.astype(BF),
    )


def _layer_vectors(l, cos_t, sin_t, attn_norm, conv_w, q_lat_norm, kv_lat_norm, q_norm, k_norm,
                   pool_scale, mlp_norm):
    gq = q_norm[l] * (QK_HEAD ** -0.5 * LOG2_E)
    q_cos, q_sin = _rope_gain_tables(gq, cos_t, sin_t)
    k_cos, k_sin = _rope_gain_tables(k_norm[l], cos_t, sin_t)
    return dict(
        attn_norm=attn_norm[l][None],
        conv_w=conv_w[l],
        q_lat_norm=q_lat_norm[l][None],
        kv_lat_norm=kv_lat_norm[l][None],
        gq_nope=gq[None, :QK_NOPE],
        gk_nope=k_norm[l][None, :QK_NOPE],
        q_trig=jnp.concatenate([q_cos, q_sin], axis=1),
        k_rope=(_twice(k_cos), _twice(k_sin)),
        pool_scale=pool_scale[l][None],
        mlp_norm=mlp_norm[l][None],
    )


def _rope_tables(total):
    pos = jnp.arange(total, dtype=F32)
    inv = ROPE_THETA ** (-jnp.arange(0, QK_ROPE, 2, dtype=F32) / QK_ROPE)
    ang = pos[:, None] * inv[None, :]
    cos, sin = jnp.cos(ang), jnp.sin(ang)
    return _twice(cos), jnp.concatenate([-sin, sin], axis=1)


class _Tiles(NamedTuple):
    rows: int
    rows_big: int
    conv_cols: int
    merge_cols: int
    mlp_hidden: int
    attn_q: int
    attn_chunk: int
    attn_band: int
    attn_heads: int


def _tiles(seq):
    return _Tiles(rows=min(512, seq), rows_big=min(1024, seq), conv_cols=1024, merge_cols=512,
                  mlp_hidden=512, attn_q=512, attn_chunk=1024, attn_band=256, attn_heads=1)


def _layer(x, w, l, p, positions, prefix, *, batch, seq, finish):
    is_meta = prefix is None
    q_trig = p['q_trig'][positions]
    k_cos, k_sin = (t[positions] for t in p['k_rope'])
    t = _tiles(seq)
    tm, tm_big = t.rows, t.rows_big
    tps = seq // tm
    if is_meta:
        conv_hist = jnp.zeros((HIST, D_CONV), F32)
        pool_hist = jnp.zeros((HIST, D_POOL), F32)
    else:
        conv_hist, pool_hist, kp, vp = prefix

    h, qn, kvn, kr = _latent(x, p['attn_norm'], w['w_lat'], l, p['q_lat_norm'], p['kv_lat_norm'],
                             tm_big)
    ya, *conv_tail = _conv_branch(h, w['w_in'], l, p['conv_w'], conv_hist, tm_big, t.conv_cols,
                                  seq // tm_big, is_meta)
    yc, *pool_tail = _pool_branch(h, w['w_pool'], w['pool_w'], l, p['pool_scale'], pool_hist,
                                  tm_big, seq // tm_big, 0 if is_meta else N_META, is_meta)
    k, v = _kv_proj(kvn, w['w_uk'], w['w_uv'], l, kr, p['gk_nope'], k_cos, k_sin, tm, N_HEADS,
                    tps)
    out_prefix = (conv_tail[0], pool_tail[0], k, v) if is_meta else None
    if not finish:
        return None, out_prefix
    q = _q_proj(qn, w['w_uq'], l, p['gq_nope'], q_trig, tm, N_HEADS, tps)
    if is_meta:
        yb = _attention_meta(q, k, v)
    else:
        pad = ((0, 128 - N_META), (0, 0))
        per_batch = lambda a: a.reshape(batch, seq, a.shape[-1])
        yb = _attention(per_batch(q), per_batch(k), per_batch(v), jnp.pad(kp, pad),
                        jnp.pad(vp, pad), batch, seq, t.attn_q, t.attn_chunk, t.attn_band,
                        t.attn_heads).reshape(batch * seq, -1)
    merged = _merge(h, ya, yb, yc, w['w_gate'], w['wa'], w['wb'], w['wc'], l, tm_big, t.merge_cols)
    x_mid, h2 = _oproj(merged, w['w_o'], l, x, p['mlp_norm'], tm)
    return _mlp(h2, w['w_up'], w['w_down'], l, x_mid, tm_big, t.mlp_hidden), out_prefix


def kernel(x, meta_tokens, attn_norm, w_in, conv_w, q_lat_norm, kv_lat_norm, w_uq, w_ukv, q_norm, k_norm, pool_w, pool_scale, w_branch_a, w_branch_b, w_branch_c, w_o, mlp_norm, w_up, w_down):
    batch, seq, d = x.shape
    depth = w_in.shape[0]
    cos_t, sin_t = _rope_tables(N_META + seq)
    xm = meta_tokens.astype(F32)
    xr = x.reshape(batch * seq, d)
    w = _stacked_weights(w_in, w_uq, w_ukv, pool_w, w_branch_a, w_branch_b, w_branch_c, w_o, w_up,
                         w_down)
    for l in range(depth):
        p = _layer_vectors(l, cos_t, sin_t, attn_norm, conv_w, q_lat_norm, kv_lat_norm, q_norm,
                           k_norm, pool_scale, mlp_norm)
        last = l == depth - 1
        xm, prefix = _layer(xm, w, l, p, slice(0, N_META), None, batch=1, seq=N_META,
                            finish=not last)
        xr, _ = _layer(xr, w, l, p, slice(N_META, None), prefix, batch=batch, seq=seq, finish=True)
    return xr.reshape(batch, seq, d)
```

```python
import functools
from typing import NamedTuple

import jax
import jax.numpy as jnp
from jax import lax
from jax.experimental import pallas as pl
from jax.experimental.pallas import tpu as pltpu

D_MODEL = 2048
N_META = 16
EPS = 1e-6
D_CONV = 1024
N_HEADS = 16
QK_NOPE = 128
QK_ROPE = 64
QK_HEAD = QK_NOPE + QK_ROPE
HALF_ROPE = QK_ROPE // 2
V_HEAD = 128
Q_LORA = 512
KV_LORA = 512
ROPE_THETA = 10000.0
D_POOL = 1024
POOL_WINDOWS = (2, 4, 8, 16)
POOL_GROUP = D_POOL // len(POOL_WINDOWS)
D_FF = 4 * D_MODEL
HIST = 16
HEAD_PAD = 256
D_LAT = Q_LORA + KV_LORA + 256
V7X_VMEM_BYTES = 64 * 2**20
VMEM_LIMIT = V7X_VMEM_BYTES - 8 * 2**20
V7X_VREG_F32 = 8 * 128
ROW_BLOCK_ELEMS = 32 * V7X_VREG_F32
NEG_BIG = -1e30
LOG2_E = 1.4426950408889634

BF = jnp.bfloat16
F32 = jnp.float32


def _dot(a, b):
    return jnp.dot(a, b, preferred_element_type=F32)


def _dot_nt(a, b):
    return lax.dot_general(a, b, (((1,), (1,)), ((), ())), preferred_element_type=F32)


def _layer_spec(l, shape, index_map):
    return pl.BlockSpec((None,) + shape, lambda *g: (l,) + index_map(*g))


def _params(*sem):
    return pltpu.CompilerParams(dimension_semantics=sem, vmem_limit_bytes=VMEM_LIMIT)


def _rms_norm(x, g):
    ms = jnp.mean(x * x, axis=-1, keepdims=True)
    return x * lax.rsqrt(ms + EPS) * g


def _conv_branch_kernel(h_ref, wu_ref, wb_ref, wc_ref, cw_ref, hist_ref, ya_ref, *rest,
                        tm, tiles_per_seq):
    carry_ref = rest[-1]
    i, j = pl.program_id(0), pl.program_id(1)
    h = h_ref[...]
    cu = _dot(h, wc_ref[...]) * _dot(h, wu_ref[...])

    @pl.when(i % tiles_per_seq == 0)
    def _():
        carry_ref[j] = hist_ref[...]

    ext = jnp.concatenate([carry_ref[j], cu], axis=0)
    tail = cu[tm - HIST:]
    carry_ref[j] = tail
    if len(rest) == 2:
        rest[0][...] = tail
    cw = cw_ref[...]
    y = (cw[2:3] * cu + cw[1:2] * ext[HIST - 1:HIST - 1 + tm] + cw[0:1] * ext[HIST - 2:HIST - 2 + tm])
    ya_ref[...] = (_dot(h, wb_ref[...]) * y).astype(ya_ref.dtype)


def _conv_branch(h, w_in, l, conv_w, hist, tm, tc, tiles_per_seq, emit_hist):
    rows = h.shape[0]
    ncb = D_CONV // tc
    assert not emit_hist or rows == tm
    kern = functools.partial(_conv_branch_kernel, tm=tm, tiles_per_seq=tiles_per_seq)
    n_out = 2 if emit_hist else 1
    return pl.pallas_call(
        kern,
        grid=(rows // tm, ncb),
        in_specs=[pl.BlockSpec((tm, D_MODEL), lambda i, j: (i, 0)),
                  _layer_spec(l, (D_MODEL, tc), lambda i, j: (0, j)),
                  _layer_spec(l, (D_MODEL, tc), lambda i, j: (0, j + ncb)),
                  _layer_spec(l, (D_MODEL, tc), lambda i, j: (0, j + 2 * ncb)),
                  pl.BlockSpec((3, tc), lambda i, j: (0, j)),
                  pl.BlockSpec((HIST, tc), lambda i, j: (0, j))],
        out_specs=[pl.BlockSpec((tm, tc), lambda i, j: (i, j)),
                   pl.BlockSpec((HIST, tc), lambda i, j: (0, j))][:n_out],
        out_shape=[jax.ShapeDtypeStruct((rows, D_CONV), BF),
                   jax.ShapeDtypeStruct((HIST, D_CONV), F32)][:n_out],
        scratch_shapes=[pltpu.VMEM((ncb, HIST, tc), F32)],
        name="conv_branch",
        compiler_params=_params("arbitrary", "arbitrary"),
    )(h, w_in, w_in, w_in, conv_w, hist)


def _latent_kernel(x_ref, g_ref, w_ref, gq_ref, gkv_ref, h_ref, qn_ref, kvn_ref, kr_ref):
    h = _rms_norm(x_ref[...], g_ref[...]).astype(h_ref.dtype)
    h_ref[...] = h
    acc = _dot(h, w_ref[...])
    qn_ref[...] = _rms_norm(acc[:, :Q_LORA], gq_ref[...]).astype(qn_ref.dtype)
    kvn_ref[...] = _rms_norm(acc[:, Q_LORA:Q_LORA + KV_LORA], gkv_ref[...]).astype(kvn_ref.dtype)
    kr_ref[...] = acc[:, Q_LORA + KV_LORA:]


def _latent(x, g, w_lat, l, gq, gkv, tm):
    rows = x.shape[0]
    row = lambda width: pl.BlockSpec((tm, width), lambda i: (i, 0))
    vec = lambda width: pl.BlockSpec((1, width), lambda i: (0, 0))
    return pl.pallas_call(
        _latent_kernel,
        grid=(rows // tm,),
        in_specs=[row(D_MODEL), vec(D_MODEL), _layer_spec(l, (D_MODEL, D_LAT), lambda i: (0, 0)),
                  vec(Q_LORA), vec(KV_LORA)],
        out_specs=[row(D_MODEL), row(Q_LORA), row(KV_LORA), row(256)],
        out_shape=[jax.ShapeDtypeStruct((rows, D_MODEL), BF),
                   jax.ShapeDtypeStruct((rows, Q_LORA), BF),
                   jax.ShapeDtypeStruct((rows, KV_LORA), BF),
                   jax.ShapeDtypeStruct((rows, 256), F32)],
        name="latent",
        compiler_params=_params("parallel"),
    )(x, g, w_lat, gq, gkv)


def _row_sumsq(a, n_real):
    keep = lax.broadcasted_iota(jnp.int32, (a.shape[1], 128), 0) < n_real
    return _dot((a * a).astype(BF), jnp.where(keep, 1.0, 0.0).astype(BF))


def _q_kernel(qn_ref, w_ref, g_ref, trig_ref, q_ref, acc_ref, ss_ref, *, heads, rb):
    acc_ref[...] = _dot(qn_ref[...], w_ref[...])
    for hd in range(heads):
        ss_ref[:, hd * 128:(hd + 1) * 128] = _row_sumsq(
            acc_ref[:, hd * HEAD_PAD:(hd + 1) * HEAD_PAD], QK_HEAD)
    g = g_ref[...]
    for r in range(0, acc_ref.shape[0], rb):
        rs = slice(r, r + rb)
        trig = trig_ref[rs, :]
        for hd in range(heads):
            c0 = hd * HEAD_PAD
            inv = lax.rsqrt(ss_ref[rs, hd * 128:(hd + 1) * 128] * (1.0 / QK_HEAD) + EPS)
            q_ref[rs, c0:c0 + QK_NOPE] = (
                acc_ref[rs, c0:c0 + QK_NOPE] * (inv * g)).astype(q_ref.dtype)
            q_ref[rs, c0 + QK_NOPE:c0 + HEAD_PAD] = (
                acc_ref[rs, c0 + QK_NOPE:c0 + HEAD_PAD] * (inv * trig)).astype(q_ref.dtype)


def _q_proj(qn, w_uq, l, g_nope, trig, tm, heads, tiles_per_seq):
    rows = qn.shape[0]
    kern = functools.partial(_q_kernel, heads=heads, rb=min(64, tm))
    return pl.pallas_call(
        kern,
        grid=(rows // tm, N_HEADS // heads),
        in_specs=[pl.BlockSpec((tm, Q_LORA), lambda i, j: (i, 0)),
                  _layer_spec(l, (Q_LORA, heads * HEAD_PAD), lambda i, j: (0, j)),
                  pl.BlockSpec((1, QK_NOPE), lambda i, j: (0, 0)),
                  pl.BlockSpec((tm, 128), lambda i, j: (i % tiles_per_seq, 0))],
        out_specs=pl.BlockSpec((tm, heads * HEAD_PAD), lambda i, j: (i, j)),
        out_shape=jax.ShapeDtypeStruct((rows, N_HEADS * HEAD_PAD), BF),
        scratch_shapes=[pltpu.VMEM((tm, heads * HEAD_PAD), F32),
                        pltpu.VMEM((tm, heads * 128), F32)],
        name="q_proj",
        compiler_params=_params("parallel", "parallel"),
    )(qn, w_uq, g_nope, trig)


def _kv_kernel(kvn_ref, wk_ref, wv_ref, kr_ref, gn_ref, cos_ref, sin_ref, k_ref, v_ref,
               acc_ref, ss_ref, *, heads, rb):
    kvn = kvn_ref[...]
    acc_ref[...] = _dot(kvn, wk_ref[...])
    v_ref[...] = _dot(kvn, wv_ref[...]).astype(v_ref.dtype)
    blk = (lax.broadcasted_iota(jnp.int32, (256, 256), 0) // 128
           == lax.broadcasted_iota(jnp.int32, (256, 256), 1) // 128)
    pair_ones = jnp.where(blk, 1.0, 0.0).astype(BF)
    for hd in range(0, heads, 2):
        a = acc_ref[:, hd * QK_NOPE:(hd + 2) * QK_NOPE]
        ss_ref[:, hd * 128:(hd + 2) * 128] = _dot((a * a).astype(BF), pair_ones)
    ss_ref[:, heads * 128:] = _row_sumsq(kr_ref[:, :128], QK_ROPE)
    gn = gn_ref[...]
    for r in range(0, acc_ref.shape[0], rb):
        rs = slice(r, r + rb)
        kr_rot = kr_ref[rs, :128] * cos_ref[rs, :] + kr_ref[rs, 128:] * sin_ref[rs, :]
        kr_ss = ss_ref[rs, heads * 128:]
        for hd in range(heads):
            ss = ss_ref[rs, hd * 128:(hd + 1) * 128] + kr_ss
            inv = lax.rsqrt(ss * (1.0 / QK_HEAD) + EPS)
            k_ref[rs, hd * HEAD_PAD:hd * HEAD_PAD + QK_NOPE] = (
                acc_ref[rs, hd * QK_NOPE:(hd + 1) * QK_NOPE] * (inv * gn)).astype(k_ref.dtype)
            k_ref[rs, hd * HEAD_PAD + QK_NOPE:(hd + 1) * HEAD_PAD] = (kr_rot * inv).astype(
                k_ref.dtype)


def _kv_proj(kvn, w_uk, w_uv, l, kr, gn, cos_t, sin_t, tm, heads, tiles_per_seq):
    rows = kvn.shape[0]
    kern = functools.partial(_kv_kernel, heads=heads, rb=min(64, tm))
    return pl.pallas_call(
        kern,
        grid=(rows // tm, N_HEADS // heads),
        in_specs=[pl.BlockSpec((tm, KV_LORA), lambda i, j: (i, 0)),
                  _layer_spec(l, (KV_LORA, heads * QK_NOPE), lambda i, j: (0, j)),
                  _layer_spec(l, (KV_LORA, heads * V_HEAD), lambda i, j: (0, j)),
                  pl.BlockSpec((tm, 256), lambda i, j: (i, 0)),
                  pl.BlockSpec((1, 128), lambda i, j: (0, 0)),
                  pl.BlockSpec((tm, 128), lambda i, j: (i % tiles_per_seq, 0)),
                  pl.BlockSpec((tm, 128), lambda i, j: (i % tiles_per_seq, 0))],
        out_specs=[pl.BlockSpec((tm, heads * HEAD_PAD), lambda i, j: (i, j)),
                   pl.BlockSpec((tm, heads * V_HEAD), lambda i, j: (i, j))],
        out_shape=[jax.ShapeDtypeStruct((rows, N_HEADS * HEAD_PAD), BF),
                   jax.ShapeDtypeStruct((rows, N_HEADS * V_HEAD), BF)],
        scratch_shapes=[pltpu.VMEM((tm, heads * QK_NOPE), F32),
                        pltpu.VMEM((tm, (heads + 1) * 128), F32)],
        name="kv_proj",
        compiler_params=_params("parallel", "parallel"),
    )(kvn, w_uk, w_uv, kr, gn, cos_t, sin_t)


def _attn_kernel(q_ref, k_ref, v_ref, kp_ref, vp_ref, o_ref, head_bufs, *, tq, tc, td, rb,
                 n_prefix):
    for hd, bufs in enumerate(head_bufs):
        for src, dst in ((v_ref, bufs[-2]), (vp_ref, bufs[-1])):
            dst[:, :V_HEAD] = src[:, hd * V_HEAD:(hd + 1) * V_HEAD]
            dst[:, V_HEAD:] = jnp.ones((dst.shape[0], dst.shape[1] - V_HEAD), dst.dtype)
    for q_tile in range(q_ref.shape[0] // tq):
        _attn_tile(q_ref, k_ref, kp_ref, o_ref, head_bufs, q_tile=q_tile, tq=tq, tc=tc, td=td,
                   rb=rb, n_prefix=n_prefix)


def _attn_tile(q_ref, k_ref, kp_ref, o_ref, head_bufs, *, q_tile, tq, tc, td, rb, n_prefix):
    d0 = q_tile * tq
    items = [None] + [(k0, min(tc, d0 - k0)) for k0 in range(0, d0, tc)]
    bands = range(tq // td)
    qcols = lambda hd: slice(hd * HEAD_PAD, (hd + 1) * HEAD_PAD)
    vcols = lambda hd: slice(hd * V_HEAD, (hd + 1) * V_HEAD)
    tile = slice(d0, d0 + tq)

    def scores(hd, item, s_ref):
        if item is not None:
            k0, width = item
            s_ref[:, :width] = _dot_nt(q_ref[tile, qcols(hd)], k_ref[k0:k0 + width, qcols(hd)])
            return
        s_ref[:, tq:tq + 128] = _dot_nt(q_ref[tile, qcols(hd)], kp_ref[:, qcols(hd)])
        for g in bands:
            seen = (g + 1) * td
            s_ref[g * td:seen, :seen] = _dot_nt(q_ref[d0 + g * td:d0 + seen, qcols(hd)],
                                                k_ref[d0:d0 + seen, qcols(hd)])

    def softmax_rows(rs, pieces, p_ref, a_ref, m_ref):
        s = jnp.concatenate([x for _, x in pieces], axis=1)
        m_old = m_ref[rs, :]
        m_new = jnp.maximum(m_old, jnp.max(s, axis=-1, keepdims=True))
        alpha = jnp.exp2(m_old - m_new)
        p = jnp.exp2(s - jnp.concatenate([m_new] * (s.shape[1] // 128), axis=1))
        m_ref[rs, :] = m_new
        a_ref[rs, :] = alpha
        at = 0
        for cols, x in pieces:
            p_ref[rs, cols] = p[:, at:at + x.shape[1]].astype(p_ref.dtype)
            at += x.shape[1]

    def softmax(item, s_ref, p_ref, a_ref, m_ref):
        if item is not None:
            width = item[1]
            rows_per_block = min(rb, ROW_BLOCK_ELEMS // width)
            for r in range(0, tq, rows_per_block):
                rs = slice(r, r + rows_per_block)
                softmax_rows(rs, [(slice(0, width), s_ref[rs, :width])], p_ref, a_ref, m_ref)
            return
        for r in range(0, tq, rb):
            rs = slice(r, r + rb)
            g = r // td
            open_cols, last = slice(0, g * td), slice(g * td, (g + 1) * td)
            row = lax.broadcasted_iota(jnp.int32, (rb, td), 0) + (r - g * td)
            col = lax.broadcasted_iota(jnp.int32, (rb, td), 1)
            colp = lax.broadcasted_iota(jnp.int32, (rb, 128), 1)
            pieces = [(open_cols, s_ref[rs, open_cols])] if g else []
            pieces += [(last, jnp.where(col <= row, s_ref[rs, last], NEG_BIG)),
                       (slice(tq, tq + 128),
                        jnp.where(colp < n_prefix, s_ref[rs, tq:tq + 128], NEG_BIG))]
            softmax_rows(rs, pieces, p_ref, a_ref, m_ref)

    def accumulate(item, p_ref, a_ref, acc_ref, vx_ref, vpx_ref):
        def rescaled(rows):
            a = a_ref[rows, :]
            return jnp.concatenate([a, a], axis=1) * acc_ref[rows, :]

        if item is not None:
            k0, width = item
            acc_ref[...] = rescaled(slice(None)) + _dot(p_ref[:, :width],
                                                        vx_ref[k0:k0 + width, :])
            return
        for g in bands:
            rows, seen = slice(g * td, (g + 1) * td), (g + 1) * td
            acc_ref[rows, :] = (rescaled(rows) + _dot(p_ref[rows, :seen], vx_ref[d0:d0 + seen, :])
                                + _dot(p_ref[rows, tq:tq + 128], vpx_ref[...]))

    for _, _, _, m_ref, acc_ref, _, _ in head_bufs:
        m_ref[...] = jnp.full(m_ref.shape, NEG_BIG, F32)
        acc_ref[...] = jnp.zeros(acc_ref.shape, F32)
    n = len(items)
    for t in range(n + 2):
        for hd, (s_refs, p_refs, a_refs, m_ref, acc_ref, vx_ref, vpx_ref) in enumerate(head_bufs):
            depth = len(s_refs)
            if t < n:
                scores(hd, items[t], s_refs[t % depth])
            if 1 <= t <= n:
                u = (t - 1) % depth
                softmax(items[t - 1], s_refs[u], p_refs[u], a_refs[u], m_ref)
            if t >= 2:
                u = (t - 2) % depth
                accumulate(items[t - 2], p_refs[u], a_refs[u], acc_ref, vx_ref, vpx_ref)
    for hd, bufs in enumerate(head_bufs):
        acc_ref = bufs[4]
        o_ref[tile, vcols(hd)] = (acc_ref[:, :V_HEAD] / acc_ref[:, V_HEAD:]).astype(o_ref.dtype)


def _attention(q, k, v, kp, vp, batch, seq, tq, tc, td, heads):
    depth = 3
    stat = pltpu.VMEM((tq, 128), F32)
    width = max(tc, tq + 128)
    head_bufs = [([pltpu.VMEM((tq, width), F32)] * depth, [pltpu.VMEM((tq, width), BF)] * depth,
                  [stat] * depth, stat, pltpu.VMEM((tq, 2 * V_HEAD), F32),
                  pltpu.VMEM((seq, 2 * V_HEAD), BF), pltpu.VMEM((128, 2 * V_HEAD), BF))] * heads
    kern = functools.partial(_attn_kernel, tq=tq, tc=tc, td=td, rb=64, n_prefix=N_META)
    return pl.pallas_call(
        kern,
        grid=(batch, N_HEADS // heads),
        in_specs=[pl.BlockSpec((None, seq, heads * HEAD_PAD), lambda b, h: (b, 0, h)),
                  pl.BlockSpec((None, seq, heads * HEAD_PAD), lambda b, h: (b, 0, h)),
                  pl.BlockSpec((None, seq, heads * V_HEAD), lambda b, h: (b, 0, h)),
                  pl.BlockSpec((128, heads * HEAD_PAD), lambda b, h: (0, h)),
                  pl.BlockSpec((128, heads * V_HEAD), lambda b, h: (0, h))],
        out_specs=pl.BlockSpec((None, seq, heads * V_HEAD), lambda b, h: (b, 0, h)),
        out_shape=jax.ShapeDtypeStruct((batch, seq, N_HEADS * V_HEAD), BF),
        scratch_shapes=[head_bufs],
        name="attention",
        compiler_params=_params("parallel", "parallel"),
    )(q, k, v, kp, vp)


def _attn_meta_kernel(q_ref, k_ref, v_ref, o_ref):
    s = _dot_nt(q_ref[...], k_ref[...])
    row = lax.broadcasted_iota(jnp.int32, s.shape, 0)
    col = lax.broadcasted_iota(jnp.int32, s.shape, 1)
    s = jnp.where(row >= col, s, NEG_BIG)
    p = jnp.exp2(s - jnp.max(s, axis=-1, keepdims=True))
    l = jnp.sum(p, axis=-1, keepdims=True)
    o_ref[...] = (_dot(p.astype(BF), v_ref[...]) / l).astype(o_ref.dtype)


def _attention_meta(q, k, v):
    return pl.pallas_call(
        _attn_meta_kernel,
        grid=(N_HEADS,),
        in_specs=[pl.BlockSpec((N_META, HEAD_PAD), lambda h: (0, h)),
                  pl.BlockSpec((N_META, HEAD_PAD), lambda h: (0, h)),
                  pl.BlockSpec((N_META, V_HEAD), lambda h: (0, h))],
        out_specs=pl.BlockSpec((N_META, V_HEAD), lambda h: (0, h)),
        out_shape=jax.ShapeDtypeStruct((N_META, N_HEADS * V_HEAD), BF),
        name="attention_meta",
        compiler_params=_params("parallel"),
    )(q, k, v)


def _pool_branch_kernel(h_ref, w_ref, pw_ref, ps_ref, hist_ref, yc_ref, *rest,
                        tm, tiles_per_seq, pos_offset):
    carry_ref = rest[-1]
    i = pl.program_id(0)
    pin = _dot(h_ref[...], w_ref[...])

    @pl.when(i % tiles_per_seq == 0)
    def _():
        carry_ref[...] = hist_ref[...]

    ext = jnp.concatenate([carry_ref[...], pin], axis=0)
    tail = pin[tm - HIST:]
    carry_ref[...] = tail
    if len(rest) == 2:
        rest[0][...] = tail
    n_ext = tm + HIST
    seen = (lax.broadcasted_iota(jnp.int32, (tm, 1), 0)
            + ((i % tiles_per_seq) * tm + pos_offset + 1)).astype(F32)
    for g, w in enumerate(POOL_WINDOWS):
        xg = ext[:, g * POOL_GROUP:(g + 1) * POOL_GROUP]
        s, span = xg, 1
        while span < w:
            s = s[span:] + s[:s.shape[0] - span]
            span *= 2
        s = s[n_ext - (w - 1) - tm:]
        pooled = s / jnp.minimum(seen, float(w)) - xg[HIST:]
        mixed = _dot(pooled.astype(BF), pw_ref[g])
        yc_ref[:, g * POOL_GROUP:(g + 1) * POOL_GROUP] = (
            mixed * ps_ref[:, g * POOL_GROUP:(g + 1) * POOL_GROUP]).astype(yc_ref.dtype)


def _pool_branch(h, w_pool, pool_w, l, pool_scale, hist, tm, tiles_per_seq, pos_offset,
                 emit_hist):
    rows = h.shape[0]
    assert not emit_hist or rows == tm
    kern = functools.partial(_pool_branch_kernel, tm=tm, tiles_per_seq=tiles_per_seq,
                             pos_offset=pos_offset)
    ng = len(POOL_WINDOWS)
    n_out = 2 if emit_hist else 1
    return pl.pallas_call(
        kern,
        grid=(rows // tm,),
        in_specs=[pl.BlockSpec((tm, D_MODEL), lambda i: (i, 0)),
                  _layer_spec(l, (D_MODEL, D_POOL), lambda i: (0, 0)),
                  _layer_spec(l, (ng, POOL_GROUP, POOL_GROUP), lambda i: (0, 0, 0)),
                  pl.BlockSpec((1, D_POOL), lambda i: (0, 0)),
                  pl.BlockSpec((HIST, D_POOL), lambda i: (0, 0))],
        out_specs=[pl.BlockSpec((tm, D_POOL), lambda i: (i, 0)),
                   pl.BlockSpec((HIST, D_POOL), lambda i: (0, 0))][:n_out],
        out_shape=[jax.ShapeDtypeStruct((rows, D_POOL), BF),
                   jax.ShapeDtypeStruct((HIST, D_POOL), F32)][:n_out],
        scratch_shapes=[pltpu.VMEM((HIST, D_POOL), F32)],
        name="pool_branch",
        compiler_params=_params("arbitrary"),
    )(h, w_pool, pool_w, pool_scale, hist)


def _merge_kernel(h_ref, ya_ref, yb_ref, yc_ref, wg0_ref, wg1_ref, wg2_ref, wa_ref, wb_ref, wc_ref,
                  o_ref):
    h = h_ref[...]
    m = jax.nn.sigmoid(_dot(h, wg0_ref[...])) * _dot(ya_ref[...], wa_ref[...])
    m += jax.nn.sigmoid(_dot(h, wg1_ref[...])) * _dot(yb_ref[...], wb_ref[...])
    m += jax.nn.sigmoid(_dot(h, wg2_ref[...])) * _dot(yc_ref[...], wc_ref[...])
    o_ref[...] = m.astype(o_ref.dtype)


def _merge(h, ya, yb, yc, w_gate, wa, wb, wc, l, tm, tn):
    rows = h.shape[0]
    ncb = D_MODEL // tn
    row = lambda width: pl.BlockSpec((tm, width), lambda i, j: (i, 0))
    col = lambda depth, off: _layer_spec(l, (depth, tn), lambda i, j: (0, j + off))
    return pl.pallas_call(
        _merge_kernel,
        grid=(rows // tm, ncb),
        in_specs=[row(D_MODEL), row(D_CONV), row(N_HEADS * V_HEAD), row(D_POOL),
                  col(D_MODEL, 0), col(D_MODEL, ncb), col(D_MODEL, 2 * ncb),
                  col(D_CONV, 0), col(N_HEADS * V_HEAD, 0), col(D_POOL, 0)],
        out_specs=pl.BlockSpec((tm, tn), lambda i, j: (i, j)),
        out_shape=jax.ShapeDtypeStruct((rows, D_MODEL), BF),
        name="merge",
        compiler_params=_params("parallel", "arbitrary"),
    )(h, ya, yb, yc, w_gate, w_gate, w_gate, wa, wb, wc)


OPROJ_BUFFERS = 3
OPROJ_LOOKAHEAD = OPROJ_BUFFERS - 1


def _oproj_kernel(m_hbm, w_ref, x_hbm, g_ref, xo_ref, h2_ref, m_buf, x_buf, sem, *, tm, n_steps):
    s = pl.program_id(0)

    def tile_copies(step):
        rows = pl.ds(pl.multiple_of(step * tm, tm), tm)
        slot = step % OPROJ_BUFFERS
        return (pltpu.make_async_copy(m_hbm.at[rows], m_buf.at[slot], sem.at[0, slot]),
                pltpu.make_async_copy(x_hbm.at[rows], x_buf.at[slot], sem.at[1, slot]))

    @pl.when(s == 0)
    def _():
        for step in range(min(OPROJ_LOOKAHEAD, n_steps)):
            for copy in tile_copies(step):
                copy.start()

    @pl.when(s + OPROJ_LOOKAHEAD < n_steps)
    def _():
        for copy in tile_copies(s + OPROJ_LOOKAHEAD):
            copy.start()

    for copy in tile_copies(s):
        copy.wait()
    slot = s % OPROJ_BUFFERS
    x = x_buf[slot] + _dot(m_buf[slot], w_ref[...])
    xo_ref[...] = x
    h2_ref[...] = _rms_norm(x, g_ref[...]).astype(h2_ref.dtype)


def _oproj(merged, w_o, l, x, g, tm):
    rows = x.shape[0]
    n_steps = rows // tm
    return pl.pallas_call(
        functools.partial(_oproj_kernel, tm=tm, n_steps=n_steps),
        grid=(n_steps,),
        in_specs=[pl.BlockSpec(memory_space=pl.ANY),
                  _layer_spec(l, (D_MODEL, D_MODEL), lambda i: (0, 0)),
                  pl.BlockSpec(memory_space=pl.ANY),
                  pl.BlockSpec((1, D_MODEL), lambda i: (0, 0))],
        out_specs=[pl.BlockSpec((tm, D_MODEL), lambda i: (i, 0)),
                   pl.BlockSpec((tm, D_MODEL), lambda i: (i, 0))],
        out_shape=[jax.ShapeDtypeStruct((rows, D_MODEL), F32),
                   jax.ShapeDtypeStruct((rows, D_MODEL), BF)],
        scratch_shapes=[pltpu.VMEM((OPROJ_BUFFERS, tm, D_MODEL), merged.dtype),
                        pltpu.VMEM((OPROJ_BUFFERS, tm, D_MODEL), x.dtype),
                        pltpu.SemaphoreType.DMA((2, OPROJ_BUFFERS))],
        name="oproj",
        compiler_params=_params("arbitrary"),
    )(merged, w_o, x, g)


def _mlp_kernel(h2_ref, wu_ref, wd_ref, x_ref, o_ref):
    @pl.when(pl.program_id(1) == 0)
    def _():
        o_ref[...] = x_ref[...]

    a = jnp.maximum(_dot(h2_ref[...], wu_ref[...]), 0.0)
    o_ref[...] += _dot((a * a).astype(BF), wd_ref[...])


def _mlp(h2, w_up, w_down, l, x, tm, tf):
    rows = x.shape[0]
    return pl.pallas_call(
        _mlp_kernel,
        grid=(rows // tm, D_FF // tf),
        in_specs=[pl.BlockSpec((tm, D_MODEL), lambda i, k: (i, 0)),
                  _layer_spec(l, (D_MODEL, tf), lambda i, k: (0, k)),
                  _layer_spec(l, (tf, D_MODEL), lambda i, k: (k, 0)),
                  pl.BlockSpec((tm, D_MODEL), lambda i, k: (i, 0))],
        out_specs=pl.BlockSpec((tm, D_MODEL), lambda i, k: (i, 0)),
        out_shape=jax.ShapeDtypeStruct((rows, D_MODEL), F32),
        name="mlp",
        compiler_params=_params("parallel", "arbitrary"),
    )(h2, w_up, w_down, x)


def _swap_halves(a):
    return jnp.concatenate([a[..., HALF_ROPE:], a[..., :HALF_ROPE]], axis=-1)


def _twice(a):
    return jnp.concatenate([a, a], axis=-1)


def _widen_head(a):
    rope = a[..., QK_NOPE:]
    return jnp.concatenate([a[..., :QK_NOPE], rope, _swap_halves(rope)], axis=-1)


def _rope_gain_tables(gain, cos_t, sin_t):
    rope_gain = gain[QK_NOPE:]
    return cos_t * rope_gain[None], sin_t * _swap_halves(rope_gain)[None]


def _stacked_weights(w_in, w_uq, w_ukv, pool_w, w_branch_a, w_branch_b, w_branch_c, w_o, w_up,
                     w_down):
    depth = w_in.shape[0]
    wi = w_in.astype(BF)
    o_q = 3 * D_CONV
    o_kr = o_q + Q_LORA + KV_LORA
    o_pool = o_kr + QK_ROPE
    o_gate = o_pool + D_POOL
    w_kr = wi[:, :, o_kr:o_pool]
    w_ukv_h = w_ukv.astype(BF).reshape(depth, KV_LORA, N_HEADS, QK_NOPE + V_HEAD)
    return dict(
        w_in=wi,
        w_lat=jnp.concatenate([wi[:, :, o_q:o_kr], _twice(w_kr), _twice(_swap_halves(w_kr))],
                              axis=2),
        w_pool=wi[:, :, o_pool:o_gate],
        w_gate=wi[:, :, o_gate:],
        w_uq=_widen_head(w_uq.astype(BF).reshape(depth, Q_LORA, N_HEADS, QK_HEAD)).reshape(
            depth, Q_LORA, N_HEADS * HEAD_PAD),
        w_uk=w_ukv_h[..., :QK_NOPE].reshape(depth, KV_LORA, N_HEADS * QK_NOPE),
        w_uv=w_ukv_h[..., QK_NOPE:].reshape(depth, KV_LORA, N_HEADS * V_HEAD),
        pool_w=pool_w.astype(BF),
        wa=w_branch_a.astype(BF),
        wb=w_branch_b.astype(BF),
        wc=w_branch_c.astype(BF),
        w_o=w_o.astype(BF),
        w_up=w_up.astype(BF),
        w_down=w_down.astype(BF),
    )


def _layer_vectors(l, cos_t, sin_t, attn_norm, conv_w, q_lat_norm, kv_lat_norm, q_norm, k_norm,
                   pool_scale, mlp_norm):
    gq = q_norm[l] * (QK_HEAD ** -0.5 * LOG2_E)
    q_cos, q_sin = _rope_gain_tables(gq, cos_t, sin_t)
    k_cos, k_sin = _rope_gain_tables(k_norm[l], cos_t, sin_t)
    return dict(
        attn_norm=attn_norm[l][None],
        conv_w=conv_w[l],
        q_lat_norm=q_lat_norm[l][None],
        kv_lat_norm=kv_lat_norm[l][None],
        gq_nope=gq[None, :QK_NOPE],
        gk_nope=k_norm[l][None, :QK_NOPE],
        q_trig=jnp.concatenate([q_cos, q_sin], axis=1),
        k_rope=(_twice(k_cos), _twice(k_sin)),
        pool_scale=pool_scale[l][None],
        mlp_norm=mlp_norm[l][None],
    )


def _rope_tables(total):
    pos = jnp.arange(total, dtype=F32)
    inv = ROPE_THETA ** (-jnp.arange(0, QK_ROPE, 2, dtype=F32) / QK_ROPE)
    ang = pos[:, None] * inv[None, :]
    cos, sin = jnp.cos(ang), jnp.sin(ang)
    return _twice(cos), jnp.concatenate([-sin, sin], axis=1)


class _Tiles(NamedTuple):
    rows: int
    rows_big: int
    conv_cols: int
    merge_cols: int
    mlp_hidden: int
    attn_q: int
    attn_chunk: int
    attn_band: int
    attn_heads: int


def _tiles(seq):
    return _Tiles(rows=min(512, seq), rows_big=min(1024, seq), conv_cols=1024, merge_cols=512,
                  mlp_hidden=512, attn_q=512, attn_chunk=1024, attn_band=256, attn_heads=1)


def _layer(x, w, l, p, positions, prefix, *, batch, seq, finish):
    is_meta = prefix is None
    q_trig = p['q_trig'][positions]
    k_cos, k_sin = (t[positions] for t in p['k_rope'])
    t = _tiles(seq)
    tm, tm_big = t.rows, t.rows_big
    tps = seq // tm
    if is_meta:
        conv_hist = jnp.zeros((HIST, D_CONV), F32)
        pool_hist = jnp.zeros((HIST, D_POOL), F32)
    else:
        conv_hist, pool_hist, kp, vp = prefix

    h, qn, kvn, kr = _latent(x, p['attn_norm'], w['w_lat'], l, p['q_lat_norm'], p['kv_lat_norm'],
                             tm_big)
    ya, *conv_tail = _conv_branch(h, w['w_in'], l, p['conv_w'], conv_hist, tm_big, t.conv_cols,
                                  seq // tm_big, is_meta)
    yc, *pool_tail = _pool_branch(h, w['w_pool'], w['pool_w'], l, p['pool_scale'], pool_hist,
                                  tm_big, seq // tm_big, 0 if is_meta else N_META, is_meta)
    k, v = _kv_proj(kvn, w['w_uk'], w['w_uv'], l, kr, p['gk_nope'], k_cos, k_sin, tm, N_HEADS,
                    tps)
    out_prefix = (conv_tail[0], pool_tail[0], k, v) if is_meta else None
    if not finish:
        return None, out_prefix
    q = _q_proj(qn, w['w_uq'], l, p['gq_nope'], q_trig, tm, N_HEADS, tps)
    if is_meta:
        yb = _attention_meta(q, k, v)
    else:
        pad = ((0, 128 - N_META), (0, 0))
        per_batch = lambda a: a.reshape(batch, seq, a.shape[-1])
        yb = _attention(per_batch(q), per_batch(k), per_batch(v), jnp.pad(kp, pad),
                        jnp.pad(vp, pad), batch, seq, t.attn_q, t.attn_chunk, t.attn_band,
                        t.attn_heads).reshape(batch * seq, -1)
    merged = _merge(h, ya, yb, yc, w['w_gate'], w['wa'], w['wb'], w['wc'], l, tm_big, t.merge_cols)
    x_mid, h2 = _oproj(merged, w['w_o'], l, x, p['mlp_norm'], tm)
    return _mlp(h2, w['w_up'], w['w_down'], l, x_mid, tm_big, t.mlp_hidden), out_prefix


def kernel(x, meta_tokens, attn_norm, w_in, conv_w, q_lat_norm, kv_lat_norm, w_uq, w_ukv, q_norm, k_norm, pool_w, pool_scale, w_branch_a, w_branch_b, w_branch_c, w_o, mlp_norm, w_up, w_down):
    batch, seq, d = x.shape
    depth = w_in.shape[0]
    cos_t, sin_t = _rope_tables(N_META + seq)
    xm = meta_tokens.astype(F32)
    xr = x.reshape(batch * seq, d)
    w = _stacked_weights(w_in, w_uq, w_ukv, pool_w, w_branch_a, w_branch_b, w_branch_c, w_o, w_up,
                         w_down)
    for l in range(depth):
        p = _layer_vectors(l, cos_t, sin_t, attn_norm, conv_w, q_lat_norm, kv_lat_norm, q_norm,
                           k_norm, pool_scale, mlp_norm)
        last = l == depth - 1
        xm, prefix = _layer(xm, w, l, p, slice(0, N_META), None, batch=1, seq=N_META,
                            finish=not last)
        xr, _ = _layer(xr, w, l, p, slice(N_META, None), prefix, batch=batch, seq=seq, finish=True)
    return xr.reshape(batch, seq, d)
```

```python
import functools
from typing import NamedTuple

import jax
import jax.numpy as jnp
from jax import lax
from jax.experimental import pallas as pl
from jax.experimental.pallas import tpu as pltpu

D_MODEL = 2048
N_META = 16
EPS = 1e-6
D_CONV = 1024
N_HEADS = 16
QK_NOPE = 128
QK_ROPE = 64
QK_HEAD = QK_NOPE + QK_ROPE
HALF_ROPE = QK_ROPE // 2
V_HEAD = 128
Q_LORA = 512
KV_LORA = 512
ROPE_THETA = 10000.0
D_POOL = 1024
POOL_WINDOWS = (2, 4, 8, 16)
POOL_GROUP = D_POOL // len(POOL_WINDOWS)
D_FF = 4 * D_MODEL
HIST = 16
HEAD_PAD = 256
D_LAT = Q_LORA + KV_LORA + 256
V7X_VMEM_BYTES = 64 * 2**20
VMEM_LIMIT = V7X_VMEM_BYTES - 8 * 2**20
V7X_VREG_F32 = 8 * 128
ROW_BLOCK_ELEMS = 32 * V7X_VREG_F32
NEG_BIG = -1e30
LOG2_E = 1.4426950408889634

BF = jnp.bfloat16
F32 = jnp.float32


def _dot(a, b):
    return jnp.dot(a, b, preferred_element_type=F32)


def _dot_nt(a, b):
    return lax.dot_general(a, b, (((1,), (1,)), ((), ())), preferred_element_type=F32)


def _layer_spec(l, shape, index_map, resident=False):
    mode = dict(pipeline_mode=pl.Buffered(1)) if resident else {}
    return pl.BlockSpec((None,) + shape, lambda *g: (l,) + index_map(*g), **mode)


def _params(*sem):
    return pltpu.CompilerParams(dimension_semantics=sem, vmem_limit_bytes=VMEM_LIMIT)


def _rms_norm(x, g):
    ms = jnp.mean(x * x, axis=-1, keepdims=True)
    return x * lax.rsqrt(ms + EPS) * g


def _conv_branch_kernel(h_ref, wu_ref, wb_ref, wc_ref, cw_ref, hist_ref, ya_ref, *rest,
                        tm, tiles_per_seq):
    carry_ref = rest[-1]
    i, j = pl.program_id(0), pl.program_id(1)
    h = h_ref[...]
    cu = _dot(h, wc_ref[...]) * _dot(h, wu_ref[...])

    @pl.when(i % tiles_per_seq == 0)
    def _():
        carry_ref[j] = hist_ref[...]

    ext = jnp.concatenate([carry_ref[j], cu], axis=0)
    tail = cu[tm - HIST:]
    carry_ref[j] = tail
    if len(rest) == 2:
        rest[0][...] = tail
    cw = cw_ref[...]
    y = (cw[2:3] * cu + cw[1:2] * ext[HIST - 1:HIST - 1 + tm] + cw[0:1] * ext[HIST - 2:HIST - 2 + tm])
    ya_ref[...] = (_dot(h, wb_ref[...]) * y).astype(ya_ref.dtype)


def _conv_branch(h, w_in, l, conv_w, hist, tm, tc, tiles_per_seq, emit_hist):
    rows = h.shape[0]
    ncb = D_CONV // tc
    assert not emit_hist or rows == tm
    kern = functools.partial(_conv_branch_kernel, tm=tm, tiles_per_seq=tiles_per_seq)
    n_out = 2 if emit_hist else 1
    return pl.pallas_call(
        kern,
        grid=(rows // tm, ncb),
        in_specs=[pl.BlockSpec((tm, D_MODEL), lambda i, j: (i, 0)),
                  _layer_spec(l, (D_MODEL, tc), lambda i, j: (0, j)),
                  _layer_spec(l, (D_MODEL, tc), lambda i, j: (0, j + ncb)),
                  _layer_spec(l, (D_MODEL, tc), lambda i, j: (0, j + 2 * ncb)),
                  pl.BlockSpec((3, tc), lambda i, j: (0, j)),
                  pl.BlockSpec((HIST, tc), lambda i, j: (0, j))],
        out_specs=[pl.BlockSpec((tm, tc), lambda i, j: (i, j)),
                   pl.BlockSpec((HIST, tc), lambda i, j: (0, j))][:n_out],
        out_shape=[jax.ShapeDtypeStruct((rows, D_CONV), BF),
                   jax.ShapeDtypeStruct((HIST, D_CONV), F32)][:n_out],
        scratch_shapes=[pltpu.VMEM((ncb, HIST, tc), F32)],
        name="conv_branch",
        compiler_params=_params("arbitrary", "arbitrary"),
    )(h, w_in, w_in, w_in, conv_w, hist)


def _latent_kernel(x_ref, g_ref, w_ref, gq_ref, gkv_ref, h_ref, qn_ref, kvn_ref, kr_ref):
    h = _rms_norm(x_ref[...], g_ref[...]).astype(h_ref.dtype)
    h_ref[...] = h
    acc = _dot(h, w_ref[...])
    qn_ref[...] = _rms_norm(acc[:, :Q_LORA], gq_ref[...]).astype(qn_ref.dtype)
    kvn_ref[...] = _rms_norm(acc[:, Q_LORA:Q_LORA + KV_LORA], gkv_ref[...]).astype(kvn_ref.dtype)
    kr_ref[...] = acc[:, Q_LORA + KV_LORA:]


def _latent(x, g, w_lat, l, gq, gkv, tm):
    rows = x.shape[0]
    row = lambda width: pl.BlockSpec((tm, width), lambda i: (i, 0))
    vec = lambda width: pl.BlockSpec((1, width), lambda i: (0, 0))
    return pl.pallas_call(
        _latent_kernel,
        grid=(rows // tm,),
        in_specs=[row(D_MODEL), vec(D_MODEL), _layer_spec(l, (D_MODEL, D_LAT), lambda i: (0, 0), True),
                  vec(Q_LORA), vec(KV_LORA)],
        out_specs=[row(D_MODEL), row(Q_LORA), row(KV_LORA), row(256)],
        out_shape=[jax.ShapeDtypeStruct((rows, D_MODEL), BF),
                   jax.ShapeDtypeStruct((rows, Q_LORA), BF),
                   jax.ShapeDtypeStruct((rows, KV_LORA), BF),
                   jax.ShapeDtypeStruct((rows, 256), F32)],
        name="latent",
        compiler_params=_params("parallel"),
    )(x, g, w_lat, gq, gkv)


def _row_sumsq(a, n_real):
    keep = lax.broadcasted_iota(jnp.int32, (a.shape[1], 128), 0) < n_real
    return _dot((a * a).astype(BF), jnp.where(keep, 1.0, 0.0).astype(BF))


def _q_kernel(qn_ref, w_ref, g_ref, trig_ref, q_ref, acc_ref, ss_ref, *, heads, rb):
    acc_ref[...] = _dot(qn_ref[...], w_ref[...])
    for hd in range(heads):
        ss_ref[:, hd * 128:(hd + 1) * 128] = _row_sumsq(
            acc_ref[:, hd * HEAD_PAD:(hd + 1) * HEAD_PAD], QK_HEAD)
    g = g_ref[...]
    for r in range(0, acc_ref.shape[0], rb):
        rs = slice(r, r + rb)
        trig = trig_ref[rs, :]
        for hd in range(heads):
            c0 = hd * HEAD_PAD
            inv = lax.rsqrt(ss_ref[rs, hd * 128:(hd + 1) * 128] * (1.0 / QK_HEAD) + EPS)
            q_ref[rs, c0:c0 + QK_NOPE] = (
                acc_ref[rs, c0:c0 + QK_NOPE] * (inv * g)).astype(q_ref.dtype)
            q_ref[rs, c0 + QK_NOPE:c0 + HEAD_PAD] = (
                acc_ref[rs, c0 + QK_NOPE:c0 + HEAD_PAD] * (inv * trig)).astype(q_ref.dtype)


def _q_proj(qn, w_uq, l, g_nope, trig, tm, heads, tiles_per_seq):
    rows = qn.shape[0]
    kern = functools.partial(_q_kernel, heads=heads, rb=min(64, tm))
    return pl.pallas_call(
        kern,
        grid=(rows // tm, N_HEADS // heads),
        in_specs=[pl.BlockSpec((tm, Q_LORA), lambda i, j: (i, 0)),
                  _layer_spec(l, (Q_LORA, heads * HEAD_PAD), lambda i, j: (0, j)),
                  pl.BlockSpec((1, QK_NOPE), lambda i, j: (0, 0)),
                  pl.BlockSpec((tm, 128), lambda i, j: (i % tiles_per_seq, 0))],
        out_specs=pl.BlockSpec((tm, heads * HEAD_PAD), lambda i, j: (i, j)),
        out_shape=jax.ShapeDtypeStruct((rows, N_HEADS * HEAD_PAD), BF),
        scratch_shapes=[pltpu.VMEM((tm, heads * HEAD_PAD), F32),
                        pltpu.VMEM((tm, heads * 128), F32)],
        name="q_proj",
        compiler_params=_params("parallel", "parallel"),
    )(qn, w_uq, g_nope, trig)


def _kv_kernel(kvn_ref, wk_ref, wv_ref, kr_ref, gn_ref, cos_ref, sin_ref, k_ref, v_ref,
               acc_ref, ss_ref, *, heads, rb):
    kvn = kvn_ref[...]
    acc_ref[...] = _dot(kvn, wk_ref[...])
    v_ref[...] = _dot(kvn, wv_ref[...]).astype(v_ref.dtype)
    blk = (lax.broadcasted_iota(jnp.int32, (256, 256), 0) // 128
           == lax.broadcasted_iota(jnp.int32, (256, 256), 1) // 128)
    pair_ones = jnp.where(blk, 1.0, 0.0).astype(BF)
    for hd in range(0, heads, 2):
        a = acc_ref[:, hd * QK_NOPE:(hd + 2) * QK_NOPE]
        ss_ref[:, hd * 128:(hd + 2) * 128] = _dot((a * a).astype(BF), pair_ones)
    ss_ref[:, heads * 128:] = _row_sumsq(kr_ref[:, :128], QK_ROPE)
    gn = gn_ref[...]
    for r in range(0, acc_ref.shape[0], rb):
        rs = slice(r, r + rb)
        kr_rot = kr_ref[rs, :128] * cos_ref[rs, :] + kr_ref[rs, 128:] * sin_ref[rs, :]
        kr_ss = ss_ref[rs, heads * 128:]
        for hd in range(heads):
            ss = ss_ref[rs, hd * 128:(hd + 1) * 128] + kr_ss
            inv = lax.rsqrt(ss * (1.0 / QK_HEAD) + EPS)
            k_ref[rs, hd * HEAD_PAD:hd * HEAD_PAD + QK_NOPE] = (
                acc_ref[rs, hd * QK_NOPE:(hd + 1) * QK_NOPE] * (inv * gn)).astype(k_ref.dtype)
            k_ref[rs, hd * HEAD_PAD + QK_NOPE:(hd + 1) * HEAD_PAD] = (kr_rot * inv).astype(
                k_ref.dtype)


def _kv_proj(kvn, w_uk, w_uv, l, kr, gn, cos_t, sin_t, tm, heads, tiles_per_seq):
    rows = kvn.shape[0]
    kern = functools.partial(_kv_kernel, heads=heads, rb=min(64, tm))
    return pl.pallas_call(
        kern,
        grid=(rows // tm, N_HEADS // heads),
        in_specs=[pl.BlockSpec((tm, KV_LORA), lambda i, j: (i, 0)),
                  _layer_spec(l, (KV_LORA, heads * QK_NOPE), lambda i, j: (0, j)),
                  _layer_spec(l, (KV_LORA, heads * V_HEAD), lambda i, j: (0, j)),
                  pl.BlockSpec((tm, 256), lambda i, j: (i, 0)),
                  pl.BlockSpec((1, 128), lambda i, j: (0, 0)),
                  pl.BlockSpec((tm, 128), lambda i, j: (i % tiles_per_seq, 0)),
                  pl.BlockSpec((tm, 128), lambda i, j: (i % tiles_per_seq, 0))],
        out_specs=[pl.BlockSpec((tm, heads * HEAD_PAD), lambda i, j: (i, j)),
                   pl.BlockSpec((tm, heads * V_HEAD), lambda i, j: (i, j))],
        out_shape=[jax.ShapeDtypeStruct((rows, N_HEADS * HEAD_PAD), BF),
                   jax.ShapeDtypeStruct((rows, N_HEADS * V_HEAD), BF)],
        scratch_shapes=[pltpu.VMEM((tm, heads * QK_NOPE), F32),
                        pltpu.VMEM((tm, (heads + 1) * 128), F32)],
        name="kv_proj",
        compiler_params=_params("parallel", "parallel"),
    )(kvn, w_uk, w_uv, kr, gn, cos_t, sin_t)


def _attn_kernel(q_ref, k_ref, v_ref, kp_ref, vp_ref, o_ref, head_bufs, *, tq, tc, td, rb,
                 n_prefix):
    for hd, bufs in enumerate(head_bufs):
        for src, dst in ((v_ref, bufs[-2]), (vp_ref, bufs[-1])):
            dst[:, :V_HEAD] = src[:, hd * V_HEAD:(hd + 1) * V_HEAD]
            dst[:, V_HEAD:] = jnp.ones((dst.shape[0], dst.shape[1] - V_HEAD), dst.dtype)
    for q_tile in range(q_ref.shape[0] // tq):
        _attn_tile(q_ref, k_ref, kp_ref, o_ref, head_bufs, q_tile=q_tile, tq=tq, tc=tc, td=td,
                   rb=rb, n_prefix=n_prefix)


def _attn_tile(q_ref, k_ref, kp_ref, o_ref, head_bufs, *, q_tile, tq, tc, td, rb, n_prefix):
    d0 = q_tile * tq
    items = [None] + [(k0, min(tc, d0 - k0)) for k0 in range(0, d0, tc)]
    bands = range(tq // td)
    qcols = lambda hd: slice(hd * HEAD_PAD, (hd + 1) * HEAD_PAD)
    vcols = lambda hd: slice(hd * V_HEAD, (hd + 1) * V_HEAD)
    tile = slice(d0, d0 + tq)

    def scores(hd, item, s_ref):
        if item is not None:
            k0, width = item
            s_ref[:, :width] = _dot_nt(q_ref[tile, qcols(hd)], k_ref[k0:k0 + width, qcols(hd)])
            return
        s_ref[:, tq:tq + 128] = _dot_nt(q_ref[tile, qcols(hd)], kp_ref[:, qcols(hd)])
        for g in bands:
            seen = (g + 1) * td
            s_ref[g * td:seen, :seen] = _dot_nt(q_ref[d0 + g * td:d0 + seen, qcols(hd)],
                                                k_ref[d0:d0 + seen, qcols(hd)])

    def softmax_rows(rs, pieces, p_ref, a_ref, m_ref):
        s = jnp.concatenate([x for _, x in pieces], axis=1)
        m_old = m_ref[rs, :]
        m_new = jnp.maximum(m_old, jnp.max(s, axis=-1, keepdims=True))
        alpha = jnp.exp2(m_old - m_new)
        p = jnp.exp2(s - jnp.concatenate([m_new] * (s.shape[1] // 128), axis=1))
        m_ref[rs, :] = m_new
        a_ref[rs, :] = alpha
        at = 0
        for cols, x in pieces:
            p_ref[rs, cols] = p[:, at:at + x.shape[1]].astype(p_ref.dtype)
            at += x.shape[1]

    def softmax(item, s_ref, p_ref, a_ref, m_ref):
        if item is not None:
            width = item[1]
            rows_per_block = min(rb, ROW_BLOCK_ELEMS // width)
            for r in range(0, tq, rows_per_block):
                rs = slice(r, r + rows_per_block)
                softmax_rows(rs, [(slice(0, width), s_ref[rs, :width])], p_ref, a_ref, m_ref)
            return
        for r in range(0, tq, rb):
            rs = slice(r, r + rb)
            g = r // td
            open_cols, last = slice(0, g * td), slice(g * td, (g + 1) * td)
            row = lax.broadcasted_iota(jnp.int32, (rb, td), 0) + (r - g * td)
            col = lax.broadcasted_iota(jnp.int32, (rb, td), 1)
            colp = lax.broadcasted_iota(jnp.int32, (rb, 128), 1)
            pieces = [(open_cols, s_ref[rs, open_cols])] if g else []
            pieces += [(last, jnp.where(col <= row, s_ref[rs, last], NEG_BIG)),
                       (slice(tq, tq + 128),
                        jnp.where(colp < n_prefix, s_ref[rs, tq:tq + 128], NEG_BIG))]
            softmax_rows(rs, pieces, p_ref, a_ref, m_ref)

    def accumulate(item, p_ref, a_ref, acc_ref, vx_ref, vpx_ref):
        def rescaled(rows):
            a = a_ref[rows, :]
            return jnp.concatenate([a, a], axis=1) * acc_ref[rows, :]

        if item is not None:
            k0, width = item
            acc_ref[...] = rescaled(slice(None)) + _dot(p_ref[:, :width],
                                                        vx_ref[k0:k0 + width, :])
            return
        for g in bands:
            rows, seen = slice(g * td, (g + 1) * td), (g + 1) * td
            acc_ref[rows, :] = (rescaled(rows) + _dot(p_ref[rows, :seen], vx_ref[d0:d0 + seen, :])
                                + _dot(p_ref[rows, tq:tq + 128], vpx_ref[...]))

    for _, _, _, m_ref, acc_ref, _, _ in head_bufs:
        m_ref[...] = jnp.full(m_ref.shape, NEG_BIG, F32)
        acc_ref[...] = jnp.zeros(acc_ref.shape, F32)
    n = len(items)
    for t in range(n + 2):
        for hd, (s_refs, p_refs, a_refs, m_ref, acc_ref, vx_ref, vpx_ref) in enumerate(head_bufs):
            depth = len(s_refs)
            if t < n:
                scores(hd, items[t], s_refs[t % depth])
            if 1 <= t <= n:
                u = (t - 1) % depth
                softmax(items[t - 1], s_refs[u], p_refs[u], a_refs[u], m_ref)
            if t >= 2:
                u = (t - 2) % depth
                accumulate(items[t - 2], p_refs[u], a_refs[u], acc_ref, vx_ref, vpx_ref)
    for hd, bufs in enumerate(head_bufs):
        acc_ref = bufs[4]
        o_ref[tile, vcols(hd)] = (acc_ref[:, :V_HEAD] / acc_ref[:, V_HEAD:]).astype(o_ref.dtype)


def _attention(q, k, v, kp, vp, batch, seq, tq, tc, td, heads):
    depth = 3
    stat = pltpu.VMEM((tq, 128), F32)
    width = max(tc, tq + 128)
    head_bufs = [([pltpu.VMEM((tq, width), F32)] * depth, [pltpu.VMEM((tq, width), BF)] * depth,
                  [stat] * depth, stat, pltpu.VMEM((tq, 2 * V_HEAD), F32),
                  pltpu.VMEM((seq, 2 * V_HEAD), BF), pltpu.VMEM((128, 2 * V_HEAD), BF))] * heads
    kern = functools.partial(_attn_kernel, tq=tq, tc=tc, td=td, rb=64, n_prefix=N_META)
    return pl.pallas_call(
        kern,
        grid=(batch, N_HEADS // heads),
        in_specs=[pl.BlockSpec((None, seq, heads * HEAD_PAD), lambda b, h: (b, 0, h)),
                  pl.BlockSpec((None, seq, heads * HEAD_PAD), lambda b, h: (b, 0, h)),
                  pl.BlockSpec((None, seq, heads * V_HEAD), lambda b, h: (b, 0, h)),
                  pl.BlockSpec((128, heads * HEAD_PAD), lambda b, h: (0, h)),
                  pl.BlockSpec((128, heads * V_HEAD), lambda b, h: (0, h))],
        out_specs=pl.BlockSpec((None, seq, heads * V_HEAD), lambda b, h: (b, 0, h)),
        out_shape=jax.ShapeDtypeStruct((batch, seq, N_HEADS * V_HEAD), BF),
        scratch_shapes=[head_bufs],
        name="attention",
        compiler_params=_params("parallel", "parallel"),
    )(q, k, v, kp, vp)


def _attn_meta_kernel(q_ref, k_ref, v_ref, o_ref):
    s = _dot_nt(q_ref[...], k_ref[...])
    row = lax.broadcasted_iota(jnp.int32, s.shape, 0)
    col = lax.broadcasted_iota(jnp.int32, s.shape, 1)
    s = jnp.where(row >= col, s, NEG_BIG)
    p = jnp.exp2(s - jnp.max(s, axis=-1, keepdims=True))
    l = jnp.sum(p, axis=-1, keepdims=True)
    o_ref[...] = (_dot(p.astype(BF), v_ref[...]) / l).astype(o_ref.dtype)


def _attention_meta(q, k, v):
    return pl.pallas_call(
        _attn_meta_kernel,
        grid=(N_HEADS,),
        in_specs=[pl.BlockSpec((N_META, HEAD_PAD), lambda h: (0, h)),
                  pl.BlockSpec((N_META, HEAD_PAD), lambda h: (0, h)),
                  pl.BlockSpec((N_META, V_HEAD), lambda h: (0, h))],
        out_specs=pl.BlockSpec((N_META, V_HEAD), lambda h: (0, h)),
        out_shape=jax.ShapeDtypeStruct((N_META, N_HEADS * V_HEAD), BF),
        name="attention_meta",
        compiler_params=_params("parallel"),
    )(q, k, v)


def _pool_branch_kernel(h_ref, w_ref, pw_ref, ps_ref, hist_ref, yc_ref, *rest,
                        tm, tiles_per_seq, pos_offset):
    carry_ref = rest[-1]
    i = pl.program_id(0)
    pin = _dot(h_ref[...], w_ref[...])

    @pl.when(i % tiles_per_seq == 0)
    def _():
        carry_ref[...] = hist_ref[...]

    ext = jnp.concatenate([carry_ref[...], pin], axis=0)
    tail = pin[tm - HIST:]
    carry_ref[...] = tail
    if len(rest) == 2:
        rest[0][...] = tail
    n_ext = tm + HIST
    seen = (lax.broadcasted_iota(jnp.int32, (tm, 1), 0)
            + ((i % tiles_per_seq) * tm + pos_offset + 1)).astype(F32)
    for g, w in enumerate(POOL_WINDOWS):
        xg = ext[:, g * POOL_GROUP:(g + 1) * POOL_GROUP]
        s, span = xg, 1
        while span < w:
            s = s[span:] + s[:s.shape[0] - span]
            span *= 2
        s = s[n_ext - (w - 1) - tm:]
        pooled = s / jnp.minimum(seen, float(w)) - xg[HIST:]
        mixed = _dot(pooled.astype(BF), pw_ref[g])
        yc_ref[:, g * POOL_GROUP:(g + 1) * POOL_GROUP] = (
            mixed * ps_ref[:, g * POOL_GROUP:(g + 1) * POOL_GROUP]).astype(yc_ref.dtype)


def _pool_branch(h, w_pool, pool_w, l, pool_scale, hist, tm, tiles_per_seq, pos_offset,
                 emit_hist):
    rows = h.shape[0]
    assert not emit_hist or rows == tm
    kern = functools.partial(_pool_branch_kernel, tm=tm, tiles_per_seq=tiles_per_seq,
                             pos_offset=pos_offset)
    ng = len(POOL_WINDOWS)
    n_out = 2 if emit_hist else 1
    return pl.pallas_call(
        kern,
        grid=(rows // tm,),
        in_specs=[pl.BlockSpec((tm, D_MODEL), lambda i: (i, 0)),
                  _layer_spec(l, (D_MODEL, D_POOL), lambda i: (0, 0), True),
                  _layer_spec(l, (ng, POOL_GROUP, POOL_GROUP), lambda i: (0, 0, 0), True),
                  pl.BlockSpec((1, D_POOL), lambda i: (0, 0)),
                  pl.BlockSpec((HIST, D_POOL), lambda i: (0, 0))],
        out_specs=[pl.BlockSpec((tm, D_POOL), lambda i: (i, 0)),
                   pl.BlockSpec((HIST, D_POOL), lambda i: (0, 0))][:n_out],
        out_shape=[jax.ShapeDtypeStruct((rows, D_POOL), BF),
                   jax.ShapeDtypeStruct((HIST, D_POOL), F32)][:n_out],
        scratch_shapes=[pltpu.VMEM((HIST, D_POOL), F32)],
        name="pool_branch",
        compiler_params=_params("arbitrary"),
    )(h, w_pool, pool_w, pool_scale, hist)


def _merge_kernel(h_ref, ya_ref, yb_ref, yc_ref, wg0_ref, wg1_ref, wg2_ref, wa_ref, wb_ref, wc_ref,
                  o_ref):
    h = h_ref[...]
    m = jax.nn.sigmoid(_dot(h, wg0_ref[...])) * _dot(ya_ref[...], wa_ref[...])
    m += jax.nn.sigmoid(_dot(h, wg1_ref[...])) * _dot(yb_ref[...], wb_ref[...])
    m += jax.nn.sigmoid(_dot(h, wg2_ref[...])) * _dot(yc_ref[...], wc_ref[...])
    o_ref[...] = m.astype(o_ref.dtype)


def _merge(h, ya, yb, yc, w_gate, wa, wb, wc, l, tm, tn):
    rows = h.shape[0]
    ncb = D_MODEL // tn
    row = lambda width: pl.BlockSpec((tm, width), lambda i, j: (i, 0))
    col = lambda depth, off: _layer_spec(l, (depth, tn), lambda i, j: (0, j + off))
    return pl.pallas_call(
        _merge_kernel,
        grid=(rows // tm, ncb),
        in_specs=[row(D_MODEL), row(D_CONV), row(N_HEADS * V_HEAD), row(D_POOL),
                  col(D_MODEL, 0), col(D_MODEL, ncb), col(D_MODEL, 2 * ncb),
                  col(D_CONV, 0), col(N_HEADS * V_HEAD, 0), col(D_POOL, 0)],
        out_specs=pl.BlockSpec((tm, tn), lambda i, j: (i, j)),
        out_shape=jax.ShapeDtypeStruct((rows, D_MODEL), BF),
        name="merge",
        compiler_params=_params("parallel", "arbitrary"),
    )(h, ya, yb, yc, w_gate, w_gate, w_gate, wa, wb, wc)


def _oproj_kernel(m_ref, w_ref, x_ref, g_ref, xo_ref, h2_ref):
    x = x_ref[...] + _dot(m_ref[...], w_ref[...])
    xo_ref[...] = x
    h2_ref[...] = _rms_norm(x, g_ref[...]).astype(h2_ref.dtype)


def _oproj(merged, w_o, l, x, g, tm):
    rows = x.shape[0]
    return pl.pallas_call(
        _oproj_kernel,
        grid=(rows // tm,),
        in_specs=[pl.BlockSpec((tm, D_MODEL), lambda i: (i, 0)),
                  _layer_spec(l, (D_MODEL, D_MODEL), lambda i: (0, 0), True),
                  pl.BlockSpec((tm, D_MODEL), lambda i: (i, 0)),
                  pl.BlockSpec((1, D_MODEL), lambda i: (0, 0))],
        out_specs=[pl.BlockSpec((tm, D_MODEL), lambda i: (i, 0)),
                   pl.BlockSpec((tm, D_MODEL), lambda i: (i, 0))],
        out_shape=[jax.ShapeDtypeStruct((rows, D_MODEL), F32),
                   jax.ShapeDtypeStruct((rows, D_MODEL), BF)],
        name="oproj",
        compiler_params=_params("parallel"),
    )(merged, w_o, x, g)


def _mlp_kernel(h2_ref, wu_ref, wd_ref, x_ref, o_ref):
    @pl.when(pl.program_id(1) == 0)
    def _():
        o_ref[...] = x_ref[...]

    a = jnp.maximum(_dot(h2_ref[...], wu_ref[...]), 0.0)
    o_ref[...] += _dot((a * a).astype(BF), wd_ref[...])


def _mlp(h2, w_up, w_down, l, x, tm, tf):
    rows = x.shape[0]
    return pl.pallas_call(
        _mlp_kernel,
        grid=(rows // tm, D_FF // tf),
        in_specs=[pl.BlockSpec((tm, D_MODEL), lambda i, k: (i, 0)),
                  _layer_spec(l, (D_MODEL, tf), lambda i, k: (0, k)),
                  _layer_spec(l, (tf, D_MODEL), lambda i, k: (k, 0)),
                  pl.BlockSpec((tm, D_MODEL), lambda i, k: (i, 0))],
        out_specs=pl.BlockSpec((tm, D_MODEL), lambda i, k: (i, 0)),
        out_shape=jax.ShapeDtypeStruct((rows, D_MODEL), F32),
        name="mlp",
        compiler_params=_params("parallel", "arbitrary"),
    )(h2, w_up, w_down, x)


def _swap_halves(a):
    return jnp.concatenate([a[..., HALF_ROPE:], a[..., :HALF_ROPE]], axis=-1)


def _twice(a):
    return jnp.concatenate([a, a], axis=-1)


def _widen_head(a):
    rope = a[..., QK_NOPE:]
    return jnp.concatenate([a[..., :QK_NOPE], rope, _swap_halves(rope)], axis=-1)


def _rope_gain_tables(gain, cos_t, sin_t):
    rope_gain = gain[QK_NOPE:]
    return cos_t * rope_gain[None], sin_t * _swap_halves(rope_gain)[None]


def _stacked_weights(w_in, w_uq, w_ukv, pool_w, w_branch_a, w_branch_b, w_branch_c, w_o, w_up,
                     w_down):
    depth = w_in.shape[0]
    wi = w_in.astype(BF)
    o_q = 3 * D_CONV
    o_kr = o_q + Q_LORA + KV_LORA
    o_pool = o_kr + QK_ROPE
    o_gate = o_pool + D_POOL
    w_kr = wi[:, :, o_kr:o_pool]
    w_ukv_h = w_ukv.astype(BF).reshape(depth, KV_LORA, N_HEADS, QK_NOPE + V_HEAD)
    return dict(
        w_in=wi,
        w_lat=jnp.concatenate([wi[:, :, o_q:o_kr], _twice(w_kr), _twice(_swap_halves(w_kr))],
                              axis=2),
        w_pool=wi[:, :, o_pool:o_gate],
        w_gate=wi[:, :, o_gate:],
        w_uq=_widen_head(w_uq.astype(BF).reshape(depth, Q_LORA, N_HEADS, QK_HEAD)).reshape(
            depth, Q_LORA, N_HEADS * HEAD_PAD),
        w_uk=w_ukv_h[..., :QK_NOPE].reshape(depth, KV_LORA, N_HEADS * QK_NOPE),
        w_uv=w_ukv_h[..., QK_NOPE:].reshape(depth, KV_LORA, N_HEADS * V_HEAD),
        pool_w=pool_w.astype(BF),
        wa=w_branch_a.astype(BF),
        wb=w_branch_b.astype(BF),
        wc=w_branch_c.astype(BF),
        w_o=w_o.astype(BF),
        w_up=w_up.astype(BF),
        w_down=w_down.astype(BF),
    )


def _layer_vectors(l, cos_t, sin_t, attn_norm, conv_w, q_lat_norm, kv_lat_norm, q_norm, k_norm,
                   pool_scale, mlp_norm):
    gq = q_norm[l] * (QK_HEAD ** -0.5 * LOG2_E)
    q_cos, q_sin = _rope_gain_tables(gq, cos_t, sin_t)
    k_cos, k_sin = _rope_gain_tables(k_norm[l], cos_t, sin_t)
    return dict(
        attn_norm=attn_norm[l][None],
        conv_w=conv_w[l],
        q_lat_norm=q_lat_norm[l][None],
        kv_lat_norm=kv_lat_norm[l][None],
        gq_nope=gq[None, :QK_NOPE],
        gk_nope=k_norm[l][None, :QK_NOPE],
        q_trig=jnp.concatenate([q_cos, q_sin], axis=1),
        k_rope=(_twice(k_cos), _twice(k_sin)),
        pool_scale=pool_scale[l][None],
        mlp_norm=mlp_norm[l][None],
    )


def _rope_tables(total):
    pos = jnp.arange(total, dtype=F32)
    inv = ROPE_THETA ** (-jnp.arange(0, QK_ROPE, 2, dtype=F32) / QK_ROPE)
    ang = pos[:, None] * inv[None, :]
    cos, sin = jnp.cos(ang), jnp.sin(ang)
    return _twice(cos), jnp.concatenate([-sin, sin], axis=1)


class _Tiles(NamedTuple):
    rows: int
    rows_big: int
    conv_cols: int
    merge_cols: int
    mlp_hidden: int
    attn_q: int
    attn_chunk: int
    attn_band: int
    attn_heads: int


def _tiles(seq):
    return _Tiles(rows=min(512, seq), rows_big=min(1024, seq), conv_cols=1024, merge_cols=512,
                  mlp_hidden=512, attn_q=512, attn_chunk=1024, attn_band=256, attn_heads=1)


def _layer(x, w, l, p, positions, prefix, *, batch, seq, finish):
    is_meta = prefix is None
    q_trig = p['q_trig'][positions]
    k_cos, k_sin = (t[positions] for t in p['k_rope'])
    t = _tiles(seq)
    tm, tm_big = t.rows, t.rows_big
    tps = seq // tm
    if is_meta:
        conv_hist = jnp.zeros((HIST, D_CONV), F32)
        pool_hist = jnp.zeros((HIST, D_POOL), F32)
    else:
        conv_hist, pool_hist, kp, vp = prefix

    h, qn, kvn, kr = _latent(x, p['attn_norm'], w['w_lat'], l, p['q_lat_norm'], p['kv_lat_norm'],
                             tm_big)
    ya, *conv_tail = _conv_branch(h, w['w_in'], l, p['conv_w'], conv_hist, tm_big, t.conv_cols,
                                  seq // tm_big, is_meta)
    yc, *pool_tail = _pool_branch(h, w['w_pool'], w['pool_w'], l, p['pool_scale'], pool_hist,
                                  tm_big, seq // tm_big, 0 if is_meta else N_META, is_meta)
    k, v = _kv_proj(kvn, w['w_uk'], w['w_uv'], l, kr, p['gk_nope'], k_cos, k_sin, tm, N_HEADS,
                    tps)
    out_prefix = (conv_tail[0], pool_tail[0], k, v) if is_meta else None
    if not finish:
        return None, out_prefix
    q = _q_proj(qn, w['w_uq'], l, p['gq_nope'], q_trig, tm, N_HEADS, tps)
    if is_meta:
        yb = _attention_meta(q, k, v)
    else:
        pad = ((0, 128 - N_META), (0, 0))
        per_batch = lambda a: a.reshape(batch, seq, a.shape[-1])
        yb = _attention(per_batch(q), per_batch(k), per_batch(v), jnp.pad(kp, pad),
                        jnp.pad(vp, pad), batch, seq, t.attn_q, t.attn_chunk, t.attn_band,
                        t.attn_heads).reshape(batch * seq, -1)
    merged = _merge(h, ya, yb, yc, w['w_gate'], w['wa'], w['wb'], w['wc'], l, tm_big, t.merge_cols)
    x_mid, h2 = _oproj(merged, w['w_o'], l, x, p['mlp_norm'], tm)
    return _mlp(h2, w['w_up'], w['w_down'], l, x_mid, tm_big, t.mlp_hidden), out_prefix


def kernel(x, meta_tokens, attn_norm, w_in, conv_w, q_lat_norm, kv_lat_norm, w_uq, w_ukv, q_norm, k_norm, pool_w, pool_scale, w_branch_a, w_branch_b, w_branch_c, w_o, mlp_norm, w_up, w_down):
    batch, seq, d = x.shape
    depth = w_in.shape[0]
    cos_t, sin_t = _rope_tables(N_META + seq)
    xm = meta_tokens.astype(F32)
    xr = x.reshape(batch * seq, d)
    w = _stacked_weights(w_in, w_uq, w_ukv, pool_w, w_branch_a, w_branch_b, w_branch_c, w_o, w_up,
                         w_down)
    for l in range(depth):
        p = _layer_vectors(l, cos_t, sin_t, attn_norm, conv_w, q_lat_norm, kv_lat_norm, q_norm,
                           k_norm, pool_scale, mlp_norm)
        last = l == depth - 1
        xm, prefix = _layer(xm, w, l, p, slice(0, N_META), None, batch=1, seq=N_META,
                            finish=not last)
        xr, _ = _layer(xr, w, l, p, slice(N_META, None), prefix, batch=batch, seq=seq, finish=True)
    return xr.reshape(batch, seq, d)
```
